```python
import numpy as np
import jax
import jax.numpy as jnp
from jax import lax

D_MODEL = 1024
BATCH = 2
SEQ = 16384
DEPTH = 2

HEAD_DIM = 64
D_MIX = D_MODEL
N_META = 16
BLOCK = 128
PAD = BLOCK - N_META
ROPE_THETA = 500000.0
ROPE_DIM = HEAD_DIM // 4
NEG = -1e30

DSA_HEADS = 4
IDX_HEADS = 4
IDX_DIM = 64
TOPK_MAX = 256

FOX_HEADS = 4

GLA_HEADS = 4
GLA_DK = 64
GLA_DV = 128
GLA_RANK = 16
GLA_TAU = 16.0
GLA_CHUNK = 64

D_FF = ((8 * D_MODEL + 3 * 256 - 1) // (3 * 256)) * 256

IN_SPLITS = (
    DSA_HEADS * HEAD_DIM, HEAD_DIM, HEAD_DIM,
    IDX_HEADS * IDX_DIM, IDX_HEADS, IDX_DIM,
    FOX_HEADS * HEAD_DIM, FOX_HEADS * HEAD_DIM, FOX_HEADS * HEAD_DIM, FOX_HEADS,
    GLA_HEADS * GLA_DK, GLA_HEADS * GLA_DK, GLA_HEADS * GLA_DV, GLA_HEADS * GLA_DV, GLA_RANK,
)
N_IN = sum(IN_SPLITS)

kernel_name = 'hybrid_dsa_fox_gla_block'


def rms_norm(x, g, eps=1e-6):
    xf = x.astype(jnp.float32)
    y = xf * lax.rsqrt(jnp.mean(xf * xf, axis=-1, keepdims=True) + eps)
    return (y * g.astype(jnp.float32)).astype(x.dtype)


def partial_rope(x, pos):
    half = ROPE_DIM // 2
    inv = jnp.power(ROPE_THETA, -jnp.arange(half, dtype=jnp.float32) * 2.0 / ROPE_DIM)
    ang = pos.astype(jnp.float32)[:, None] * inv[None, :]
    cos = jnp.cos(ang)[None, :, None, :]
    sin = jnp.sin(ang)[None, :, None, :]
    xf = x.astype(jnp.float32)
    x1, x2, rest = xf[..., :half], xf[..., half:ROPE_DIM], xf[..., ROPE_DIM:]
    out = jnp.concatenate([x1 * cos - x2 * sin, x2 * cos + x1 * sin, rest], axis=-1)
    return out.astype(x.dtype)


def dsa_attention(q, k, v, iq, iw, ik, k_top):
    bsz, n_pad, n_heads, d_head = q.shape
    n_blocks = n_pad // BLOCK
    key_idx = jnp.arange(n_pad)
    gather = jax.vmap(lambda arr, idx: arr[idx])

    def one_block(i):
        t0 = i * BLOCK
        qb = lax.dynamic_slice_in_dim(q, t0, BLOCK, axis=1)
        iqb = lax.dynamic_slice_in_dim(iq, t0, BLOCK, axis=1)
        iwb = lax.dynamic_slice_in_dim(iw, t0, BLOCK, axis=1).astype(jnp.float32) * IDX_HEADS ** -0.5
        q_idx = t0 + jnp.arange(BLOCK)
        admissible = (key_idx[None, :] <= q_idx[:, None]) & (key_idx[None, :] >= PAD)
        dots = jnp.einsum('bqhd,bkd->bqhk', iqb, ik).astype(jnp.float32) * IDX_DIM ** -0.5
        score = jnp.einsum('bqh,bqhk->bqk', iwb, jax.nn.relu(dots))
        score = jnp.where(admissible[None], score, NEG)
        _, sel = lax.top_k(score, k_top)
        valid = (sel <= q_idx[None, :, None]) & (sel >= PAD)
        k_sel = gather(k, sel)
        v_sel = gather(v, sel)
        logits = jnp.einsum('bqhd,bqkd->bqhk', qb, k_sel).astype(jnp.float32) * d_head ** -0.5
        logits = jnp.where(valid[:, :, None, :], logits, NEG)
        probs = jax.nn.softmax(logits, axis=-1).astype(v.dtype)
        return jnp.einsum('bqhk,bqkd->bqhd', probs, v_sel)

    out = lax.map(one_block, jnp.arange(n_blocks))
    return jnp.moveaxis(out, 0, 1).reshape(bsz, n_pad, n_heads, d_head)


def forgetting_attention(q, k, v, log_f):
    bsz, n_pad, n_heads, d_head = q.shape
    n_blocks = n_pad // BLOCK
    key_idx = jnp.arange(n_pad)
    c = jnp.moveaxis(jnp.cumsum(log_f, axis=1), 1, 2)

    def one_block(i):
        t0 = i * BLOCK
        qb = lax.dynamic_slice_in_dim(q, t0, BLOCK, axis=1)
        cq = lax.dynamic_slice_in_dim(c, t0, BLOCK, axis=2)
        q_idx = t0 + jnp.arange(BLOCK)
        mask = (key_idx[None, :] <= q_idx[:, None]) & (key_idx[None, :] >= PAD)
        logits = (jnp.einsum('bqhd,bkhd->bhqk', qb, k).astype(jnp.float32) * d_head ** -0.5
                  + cq[..., None] - c[:, :, None, :])
        logits = jnp.where(mask[None, None], logits, NEG)
        probs = jax.nn.softmax(logits, axis=-1).astype(v.dtype)
        return jnp.einsum('bhqk,bkhd->bqhd', probs, v)

    out = lax.map(one_block, jnp.arange(n_blocks))
    return jnp.moveaxis(out, 0, 1).reshape(bsz, n_pad, n_heads, d_head)


def gla_chunked(q, k, v, log_a):
    bsz, n_pad, n_heads, d_k = q.shape
    d_v = v.shape[-1]
    n_chunks = n_pad // GLA_CHUNK

    def to_chunks(a):
        return jnp.moveaxis(a.reshape(bsz, n_chunks, GLA_CHUNK, *a.shape[2:]), 1, 0)

    causal = jnp.tril(jnp.ones((GLA_CHUNK, GLA_CHUNK), dtype=bool))

    def step(state, inp):
        qc, kc, vc, gc = inp
        b = jnp.cumsum(gc, axis=1)
        o_inter = jnp.einsum('bthd,bhde->bthe', qc * jnp.exp(b), state)
        diff = b[:, :, None] - b[:, None, :]
        decay = jnp.where(causal[None, :, :, None, None], jnp.exp(jnp.minimum(diff, 0.0)), 0.0)
        att = jnp.einsum('bthd,bshd,btshd->bhts', qc, kc, decay)
        o_intra = jnp.einsum('bhts,bshe->bthe', att, vc)
        b_last = b[:, -1]
        state = (state * jnp.exp(b_last)[..., None]
                 + jnp.einsum('bshd,bshe->bhde', kc * jnp.exp(b_last[:, None] - b), vc))
        return state, o_inter + o_intra

    s0 = jnp.zeros((bsz, n_heads, d_k, d_v), jnp.float32)
    _, out = lax.scan(step, s0, (to_chunks(q), to_chunks(k), to_chunks(v), to_chunks(log_a)))
    return jnp.moveaxis(out, 0, 1).reshape(bsz, n_pad, n_heads, d_v)


def hybrid_layer(h, pos, k_top, attn_norm, w_in, dsa_q_norm, dsa_k_norm, fox_q_norm, fox_k_norm,
                 fox_f_bias, gla_gate_w2, gla_gate_b, gla_out_norm, w_out, ffn_norm, w_gate_up, w_down):
    bsz, n_tok, _ = h.shape
    n_pad = n_tok + PAD
    a = rms_norm(h, attn_norm)
    proj = jnp.pad(a @ w_in, ((0, 0), (PAD, 0), (0, 0)))
    split_points = np.cumsum(IN_SPLITS)[:-1].tolist()
    (dq, dk, dv, iq, iw, ik, fq, fk, fv, ff, gq, gk, gv, gg, glr) = jnp.split(proj, split_points, axis=-1)

    def heads(t, n):
        return t.reshape(bsz, n_pad, n, t.shape[-1] // n)

    dq = partial_rope(rms_norm(heads(dq, DSA_HEADS), dsa_q_norm), pos)
    dk = partial_rope(rms_norm(dk, dsa_k_norm)[:, :, None], pos)[:, :, 0]
    iq = partial_rope(heads(iq, IDX_HEADS), pos)
    ik = partial_rope(ik[:, :, None], pos)[:, :, 0]
    o_a = dsa_attention(dq, dk, dv, iq, iw, ik, k_top)

    fq = rms_norm(heads(fq, FOX_HEADS), fox_q_norm)
    fk = rms_norm(heads(fk, FOX_HEADS), fox_k_norm)
    log_f = jax.nn.log_sigmoid((ff + fox_f_bias).astype(jnp.float32))
    o_b = forgetting_attention(fq, fk, heads(fv, FOX_HEADS), log_f)

    log_a = jax.nn.log_sigmoid((glr @ gla_gate_w2 + gla_gate_b).astype(jnp.float32)) / GLA_TAU
    o_c = gla_chunked(heads(gq, GLA_HEADS).astype(jnp.float32) * GLA_DK ** -0.5,
                      heads(gk, GLA_HEADS).astype(jnp.float32),
                      heads(gv, GLA_HEADS).astype(jnp.float32),
                      heads(log_a, GLA_HEADS))
    o_c = rms_norm(o_c, gla_out_norm).astype(h.dtype) * jax.nn.silu(heads(gg, GLA_HEADS))

    mixed = jnp.concatenate([o_a.reshape(bsz, n_pad, -1), o_b.reshape(bsz, n_pad, -1),
                             o_c.reshape(bsz, n_pad, -1)], axis=-1)[:, PAD:]
    h = h + mixed @ w_out

    f = rms_norm(h, ffn_norm)
    gate, up = jnp.split(f @ w_gate_up, 2, axis=-1)
    return h + (jax.nn.silu(gate) * up) @ w_down


def setup_inputs(seed: int = 0) -> dict:
    key = jax.random.key(seed)
    ks = jax.random.split(key, 16)
    f32 = jnp.float32

    def normal(k, shape, scale):
        return jax.random.normal(k, shape, f32) * scale

    def gain(k, shape):
        return 1.0 + 0.02 * jax.random.normal(k, shape, f32)

    return {
        'x': normal(ks[0], (BATCH, SEQ, D_MODEL), 1.0),
        'meta_tokens': normal(ks[1], (N_META, D_MODEL), 1.0),
        'attn_norm': gain(ks[2], (DEPTH, D_MODEL)),
        'w_in': normal(ks[3], (DEPTH, D_MODEL, N_IN), D_MODEL ** -0.5),
        'dsa_q_norm': gain(ks[4], (DEPTH, HEAD_DIM)),
        'dsa_k_norm': gain(ks[5], (DEPTH, HEAD_DIM)),
        'fox_q_norm': gain(ks[6], (DEPTH, HEAD_DIM)),
        'fox_k_norm': gain(ks[7], (DEPTH, HEAD_DIM)),
        'fox_f_bias': 2.0 + normal(ks[8], (DEPTH, FOX_HEADS), 0.1),
        'gla_gate_w2': normal(ks[9], (DEPTH, GLA_RANK, GLA_HEADS * GLA_DK), GLA_RANK ** -0.5),
        'gla_gate_b': normal(ks[10], (DEPTH, GLA_HEADS * GLA_DK), 0.1),
        'gla_out_norm': gain(ks[11], (DEPTH, GLA_DV)),
        'w_out': normal(ks[12], (DEPTH, D_MIX, D_MODEL), D_MIX ** -0.5),
        'ffn_norm': gain(ks[13], (DEPTH, D_MODEL)),
        'w_gate_up': normal(ks[14], (DEPTH, D_MODEL, 2 * D_FF), D_MODEL ** -0.5),
        'w_down': normal(ks[15], (DEPTH, D_FF, D_MODEL), D_FF ** -0.5),
    }


def reference(x, meta_tokens, attn_norm, w_in, dsa_q_norm, dsa_k_norm, fox_q_norm, fox_k_norm,
              fox_f_bias, gla_gate_w2, gla_gate_b, gla_out_norm, w_out, ffn_norm, w_gate_up, w_down):
    bsz, seq, _ = x.shape
    k_top = min(TOPK_MAX, seq // 4)
    meta = jnp.broadcast_to(meta_tokens[None].astype(x.dtype), (bsz, N_META, D_MODEL))
    h = jnp.concatenate([meta, x], axis=1)
    pos = jnp.arange(PAD + N_META + seq) - PAD
    for l in range(DEPTH):
        h = hybrid_layer(h, pos, k_top, attn_norm[l], w_in[l], dsa_q_norm[l], dsa_k_norm[l],
                         fox_q_norm[l], fox_k_norm[l], fox_f_bias[l], gla_gate_w2[l], gla_gate_b[l],
                         gla_out_norm[l], w_out[l], ffn_norm[l], w_gate_up[l], w_down[l])
    return h[:, N_META:]
```

```python
import functools

import numpy as np
import jax
import jax.numpy as jnp
from jax import lax
from jax.experimental import pallas as pl
from jax.experimental.pallas import tpu as pltpu

F32 = jnp.float32
BF16 = jnp.bfloat16

D_MODEL = 1024
HEAD_DIM = 64
N_META = 16
ROPE_THETA = 500000.0
ROPE_DIM = HEAD_DIM // 4
ROPE_HALF = ROPE_DIM // 2
NEG = -1e30
EPS = 1e-6

N_HEADS = 4
TOPK_MAX = 256
GLA_DV = 128
GLA_RANK = 16
GLA_TAU = 16.0
GLA_CHUNK = 64
GLA_SUB = 16
D_FF = 2816

LANES = 128
SEQ_ALIGN = 256
VMEM_LIMIT = 56 * 1024 * 1024

_SLABS = (("dq", 256), ("dkk", 128), ("dvv", 128), ("iq", 256), ("ikk", 128),
          ("fq", 256), ("fk", 256), ("fv", 256), ("gq", 256), ("gk", 256),
          ("gv", 512), ("gg", 512), ("small", 128))
_SLAB_OFF = {}
_off = 0
for _name, _width in _SLABS:
    _SLAB_OFF[_name] = (_off, _width)
    _off += _width
N_PROJ = _off
SM_IW, SM_FF, SM_GLR = 0, 4, 8


def _work_len(n_tok):
    return -(-n_tok // SEQ_ALIGN) * SEQ_ALIGN


def _row_tile(t_work):
    for cand in (640, 512, 384, 256):
        if t_work % cand == 0:
            return cand
    raise ValueError(f"unsupported working length {t_work}")


def _dot(a, b):
    return jnp.dot(a, b, preferred_element_type=F32)


def _dot_nt(a, b):
    return lax.dot_general(a, b, (((1,), (1,)), ((), ())), preferred_element_type=F32)


def _dot_tn(a, b):
    return lax.dot_general(a, b, (((0,), (0,)), ((), ())), preferred_element_type=F32)


def _split3(x):
    h1 = x.astype(BF16)
    r1 = x - h1.astype(F32)
    h2 = r1.astype(BF16)
    h3 = (r1 - h2.astype(F32)).astype(BF16)
    return h1, h2, h3


def _log_sigmoid(x):
    return jnp.minimum(x, 0.0) - jnp.log1p(jnp.exp(-jnp.abs(x)))


def _silu(x):
    return x / (1.0 + jnp.exp(-x))


def _group_rms(y, gmat, gain):
    yy = y * y
    hi = yy.astype(BF16)
    lo = (yy - hi.astype(F32)).astype(BF16)
    ss = _dot(hi, gmat) + _dot(lo, gmat)
    return y * lax.rsqrt(ss * (1.0 / HEAD_DIM) + EPS) * gain


def _rope(y, cos, sin):
    width = y.shape[-1]
    lane = lax.broadcasted_iota(jnp.int32, y.shape, 1) % HEAD_DIM
    upper = pltpu.roll(y, width - ROPE_HALF, axis=1)
    lower = pltpu.roll(y, ROPE_HALF, axis=1)
    partner = jnp.where(lane < ROPE_HALF, upper, lower)
    return y * cos + partner * sin


def _inproj_kernel(x_ref, an_ref, w_ref, cos_ref, sin_ref, g256_ref, vec_ref, w2_ref, sm_ref,
                   dq_ref, dkk_ref, dvv_ref, iq_ref, ikk_ref, fq_ref, fk_ref, fv_ref,
                   gq_ref, gk_ref, gv_ref, gg_ref, la_ref, small_ref):
    x = x_ref[0]
    ms = jnp.mean(x * x, axis=-1, keepdims=True)
    a = (x * lax.rsqrt(ms + EPS) * an_ref[...]).astype(BF16)

    def proj(name):
        off, width = _SLAB_OFF[name]
        return _dot(a, w_ref[:, off:off + width])

    cos = cos_ref[...]
    sin = sin_ref[...]
    g256 = g256_ref[...]
    g128 = g256[:LANES, :LANES]
    dqn, dkn, fqn, fkn = vec_ref[0:1, :], vec_ref[1:2, :LANES], vec_ref[2:3, :], vec_ref[3:4, :]
    scale = HEAD_DIM ** -0.5

    dq_ref[0] = (_rope(_group_rms(proj("dq"), g256, dqn), cos, sin) * scale).astype(BF16)
    dkk_ref[0] = _rope(_group_rms(proj("dkk"), g128, dkn), cos[:, :LANES], sin[:, :LANES]).astype(BF16)
    dvv_ref[0] = proj("dvv").astype(BF16)
    iq_ref[0] = (_rope(proj("iq"), cos, sin) * scale).astype(BF16)
    ikk_ref[0] = _rope(proj("ikk"), cos[:, :LANES], sin[:, :LANES]).astype(BF16)
    fq_ref[0] = (_group_rms(proj("fq"), g256, fqn) * scale).astype(BF16)
    fk_ref[0] = _group_rms(proj("fk"), g256, fkn).astype(BF16)
    fv_ref[0] = proj("fv").astype(BF16)
    gq_ref[0] = proj("gq") * scale
    gk_ref[0] = proj("gk")
    gv_ref[0] = proj("gv").astype(BF16)
    gg_ref[0] = proj("gg")

    small = proj("small")
    lane = lax.broadcasted_iota(jnp.int32, small.shape, 1)
    small_ref[0] = jnp.where(lane < SM_FF, small * (N_HEADS ** -0.5),
                             _log_sigmoid(small + sm_ref[0:1, :]))
    gate = _dot(small.astype(BF16), w2_ref[...]) + vec_ref[4:5, :]
    la_ref[0] = _log_sigmoid(gate) * (1.0 / GLA_TAU)


def _inproj(h, an, w_perm, cos_t, sin_t, g256, vecs, w2_pad, sm_bias):
    bsz, t_work, _ = h.shape
    tm = _row_tile(t_work)
    grid = (bsz, t_work // tm)

    def rows(width, dtype):
        return (jax.ShapeDtypeStruct((bsz, t_work, width), dtype),
                pl.BlockSpec((1, tm, width), lambda b, j: (b, j, 0)))

    outs = [rows(256, BF16), rows(128, BF16), rows(128, BF16), rows(256, BF16), rows(128, BF16),
            rows(256, BF16), rows(256, BF16), rows(256, BF16), rows(256, F32), rows(256, F32),
            rows(512, BF16), rows(512, F32), rows(256, F32), rows(128, F32)]
    const = lambda shape: pl.BlockSpec(shape, lambda b, j: (0,) * len(shape))
    return pl.pallas_call(
        _inproj_kernel,
        grid=grid,
        in_specs=[pl.BlockSpec((1, tm, D_MODEL), lambda b, j: (b, j, 0)),
                  const((1, D_MODEL)), const((D_MODEL, N_PROJ)),
                  pl.BlockSpec((tm, 256), lambda b, j: (j, 0)),
                  pl.BlockSpec((tm, 256), lambda b, j: (j, 0)),
                  const((256, 256)), const((8, 256)), const((LANES, 256)), const((8, LANES))],
        out_specs=[o[1] for o in outs],
        out_shape=[o[0] for o in outs],
        compiler_params=pltpu.CompilerParams(
            dimension_semantics=("parallel", "parallel"), vmem_limit_bytes=VMEM_LIMIT),
        name="inproj",
    )(h, an, w_perm, cos_t, sin_t, g256, vecs, w2_pad, sm_bias)


def _prep_layer_params(l, attn_norm, w_in, dsa_q_norm, dsa_k_norm, fox_q_norm, fox_k_norm,
                       fox_f_bias, gla_gate_w2, gla_gate_b):
    w = w_in[l]
    splits = np.cumsum([256, 64, 64, 256, 4, 64, 256, 256, 256, 4, 256, 256, 512, 512, 16])[:-1]
    (dq, dk, dv, iq, iw, ik, fq, fk, fv, ff, gq, gk, gv, gg, glr) = jnp.split(w, splits, axis=1)
    small = jnp.concatenate(
        [iw, ff, glr, jnp.zeros((D_MODEL, LANES - 4 - 4 - GLA_RANK), w.dtype)], axis=1)
    w_perm = jnp.concatenate([dq, dk, dk, dv, dv, iq, ik, ik, fq, fk, fv, gq, gk, gv, gg, small],
                             axis=1).astype(BF16)
    tile4 = lambda g: jnp.tile(g, N_HEADS)
    vecs = jnp.zeros((8, 256), F32)
    vecs = vecs.at[0].set(tile4(dsa_q_norm[l])).at[1].set(tile4(dsa_k_norm[l]))
    vecs = vecs.at[2].set(tile4(fox_q_norm[l])).at[3].set(tile4(fox_k_norm[l]))
    vecs = vecs.at[4].set(gla_gate_b[l])
    w2_pad = jnp.zeros((LANES, 256), F32).at[SM_GLR:SM_GLR + GLA_RANK].set(gla_gate_w2[l]).astype(BF16)
    sm_bias = jnp.zeros((8, LANES), F32).at[0, SM_FF:SM_FF + N_HEADS].set(fox_f_bias[l])
    return attn_norm[l][None, :], w_perm, vecs, w2_pad, sm_bias


def _rope_tables(t_work):
    inv = jnp.power(ROPE_THETA, -jnp.arange(ROPE_HALF, dtype=F32) * 2.0 / ROPE_DIM)
    ang = jnp.arange(t_work).astype(F32)[:, None] * inv[None, :]
    cos, sin = jnp.cos(ang), jnp.sin(ang)
    rest = HEAD_DIM - ROPE_DIM
    cos64 = jnp.concatenate([cos, cos, jnp.ones((t_work, rest), F32)], axis=1)
    sin64 = jnp.concatenate([-sin, sin, jnp.zeros((t_work, rest), F32)], axis=1)
    return jnp.tile(cos64, (1, N_HEADS)), jnp.tile(sin64, (1, N_HEADS))


def _group_matrix():
    idx = np.arange(256) // HEAD_DIM
    return jnp.asarray((idx[:, None] == idx[None, :]).astype(np.float32), dtype=BF16)


DSA_TQ = 128
DSA_TK = 256
KEY_NEG_INF = -2139095041
KEY_POS_INF = 2139095040
MAX_BISECT = 40


def _key_to_f32(key):
    bits = key ^ ((key >> 31) & 0x7FFFFFFF)
    return lax.bitcast_convert_type(bits, F32)


def _dsa_kernel(k_top, dq_ref, kk_ref, vv_ref, iq_ref, ik_ref, sm_ref, tri_ref, o_ref, s_ref):
    i = pl.program_id(1)
    q0 = i * DSA_TQ
    n_tiles = (q0 + DSA_TQ + DSA_TK - 1) // DSA_TK
    last = n_tiles - 1
    lane = lax.broadcasted_iota(jnp.int32, (DSA_TQ, LANES), 1)
    row = q0 + lax.broadcasted_iota(jnp.int32, (DSA_TQ, DSA_TK), 0)
    col = lax.broadcasted_iota(jnp.int32, (DSA_TQ, DSA_TK), 1)
    head_lanes = [(lane < HEAD_DIM) if h % 2 == 0 else (lane >= HEAD_DIM) for h in range(N_HEADS)]

    def masked_heads(ref):
        out = []
        for h in range(N_HEADS):
            slab = ref[0, :, (h // 2) * LANES:(h // 2 + 1) * LANES]
            out.append(jnp.where(head_lanes[h], slab, jnp.zeros_like(slab)))
        return out

    iq_h = masked_heads(iq_ref)
    w_h = [jnp.broadcast_to(sm_ref[0, :, SM_IW + h:SM_IW + h + 1], (DSA_TQ, DSA_TK))
           for h in range(N_HEADS)]

    def score_tile(j, carry):
        k0 = pl.multiple_of(j * DSA_TK, DSA_TK)
        ik_t = ik_ref[0, pl.ds(k0, DSA_TK), :]
        s = w_h[0] * jnp.maximum(_dot_nt(iq_h[0], ik_t), 0.0)
        for h in range(1, N_HEADS):
            s = s + w_h[h] * jnp.maximum(_dot_nt(iq_h[h], ik_t), 0.0)
        s = jnp.where(k0 + col <= row, s, NEG)

        s_ref[:, pl.ds(k0, DSA_TK)] = s
        return carry

    lax.fori_loop(0, n_tiles, score_tile, 0)

    kf = jnp.float32(k_top)

    def count_above(t):
        tb = jnp.broadcast_to(t, (DSA_TQ, LANES))

        def body(j, acc):
            k0 = pl.multiple_of(j * DSA_TK, DSA_TK)
            for part in range(DSA_TK // LANES):
                s = s_ref[:, pl.ds(k0 + part * LANES, LANES)]
                acc = acc + jnp.where(s > tb, 1.0, 0.0)
            return acc

        acc = lax.fori_loop(0, n_tiles, body, jnp.zeros((DSA_TQ, LANES), F32))
        return jnp.sum(acc, axis=-1, keepdims=True)

    def midpoint(lo, hi):
        return (lo >> 1) + (hi >> 1) + (lo & hi & 1)

    def unresolved(lo, hi):
        return jnp.max(jnp.where(midpoint(lo, hi) != lo, 1, 0))

    def bisect_cond(state):
        it, pending = state[0], state[1]
        return (pending > 0) & (it < MAX_BISECT)

    def bisect_body(state):
        it, _, lo, hi, chi = state
        mid = midpoint(lo, hi)
        c = count_above(_key_to_f32(mid))
        live = mid != lo
        ge = c >= kf
        keep_hi = ge & (c != kf)
        lo = jnp.where(live & ge, mid, lo)
        hi = jnp.where(live & ~keep_hi, mid, hi)
        chi = jnp.where(live & ~keep_hi, c, chi)
        return it + 1, unresolved(lo, hi), lo, hi, chi

    lo0 = jnp.full((DSA_TQ, 1), KEY_NEG_INF, jnp.int32)
    hi0 = jnp.full((DSA_TQ, 1), KEY_POS_INF, jnp.int32)
    _, _, _, hi, chi = lax.while_loop(
        bisect_cond, bisect_body,
        (jnp.int32(0), jnp.int32(1), lo0, hi0, jnp.zeros((DSA_TQ, 1), F32)))
    thr = jnp.broadcast_to(_key_to_f32(hi), (DSA_TQ, DSA_TK))
    n_ties = jnp.broadcast_to(kf - chi, (DSA_TQ, DSA_TK))

    q_h = masked_heads(dq_ref)
    tri = tri_ref[...]

    def attend(j, carry, causal):
        seen, stats = carry
        k0 = pl.multiple_of(j * DSA_TK, DSA_TK)
        s = s_ref[:, pl.ds(k0, DSA_TK)]
        tie = s == thr
        rank = _dot(jnp.where(tie, 1.0, 0.0).astype(BF16), tri) + seen
        sel = (s > thr) | (tie & (rank <= n_ties))
        if causal:
            sel = sel & (k0 + col <= row)
        kk_t = kk_ref[0, pl.ds(k0, DSA_TK), :]
        vv_t = vv_ref[0, pl.ds(k0, DSA_TK), :]
        new_stats = []
        for h in range(N_HEADS):
            m, l, acc = stats[h]
            logit = jnp.where(sel, _dot_nt(q_h[h], kk_t), NEG)
            m_new = jnp.maximum(m, jnp.max(logit, axis=-1, keepdims=True))
            alpha = jnp.exp(m - m_new)
            p = jnp.exp(logit - m_new)
            l = l * alpha + jnp.sum(p, axis=-1, keepdims=True)
            acc = acc * alpha + _dot(p.astype(BF16), vv_t)
            new_stats.append((m_new, l, acc))
        return rank[:, DSA_TK - 1:DSA_TK], tuple(new_stats)

    init = (jnp.zeros((DSA_TQ, 1), F32),
            tuple((jnp.full((DSA_TQ, 1), NEG, F32), jnp.zeros((DSA_TQ, 1), F32),
                   jnp.zeros((DSA_TQ, LANES), F32)) for _ in range(N_HEADS)))
    carry = lax.fori_loop(0, last, functools.partial(attend, causal=False), init)
    _, stats = attend(last, carry, causal=True)
    out_h = [acc / l for (_, l, acc) in stats]
    o_ref[0] = jnp.concatenate(
        [jnp.where(lane < HEAD_DIM, out_h[0], out_h[1]),
         jnp.where(lane < HEAD_DIM, out_h[2], out_h[3])], axis=1).astype(o_ref.dtype)


def _dsa_attention(dq, dkk, dvv, iq, ikk, small, k_top):
    bsz, t_work, _ = dq.shape
    tri = jnp.asarray(np.triu(np.ones((DSA_TK, DSA_TK), np.float32)), dtype=BF16)
    tile = lambda width: pl.BlockSpec((1, DSA_TQ, width), lambda b, i: (b, i, 0))
    full = pl.BlockSpec((1, t_work, LANES), lambda b, i: (b, 0, 0))
    return pl.pallas_call(
        functools.partial(_dsa_kernel, k_top),
        grid=(bsz, t_work // DSA_TQ),
        in_specs=[tile(256), full, full, tile(256), full, tile(LANES),
                  pl.BlockSpec((DSA_TK, DSA_TK), lambda b, i: (0, 0))],
        out_specs=tile(256),
        out_shape=jax.ShapeDtypeStruct((bsz, t_work, 256), BF16),
        scratch_shapes=[pltpu.VMEM((DSA_TQ, t_work), F32)],
        compiler_params=pltpu.CompilerParams(
            dimension_semantics=("parallel", "parallel"), vmem_limit_bytes=VMEM_LIMIT),
        name="dsa_attention",
    )(dq, dkk, dvv, iq, ikk, small, tri)


FOX_TQ = 128
FOX_TK = 256
CUM_T = 256


def _cumsum_kernel(x_ref, tri_ref, out_ref, carry_ref):
    @pl.when(pl.program_id(1) == 0)
    def _():
        carry_ref[...] = jnp.zeros_like(carry_ref)

    tri = tri_ref[...]
    h1, h2, h3 = _split3(x_ref[0])
    c = _dot(tri, h1) + _dot(tri, h2) + _dot(tri, h3) + carry_ref[0:1, :]
    carry_ref[...] = jnp.broadcast_to(c[CUM_T - 1:CUM_T, :], carry_ref.shape)
    out_ref[0] = c.T[0:8, :]


def _fox_cumsum(small):
    bsz, t_work, _ = small.shape
    tri = jnp.asarray(np.tril(np.ones((CUM_T, CUM_T), np.float32)), dtype=BF16)
    return pl.pallas_call(
        _cumsum_kernel,
        grid=(bsz, t_work // CUM_T),
        in_specs=[pl.BlockSpec((1, CUM_T, LANES), lambda b, j: (b, j, 0)),
                  pl.BlockSpec((CUM_T, CUM_T), lambda b, j: (0, 0))],
        out_specs=pl.BlockSpec((1, 8, CUM_T), lambda b, j: (b, 0, j)),
        out_shape=jax.ShapeDtypeStruct((bsz, 8, t_work), F32),
        scratch_shapes=[pltpu.VMEM((8, LANES), F32)],
        compiler_params=pltpu.CompilerParams(dimension_semantics=("parallel", "arbitrary")),
        name="fox_cumsum",
    )(small, tri)


def _flash_tile(q_m, k_t, v_t, bias, carry, mask):
    m, l, acc = carry
    s = _dot_nt(q_m, k_t) + bias
    if mask is not None:
        s = jnp.where(mask, s, NEG)
    m_new = jnp.maximum(m, jnp.max(s, axis=-1, keepdims=True))
    alpha = jnp.exp(m - m_new)
    p = jnp.exp(s - m_new)
    l = l * alpha + jnp.sum(p, axis=-1, keepdims=True)
    acc = acc * alpha + _dot(p.astype(BF16), v_t)
    return m_new, l, acc


def _fox_kernel(q_ref, k_ref, v_ref, c_ref, o_ref):
    i = pl.program_id(1)
    q0 = pl.multiple_of(i * FOX_TQ, FOX_TQ)
    n_full = (i * FOX_TQ) // FOX_TK
    lane = lax.broadcasted_iota(jnp.int32, (FOX_TQ, LANES), 1)
    row = q0 + lax.broadcasted_iota(jnp.int32, (FOX_TQ, FOX_TK), 0)
    col = lax.broadcasted_iota(jnp.int32, (FOX_TQ, FOX_TK), 1)
    outs = []
    for slab in range(2):
        lanes = slice(slab * LANES, (slab + 1) * LANES)
        q_slab = q_ref[0, :, lanes]
        res = []
        for half in range(2):
            head = 2 * slab + half
            in_head = (lane < HEAD_DIM) if half == 0 else (lane >= HEAD_DIM)
            q_m = jnp.where(in_head, q_slab, jnp.zeros_like(q_slab))
            c_row = c_ref.at[0, N_HEADS + head:N_HEADS + head + 1, :]
            c0 = c_row[:, pl.ds(q0, FOX_TQ)][:, 0:1]

            def step(j, carry, mask=None):
                k0 = pl.multiple_of(j * FOX_TK, FOX_TK)
                bias = c0 - c_row[:, pl.ds(k0, FOX_TK)]
                return _flash_tile(q_m, k_ref[0, pl.ds(k0, FOX_TK), lanes],
                                   v_ref[0, pl.ds(k0, FOX_TK), lanes], bias, carry, mask)

            init = (jnp.full((FOX_TQ, 1), NEG, F32), jnp.zeros((FOX_TQ, 1), F32),
                    jnp.zeros((FOX_TQ, LANES), F32))
            carry = lax.fori_loop(0, n_full, step, init)
            m, l, acc = step(n_full, carry, mask=(n_full * FOX_TK + col) <= row)
            res.append(acc / l)
        outs.append(jnp.where(lane < HEAD_DIM, res[0], res[1]))
    o_ref[0] = jnp.concatenate(outs, axis=1).astype(o_ref.dtype)


def _fox_attention(fq, fk, fv, c_t):
    bsz, t_work, width = fq.shape
    full = pl.BlockSpec((1, t_work, width), lambda b, i: (b, 0, 0))
    return pl.pallas_call(
        _fox_kernel,
        grid=(bsz, t_work // FOX_TQ),
        in_specs=[pl.BlockSpec((1, FOX_TQ, width), lambda b, i: (b, i, 0)), full, full,
                  pl.BlockSpec((1, 8, t_work), lambda b, i: (b, 0, 0))],
        out_specs=pl.BlockSpec((1, FOX_TQ, width), lambda b, i: (b, i, 0)),
        out_shape=jax.ShapeDtypeStruct((bsz, t_work, width), BF16),
        compiler_params=pltpu.CompilerParams(
            dimension_semantics=("parallel", "parallel"), vmem_limit_bytes=VMEM_LIMIT),
        name="fox_attention",
    )(fq, fk, fv, c_t)


GLA_TG = 256
GLA_NSUB = GLA_CHUNK // GLA_SUB


def _gla_kernel(q_ref, k_ref, v_ref, g_ref, la_ref, tri_ref, e_ref, gn_ref, o_ref, st_ref):
    @pl.when(pl.program_id(1) == 0)
    def _():
        st_ref[...] = jnp.zeros_like(st_ref)

    tri = tri_ref[...]
    emat = e_ref[...]
    lane = lax.broadcasted_iota(jnp.int32, (GLA_CHUNK, LANES), 1)
    rowblk = lax.broadcasted_iota(jnp.int32, (GLA_CHUNK, LANES), 0) // GLA_SUB
    tblk = lax.broadcasted_iota(jnp.int32, (GLA_CHUNK, GLA_CHUNK), 0) // GLA_SUB
    sblk = lax.broadcasted_iota(jnp.int32, (GLA_CHUNK, GLA_CHUNK), 1) // GLA_SUB
    trow = lax.broadcasted_iota(jnp.int32, (GLA_SUB, 256), 0)

    def chunk(c, carry):
        r0 = pl.multiple_of(c * GLA_CHUNK, GLA_CHUNK)
        rows = pl.ds(r0, GLA_CHUNK)
        h1, h2, h3 = _split3(la_ref[0, rows, :])
        b = _dot(tri, h1) + _dot(tri, h2) + _dot(tri, h3)
        q = q_ref[0, rows, :]
        k = k_ref[0, rows, :]
        v = v_ref[0, rows, :]
        b_last = b[GLA_CHUNK - 1:GLA_CHUNK, :]
        qd = q * jnp.exp(b)
        kd = (k * jnp.exp(b_last - b)).astype(BF16)
        starts = [jnp.zeros((1, 256), F32)] + [b[GLA_SUB * i - 1:GLA_SUB * i, :]
                                               for i in range(1, GLA_NSUB)]
        bsel = jnp.concatenate([jnp.broadcast_to(s, (GLA_SUB, 256)) for s in starts], axis=0)
        qn = q * jnp.exp(b - bsel)

        diag = []
        for i in range(GLA_NSUB):
            rs = slice(GLA_SUB * i, GLA_SUB * (i + 1))
            b_i, q_i, k_i = b[rs], q[rs], k[rs]
            v_i = v[rs].astype(F32)
            ps = []
            for s in range(GLA_SUB):
                d = jnp.exp(jnp.minimum(b_i - b_i[s:s + 1], 0.0))
                ps.append(jnp.where(trow >= s, q_i * d * k_i[s:s + 1], 0.0).astype(BF16))
            r = _dot(jnp.concatenate(ps, axis=0), emat)
            od = r[0:GLA_SUB] * v_i[0:1]
            for s in range(1, GLA_SUB):
                od = od + r[GLA_SUB * s:GLA_SUB * (s + 1)] * v_i[s:s + 1]
            diag.append(od)
        o_diag = jnp.concatenate(diag, axis=0)

        for slab in range(2):
            ls = slice(slab * LANES, (slab + 1) * LANES)
            qn_s, k_s, b_s = qn[:, ls], k[:, ls], b[:, ls]
            khat = jnp.concatenate(
                [(k_s * jnp.exp(jnp.minimum(starts[i][:, ls] - b_s, 0.0))).astype(BF16)
                 for i in range(1, GLA_NSUB)], axis=1)
            for half in range(2):
                head = 2 * slab + half
                hs = slice(head * GLA_DV, (head + 1) * GLA_DV)
                in_head = (lane < HEAD_DIM) if half == 0 else (lane >= HEAD_DIM)
                qm = jnp.where(in_head, qn_s, 0.0)
                qhat = jnp.concatenate([jnp.where(rowblk == i, qm, 0.0).astype(BF16)
                                        for i in range(1, GLA_NSUB)], axis=1)
                att = jnp.where(sblk < tblk, _dot_nt(qhat, khat), 0.0)
                v_h = v[:, hs]
                st = st_ref[head]
                o = (_dot_nt(jnp.where(in_head, qd[:, ls], 0.0).astype(BF16), st.astype(BF16))
                     + _dot(att.astype(BF16), v_h) + o_diag[:, hs])
                st_ref[head] = st * jnp.exp(b_last[:, ls]) + _dot_tn(v_h, kd[:, ls])
                y = o * lax.rsqrt(jnp.mean(o * o, axis=-1, keepdims=True) + EPS) * gn_ref[:, hs]
                o_ref[0, rows, hs] = (y * _silu(g_ref[0, rows, hs])).astype(o_ref.dtype)
        return carry

    lax.fori_loop(0, GLA_TG // GLA_CHUNK, chunk, 0)


def _gla(gq, gk, gv, gg, la, gain):
    bsz, t_work, _ = gq.shape
    tri = jnp.asarray(np.tril(np.ones((GLA_CHUNK, GLA_CHUNK), np.float32)), dtype=BF16)
    emat = jnp.asarray(
        (np.arange(256)[:, None] // HEAD_DIM == np.arange(512)[None, :] // GLA_DV).astype(np.float32),
        dtype=BF16)
    rows = lambda width: pl.BlockSpec((1, GLA_TG, width), lambda b, j: (b, j, 0))
    const = lambda shape: pl.BlockSpec(shape, lambda b, j: (0,) * len(shape))
    return pl.pallas_call(
        _gla_kernel,
        grid=(bsz, t_work // GLA_TG),
        in_specs=[rows(256), rows(256), rows(512), rows(512), rows(256),
                  const((GLA_CHUNK, GLA_CHUNK)), const((256, 512)), const((1, 512))],
        out_specs=rows(512),
        out_shape=jax.ShapeDtypeStruct((bsz, t_work, 512), BF16),
        scratch_shapes=[pltpu.VMEM((N_HEADS, GLA_DV, LANES), F32)],
        compiler_params=pltpu.CompilerParams(
            dimension_semantics=("parallel", "arbitrary"), vmem_limit_bytes=VMEM_LIMIT),
        name="gla",
    )(gq, gk, gv, gg, la, tri, emat, gain)


FFN_CHUNK = 256


def _ffn_kernel(h_ref, oa_ref, ob_ref, oc_ref, wo_ref, fn_ref, wgu_ref, wd_ref, out_ref):
    h1 = (h_ref[0] + _dot(oa_ref[0], wo_ref[0:256, :]) + _dot(ob_ref[0], wo_ref[256:512, :])
          + _dot(oc_ref[0], wo_ref[512:1024, :]))
    ms = jnp.mean(h1 * h1, axis=-1, keepdims=True)
    f = (h1 * lax.rsqrt(ms + EPS) * fn_ref[...]).astype(BF16)
    out_ref[0] = h1
    for c in range(0, D_FF, FFN_CHUNK):
        gate = _dot(f, wgu_ref[:, c:c + FFN_CHUNK])
        up = _dot(f, wgu_ref[:, D_FF + c:D_FF + c + FFN_CHUNK])
        out_ref[0] += _dot((_silu(gate) * up).astype(BF16), wd_ref[c:c + FFN_CHUNK, :])


def _outproj_ffn(h, oa, ob, oc, wo, fn, wgu, wd):
    bsz, t_work, _ = h.shape
    tm = _row_tile(t_work)
    rows = lambda width: pl.BlockSpec((1, tm, width), lambda b, j: (b, j, 0))
    const = lambda shape: pl.BlockSpec(shape, lambda b, j: (0,) * len(shape),
                                       pipeline_mode=pl.Buffered(1))
    return pl.pallas_call(
        _ffn_kernel,
        grid=(bsz, t_work // tm),
        in_specs=[rows(D_MODEL), rows(256), rows(256), rows(512), const((D_MODEL, D_MODEL)),
                  const((1, D_MODEL)), const((D_MODEL, 2 * D_FF)), const((D_FF, D_MODEL))],
        out_specs=rows(D_MODEL),
        out_shape=jax.ShapeDtypeStruct(h.shape, F32),
        compiler_params=pltpu.CompilerParams(
            dimension_semantics=("parallel", "parallel"), vmem_limit_bytes=VMEM_LIMIT),
        name="outproj_ffn",
    )(h, oa, ob, oc, wo, fn, wgu, wd)


def kernel(x, meta_tokens, attn_norm, w_in, dsa_q_norm, dsa_k_norm, fox_q_norm, fox_k_norm,
           fox_f_bias, gla_gate_w2, gla_gate_b, gla_out_norm, w_out, ffn_norm, w_gate_up, w_down):
    bsz, seq, _ = x.shape
    n_tok = N_META + seq
    t_work = _work_len(n_tok)
    meta = jnp.broadcast_to(meta_tokens[None].astype(x.dtype), (bsz, N_META, D_MODEL))
    h = jnp.concatenate([meta, x, jnp.zeros((bsz, t_work - n_tok, D_MODEL), x.dtype)], axis=1)
    cos_t, sin_t = _rope_tables(t_work)
    g256 = _group_matrix()
    k_top = min(TOPK_MAX, seq // 4)
    for l in range(w_in.shape[0]):
        an, w_perm, vecs, w2_pad, sm_bias = _prep_layer_params(
            l, attn_norm, w_in, dsa_q_norm, dsa_k_norm, fox_q_norm, fox_k_norm, fox_f_bias,
            gla_gate_w2, gla_gate_b)
        (dq, dkk, dvv, iq, ikk, fq, fk, fv, gq, gk, gv, gg, la, small) = _inproj(
            h, an, w_perm, cos_t, sin_t, g256, vecs, w2_pad, sm_bias)
        oa = _dsa_attention(dq, dkk, dvv, iq, ikk, small, k_top)
        ob = _fox_attention(fq, fk, fv, _fox_cumsum(small))
        oc = _gla(gq, gk, gv, gg, la, jnp.tile(gla_out_norm[l], N_HEADS)[None, :])
        h = _outproj_ffn(h, oa, ob, oc, w_out[l].astype(BF16), ffn_norm[l][None, :],
                         w_gate_up[l].astype(BF16), w_down[l].astype(BF16))
    return h[:, N_META:n_tok]
```

```python
import functools

import numpy as np
import jax
import jax.numpy as jnp
from jax import lax
from jax.experimental import pallas as pl
from jax.experimental.pallas import tpu as pltpu

F32 = jnp.float32
BF16 = jnp.bfloat16

D_MODEL = 1024
HEAD_DIM = 64
N_META = 16
ROPE_THETA = 500000.0
ROPE_DIM = HEAD_DIM // 4
ROPE_HALF = ROPE_DIM // 2
NEG = -1e30
EPS = 1e-6

N_HEADS = 4
TOPK_MAX = 256
GLA_DV = 128
GLA_RANK = 16
GLA_TAU = 16.0
GLA_CHUNK = 64
GLA_SUB = 16
D_FF = 2816

LANES = 128
SEQ_ALIGN = 512
VMEM_LIMIT = 56 * 1024 * 1024

_SLABS = (("dq", 256), ("dkk", 128), ("dva", 128), ("iq", 256), ("ikk", 128),
          ("fq", 256), ("fk", 256), ("fva", 512), ("gq", 256), ("gk", 256),
          ("gv", 512), ("gg", 512), ("small", 128))
_SLAB_OFF = {}
_off = 0
for _name, _width in _SLABS:
    _SLAB_OFF[_name] = (_off, _width)
    _off += _width
N_PROJ = _off
SM_IW, SM_FF, SM_GLR = 0, 4, 8


def _work_len(n_tok):
    return -(-n_tok // SEQ_ALIGN) * SEQ_ALIGN


def _row_tile(t_work):
    for cand in (768, 640, 512):
        if t_work % cand == 0:
            return cand
    raise ValueError(f"unsupported working length {t_work}")


def _dot(a, b):
    return jnp.dot(a, b, preferred_element_type=F32)


def _dot_nt(a, b):
    return lax.dot_general(a, b, (((1,), (1,)), ((), ())), preferred_element_type=F32)


def _dot_tn(a, b):
    return lax.dot_general(a, b, (((0,), (0,)), ((), ())), preferred_element_type=F32)


def _split3(x):
    h1 = x.astype(BF16)
    r1 = x - h1.astype(F32)
    h2 = r1.astype(BF16)
    h3 = (r1 - h2.astype(F32)).astype(BF16)
    return h1, h2, h3


def _log_sigmoid(x):
    return jnp.minimum(x, 0.0) - jnp.log1p(jnp.exp(-jnp.abs(x)))


def _silu(x):
    return x / (1.0 + jnp.exp(-x))


def _group_rms(y, gmat, gain):
    yy = y * y
    hi = yy.astype(BF16)
    lo = (yy - hi.astype(F32)).astype(BF16)
    ss = _dot(hi, gmat) + _dot(lo, gmat)
    return y * lax.rsqrt(ss * (1.0 / HEAD_DIM) + EPS) * gain


def _rope(y, cos, sin):
    width = y.shape[-1]
    lane = lax.broadcasted_iota(jnp.int32, y.shape, 1) % HEAD_DIM
    upper = pltpu.roll(y, width - ROPE_HALF, axis=1)
    lower = pltpu.roll(y, ROPE_HALF, axis=1)
    partner = jnp.where(lane < ROPE_HALF, upper, lower)
    return y * cos + partner * sin


def _inproj_kernel(x_ref, an_ref, w_ref, cos_ref, sin_ref, g256_ref, vec_ref, w2_ref, sm_ref,
                   dq_ref, dkk_ref, dva_ref, iq_ref, ikk_ref, fq_ref, fk_ref, fva_ref,
                   gq_ref, gk_ref, gv_ref, gg_ref, la_ref, small_ref):
    x = x_ref[0]
    ms = jnp.mean(x * x, axis=-1, keepdims=True)
    a = (x * lax.rsqrt(ms + EPS) * an_ref[...]).astype(BF16)

    def proj(name):
        off, width = _SLAB_OFF[name]
        return _dot(a, w_ref[:, off:off + width])

    def ones_in_upper_half(y):
        lane = lax.broadcasted_iota(jnp.int32, y.shape, 1) % LANES
        return jnp.where(lane < HEAD_DIM, y, 1.0)

    cos = cos_ref[...]
    sin = sin_ref[...]
    g256 = g256_ref[...]
    g128 = g256[:LANES, :LANES]
    dqn, dkn, fqn, fkn = vec_ref[0:1, :], vec_ref[1:2, :LANES], vec_ref[2:3, :], vec_ref[3:4, :]
    scale = HEAD_DIM ** -0.5

    dq_ref[0] = (_rope(_group_rms(proj("dq"), g256, dqn), cos, sin) * scale).astype(BF16)
    dkk_ref[0] = _rope(_group_rms(proj("dkk"), g128, dkn), cos[:, :LANES], sin[:, :LANES]).astype(BF16)
    dva_ref[0] = ones_in_upper_half(proj("dva")).astype(BF16)
    iq_ref[0] = (_rope(proj("iq"), cos, sin) * scale).astype(BF16)
    ikk_ref[0] = _rope(proj("ikk"), cos[:, :LANES], sin[:, :LANES]).astype(BF16)
    fq_ref[0] = (_group_rms(proj("fq"), g256, fqn) * scale).astype(BF16)
    fk_ref[0] = _group_rms(proj("fk"), g256, fkn).astype(BF16)
    fva_ref[0] = ones_in_upper_half(proj("fva")).astype(BF16)
    gq_ref[0] = proj("gq") * scale
    gk_ref[0] = proj("gk")
    gv_ref[0] = proj("gv").astype(BF16)
    gg_ref[0] = proj("gg")

    small = proj("small")
    lane = lax.broadcasted_iota(jnp.int32, small.shape, 1)
    small_ref[0] = jnp.where(lane < SM_FF, small * (N_HEADS ** -0.5),
                             _log_sigmoid(small + sm_ref[0:1, :]))
    gate = _dot(small.astype(BF16), w2_ref[...]) + vec_ref[4:5, :]
    la_ref[0] = _log_sigmoid(gate) * (1.0 / GLA_TAU)


def _inproj(h, an, w_perm, cos_t, sin_t, g256, vecs, w2_pad, sm_bias):
    bsz, t_work, _ = h.shape
    tm = _row_tile(t_work)
    grid = (bsz, t_work // tm)

    def rows(width, dtype):
        return (jax.ShapeDtypeStruct((bsz, t_work, width), dtype),
                pl.BlockSpec((1, tm, width), lambda b, j: (b, j, 0)))

    outs = [rows(256, BF16), rows(128, BF16), rows(128, BF16), rows(256, BF16), rows(128, BF16),
            rows(256, BF16), rows(256, BF16), rows(512, BF16), rows(256, F32), rows(256, F32),
            rows(512, BF16), rows(512, F32), rows(256, F32), rows(128, F32)]
    const = lambda shape: pl.BlockSpec(shape, lambda b, j: (0,) * len(shape))
    return pl.pallas_call(
        _inproj_kernel,
        grid=grid,
        in_specs=[pl.BlockSpec((1, tm, D_MODEL), lambda b, j: (b, j, 0)),
                  const((1, D_MODEL)), const((D_MODEL, N_PROJ)),
                  pl.BlockSpec((tm, 256), lambda b, j: (j, 0)),
                  pl.BlockSpec((tm, 256), lambda b, j: (j, 0)),
                  const((256, 256)), const((8, 256)), const((LANES, 256)), const((8, LANES))],
        out_specs=[o[1] for o in outs],
        out_shape=[o[0] for o in outs],
        compiler_params=pltpu.CompilerParams(
            dimension_semantics=("parallel", "parallel"), vmem_limit_bytes=VMEM_LIMIT),
        name="inproj",
    )(h, an, w_perm, cos_t, sin_t, g256, vecs, w2_pad, sm_bias)


def _prep_layer_params(l, attn_norm, w_in, dsa_q_norm, dsa_k_norm, fox_q_norm, fox_k_norm,
                       fox_f_bias, gla_gate_w2, gla_gate_b):
    w = w_in[l]
    splits = np.cumsum([256, 64, 64, 256, 4, 64, 256, 256, 256, 4, 256, 256, 512, 512, 16])[:-1]
    (dq, dk, dv, iq, iw, ik, fq, fk, fv, ff, gq, gk, gv, gg, glr) = jnp.split(w, splits, axis=1)
    small = jnp.concatenate(
        [iw, ff, glr, jnp.zeros((D_MODEL, LANES - 4 - 4 - GLA_RANK), w.dtype)], axis=1)
    z64 = jnp.zeros((D_MODEL, HEAD_DIM), w.dtype)
    fva = [part for h in range(N_HEADS) for part in (fv[:, h * HEAD_DIM:(h + 1) * HEAD_DIM], z64)]
    w_perm = jnp.concatenate([dq, dk, dk, dv, z64, iq, ik, ik, fq, fk, *fva, gq, gk, gv, gg, small],
                             axis=1).astype(BF16)
    tile4 = lambda g: jnp.tile(g, N_HEADS)
    vecs = jnp.zeros((8, 256), F32)
    vecs = vecs.at[0].set(tile4(dsa_q_norm[l])).at[1].set(tile4(dsa_k_norm[l]))
    vecs = vecs.at[2].set(tile4(fox_q_norm[l])).at[3].set(tile4(fox_k_norm[l]))
    vecs = vecs.at[4].set(gla_gate_b[l])
    w2_pad = jnp.zeros((LANES, 256), F32).at[SM_GLR:SM_GLR + GLA_RANK].set(gla_gate_w2[l]).astype(BF16)
    sm_bias = jnp.zeros((8, LANES), F32).at[0, SM_FF:SM_FF + N_HEADS].set(fox_f_bias[l])
    return attn_norm[l][None, :], w_perm, vecs, w2_pad, sm_bias


def _rope_tables(t_work):
    inv = jnp.power(ROPE_THETA, -jnp.arange(ROPE_HALF, dtype=F32) * 2.0 / ROPE_DIM)
    ang = jnp.arange(t_work).astype(F32)[:, None] * inv[None, :]
    cos, sin = jnp.cos(ang), jnp.sin(ang)
    rest = HEAD_DIM - ROPE_DIM
    cos64 = jnp.concatenate([cos, cos, jnp.ones((t_work, rest), F32)], axis=1)
    sin64 = jnp.concatenate([-sin, sin, jnp.zeros((t_work, rest), F32)], axis=1)
    return jnp.tile(cos64, (1, N_HEADS)), jnp.tile(sin64, (1, N_HEADS))


def _group_matrix():
    idx = np.arange(256) // HEAD_DIM
    return jnp.asarray((idx[:, None] == idx[None, :]).astype(np.float32), dtype=BF16)


DSA_TQ = 128
DSA_TK = 512
KEY_NEG_INF = -2139095041
KEY_NEG_ZERO = -1
KEY_NEG_MIN_NORMAL = -8388609
SEARCH_FEW = 4.0
MAX_PROBES = 100


def _key_to_f32(key):
    bits = key ^ ((key >> 31) & 0x7FFFFFFF)
    return lax.bitcast_convert_type(bits, F32)


def _f32_to_key(value):
    bits = lax.bitcast_convert_type(value, jnp.int32)
    return bits ^ ((bits >> 31) & 0x7FFFFFFF)


def _head_lane_mask(shape, head):
    lane = lax.broadcasted_iota(jnp.int32, shape, 1)
    return (lane < HEAD_DIM) if head % 2 == 0 else (lane >= HEAD_DIM)


def _masked_heads(slabs):
    return [jnp.where(_head_lane_mask(slabs[h // 2].shape, h), slabs[h // 2],
                      jnp.zeros_like(slabs[h // 2])) for h in range(N_HEADS)]


def _softmax_step(s, m, acc, v_aug):
    m_new = jnp.maximum(m, jnp.max(s, axis=-1, keepdims=True))
    p = jnp.exp(s - m_new)
    return m_new, acc * jnp.exp(m - m_new) + _dot(p.astype(BF16), v_aug)


def _normalise_heads(accs):
    outs = [acc / pltpu.roll(acc, HEAD_DIM, axis=1) for acc in accs]
    lane = lax.broadcasted_iota(jnp.int32, outs[0].shape, 1)
    return [jnp.where(lane < HEAD_DIM, outs[2 * p], pltpu.roll(outs[2 * p + 1], HEAD_DIM, axis=1))
            for p in range(2)]


def _dsa_kernel(k_top, dq_ref, kk_ref, va_ref, iq_ref, ik_ref, sm_ref, tri_ref, o_ref, s_ref):
    i = pl.program_id(1)
    q0 = i * DSA_TQ
    n_tiles = (q0 + DSA_TQ + DSA_TK - 1) // DSA_TK
    last = n_tiles - 1
    row = q0 + lax.broadcasted_iota(jnp.int32, (DSA_TQ, DSA_TK), 0)
    col = lax.broadcasted_iota(jnp.int32, (DSA_TQ, DSA_TK), 1)

    iq_all = jnp.concatenate(
        _masked_heads([iq_ref[0, :, 0:LANES], iq_ref[0, :, LANES:2 * LANES]]), axis=0)
    w_h = [jnp.broadcast_to(sm_ref[0, :, SM_IW + h:SM_IW + h + 1], (DSA_TQ, DSA_TK))
           for h in range(N_HEADS)]

    def score_tile(j, carry):
        k0 = pl.multiple_of(j * DSA_TK, DSA_TK)
        dots = jnp.maximum(_dot_nt(iq_all, ik_ref[0, pl.ds(k0, DSA_TK), :]), 0.0)
        s = w_h[0] * dots[0:DSA_TQ]
        for h in range(1, N_HEADS):
            s = s + w_h[h] * dots[h * DSA_TQ:(h + 1) * DSA_TQ]
        s = jnp.where(k0 + col <= row, s, NEG)
        s_ref[:, pl.ds(k0, DSA_TK)] = s
        for part in range(DSA_TK // LANES):
            carry = jnp.maximum(carry, s[:, part * LANES:(part + 1) * LANES])
        return carry

    row_max = jnp.max(lax.fori_loop(0, n_tiles, score_tile, jnp.full((DSA_TQ, LANES), NEG, F32)),
                      axis=-1, keepdims=True)

    kf = jnp.float32(k_top)

    def count_above(t):
        tb = jnp.broadcast_to(t, (DSA_TQ, LANES))

        def body(j, acc):
            k0 = pl.multiple_of(j * DSA_TK, DSA_TK)
            for part in range(DSA_TK // LANES):
                s = s_ref[:, pl.ds(k0 + part * LANES, LANES)]
                acc = acc + jnp.where(s > tb, 1.0, 0.0)
            return acc

        acc = lax.fori_loop(0, n_tiles, body, jnp.zeros((DSA_TQ, LANES), F32))
        return jnp.sum(acc, axis=-1, keepdims=True)

    def midpoint(lo, hi):
        return (lo >> 1) + (hi >> 1) + (lo & hi & 1)

    def converged(lo, hi):
        return (midpoint(lo, hi) == lo) | ((lo >= KEY_NEG_MIN_NORMAL) & (hi <= 0))

    def search_cond(state):
        it, pending = state[0], state[1]
        return (pending > 0) & (it < MAX_PROBES)

    col1 = lambda value, dtype: jnp.full((DSA_TQ, 1), value, dtype)

    def search_body(state):
        it, _, lo, hi, clo, chi, wlo, whi, side = state
        f_lo, f_hi = _key_to_f32(lo), _key_to_f32(hi)
        g_lo = (clo - (kf - 0.5)) * wlo
        g_hi = ((kf - 0.5) - chi) * whi
        halve = (clo - chi <= SEARCH_FEW) | (col1(it % 4, jnp.int32) == 3)
        guess = _f32_to_key(f_lo + (f_hi - f_lo) * jnp.where(halve, 0.5, g_lo / (g_lo + g_hi)))
        fixed = col1(jnp.where(it == 0, KEY_NEG_ZERO, KEY_NEG_MIN_NORMAL), jnp.int32)
        early = col1(it, jnp.int32) < 2
        probe = jnp.where((guess > lo) & (guess < hi) & ~early, guess, midpoint(lo, hi))
        probe = jnp.where(early & (lo < fixed) & (fixed < hi), fixed, probe)
        probe = jnp.where(converged(lo, hi), lo, probe)
        c = count_above(_key_to_f32(probe))
        live = probe != lo
        up = live & (c >= kf)
        down = live & (c <= kf)
        wlo = jnp.where(down & (side < 0), wlo * 0.5, jnp.where(up, 1.0, wlo))
        whi = jnp.where(up & (side > 0), whi * 0.5, jnp.where(down, 1.0, whi))
        side = jnp.where(up, 1, jnp.where(down, -1, side))
        lo, clo = jnp.where(up, probe, lo), jnp.where(up, c, clo)
        hi, chi = jnp.where(down, probe, hi), jnp.where(down, c, chi)
        pending = jnp.max(jnp.where(converged(lo, hi), 0, 1))
        return it + 1, pending, lo, hi, clo, chi, wlo, whi, side

    n_swept = (n_tiles * DSA_TK).astype(F32)
    state = lax.while_loop(
        search_cond, search_body,
        (jnp.int32(0), jnp.int32(1), col1(KEY_NEG_INF, jnp.int32), _f32_to_key(row_max),
         jnp.broadcast_to(n_swept, (DSA_TQ, 1)), col1(0.0, F32), col1(1.0, F32), col1(1.0, F32),
         col1(0, jnp.int32)))
    hi, chi = state[3], state[5]
    thr = jnp.broadcast_to(_key_to_f32(hi), (DSA_TQ, DSA_TK))
    n_ties = jnp.broadcast_to(kf - chi, (DSA_TQ, DSA_TK))

    q_all = jnp.concatenate(
        _masked_heads([dq_ref[0, :, 0:LANES], dq_ref[0, :, LANES:2 * LANES]]), axis=0)
    tri = tri_ref[...]

    def attend(j, carry, causal):
        seen, stats = carry
        k0 = pl.multiple_of(j * DSA_TK, DSA_TK)
        s = s_ref[:, pl.ds(k0, DSA_TK)]
        tie = s == thr
        rank = _dot(jnp.where(tie, 1.0, 0.0).astype(BF16), tri) + seen
        sel = (s > thr) | (tie & (rank <= n_ties))
        if causal:
            sel = sel & (k0 + col <= row)
        logits = _dot_nt(q_all, kk_ref[0, pl.ds(k0, DSA_TK), :])
        logits = jnp.where(sel[None], logits.reshape(N_HEADS, DSA_TQ, DSA_TK), NEG)
        m, acc = _softmax_step(logits.reshape(N_HEADS * DSA_TQ, DSA_TK), *stats,
                               va_ref[0, pl.ds(k0, DSA_TK), :])
        return rank[:, DSA_TK - 1:DSA_TK], (m, acc)

    init = (jnp.zeros((DSA_TQ, 1), F32),
            (jnp.full((N_HEADS * DSA_TQ, 1), NEG, F32), jnp.zeros((N_HEADS * DSA_TQ, LANES), F32)))
    carry = lax.fori_loop(0, last, functools.partial(attend, causal=False), init)
    _, (_, acc) = attend(last, carry, causal=True)
    o_ref[0] = jnp.concatenate(
        _normalise_heads([acc[h * DSA_TQ:(h + 1) * DSA_TQ] for h in range(N_HEADS)]),
        axis=1).astype(o_ref.dtype)


def _dsa_attention(dq, dkk, dva, iq, ikk, small, k_top):
    bsz, t_work, _ = dq.shape
    tri = jnp.asarray(np.triu(np.ones((DSA_TK, DSA_TK), np.float32)), dtype=BF16)
    tile = lambda width: pl.BlockSpec((1, DSA_TQ, width), lambda b, i: (b, i, 0))
    full = pl.BlockSpec((1, t_work, LANES), lambda b, i: (b, 0, 0))
    return pl.pallas_call(
        functools.partial(_dsa_kernel, k_top),
        grid=(bsz, t_work // DSA_TQ),
        in_specs=[tile(256), full, full, tile(256), full, tile(LANES),
                  pl.BlockSpec((DSA_TK, DSA_TK), lambda b, i: (0, 0))],
        out_specs=tile(256),
        out_shape=jax.ShapeDtypeStruct((bsz, t_work, 256), BF16),
        scratch_shapes=[pltpu.VMEM((DSA_TQ, t_work), F32)],
        compiler_params=pltpu.CompilerParams(
            dimension_semantics=("parallel", "parallel"), vmem_limit_bytes=VMEM_LIMIT),
        name="dsa_attention",
    )(dq, dkk, dva, iq, ikk, small, tri)


FOX_TQ = 256
FOX_TK = 512
CUM_T = 256


def _cumsum_kernel(x_ref, tri_ref, out_ref, carry_ref):
    @pl.when(pl.program_id(1) == 0)
    def _():
        carry_ref[...] = jnp.zeros_like(carry_ref)

    tri = tri_ref[...]
    h1, h2, h3 = _split3(x_ref[0])
    c = _dot(tri, h1) + _dot(tri, h2) + _dot(tri, h3) + carry_ref[0:1, :]
    carry_ref[...] = jnp.broadcast_to(c[CUM_T - 1:CUM_T, :], carry_ref.shape)
    out_ref[0] = c.T[0:8, :]


def _fox_cumsum(small):
    bsz, t_work, _ = small.shape
    tri = jnp.asarray(np.tril(np.ones((CUM_T, CUM_T), np.float32)), dtype=BF16)
    return pl.pallas_call(
        _cumsum_kernel,
        grid=(bsz, t_work // CUM_T),
        in_specs=[pl.BlockSpec((1, CUM_T, LANES), lambda b, j: (b, j, 0)),
                  pl.BlockSpec((CUM_T, CUM_T), lambda b, j: (0, 0))],
        out_specs=pl.BlockSpec((1, 8, CUM_T), lambda b, j: (b, 0, j)),
        out_shape=jax.ShapeDtypeStruct((bsz, 8, t_work), F32),
        scratch_shapes=[pltpu.VMEM((8, LANES), F32)],
        compiler_params=pltpu.CompilerParams(dimension_semantics=("parallel", "arbitrary")),
        name="fox_cumsum",
    )(small, tri)


def _fox_kernel(q_ref, k_ref, v_ref, c_ref, o_ref):
    i = pl.program_id(1)
    q0 = pl.multiple_of(i * FOX_TQ, FOX_TQ)
    n_full = q0 // FOX_TK
    row = q0 + lax.broadcasted_iota(jnp.int32, (FOX_TQ, FOX_TK), 0)
    col = n_full * FOX_TK + lax.broadcasted_iota(jnp.int32, (FOX_TQ, FOX_TK), 1)
    q_h = _masked_heads([q_ref[0, :, 0:LANES], q_ref[0, :, LANES:2 * LANES]])
    c_rows = [c_ref.at[0, N_HEADS + h:N_HEADS + h + 1, :] for h in range(N_HEADS)]
    c0 = [c_rows[h][:, pl.ds(q0, FOX_TQ)][:, 0:1] for h in range(N_HEADS)]

    def step(j, carry, diag):
        k0 = pl.multiple_of(j * FOX_TK, FOX_TK)
        keys = pl.ds(k0, FOX_TK)
        k_slabs = [k_ref[0, keys, 0:LANES], k_ref[0, keys, LANES:2 * LANES]]
        new = []
        for h in range(N_HEADS):
            m, acc = carry[h]
            s = _dot_nt(q_h[h], k_slabs[h // 2]) + (c0[h] - c_rows[h][:, keys])
            if diag:
                s = jnp.where(col <= row, s, NEG)
            new.append(_softmax_step(s, m, acc, v_ref[0, keys, h * LANES:(h + 1) * LANES]))
        return tuple(new)

    init = tuple((jnp.full((FOX_TQ, 1), NEG, F32), jnp.zeros((FOX_TQ, LANES), F32))
                 for _ in range(N_HEADS))
    carry = lax.fori_loop(0, n_full, functools.partial(step, diag=False), init)
    carry = step(n_full, carry, diag=True)
    o_ref[0] = jnp.concatenate(_normalise_heads([acc for _, acc in carry]),
                               axis=1).astype(o_ref.dtype)


def _fox_attention(fq, fk, fva, c_t):
    bsz, t_work, width = fq.shape
    full = lambda w: pl.BlockSpec((1, t_work, w), lambda b, i: (b, 0, 0))
    return pl.pallas_call(
        _fox_kernel,
        grid=(bsz, t_work // FOX_TQ),
        in_specs=[pl.BlockSpec((1, FOX_TQ, width), lambda b, i: (b, i, 0)), full(width),
                  full(N_HEADS * LANES), pl.BlockSpec((1, 8, t_work), lambda b, i: (b, 0, 0))],
        out_specs=pl.BlockSpec((1, FOX_TQ, width), lambda b, i: (b, i, 0)),
        out_shape=jax.ShapeDtypeStruct((bsz, t_work, width), BF16),
        compiler_params=pltpu.CompilerParams(
            dimension_semantics=("parallel", "parallel"), vmem_limit_bytes=VMEM_LIMIT),
        name="fox_attention",
    )(fq, fk, fva, c_t)


GLA_TG = 256
GLA_NSUB = GLA_CHUNK // GLA_SUB


def _gla_kernel(q_ref, k_ref, v_ref, g_ref, la_ref, tri_ref, e_ref, gn_ref, o_ref, st_ref):
    @pl.when(pl.program_id(1) == 0)
    def _():
        st_ref[...] = jnp.zeros_like(st_ref)

    tri = tri_ref[...]
    emat = e_ref[...]
    lane = lax.broadcasted_iota(jnp.int32, (GLA_CHUNK, LANES), 1)
    rowblk = lax.broadcasted_iota(jnp.int32, (GLA_CHUNK, LANES), 0) // GLA_SUB
    tblk = lax.broadcasted_iota(jnp.int32, (GLA_CHUNK, GLA_CHUNK), 0) // GLA_SUB
    sblk = lax.broadcasted_iota(jnp.int32, (GLA_CHUNK, GLA_CHUNK), 1) // GLA_SUB
    trow = lax.broadcasted_iota(jnp.int32, (GLA_SUB, 256), 0)

    def chunk(c, carry):
        r0 = pl.multiple_of(c * GLA_CHUNK, GLA_CHUNK)
        rows = pl.ds(r0, GLA_CHUNK)
        h1, h2, h3 = _split3(la_ref[0, rows, :])
        b = _dot(tri, h1) + _dot(tri, h2) + _dot(tri, h3)
        q = q_ref[0, rows, :]
        k = k_ref[0, rows, :]
        v = v_ref[0, rows, :]
        b_last = b[GLA_CHUNK - 1:GLA_CHUNK, :]
        qd = q * jnp.exp(b)
        kd = (k * jnp.exp(b_last - b)).astype(BF16)
        starts = [jnp.zeros((1, 256), F32)] + [b[GLA_SUB * i - 1:GLA_SUB * i, :]
                                               for i in range(1, GLA_NSUB)]
        bsel = jnp.concatenate([jnp.broadcast_to(s, (GLA_SUB, 256)) for s in starts], axis=0)
        qn = q * jnp.exp(b - bsel)

        diag = []
        for i in range(GLA_NSUB):
            rs = slice(GLA_SUB * i, GLA_SUB * (i + 1))
            b_i, q_i, k_i = b[rs], q[rs], k[rs]
            v_i = v[rs].astype(F32)
            ps = []
            for s in range(GLA_SUB):
                d = jnp.exp(jnp.minimum(b_i - b_i[s:s + 1], 0.0))
                ps.append(jnp.where(trow >= s, q_i * d * k_i[s:s + 1], 0.0).astype(BF16))
            r = _dot(jnp.concatenate(ps, axis=0), emat)
            od = r[0:GLA_SUB] * v_i[0:1]
            for s in range(1, GLA_SUB):
                od = od + r[GLA_SUB * s:GLA_SUB * (s + 1)] * v_i[s:s + 1]
            diag.append(od)
        o_diag = jnp.concatenate(diag, axis=0)

        for slab in range(2):
            ls = slice(slab * LANES, (slab + 1) * LANES)
            qn_s, k_s, b_s = qn[:, ls], k[:, ls], b[:, ls]
            khat = jnp.concatenate(
                [(k_s * jnp.exp(jnp.minimum(starts[i][:, ls] - b_s, 0.0))).astype(BF16)
                 for i in range(1, GLA_NSUB)], axis=1)
            for half in range(2):
                head = 2 * slab + half
                hs = slice(head * GLA_DV, (head + 1) * GLA_DV)
                in_head = (lane < HEAD_DIM) if half == 0 else (lane >= HEAD_DIM)
                qm = jnp.where(in_head, qn_s, 0.0)
                qhat = jnp.concatenate([jnp.where(rowblk == i, qm, 0.0).astype(BF16)
                                        for i in range(1, GLA_NSUB)], axis=1)
                att = jnp.where(sblk < tblk, _dot_nt(qhat, khat), 0.0)
                v_h = v[:, hs]
                st = st_ref[head]
                o = (_dot_nt(jnp.where(in_head, qd[:, ls], 0.0).astype(BF16), st.astype(BF16))
                     + _dot(att.astype(BF16), v_h) + o_diag[:, hs])
                st_ref[head] = st * jnp.exp(b_last[:, ls]) + _dot_tn(v_h, kd[:, ls])
                y = o * lax.rsqrt(jnp.mean(o * o, axis=-1, keepdims=True) + EPS) * gn_ref[:, hs]
                o_ref[0, rows, hs] = (y * _silu(g_ref[0, rows, hs])).astype(o_ref.dtype)
        return carry

    lax.fori_loop(0, GLA_TG // GLA_CHUNK, chunk, 0)


def _gla(gq, gk, gv, gg, la, gain):
    bsz, t_work, _ = gq.shape
    tri = jnp.asarray(np.tril(np.ones((GLA_CHUNK, GLA_CHUNK), np.float32)), dtype=BF16)
    emat = jnp.asarray(
        (np.arange(256)[:, None] // HEAD_DIM == np.arange(512)[None, :] // GLA_DV).astype(np.float32),
        dtype=BF16)
    rows = lambda width: pl.BlockSpec((1, GLA_TG, width), lambda b, j: (b, j, 0))
    const = lambda shape: pl.BlockSpec(shape, lambda b, j: (0,) * len(shape))
    return pl.pallas_call(
        _gla_kernel,
        grid=(bsz, t_work // GLA_TG),
        in_specs=[rows(256), rows(256), rows(512), rows(512), rows(256),
                  const((GLA_CHUNK, GLA_CHUNK)), const((256, 512)), const((1, 512))],
        out_specs=rows(512),
        out_shape=jax.ShapeDtypeStruct((bsz, t_work, 512), BF16),
        scratch_shapes=[pltpu.VMEM((N_HEADS, GLA_DV, LANES), F32)],
        compiler_params=pltpu.CompilerParams(
            dimension_semantics=("parallel", "arbitrary"), vmem_limit_bytes=VMEM_LIMIT),
        name="gla",
    )(gq, gk, gv, gg, la, tri, emat, gain)


FFN_CHUNK = 256


def _ffn_kernel(h_ref, oa_ref, ob_ref, oc_ref, wo_ref, fn_ref, wgu_ref, wd_ref, out_ref):
    h1 = (h_ref[0] + _dot(oa_ref[0], wo_ref[0:256, :]) + _dot(ob_ref[0], wo_ref[256:512, :])
          + _dot(oc_ref[0], wo_ref[512:1024, :]))
    ms = jnp.mean(h1 * h1, axis=-1, keepdims=True)
    f = (h1 * lax.rsqrt(ms + EPS) * fn_ref[...]).astype(BF16)
    out_ref[0] = h1
    for c in range(0, D_FF, FFN_CHUNK):
        gate = _dot(f, wgu_ref[:, c:c + FFN_CHUNK])
        up = _dot(f, wgu_ref[:, D_FF + c:D_FF + c + FFN_CHUNK])
        out_ref[0] += _dot((_silu(gate) * up).astype(BF16), wd_ref[c:c + FFN_CHUNK, :])


def _outproj_ffn(h, oa, ob, oc, wo, fn, wgu, wd):
    bsz, t_work, _ = h.shape
    tm = _row_tile(t_work)
    rows = lambda width: pl.BlockSpec((1, tm, width), lambda b, j: (b, j, 0))
    const = lambda shape: pl.BlockSpec(shape, lambda b, j: (0,) * len(shape),
                                       pipeline_mode=pl.Buffered(1))
    return pl.pallas_call(
        _ffn_kernel,
        grid=(bsz, t_work // tm),
        in_specs=[rows(D_MODEL), rows(256), rows(256), rows(512), const((D_MODEL, D_MODEL)),
                  const((1, D_MODEL)), const((D_MODEL, 2 * D_FF)), const((D_FF, D_MODEL))],
        out_specs=rows(D_MODEL),
        out_shape=jax.ShapeDtypeStruct(h.shape, F32),
        compiler_params=pltpu.CompilerParams(
            dimension_semantics=("parallel", "parallel"), vmem_limit_bytes=VMEM_LIMIT),
        name="outproj_ffn",
    )(h, oa, ob, oc, wo, fn, wgu, wd)


def kernel(x, meta_tokens, attn_norm, w_in, dsa_q_norm, dsa_k_norm, fox_q_norm, fox_k_norm,
           fox_f_bias, gla_gate_w2, gla_gate_b, gla_out_norm, w_out, ffn_norm, w_gate_up, w_down):
    bsz, seq, _ = x.shape
    n_tok = N_META + seq
    t_work = _work_len(n_tok)
    meta = jnp.broadcast_to(meta_tokens[None].astype(x.dtype), (bsz, N_META, D_MODEL))
    h = jnp.concatenate([meta, x, jnp.zeros((bsz, t_work - n_tok, D_MODEL), x.dtype)], axis=1)
    cos_t, sin_t = _rope_tables(t_work)
    g256 = _group_matrix()
    k_top = min(TOPK_MAX, seq // 4)
    for l in range(w_in.shape[0]):
        an, w_perm, vecs, w2_pad, sm_bias = _prep_layer_params(
            l, attn_norm, w_in, dsa_q_norm, dsa_k_norm, fox_q_norm, fox_k_norm, fox_f_bias,
            gla_gate_w2, gla_gate_b)
        (dq, dkk, dva, iq, ikk, fq, fk, fva, gq, gk, gv, gg, la, small) = _inproj(
            h, an, w_perm, cos_t, sin_t, g256, vecs, w2_pad, sm_bias)
        oa = _dsa_attention(dq, dkk, dva, iq, ikk, small, k_top)
        ob = _fox_attention(fq, fk, fva, _fox_cumsum(small))
        oc = _gla(gq, gk, gv, gg, la, jnp.tile(gla_out_norm[l], N_HEADS)[None, :])
        h = _outproj_ffn(h, oa, ob, oc, w_out[l].astype(BF16), ffn_norm[l][None, :],
                         w_gate_up[l].astype(BF16), w_down[l].astype(BF16))
    return h[:, N_META:n_tok]
```

```python
import functools

import numpy as np
import jax
import jax.numpy as jnp
from jax import lax
from jax.experimental import pallas as pl
from jax.experimental.pallas import tpu as pltpu

F32 = jnp.float32
BF16 = jnp.bfloat16

D_MODEL = 1024
HEAD_DIM = 64
N_META = 16
ROPE_THETA = 500000.0
ROPE_DIM = HEAD_DIM // 4
ROPE_HALF = ROPE_DIM // 2
NEG = -1e30
EPS = 1e-6

N_HEADS = 4
TOPK_MAX = 256
GLA_DV = 128
GLA_RANK = 16
GLA_TAU = 16.0
GLA_CHUNK = 64
GLA_SUB = 16
D_FF = 2816

LOG2E = 1.4426950408889634
BIAS_TERMS = 3
LANES = 128
SEQ_ALIGN = 512
VMEM_LIMIT = 56 * 1024 * 1024

_SLABS = (("dq", 256), ("dkk", 128), ("dva", 128), ("iq", 256), ("ikk", 128),
          ("fqa", 512), ("fkp", 512), ("fva", 512), ("gq", 256), ("gk", 256),
          ("gv", 512), ("gg", 512), ("small", 128))
_SLAB_OFF = {}
_off = 0
for _name, _width in _SLABS:
    _SLAB_OFF[_name] = (_off, _width)
    _off += _width
N_PROJ = _off
SM_IW, SM_FF, SM_GLR = 0, 4, 8


def _work_len(n_tok):
    return -(-n_tok // SEQ_ALIGN) * SEQ_ALIGN


def _row_tile(t_work):
    for cand in (768, 640, 512):
        if t_work % cand == 0:
            return cand
    raise ValueError(f"unsupported working length {t_work}")


def _dot(a, b):
    return jnp.dot(a, b, preferred_element_type=F32)


def _dot_nt(a, b):
    return lax.dot_general(a, b, (((1,), (1,)), ((), ())), preferred_element_type=F32)


def _dot_tn(a, b):
    return lax.dot_general(a, b, (((0,), (0,)), ((), ())), preferred_element_type=F32)


def _split3(x):
    h1 = x.astype(BF16)
    r1 = x - h1.astype(F32)
    h2 = r1.astype(BF16)
    h3 = (r1 - h2.astype(F32)).astype(BF16)
    return h1, h2, h3


def _log_sigmoid(x):
    return jnp.minimum(x, 0.0) - jnp.log1p(jnp.exp(-jnp.abs(x)))


def _silu(x):
    return x / (1.0 + jnp.exp(-x))


def _group_rms(y, gmat, gain):
    yy = y * y
    hi = yy.astype(BF16)
    lo = (yy - hi.astype(F32)).astype(BF16)
    ss = _dot(hi, gmat) + _dot(lo, gmat)
    return y * lax.rsqrt(ss * (1.0 / HEAD_DIM) + EPS) * gain


def _rope(y, cos, sin):
    width = y.shape[-1]
    lane = lax.broadcasted_iota(jnp.int32, y.shape, 1) % HEAD_DIM
    upper = pltpu.roll(y, width - ROPE_HALF, axis=1)
    lower = pltpu.roll(y, ROPE_HALF, axis=1)
    partner = jnp.where(lane < ROPE_HALF, upper, lower)
    return y * cos + partner * sin


def _inproj_kernel(x_ref, an_ref, w_ref, cos_ref, sin_ref, g256_ref, vec_ref, w2_ref, sm_ref,
                   dq_ref, dkk_ref, dva_ref, iq_ref, ikk_ref, fq_ref, fk_ref, fva_ref,
                   gq_ref, gk_ref, gv_ref, gg_ref, la_ref, small_ref):
    x = x_ref[0]
    ms = jnp.mean(x * x, axis=-1, keepdims=True)
    a = (x * lax.rsqrt(ms + EPS) * an_ref[...]).astype(BF16)

    def proj(name):
        off, width = _SLAB_OFF[name]
        return _dot(a, w_ref[:, off:off + width])

    def ones_in_upper_half(y):
        lane = lax.broadcasted_iota(jnp.int32, y.shape, 1) % LANES
        return jnp.where(lane < HEAD_DIM, y, 1.0)

    cos = cos_ref[...]
    sin = sin_ref[...]
    g256 = g256_ref[...]
    g128 = g256[:LANES, :LANES]
    dqn, dkn, fqn, fkn = vec_ref[0:1, :], vec_ref[1:2, :LANES], vec_ref[2:3, :], vec_ref[3:4, :]
    scale = HEAD_DIM ** -0.5

    dq_ref[0] = (_rope(_group_rms(proj("dq"), g256, dqn), cos, sin) * (scale * LOG2E)).astype(BF16)
    dkk_ref[0] = _rope(_group_rms(proj("dkk"), g128, dkn), cos[:, :LANES], sin[:, :LANES]).astype(BF16)
    dva_ref[0] = ones_in_upper_half(proj("dva")).astype(BF16)
    iq_ref[0] = (_rope(proj("iq"), cos, sin) * scale).astype(BF16)
    ikk_ref[0] = _rope(proj("ikk"), cos[:, :LANES], sin[:, :LANES]).astype(BF16)
    fq, fk = proj("fqa"), proj("fkp")
    lane = lax.broadcasted_iota(jnp.int32, (fq.shape[0], LANES), 1)
    bias_lanes = (lane >= HEAD_DIM) & (lane < HEAD_DIM + BIAS_TERMS)
    for h in range(N_HEADS):
        hs = slice(h * LANES, (h + 1) * LANES)
        q_h = _group_rms(fq[:, hs], g128, fqn[:, :LANES]) * (scale * LOG2E)
        fq_ref[0, :, hs] = jnp.where(bias_lanes, 1.0, q_h).astype(BF16)
        fk_ref[0, :, hs] = _group_rms(fk[:, hs], g128, fkn[:, :LANES]).astype(BF16)
    fva_ref[0] = ones_in_upper_half(proj("fva")).astype(BF16)
    gq_ref[0] = proj("gq") * scale
    gk_ref[0] = proj("gk")
    gv_ref[0] = proj("gv").astype(BF16)
    gg_ref[0] = proj("gg")

    small = proj("small")
    lane = lax.broadcasted_iota(jnp.int32, small.shape, 1)
    small_ref[0] = jnp.where(lane < SM_FF, small * (N_HEADS ** -0.5),
                             _log_sigmoid(small + sm_ref[0:1, :]))
    gate = _dot(small.astype(BF16), w2_ref[...]) + vec_ref[4:5, :]
    la_ref[0] = _log_sigmoid(gate) * (1.0 / GLA_TAU)


def _inproj(h, an, w_perm, cos_t, sin_t, g256, vecs, w2_pad, sm_bias):
    bsz, t_work, _ = h.shape
    tm = _row_tile(t_work)
    grid = (bsz, t_work // tm)

    def rows(width, dtype):
        return (jax.ShapeDtypeStruct((bsz, t_work, width), dtype),
                pl.BlockSpec((1, tm, width), lambda b, j: (b, j, 0)))

    outs = [rows(256, BF16), rows(128, BF16), rows(128, BF16), rows(256, BF16), rows(128, BF16),
            rows(512, BF16), rows(512, BF16), rows(512, BF16), rows(256, F32), rows(256, F32),
            rows(512, BF16), rows(512, F32), rows(256, F32), rows(128, F32)]
    const = lambda shape: pl.BlockSpec(shape, lambda b, j: (0,) * len(shape))
    return pl.pallas_call(
        _inproj_kernel,
        grid=grid,
        in_specs=[pl.BlockSpec((1, tm, D_MODEL), lambda b, j: (b, j, 0)),
                  const((1, D_MODEL)), const((D_MODEL, N_PROJ)),
                  pl.BlockSpec((tm, 256), lambda b, j: (j, 0)),
                  pl.BlockSpec((tm, 256), lambda b, j: (j, 0)),
                  const((256, 256)), const((8, 256)), const((LANES, 256)), const((8, LANES))],
        out_specs=[o[1] for o in outs],
        out_shape=[o[0] for o in outs],
        compiler_params=pltpu.CompilerParams(
            dimension_semantics=("parallel", "parallel"), vmem_limit_bytes=VMEM_LIMIT),
        name="inproj",
    )(h, an, w_perm, cos_t, sin_t, g256, vecs, w2_pad, sm_bias)


def _prep_layer_params(l, attn_norm, w_in, dsa_q_norm, dsa_k_norm, fox_q_norm, fox_k_norm,
                       fox_f_bias, gla_gate_w2, gla_gate_b):
    w = w_in[l]
    splits = np.cumsum([256, 64, 64, 256, 4, 64, 256, 256, 256, 4, 256, 256, 512, 512, 16])[:-1]
    (dq, dk, dv, iq, iw, ik, fq, fk, fv, ff, gq, gk, gv, gg, glr) = jnp.split(w, splits, axis=1)
    small = jnp.concatenate(
        [iw, ff, glr, jnp.zeros((D_MODEL, LANES - 4 - 4 - GLA_RANK), w.dtype)], axis=1)
    z64 = jnp.zeros((D_MODEL, HEAD_DIM), w.dtype)
    per_head = lambda t: [part for h in range(N_HEADS)
                          for part in (t[:, h * HEAD_DIM:(h + 1) * HEAD_DIM], z64)]
    w_perm = jnp.concatenate([dq, dk, dk, dv, z64, iq, ik, ik, *per_head(fq), *per_head(fk),
                              *per_head(fv), gq, gk, gv, gg, small], axis=1).astype(BF16)
    tile4 = lambda g: jnp.tile(g, N_HEADS)
    vecs = jnp.zeros((8, 256), F32)
    vecs = vecs.at[0].set(tile4(dsa_q_norm[l])).at[1].set(tile4(dsa_k_norm[l]))
    vecs = vecs.at[2].set(tile4(fox_q_norm[l])).at[3].set(tile4(fox_k_norm[l]))
    vecs = vecs.at[4].set(gla_gate_b[l])
    w2_pad = jnp.zeros((LANES, 256), F32).at[SM_GLR:SM_GLR + GLA_RANK].set(gla_gate_w2[l]).astype(BF16)
    sm_bias = jnp.zeros((8, LANES), F32).at[0, SM_FF:SM_FF + N_HEADS].set(fox_f_bias[l])
    return attn_norm[l][None, :], w_perm, vecs, w2_pad, sm_bias


def _rope_tables(t_work):
    inv = jnp.power(ROPE_THETA, -jnp.arange(ROPE_HALF, dtype=F32) * 2.0 / ROPE_DIM)
    ang = jnp.arange(t_work).astype(F32)[:, None] * inv[None, :]
    cos, sin = jnp.cos(ang), jnp.sin(ang)
    rest = HEAD_DIM - ROPE_DIM
    cos64 = jnp.concatenate([cos, cos, jnp.ones((t_work, rest), F32)], axis=1)
    sin64 = jnp.concatenate([-sin, sin, jnp.zeros((t_work, rest), F32)], axis=1)
    return jnp.tile(cos64, (1, N_HEADS)), jnp.tile(sin64, (1, N_HEADS))


def _group_matrix():
    idx = np.arange(256) // HEAD_DIM
    return jnp.asarray((idx[:, None] == idx[None, :]).astype(np.float32), dtype=BF16)


DSA_TQ = 256
DSA_TK = 512
KEY_NEG_INF = -2139095041
KEY_NEG_ZERO = -1
KEY_NEG_MIN_NORMAL = -8388609
SEARCH_FEW = 4.0
MAX_PROBES = 136
COUNT_ROWS = 128


def _key_to_f32(key):
    bits = key ^ ((key >> 31) & 0x7FFFFFFF)
    return lax.bitcast_convert_type(bits, F32)


def _f32_to_key(value):
    bits = lax.bitcast_convert_type(value, jnp.int32)
    return bits ^ ((bits >> 31) & 0x7FFFFFFF)


def _head_lane_mask(shape, head):
    lane = lax.broadcasted_iota(jnp.int32, shape, 1)
    return (lane < HEAD_DIM) if head % 2 == 0 else (lane >= HEAD_DIM)


def _masked_heads(slabs):
    return [jnp.where(_head_lane_mask(slabs[h // 2].shape, h), slabs[h // 2],
                      jnp.zeros_like(slabs[h // 2])) for h in range(N_HEADS)]


def _softmax_step(s, m, acc, v_aug):
    m_new = jnp.maximum(m, jnp.max(s, axis=-1, keepdims=True))
    p = jnp.exp2(s - m_new)
    return m_new, acc * jnp.exp2(m - m_new) + _dot(p.astype(BF16), v_aug)


def _normalise_heads(accs):
    outs = [acc / pltpu.roll(acc, HEAD_DIM, axis=1) for acc in accs]
    lane = lax.broadcasted_iota(jnp.int32, outs[0].shape, 1)
    return [jnp.where(lane < HEAD_DIM, outs[2 * p], pltpu.roll(outs[2 * p + 1], HEAD_DIM, axis=1))
            for p in range(2)]


def _dsa_kernel(k_top, dq_ref, kk_ref, va_ref, iq_ref, ik_ref, wt_ref, tri_ref, o_ref, s_ref):
    i = pl.program_id(1)
    q0 = i * DSA_TQ
    n_tiles = (q0 + DSA_TQ + DSA_TK - 1) // DSA_TK
    last = n_tiles - 1
    key = lax.broadcasted_iota(jnp.int32, (DSA_TK, DSA_TQ), 0)
    qry = q0 + lax.broadcasted_iota(jnp.int32, (DSA_TK, DSA_TQ), 1)
    head = lambda x, h: x[:, h * DSA_TQ:(h + 1) * DSA_TQ]

    iq_all = jnp.concatenate(
        _masked_heads([iq_ref[0, :, 0:LANES], iq_ref[0, :, LANES:2 * LANES]]), axis=0)
    w_h = [wt_ref[0, SM_IW + h:SM_IW + h + 1, :] for h in range(N_HEADS)]

    def score_tile(j, carry):
        k0 = pl.multiple_of(j * DSA_TK, DSA_TK)
        dots = jnp.maximum(_dot_nt(ik_ref[0, pl.ds(k0, DSA_TK), :], iq_all), 0.0)
        s = w_h[0] * head(dots, 0)
        for h in range(1, N_HEADS):
            s = s + w_h[h] * head(dots, h)
        s = jnp.where(k0 + key <= qry, s, NEG)
        s_ref[pl.ds(k0, DSA_TK), :] = s
        return jnp.maximum(carry, jnp.max(s, axis=0, keepdims=True))

    row_max = lax.fori_loop(0, n_tiles, score_tile, jnp.full((1, DSA_TQ), NEG, F32))

    kf = jnp.float32(k_top)

    def count_above(t):
        tb = jnp.broadcast_to(t, (COUNT_ROWS, DSA_TQ))

        def body(j, acc):
            k0 = pl.multiple_of(j * DSA_TK, DSA_TK)
            for part in range(DSA_TK // COUNT_ROWS):
                s = s_ref[pl.ds(k0 + part * COUNT_ROWS, COUNT_ROWS), :]
                acc = acc + jnp.where(s > tb, 1.0, 0.0)
            return acc

        acc = lax.fori_loop(0, n_tiles, body, jnp.zeros((COUNT_ROWS, DSA_TQ), F32))
        return jnp.sum(acc, axis=0, keepdims=True)

    def midpoint(lo, hi):
        return (lo >> 1) + (hi >> 1) + (lo & hi & 1)

    def converged(lo, hi):
        return (midpoint(lo, hi) == lo) | ((lo >= KEY_NEG_MIN_NORMAL) & (hi <= 0))

    def search_cond(state):
        it, pending = state[0], state[1]
        return (pending > 0) & (it < MAX_PROBES)

    col1 = lambda value, dtype: jnp.full((1, DSA_TQ), value, dtype)

    def search_body(state):
        it, _, lo, hi, clo, chi, wlo, whi, side = state
        f_lo, f_hi = _key_to_f32(lo), _key_to_f32(hi)
        g_lo = (clo - (kf - 0.5)) * wlo
        g_hi = ((kf - 0.5) - chi) * whi
        frac = jnp.where(clo - chi <= SEARCH_FEW, 0.5, g_lo / (g_lo + g_hi))
        guess = _f32_to_key(f_lo + (f_hi - f_lo) * frac)
        fixed = col1(jnp.where(it == 0, KEY_NEG_ZERO, KEY_NEG_MIN_NORMAL), jnp.int32)
        early = col1(it, jnp.int32) < 2
        guided = col1(it % 4, jnp.int32) != 3
        probe = jnp.where((guess > lo) & (guess < hi) & guided & ~early, guess, midpoint(lo, hi))
        probe = jnp.where(early & (lo < fixed) & (fixed < hi), fixed, probe)
        probe = jnp.where(converged(lo, hi), lo, probe)
        c = count_above(_key_to_f32(probe))
        live = probe != lo
        up = live & (c >= kf)
        down = live & (c <= kf)
        wlo = jnp.where(down & (side < 0), wlo * 0.5, jnp.where(up, 1.0, wlo))
        whi = jnp.where(up & (side > 0), whi * 0.5, jnp.where(down, 1.0, whi))
        side = jnp.where(up, 1, jnp.where(down, -1, side))
        lo, clo = jnp.where(up, probe, lo), jnp.where(up, c, clo)
        hi, chi = jnp.where(down, probe, hi), jnp.where(down, c, chi)
        pending = jnp.max(jnp.where(converged(lo, hi), 0, 1))
        return it + 1, pending, lo, hi, clo, chi, wlo, whi, side

    n_swept = (n_tiles * DSA_TK).astype(F32)
    state = lax.while_loop(
        search_cond, search_body,
        (jnp.int32(0), jnp.int32(1), col1(KEY_NEG_INF, jnp.int32), _f32_to_key(row_max),
         jnp.broadcast_to(n_swept, (1, DSA_TQ)), col1(0.0, F32), col1(1.0, F32), col1(1.0, F32),
         col1(0, jnp.int32)))
    thr = _key_to_f32(state[3])
    n_ties = kf - state[5]

    to_column = lambda r: jnp.broadcast_to(r, (8, DSA_TQ)).T[:, 0:1]
    thr_c, ties_c = to_column(thr), to_column(n_ties)
    q_all = jnp.concatenate(
        _masked_heads([dq_ref[0, :, 0:LANES], dq_ref[0, :, LANES:2 * LANES]]), axis=0)
    tri = tri_ref[...]
    qrow = q0 + lax.broadcasted_iota(jnp.int32, (DSA_TQ, DSA_TK), 0)
    kcol = lax.broadcasted_iota(jnp.int32, (DSA_TQ, DSA_TK), 1)
    n_blocks = DSA_TK // LANES

    def attend(j, carry, causal):
        seen, m, acc = carry
        k0 = pl.multiple_of(j * DSA_TK, DSA_TK)
        s = s_ref[pl.ds(k0, DSA_TK), :].T
        tie = s == thr_c
        tie_b = jnp.where(tie, 1.0, 0.0).astype(BF16)
        local = [_dot(tie_b[:, b * LANES:(b + 1) * LANES], tri) for b in range(n_blocks)]
        ranks = []
        for b in range(n_blocks):
            ranks.append(local[b] + seen)
            seen = seen + local[b][:, LANES - 1:LANES]
        sel = (s > thr_c) | (tie & (jnp.concatenate(ranks, axis=1) <= ties_c))
        if causal:
            sel = sel & (k0 + kcol <= qrow)
        logits = _dot_nt(q_all, kk_ref[0, pl.ds(k0, DSA_TK), :])
        logits = jnp.where(sel[None], logits.reshape(N_HEADS, DSA_TQ, DSA_TK), NEG)
        m, acc = _softmax_step(logits.reshape(N_HEADS * DSA_TQ, DSA_TK), m, acc,
                               va_ref[0, pl.ds(k0, DSA_TK), :])
        return seen, m, acc

    init = (jnp.zeros((DSA_TQ, 1), F32), jnp.full((N_HEADS * DSA_TQ, 1), NEG, F32),
            jnp.zeros((N_HEADS * DSA_TQ, LANES), F32))
    carry = lax.fori_loop(0, last, functools.partial(attend, causal=False), init)
    _, _, acc = attend(last, carry, causal=True)
    o_ref[0] = jnp.concatenate(
        _normalise_heads([acc[h * DSA_TQ:(h + 1) * DSA_TQ] for h in range(N_HEADS)]),
        axis=1).astype(o_ref.dtype)


def _dsa_attention(dq, dkk, dva, iq, ikk, small_t, k_top):
    bsz, t_work, _ = dq.shape
    tri = jnp.asarray(np.triu(np.ones((LANES, LANES), np.float32)), dtype=BF16)
    tile = lambda width: pl.BlockSpec((1, DSA_TQ, width), lambda b, i: (b, i, 0))
    full = pl.BlockSpec((1, t_work, LANES), lambda b, i: (b, 0, 0), pipeline_mode=pl.Buffered(1))
    return pl.pallas_call(
        functools.partial(_dsa_kernel, k_top),
        grid=(bsz, t_work // DSA_TQ),
        in_specs=[tile(256), full, full, tile(256), full,
                  pl.BlockSpec((1, 8, DSA_TQ), lambda b, i: (b, 0, i)),
                  pl.BlockSpec((LANES, LANES), lambda b, i: (0, 0))],
        out_specs=tile(256),
        out_shape=jax.ShapeDtypeStruct((bsz, t_work, 256), BF16),
        scratch_shapes=[pltpu.VMEM((t_work, DSA_TQ), F32)],
        compiler_params=pltpu.CompilerParams(
            dimension_semantics=("parallel", "parallel"), vmem_limit_bytes=VMEM_LIMIT),
        name="dsa_attention",
    )(dq, dkk, dva, iq, ikk, small_t, tri)


FOX_TQ = 512
FOX_TK = 512
CUM_T = 256


def _fox_prep_kernel(x_ref, tri_ref, place_ref, k_ref, xt_ref, kb_ref, carry_ref):
    @pl.when(pl.program_id(1) == 0)
    def _():
        carry_ref[...] = jnp.zeros_like(carry_ref)

    tri = tri_ref[...]
    x = x_ref[0]
    h1, h2, h3 = _split3(x)
    c = _dot(tri, h1) + _dot(tri, h2) + _dot(tri, h3) + carry_ref[0:1, :]
    carry_ref[...] = jnp.broadcast_to(c[CUM_T - 1:CUM_T, :], carry_ref.shape)
    xt_ref[0] = x.T[0:8, :]
    terms = _split3(c * -LOG2E)
    bias = sum(_dot(terms[t], place_ref[t]) for t in range(BIAS_TERMS))
    lane = lax.broadcasted_iota(jnp.int32, bias.shape, 1) % LANES
    kb_ref[0] = jnp.where((lane >= HEAD_DIM) & (lane < HEAD_DIM + BIAS_TERMS),
                          bias.astype(BF16), k_ref[0])


def _fox_prep(small, fk):
    bsz, t_work, width = fk.shape
    tri = jnp.asarray(np.tril(np.ones((CUM_T, CUM_T), np.float32)), dtype=BF16)
    place = np.zeros((BIAS_TERMS, LANES, width), np.float32)
    for t in range(BIAS_TERMS):
        for h in range(N_HEADS):
            place[t, SM_FF + h, h * LANES + HEAD_DIM + t] = 1.0
    rows = lambda w: pl.BlockSpec((1, CUM_T, w), lambda b, j: (b, j, 0))
    return pl.pallas_call(
        _fox_prep_kernel,
        grid=(bsz, t_work // CUM_T),
        in_specs=[rows(LANES), pl.BlockSpec((CUM_T, CUM_T), lambda b, j: (0, 0)),
                  pl.BlockSpec((BIAS_TERMS, LANES, width), lambda b, j: (0, 0, 0)), rows(width)],
        out_specs=[pl.BlockSpec((1, 8, CUM_T), lambda b, j: (b, 0, j)), rows(width)],
        out_shape=[jax.ShapeDtypeStruct((bsz, 8, t_work), F32),
                   jax.ShapeDtypeStruct(fk.shape, BF16)],
        scratch_shapes=[pltpu.VMEM((8, LANES), F32)],
        compiler_params=pltpu.CompilerParams(dimension_semantics=("parallel", "arbitrary")),
        name="fox_prep",
    )(small, tri, jnp.asarray(place, dtype=BF16), fk)


def _fox_kernel(q_ref, k_ref, v_ref, o_ref):
    i = pl.program_id(1)
    q0 = pl.multiple_of(i * FOX_TQ, FOX_TQ)
    n_full = q0 // FOX_TK
    row = q0 + lax.broadcasted_iota(jnp.int32, (FOX_TQ, FOX_TK), 0)
    col = n_full * FOX_TK + lax.broadcasted_iota(jnp.int32, (FOX_TQ, FOX_TK), 1)

    def step(j, carry, diag):
        k0 = pl.multiple_of(j * FOX_TK, FOX_TK)
        keys = pl.ds(k0, FOX_TK)
        new = []
        for h in range(N_HEADS):
            hs = slice(h * LANES, (h + 1) * LANES)
            m, acc = carry[h]
            s = _dot_nt(q_ref[0, :, hs], k_ref[0, keys, hs])
            if diag:
                s = jnp.where(col <= row, s, NEG)
            new.append(_softmax_step(s, m, acc, v_ref[0, keys, hs]))
        return tuple(new)

    init = tuple((jnp.full((FOX_TQ, 1), NEG, F32), jnp.zeros((FOX_TQ, LANES), F32))
                 for _ in range(N_HEADS))
    carry = lax.fori_loop(0, n_full, functools.partial(step, diag=False), init)
    carry = step(n_full, carry, diag=True)
    o_ref[0] = jnp.concatenate(_normalise_heads([acc for _, acc in carry]),
                               axis=1).astype(o_ref.dtype)


def _fox_attention(fqa, fkb, fva):
    bsz, t_work, width = fqa.shape
    full = pl.BlockSpec((1, t_work, width), lambda b, i: (b, 0, 0), pipeline_mode=pl.Buffered(1))
    return pl.pallas_call(
        _fox_kernel,
        grid=(bsz, t_work // FOX_TQ),
        in_specs=[pl.BlockSpec((1, FOX_TQ, width), lambda b, i: (b, i, 0)), full, full],
        out_specs=pl.BlockSpec((1, FOX_TQ, 2 * LANES), lambda b, i: (b, i, 0)),
        out_shape=jax.ShapeDtypeStruct((bsz, t_work, 2 * LANES), BF16),
        compiler_params=pltpu.CompilerParams(
            dimension_semantics=("parallel", "parallel"), vmem_limit_bytes=VMEM_LIMIT),
        name="fox_attention",
    )(fqa, fkb, fva)


GLA_TG = 256
GLA_NSUB = GLA_CHUNK // GLA_SUB


def _gla_kernel(q_ref, k_ref, v_ref, g_ref, la_ref, tri_ref, e_ref, gn_ref, o_ref, st_ref):
    @pl.when(pl.program_id(1) == 0)
    def _():
        st_ref[...] = jnp.zeros_like(st_ref)

    tri = tri_ref[...]
    emat = e_ref[...]
    lane = lax.broadcasted_iota(jnp.int32, (GLA_CHUNK, LANES), 1)
    rowblk = lax.broadcasted_iota(jnp.int32, (GLA_CHUNK, LANES), 0) // GLA_SUB
    tblk = lax.broadcasted_iota(jnp.int32, (GLA_CHUNK, GLA_CHUNK), 0) // GLA_SUB
    sblk = lax.broadcasted_iota(jnp.int32, (GLA_CHUNK, GLA_CHUNK), 1) // GLA_SUB
    trow = lax.broadcasted_iota(jnp.int32, (GLA_SUB, 256), 0)

    def chunk(c, carry):
        r0 = pl.multiple_of(c * GLA_CHUNK, GLA_CHUNK)
        rows = pl.ds(r0, GLA_CHUNK)
        h1, h2, h3 = _split3(la_ref[0, rows, :])
        b = _dot(tri, h1) + _dot(tri, h2) + _dot(tri, h3)
        q = q_ref[0, rows, :]
        k = k_ref[0, rows, :]
        v = v_ref[0, rows, :]
        b_last = b[GLA_CHUNK - 1:GLA_CHUNK, :]
        qd = q * jnp.exp(b)
        kd = (k * jnp.exp(b_last - b)).astype(BF16)
        starts = [jnp.zeros((1, 256), F32)] + [b[GLA_SUB * i - 1:GLA_SUB * i, :]
                                               for i in range(1, GLA_NSUB)]
        bsel = jnp.concatenate([jnp.broadcast_to(s, (GLA_SUB, 256)) for s in starts], axis=0)
        qn = q * jnp.exp(b - bsel)

        diag = []
        for i in range(GLA_NSUB):
            rs = slice(GLA_SUB * i, GLA_SUB * (i + 1))
            b_i, q_i, k_i = b[rs], q[rs], k[rs]
            v_i = v[rs].astype(F32)
            ps = []
            for s in range(GLA_SUB):
                d = jnp.exp(jnp.minimum(b_i - b_i[s:s + 1], 0.0))
                ps.append(jnp.where(trow >= s, q_i * d * k_i[s:s + 1], 0.0).astype(BF16))
            r = _dot(jnp.concatenate(ps, axis=0), emat)
            od = r[0:GLA_SUB] * v_i[0:1]
            for s in range(1, GLA_SUB):
                od = od + r[GLA_SUB * s:GLA_SUB * (s + 1)] * v_i[s:s + 1]
            diag.append(od)
        o_diag = jnp.concatenate(diag, axis=0)

        for slab in range(2):
            ls = slice(slab * LANES, (slab + 1) * LANES)
            qn_s, k_s, b_s = qn[:, ls], k[:, ls], b[:, ls]
            khat = jnp.concatenate(
                [(k_s * jnp.exp(jnp.minimum(starts[i][:, ls] - b_s, 0.0))).astype(BF16)
                 for i in range(1, GLA_NSUB)], axis=1)
            for half in range(2):
                head = 2 * slab + half
                hs = slice(head * GLA_DV, (head + 1) * GLA_DV)
                in_head = (lane < HEAD_DIM) if half == 0 else (lane >= HEAD_DIM)
                qm = jnp.where(in_head, qn_s, 0.0)
                qhat = jnp.concatenate([jnp.where(rowblk == i, qm, 0.0).astype(BF16)
                                        for i in range(1, GLA_NSUB)], axis=1)
                att = jnp.where(sblk < tblk, _dot_nt(qhat, khat), 0.0)
                v_h = v[:, hs]
                st = st_ref[head]
                o = (_dot_nt(jnp.where(in_head, qd[:, ls], 0.0).astype(BF16), st.astype(BF16))
                     + _dot(att.astype(BF16), v_h) + o_diag[:, hs])
                st_ref[head] = st * jnp.exp(b_last[:, ls]) + _dot_tn(v_h, kd[:, ls])
                y = o * lax.rsqrt(jnp.mean(o * o, axis=-1, keepdims=True) + EPS) * gn_ref[:, hs]
                o_ref[0, rows, hs] = (y * _silu(g_ref[0, rows, hs])).astype(o_ref.dtype)
        return carry

    lax.fori_loop(0, GLA_TG // GLA_CHUNK, chunk, 0)


def _gla(gq, gk, gv, gg, la, gain):
    bsz, t_work, _ = gq.shape
    tri = jnp.asarray(np.tril(np.ones((GLA_CHUNK, GLA_CHUNK), np.float32)), dtype=BF16)
    emat = jnp.asarray(
        (np.arange(256)[:, None] // HEAD_DIM == np.arange(512)[None, :] // GLA_DV).astype(np.float32),
        dtype=BF16)
    rows = lambda width: pl.BlockSpec((1, GLA_TG, width), lambda b, j: (b, j, 0))
    const = lambda shape: pl.BlockSpec(shape, lambda b, j: (0,) * len(shape))
    return pl.pallas_call(
        _gla_kernel,
        grid=(bsz, t_work // GLA_TG),
        in_specs=[rows(256), rows(256), rows(512), rows(512), rows(256),
                  const((GLA_CHUNK, GLA_CHUNK)), const((256, 512)), const((1, 512))],
        out_specs=rows(512),
        out_shape=jax.ShapeDtypeStruct((bsz, t_work, 512), BF16),
        scratch_shapes=[pltpu.VMEM((N_HEADS, GLA_DV, LANES), F32)],
        compiler_params=pltpu.CompilerParams(
            dimension_semantics=("parallel", "arbitrary"), vmem_limit_bytes=VMEM_LIMIT),
        name="gla",
    )(gq, gk, gv, gg, la, tri, emat, gain)


FFN_CHUNK = 256


def _ffn_kernel(h_ref, oa_ref, ob_ref, oc_ref, wo_ref, fn_ref, wgu_ref, wd_ref, out_ref):
    h1 = (h_ref[0] + _dot(oa_ref[0], wo_ref[0:256, :]) + _dot(ob_ref[0], wo_ref[256:512, :])
          + _dot(oc_ref[0], wo_ref[512:1024, :]))
    ms = jnp.mean(h1 * h1, axis=-1, keepdims=True)
    f = (h1 * lax.rsqrt(ms + EPS) * fn_ref[...]).astype(BF16)
    out_ref[0] = h1
    for c in range(0, D_FF, FFN_CHUNK):
        gate = _dot(f, wgu_ref[:, c:c + FFN_CHUNK])
        up = _dot(f, wgu_ref[:, D_FF + c:D_FF + c + FFN_CHUNK])
        out_ref[0] += _dot((_silu(gate) * up).astype(BF16), wd_ref[c:c + FFN_CHUNK, :])


def _outproj_ffn(h, oa, ob, oc, wo, fn, wgu, wd):
    bsz, t_work, _ = h.shape
    tm = _row_tile(t_work)
    rows = lambda width: pl.BlockSpec((1, tm, width), lambda b, j: (b, j, 0))
    const = lambda shape: pl.BlockSpec(shape, lambda b, j: (0,) * len(shape),
                                       pipeline_mode=pl.Buffered(1))
    return pl.pallas_call(
        _ffn_kernel,
        grid=(bsz, t_work // tm),
        in_specs=[rows(D_MODEL), rows(256), rows(256), rows(512), const((D_MODEL, D_MODEL)),
                  const((1, D_MODEL)), const((D_MODEL, 2 * D_FF)), const((D_FF, D_MODEL))],
        out_specs=rows(D_MODEL),
        out_shape=jax.ShapeDtypeStruct(h.shape, F32),
        compiler_params=pltpu.CompilerParams(
            dimension_semantics=("parallel", "parallel"), vmem_limit_bytes=VMEM_LIMIT),
        name="outproj_ffn",
    )(h, oa, ob, oc, wo, fn, wgu, wd)


def kernel(x, meta_tokens, attn_norm, w_in, dsa_q_norm, dsa_k_norm, fox_q_norm, fox_k_norm,
           fox_f_bias, gla_gate_w2, gla_gate_b, gla_out_norm, w_out, ffn_norm, w_gate_up, w_down):
    bsz, seq, _ = x.shape
    n_tok = N_META + seq
    t_work = _work_len(n_tok)
    meta = jnp.broadcast_to(meta_tokens[None].astype(x.dtype), (bsz, N_META, D_MODEL))
    h = jnp.concatenate([meta, x, jnp.zeros((bsz, t_work - n_tok, D_MODEL), x.dtype)], axis=1)
    cos_t, sin_t = _rope_tables(t_work)
    g256 = _group_matrix()
    k_top = min(TOPK_MAX, seq // 4)
    for l in range(w_in.shape[0]):
        an, w_perm, vecs, w2_pad, sm_bias = _prep_layer_params(
            l, attn_norm, w_in, dsa_q_norm, dsa_k_norm, fox_q_norm, fox_k_norm, fox_f_bias,
            gla_gate_w2, gla_gate_b)
        (dq, dkk, dva, iq, ikk, fq, fk, fva, gq, gk, gv, gg, la, small) = _inproj(
            h, an, w_perm, cos_t, sin_t, g256, vecs, w2_pad, sm_bias)
        small_t, fkb = _fox_prep(small, fk)
        oa = _dsa_attention(dq, dkk, dva, iq, ikk, small_t, k_top)
        ob = _fox_attention(fq, fkb, fva)
        oc = _gla(gq, gk, gv, gg, la, jnp.tile(gla_out_norm[l], N_HEADS)[None, :])
        h = _outproj_ffn(h, oa, ob, oc, w_out[l].astype(BF16), ffn_norm[l][None, :],
                         w_gate_up[l].astype(BF16), w_down[l].astype(BF16))
    return h[:, N_META:n_tok]
```

```python
import functools

import numpy as np
import jax
import jax.numpy as jnp
from jax import lax
from jax.experimental import pallas as pl
from jax.experimental.pallas import tpu as pltpu

F32 = jnp.float32
BF16 = jnp.bfloat16

D_MODEL = 1024
HEAD_DIM = 64
N_META = 16
ROPE_THETA = 500000.0
ROPE_DIM = HEAD_DIM // 4
ROPE_HALF = ROPE_DIM // 2
NEG = -1e30
EPS = 1e-6

N_HEADS = 4
TOPK_MAX = 256
GLA_DV = 128
GLA_RANK = 16
GLA_TAU = 16.0
GLA_CHUNK = 64
GLA_SUB = 16
D_FF = 2816

LOG2E = 1.4426950408889634
BIAS_TERMS = 3
LANES = 128
SEQ_ALIGN = 512
VMEM_LIMIT = 56 * 1024 * 1024

_SLABS = (("dq", 256), ("dkk", 128), ("dva", 128), ("iq", 256), ("ikk", 128),
          ("fqa", 512), ("fkp", 512), ("fva", 512), ("gq", 256), ("gk", 256),
          ("gv", 512), ("gg", 512), ("small", 128))
_SLAB_OFF = {}
_off = 0
for _name, _width in _SLABS:
    _SLAB_OFF[_name] = (_off, _width)
    _off += _width
N_PROJ = _off
SM_IW, SM_FF, SM_GLR = 0, 4, 8


def _work_len(n_tok):
    return -(-n_tok // SEQ_ALIGN) * SEQ_ALIGN


def _row_tile(t_work):
    for cand in (768, 640, 512):
        if t_work % cand == 0:
            return cand
    raise ValueError(f"unsupported working length {t_work}")


def _dot(a, b):
    return jnp.dot(a, b, preferred_element_type=F32)


def _dot_nt(a, b):
    return lax.dot_general(a, b, (((1,), (1,)), ((), ())), preferred_element_type=F32)


def _dot_tn(a, b):
    return lax.dot_general(a, b, (((0,), (0,)), ((), ())), preferred_element_type=F32)


def _split3(x):
    h1 = x.astype(BF16)
    r1 = x - h1.astype(F32)
    h2 = r1.astype(BF16)
    h3 = (r1 - h2.astype(F32)).astype(BF16)
    return h1, h2, h3


def _log_sigmoid(x):
    return jnp.minimum(x, 0.0) - jnp.log1p(jnp.exp(-jnp.abs(x)))


def _silu(x):
    return x / (1.0 + jnp.exp(-x))


def _group_rms(y, gmat, gain):
    yy = y * y
    hi = yy.astype(BF16)
    lo = (yy - hi.astype(F32)).astype(BF16)
    ss = _dot(hi, gmat) + _dot(lo, gmat)
    return y * lax.rsqrt(ss * (1.0 / HEAD_DIM) + EPS) * gain


def _rope(y, cos, sin):
    width = y.shape[-1]
    lane = lax.broadcasted_iota(jnp.int32, y.shape, 1) % HEAD_DIM
    upper = pltpu.roll(y, width - ROPE_HALF, axis=1)
    lower = pltpu.roll(y, ROPE_HALF, axis=1)
    partner = jnp.where(lane < ROPE_HALF, upper, lower)
    return y * cos + partner * sin


def _inproj_kernel(x_ref, an_ref, w_ref, cos_ref, sin_ref, g256_ref, vec_ref, w2_ref, sm_ref,
                   dq_ref, dkk_ref, dva_ref, iq_ref, ikk_ref, fq_ref, fk_ref, fva_ref,
                   gq_ref, gk_ref, gv_ref, gg_ref, la_ref, small_ref):
    x = x_ref[0]
    ms = jnp.mean(x * x, axis=-1, keepdims=True)
    a = (x * lax.rsqrt(ms + EPS) * an_ref[...]).astype(BF16)

    def proj(name):
        off, width = _SLAB_OFF[name]
        return _dot(a, w_ref[:, off:off + width])

    def ones_in_upper_half(y):
        lane = lax.broadcasted_iota(jnp.int32, y.shape, 1) % LANES
        return jnp.where(lane < HEAD_DIM, y, 1.0)

    cos = cos_ref[...]
    sin = sin_ref[...]
    g256 = g256_ref[...]
    g128 = g256[:LANES, :LANES]
    dqn, dkn, fqn, fkn = vec_ref[0:1, :], vec_ref[1:2, :LANES], vec_ref[2:3, :], vec_ref[3:4, :]
    scale = HEAD_DIM ** -0.5

    dq_ref[0] = (_rope(_group_rms(proj("dq"), g256, dqn), cos, sin) * (scale * LOG2E)).astype(BF16)
    dkk_ref[0] = _rope(_group_rms(proj("dkk"), g128, dkn), cos[:, :LANES], sin[:, :LANES]).astype(BF16)
    dva_ref[0] = ones_in_upper_half(proj("dva")).astype(BF16)
    iq_ref[0] = (_rope(proj("iq"), cos, sin) * scale).astype(BF16)
    ikk_ref[0] = _rope(proj("ikk"), cos[:, :LANES], sin[:, :LANES]).astype(BF16)
    fq, fk = proj("fqa"), proj("fkp")
    lane = lax.broadcasted_iota(jnp.int32, (fq.shape[0], LANES), 1)
    bias_lanes = (lane >= HEAD_DIM) & (lane < HEAD_DIM + BIAS_TERMS)
    for h in range(N_HEADS):
        hs = slice(h * LANES, (h + 1) * LANES)
        q_h = _group_rms(fq[:, hs], g128, fqn[:, :LANES]) * (scale * LOG2E)
        fq_ref[0, :, hs] = jnp.where(bias_lanes, 1.0, q_h).astype(BF16)
        fk_ref[0, :, hs] = _group_rms(fk[:, hs], g128, fkn[:, :LANES]).astype(BF16)
    fva_ref[0] = ones_in_upper_half(proj("fva")).astype(BF16)
    gq_ref[0] = proj("gq") * scale
    gk_ref[0] = proj("gk")
    gv_ref[0] = proj("gv").astype(BF16)
    gg_ref[0] = proj("gg")

    small = proj("small")
    lane = lax.broadcasted_iota(jnp.int32, small.shape, 1)
    small_ref[0] = jnp.where(lane < SM_FF, small * (N_HEADS ** -0.5),
                             _log_sigmoid(small + sm_ref[0:1, :]))
    gate = _dot(small.astype(BF16), w2_ref[...]) + vec_ref[4:5, :]
    la_ref[0] = _log_sigmoid(gate) * (1.0 / GLA_TAU)


def _inproj(h, an, w_perm, cos_t, sin_t, g256, vecs, w2_pad, sm_bias):
    bsz, t_work, _ = h.shape
    tm = _row_tile(t_work)
    grid = (bsz, t_work // tm)

    def rows(width, dtype):
        return (jax.ShapeDtypeStruct((bsz, t_work, width), dtype),
                pl.BlockSpec((1, tm, width), lambda b, j: (b, j, 0)))

    outs = [rows(256, BF16), rows(128, BF16), rows(128, BF16), rows(256, BF16), rows(128, BF16),
            rows(512, BF16), rows(512, BF16), rows(512, BF16), rows(256, F32), rows(256, F32),
            rows(512, BF16), rows(512, F32), rows(256, F32), rows(128, F32)]
    const = lambda shape: pl.BlockSpec(shape, lambda b, j: (0,) * len(shape))
    return pl.pallas_call(
        _inproj_kernel,
        grid=grid,
        in_specs=[pl.BlockSpec((1, tm, D_MODEL), lambda b, j: (b, j, 0)),
                  const((1, D_MODEL)), const((D_MODEL, N_PROJ)),
                  pl.BlockSpec((tm, 256), lambda b, j: (j, 0)),
                  pl.BlockSpec((tm, 256), lambda b, j: (j, 0)),
                  const((256, 256)), const((8, 256)), const((LANES, 256)), const((8, LANES))],
        out_specs=[o[1] for o in outs],
        out_shape=[o[0] for o in outs],
        compiler_params=pltpu.CompilerParams(
            dimension_semantics=("parallel", "parallel"), vmem_limit_bytes=VMEM_LIMIT),
        name="inproj",
    )(h, an, w_perm, cos_t, sin_t, g256, vecs, w2_pad, sm_bias)


def _prep_layer_params(l, attn_norm, w_in, dsa_q_norm, dsa_k_norm, fox_q_norm, fox_k_norm,
                       fox_f_bias, gla_gate_w2, gla_gate_b):
    w = w_in[l]
    splits = np.cumsum([256, 64, 64, 256, 4, 64, 256, 256, 256, 4, 256, 256, 512, 512, 16])[:-1]
    (dq, dk, dv, iq, iw, ik, fq, fk, fv, ff, gq, gk, gv, gg, glr) = jnp.split(w, splits, axis=1)
    small = jnp.concatenate(
        [iw, ff, glr, jnp.zeros((D_MODEL, LANES - 4 - 4 - GLA_RANK), w.dtype)], axis=1)
    z64 = jnp.zeros((D_MODEL, HEAD_DIM), w.dtype)
    per_head = lambda t: [part for h in range(N_HEADS)
                          for part in (t[:, h * HEAD_DIM:(h + 1) * HEAD_DIM], z64)]
    w_perm = jnp.concatenate([dq, dk, dk, dv, z64, iq, ik, ik, *per_head(fq), *per_head(fk),
                              *per_head(fv), gq, gk, gv, gg, small], axis=1).astype(BF16)
    tile4 = lambda g: jnp.tile(g, N_HEADS)
    vecs = jnp.zeros((8, 256), F32)
    vecs = vecs.at[0].set(tile4(dsa_q_norm[l])).at[1].set(tile4(dsa_k_norm[l]))
    vecs = vecs.at[2].set(tile4(fox_q_norm[l])).at[3].set(tile4(fox_k_norm[l]))
    vecs = vecs.at[4].set(gla_gate_b[l])
    w2_pad = jnp.zeros((LANES, 256), F32).at[SM_GLR:SM_GLR + GLA_RANK].set(gla_gate_w2[l]).astype(BF16)
    sm_bias = jnp.zeros((8, LANES), F32).at[0, SM_FF:SM_FF + N_HEADS].set(fox_f_bias[l])
    return attn_norm[l][None, :], w_perm, vecs, w2_pad, sm_bias


def _rope_tables(t_work):
    inv = jnp.power(ROPE_THETA, -jnp.arange(ROPE_HALF, dtype=F32) * 2.0 / ROPE_DIM)
    ang = jnp.arange(t_work).astype(F32)[:, None] * inv[None, :]
    cos, sin = jnp.cos(ang), jnp.sin(ang)
    rest = HEAD_DIM - ROPE_DIM
    cos64 = jnp.concatenate([cos, cos, jnp.ones((t_work, rest), F32)], axis=1)
    sin64 = jnp.concatenate([-sin, sin, jnp.zeros((t_work, rest), F32)], axis=1)
    return jnp.tile(cos64, (1, N_HEADS)), jnp.tile(sin64, (1, N_HEADS))


def _group_matrix():
    idx = np.arange(256) // HEAD_DIM
    return jnp.asarray((idx[:, None] == idx[None, :]).astype(np.float32), dtype=BF16)


DSA_TQ = 256
DSA_TK = 512
KEY_NEG_INF = -2139095041
KEY_NEG_ZERO = -1
KEY_NEG_MIN_NORMAL = -8388609
SEARCH_FEW = 4.0
GUIDED_PROBES = 64
MAX_PROBES = GUIDED_PROBES + 40
COUNT_ROWS = 128


def _key_to_f32(key):
    bits = key ^ ((key >> 31) & 0x7FFFFFFF)
    return lax.bitcast_convert_type(bits, F32)


def _f32_to_key(value):
    bits = lax.bitcast_convert_type(value, jnp.int32)
    return bits ^ ((bits >> 31) & 0x7FFFFFFF)


def _head_lane_mask(shape, head):
    lane = lax.broadcasted_iota(jnp.int32, shape, 1)
    return (lane < HEAD_DIM) if head % 2 == 0 else (lane >= HEAD_DIM)


def _masked_heads(slabs):
    return [jnp.where(_head_lane_mask(slabs[h // 2].shape, h), slabs[h // 2],
                      jnp.zeros_like(slabs[h // 2])) for h in range(N_HEADS)]


def _softmax_step(s, m, acc, v_aug):
    m_new = jnp.maximum(m, jnp.max(s, axis=-1, keepdims=True))
    p = jnp.exp2(s - m_new)
    return m_new, acc * jnp.exp2(m - m_new) + _dot(p.astype(BF16), v_aug)


def _normalise_heads(accs):
    outs = [acc / pltpu.roll(acc, HEAD_DIM, axis=1) for acc in accs]
    lane = lax.broadcasted_iota(jnp.int32, outs[0].shape, 1)
    return [jnp.where(lane < HEAD_DIM, outs[2 * p], pltpu.roll(outs[2 * p + 1], HEAD_DIM, axis=1))
            for p in range(2)]


def _dsa_kernel(k_top, dq_ref, kk_ref, va_ref, iq_ref, ik_ref, wt_ref, tri_ref, o_ref, s_ref):
    i = pl.program_id(1)
    q0 = i * DSA_TQ
    n_tiles = (q0 + DSA_TQ + DSA_TK - 1) // DSA_TK
    last = n_tiles - 1
    key = lax.broadcasted_iota(jnp.int32, (DSA_TK, DSA_TQ), 0)
    qry = q0 + lax.broadcasted_iota(jnp.int32, (DSA_TK, DSA_TQ), 1)
    head = lambda x, h: x[:, h * DSA_TQ:(h + 1) * DSA_TQ]

    iq_all = jnp.concatenate(
        _masked_heads([iq_ref[0, :, 0:LANES], iq_ref[0, :, LANES:2 * LANES]]), axis=0)
    w_h = [wt_ref[0, SM_IW + h:SM_IW + h + 1, :] for h in range(N_HEADS)]

    def score_tile(j, carry):
        k0 = pl.multiple_of(j * DSA_TK, DSA_TK)
        dots = jnp.maximum(_dot_nt(ik_ref[0, pl.ds(k0, DSA_TK), :], iq_all), 0.0)
        s = w_h[0] * head(dots, 0)
        for h in range(1, N_HEADS):
            s = s + w_h[h] * head(dots, h)
        s = jnp.where(k0 + key <= qry, s, NEG)
        s_ref[pl.ds(k0, DSA_TK), :] = s
        return jnp.maximum(carry, jnp.max(s, axis=0, keepdims=True))

    row_max = lax.fori_loop(0, n_tiles, score_tile, jnp.full((1, DSA_TQ), NEG, F32))

    kf = jnp.float32(k_top)

    def count_above(t):
        tb = jnp.broadcast_to(t, (COUNT_ROWS, DSA_TQ))

        def body(j, acc):
            k0 = pl.multiple_of(j * DSA_TK, DSA_TK)
            for part in range(DSA_TK // COUNT_ROWS):
                s = s_ref[pl.ds(k0 + part * COUNT_ROWS, COUNT_ROWS), :]
                acc = acc + jnp.where(s > tb, 1.0, 0.0)
            return acc

        acc = lax.fori_loop(0, n_tiles, body, jnp.zeros((COUNT_ROWS, DSA_TQ), F32))
        return jnp.sum(acc, axis=0, keepdims=True)

    def midpoint(lo, hi):
        return (lo >> 1) + (hi >> 1) + (lo & hi & 1)

    def converged(lo, hi):
        return (midpoint(lo, hi) == lo) | ((lo >= KEY_NEG_MIN_NORMAL) & (hi <= 0))

    def search_cond(state):
        it, pending = state[0], state[1]
        return (pending > 0) & (it < MAX_PROBES)

    col1 = lambda value, dtype: jnp.full((1, DSA_TQ), value, dtype)

    def search_body(state):
        it, _, lo, hi, clo, chi, wlo, whi, side = state
        f_lo, f_hi = _key_to_f32(lo), _key_to_f32(hi)
        g_lo = (clo - (kf - 0.5)) * wlo
        g_hi = ((kf - 0.5) - chi) * whi
        halve = (clo - chi <= SEARCH_FEW) | (col1(it % 4, jnp.int32) == 3)
        guess = _f32_to_key(f_lo + (f_hi - f_lo) * jnp.where(halve, 0.5, g_lo / (g_lo + g_hi)))
        fixed = col1(jnp.where(it == 0, KEY_NEG_ZERO, KEY_NEG_MIN_NORMAL), jnp.int32)
        early = col1(it, jnp.int32) < 2
        guided = col1(it, jnp.int32) < GUIDED_PROBES
        probe = jnp.where((guess > lo) & (guess < hi) & guided & ~early, guess, midpoint(lo, hi))
        probe = jnp.where(early & (lo < fixed) & (fixed < hi), fixed, probe)
        probe = jnp.where(converged(lo, hi), lo, probe)
        c = count_above(_key_to_f32(probe))
        live = probe != lo
        up = live & (c >= kf)
        down = live & (c <= kf)
        wlo = jnp.where(down & (side < 0), wlo * 0.5, jnp.where(up, 1.0, wlo))
        whi = jnp.where(up & (side > 0), whi * 0.5, jnp.where(down, 1.0, whi))
        side = jnp.where(up, 1, jnp.where(down, -1, side))
        lo, clo = jnp.where(up, probe, lo), jnp.where(up, c, clo)
        hi, chi = jnp.where(down, probe, hi), jnp.where(down, c, chi)
        pending = jnp.max(jnp.where(converged(lo, hi), 0, 1))
        return it + 1, pending, lo, hi, clo, chi, wlo, whi, side

    n_swept = (n_tiles * DSA_TK).astype(F32)
    state = lax.while_loop(
        search_cond, search_body,
        (jnp.int32(0), jnp.int32(1), col1(KEY_NEG_INF, jnp.int32), _f32_to_key(row_max),
         jnp.broadcast_to(n_swept, (1, DSA_TQ)), col1(0.0, F32), col1(1.0, F32), col1(1.0, F32),
         col1(0, jnp.int32)))
    thr = _key_to_f32(state[3])
    n_ties = kf - state[5]

    to_column = lambda r: jnp.broadcast_to(r, (8, DSA_TQ)).T[:, 0:1]
    thr_c, ties_c = to_column(thr), to_column(n_ties)
    q_all = jnp.concatenate(
        _masked_heads([dq_ref[0, :, 0:LANES], dq_ref[0, :, LANES:2 * LANES]]), axis=0)
    tri = tri_ref[...]
    qrow = q0 + lax.broadcasted_iota(jnp.int32, (DSA_TQ, DSA_TK), 0)
    kcol = lax.broadcasted_iota(jnp.int32, (DSA_TQ, DSA_TK), 1)
    n_blocks = DSA_TK // LANES

    def attend(j, carry, causal):
        seen, m, acc = carry
        k0 = pl.multiple_of(j * DSA_TK, DSA_TK)
        s = s_ref[pl.ds(k0, DSA_TK), :].T
        tie = s == thr_c
        tie_b = jnp.where(tie, 1.0, 0.0).astype(BF16)
        local = [_dot(tie_b[:, b * LANES:(b + 1) * LANES], tri) for b in range(n_blocks)]
        ranks = []
        for b in range(n_blocks):
            ranks.append(local[b] + seen)
            seen = seen + local[b][:, LANES - 1:LANES]
        sel = (s > thr_c) | (tie & (jnp.concatenate(ranks, axis=1) <= ties_c))
        if causal:
            sel = sel & (k0 + kcol <= qrow)
        logits = _dot_nt(q_all, kk_ref[0, pl.ds(k0, DSA_TK), :])
        logits = jnp.where(sel[None], logits.reshape(N_HEADS, DSA_TQ, DSA_TK), NEG)
        m, acc = _softmax_step(logits.reshape(N_HEADS * DSA_TQ, DSA_TK), m, acc,
                               va_ref[0, pl.ds(k0, DSA_TK), :])
        return seen, m, acc

    init = (jnp.zeros((DSA_TQ, 1), F32), jnp.full((N_HEADS * DSA_TQ, 1), NEG, F32),
            jnp.zeros((N_HEADS * DSA_TQ, LANES), F32))
    carry = lax.fori_loop(0, last, functools.partial(attend, causal=False), init)
    _, _, acc = attend(last, carry, causal=True)
    o_ref[0] = jnp.concatenate(
        _normalise_heads([acc[h * DSA_TQ:(h + 1) * DSA_TQ] for h in range(N_HEADS)]),
        axis=1).astype(o_ref.dtype)


def _dsa_attention(dq, dkk, dva, iq, ikk, small_t, k_top):
    bsz, t_work, _ = dq.shape
    tri = jnp.asarray(np.triu(np.ones((LANES, LANES), np.float32)), dtype=BF16)
    tile = lambda width: pl.BlockSpec((1, DSA_TQ, width), lambda b, i: (b, i, 0))
    full = pl.BlockSpec((1, t_work, LANES), lambda b, i: (b, 0, 0), pipeline_mode=pl.Buffered(1))
    return pl.pallas_call(
        functools.partial(_dsa_kernel, k_top),
        grid=(bsz, t_work // DSA_TQ),
        in_specs=[tile(256), full, full, tile(256), full,
                  pl.BlockSpec((1, 8, DSA_TQ), lambda b, i: (b, 0, i)),
                  pl.BlockSpec((LANES, LANES), lambda b, i: (0, 0))],
        out_specs=tile(256),
        out_shape=jax.ShapeDtypeStruct((bsz, t_work, 256), BF16),
        scratch_shapes=[pltpu.VMEM((t_work, DSA_TQ), F32)],
        compiler_params=pltpu.CompilerParams(
            dimension_semantics=("parallel", "parallel"), vmem_limit_bytes=VMEM_LIMIT),
        name="dsa_attention",
    )(dq, dkk, dva, iq, ikk, small_t, tri)


FOX_TQ = 512
FOX_TK = 512
CUM_T = 256


def _fox_prep_kernel(x_ref, tri_ref, place_ref, k_ref, xt_ref, kb_ref, carry_ref):
    @pl.when(pl.program_id(1) == 0)
    def _():
        carry_ref[...] = jnp.zeros_like(carry_ref)

    tri = tri_ref[...]
    x = x_ref[0]
    h1, h2, h3 = _split3(x)
    c = _dot(tri, h1) + _dot(tri, h2) + _dot(tri, h3) + carry_ref[0:1, :]
    carry_ref[...] = jnp.broadcast_to(c[CUM_T - 1:CUM_T, :], carry_ref.shape)
    xt_ref[0] = x.T[0:8, :]
    terms = _split3(c * -LOG2E)
    bias = sum(_dot(terms[t], place_ref[t]) for t in range(BIAS_TERMS))
    lane = lax.broadcasted_iota(jnp.int32, bias.shape, 1) % LANES
    kb_ref[0] = jnp.where((lane >= HEAD_DIM) & (lane < HEAD_DIM + BIAS_TERMS),
                          bias.astype(BF16), k_ref[0])


def _fox_prep(small, fk):
    bsz, t_work, width = fk.shape
    tri = jnp.asarray(np.tril(np.ones((CUM_T, CUM_T), np.float32)), dtype=BF16)
    place = np.zeros((BIAS_TERMS, LANES, width), np.float32)
    for t in range(BIAS_TERMS):
        for h in range(N_HEADS):
            place[t, SM_FF + h, h * LANES + HEAD_DIM + t] = 1.0
    rows = lambda w: pl.BlockSpec((1, CUM_T, w), lambda b, j: (b, j, 0))
    return pl.pallas_call(
        _fox_prep_kernel,
        grid=(bsz, t_work // CUM_T),
        in_specs=[rows(LANES), pl.BlockSpec((CUM_T, CUM_T), lambda b, j: (0, 0)),
                  pl.BlockSpec((BIAS_TERMS, LANES, width), lambda b, j: (0, 0, 0)), rows(width)],
        out_specs=[pl.BlockSpec((1, 8, CUM_T), lambda b, j: (b, 0, j)), rows(width)],
        out_shape=[jax.ShapeDtypeStruct((bsz, 8, t_work), F32),
                   jax.ShapeDtypeStruct(fk.shape, BF16)],
        scratch_shapes=[pltpu.VMEM((8, LANES), F32)],
        compiler_params=pltpu.CompilerParams(dimension_semantics=("parallel", "arbitrary")),
        name="fox_prep",
    )(small, tri, jnp.asarray(place, dtype=BF16), fk)


def _fox_kernel(q_ref, k_ref, v_ref, o_ref):
    i = pl.program_id(1)
    q0 = pl.multiple_of(i * FOX_TQ, FOX_TQ)
    n_full = q0 // FOX_TK
    row = q0 + lax.broadcasted_iota(jnp.int32, (FOX_TQ, FOX_TK), 0)
    col = n_full * FOX_TK + lax.broadcasted_iota(jnp.int32, (FOX_TQ, FOX_TK), 1)

    def step(j, carry, diag):
        k0 = pl.multiple_of(j * FOX_TK, FOX_TK)
        keys = pl.ds(k0, FOX_TK)
        new = []
        for h in range(N_HEADS):
            hs = slice(h * LANES, (h + 1) * LANES)
            m, acc = carry[h]
            s = _dot_nt(q_ref[0, :, hs], k_ref[0, keys, hs])
            if diag:
                s = jnp.where(col <= row, s, NEG)
            new.append(_softmax_step(s, m, acc, v_ref[0, keys, hs]))
        return tuple(new)

    init = tuple((jnp.full((FOX_TQ, 1), NEG, F32), jnp.zeros((FOX_TQ, LANES), F32))
                 for _ in range(N_HEADS))
    carry = lax.fori_loop(0, n_full, functools.partial(step, diag=False), init)
    carry = step(n_full, carry, diag=True)
    o_ref[0] = jnp.concatenate(_normalise_heads([acc for _, acc in carry]),
                               axis=1).astype(o_ref.dtype)


def _fox_attention(fqa, fkb, fva):
    bsz, t_work, width = fqa.shape
    full = pl.BlockSpec((1, t_work, width), lambda b, i: (b, 0, 0), pipeline_mode=pl.Buffered(1))
    return pl.pallas_call(
        _fox_kernel,
        grid=(bsz, t_work // FOX_TQ),
        in_specs=[pl.BlockSpec((1, FOX_TQ, width), lambda b, i: (b, i, 0)), full, full],
        out_specs=pl.BlockSpec((1, FOX_TQ, 2 * LANES), lambda b, i: (b, i, 0)),
        out_shape=jax.ShapeDtypeStruct((bsz, t_work, 2 * LANES), BF16),
        compiler_params=pltpu.CompilerParams(
            dimension_semantics=("parallel", "parallel"), vmem_limit_bytes=VMEM_LIMIT),
        name="fox_attention",
    )(fqa, fkb, fva)


GLA_TG = 256
GLA_NSUB = GLA_CHUNK // GLA_SUB


def _gla_kernel(q_ref, k_ref, v_ref, g_ref, la_ref, tri_ref, e_ref, gn_ref, o_ref, st_ref):
    @pl.when(pl.program_id(1) == 0)
    def _():
        st_ref[...] = jnp.zeros_like(st_ref)

    tri = tri_ref[...]
    emat = e_ref[...]
    lane = lax.broadcasted_iota(jnp.int32, (GLA_CHUNK, LANES), 1)
    rowblk = lax.broadcasted_iota(jnp.int32, (GLA_CHUNK, LANES), 0) // GLA_SUB
    tblk = lax.broadcasted_iota(jnp.int32, (GLA_CHUNK, GLA_CHUNK), 0) // GLA_SUB
    sblk = lax.broadcasted_iota(jnp.int32, (GLA_CHUNK, GLA_CHUNK), 1) // GLA_SUB
    trow = lax.broadcasted_iota(jnp.int32, (GLA_SUB, 256), 0)

    def chunk(c, carry):
        r0 = pl.multiple_of(c * GLA_CHUNK, GLA_CHUNK)
        rows = pl.ds(r0, GLA_CHUNK)
        h1, h2, h3 = _split3(la_ref[0, rows, :])
        b = _dot(tri, h1) + _dot(tri, h2) + _dot(tri, h3)
        q = q_ref[0, rows, :]
        k = k_ref[0, rows, :]
        v = v_ref[0, rows, :]
        b_last = b[GLA_CHUNK - 1:GLA_CHUNK, :]
        qd = q * jnp.exp(b)
        kd = (k * jnp.exp(b_last - b)).astype(BF16)
        starts = [jnp.zeros((1, 256), F32)] + [b[GLA_SUB * i - 1:GLA_SUB * i, :]
                                               for i in range(1, GLA_NSUB)]
        bsel = jnp.concatenate([jnp.broadcast_to(s, (GLA_SUB, 256)) for s in starts], axis=0)
        qn = q * jnp.exp(b - bsel)

        diag = []
        for i in range(GLA_NSUB):
            rs = slice(GLA_SUB * i, GLA_SUB * (i + 1))
            b_i, q_i, k_i = b[rs], q[rs], k[rs]
            v_i = v[rs].astype(F32)
            ps = []
            for s in range(GLA_SUB):
                d = jnp.exp(jnp.minimum(b_i - b_i[s:s + 1], 0.0))
                ps.append(jnp.where(trow >= s, q_i * d * k_i[s:s + 1], 0.0).astype(BF16))
            r = _dot(jnp.concatenate(ps, axis=0), emat)
            od = r[0:GLA_SUB] * v_i[0:1]
            for s in range(1, GLA_SUB):
                od = od + r[GLA_SUB * s:GLA_SUB * (s + 1)] * v_i[s:s + 1]
            diag.append(od)
        o_diag = jnp.concatenate(diag, axis=0)

        for slab in range(2):
            ls = slice(slab * LANES, (slab + 1) * LANES)
            qn_s, k_s, b_s = qn[:, ls], k[:, ls], b[:, ls]
            khat = jnp.concatenate(
                [(k_s * jnp.exp(jnp.minimum(starts[i][:, ls] - b_s, 0.0))).astype(BF16)
                 for i in range(1, GLA_NSUB)], axis=1)
            for half in range(2):
                head = 2 * slab + half
                hs = slice(head * GLA_DV, (head + 1) * GLA_DV)
                in_head = (lane < HEAD_DIM) if half == 0 else (lane >= HEAD_DIM)
                qm = jnp.where(in_head, qn_s, 0.0)
                qhat = jnp.concatenate([jnp.where(rowblk == i, qm, 0.0).astype(BF16)
                                        for i in range(1, GLA_NSUB)], axis=1)
                att = jnp.where(sblk < tblk, _dot_nt(qhat, khat), 0.0)
                v_h = v[:, hs]
                st = st_ref[head]
                o = (_dot_nt(jnp.where(in_head, qd[:, ls], 0.0).astype(BF16), st.astype(BF16))
                     + _dot(att.astype(BF16), v_h) + o_diag[:, hs])
                st_ref[head] = st * jnp.exp(b_last[:, ls]) + _dot_tn(v_h, kd[:, ls])
                y = o * lax.rsqrt(jnp.mean(o * o, axis=-1, keepdims=True) + EPS) * gn_ref[:, hs]
                o_ref[0, rows, hs] = (y * _silu(g_ref[0, rows, hs])).astype(o_ref.dtype)
        return carry

    lax.fori_loop(0, GLA_TG // GLA_CHUNK, chunk, 0)


def _gla(gq, gk, gv, gg, la, gain):
    bsz, t_work, _ = gq.shape
    tri = jnp.asarray(np.tril(np.ones((GLA_CHUNK, GLA_CHUNK), np.float32)), dtype=BF16)
    emat = jnp.asarray(
        (np.arange(256)[:, None] // HEAD_DIM == np.arange(512)[None, :] // GLA_DV).astype(np.float32),
        dtype=BF16)
    rows = lambda width: pl.BlockSpec((1, GLA_TG, width), lambda b, j: (b, j, 0))
    const = lambda shape: pl.BlockSpec(shape, lambda b, j: (0,) * len(shape))
    return pl.pallas_call(
        _gla_kernel,
        grid=(bsz, t_work // GLA_TG),
        in_specs=[rows(256), rows(256), rows(512), rows(512), rows(256),
                  const((GLA_CHUNK, GLA_CHUNK)), const((256, 512)), const((1, 512))],
        out_specs=rows(512),
        out_shape=jax.ShapeDtypeStruct((bsz, t_work, 512), BF16),
        scratch_shapes=[pltpu.VMEM((N_HEADS, GLA_DV, LANES), F32)],
        compiler_params=pltpu.CompilerParams(
            dimension_semantics=("parallel", "arbitrary"), vmem_limit_bytes=VMEM_LIMIT),
        name="gla",
    )(gq, gk, gv, gg, la, tri, emat, gain)


FFN_CHUNK = 256


def _ffn_kernel(h_ref, oa_ref, ob_ref, oc_ref, wo_ref, fn_ref, wgu_ref, wd_ref, out_ref):
    h1 = (h_ref[0] + _dot(oa_ref[0], wo_ref[0:256, :]) + _dot(ob_ref[0], wo_ref[256:512, :])
          + _dot(oc_ref[0], wo_ref[512:1024, :]))
    ms = jnp.mean(h1 * h1, axis=-1, keepdims=True)
    f = (h1 * lax.rsqrt(ms + EPS) * fn_ref[...]).astype(BF16)
    out_ref[0] = h1
    for c in range(0, D_FF, FFN_CHUNK):
        gate = _dot(f, wgu_ref[:, c:c + FFN_CHUNK])
        up = _dot(f, wgu_ref[:, D_FF + c:D_FF + c + FFN_CHUNK])
        out_ref[0] += _dot((_silu(gate) * up).astype(BF16), wd_ref[c:c + FFN_CHUNK, :])


def _outproj_ffn(h, oa, ob, oc, wo, fn, wgu, wd):
    bsz, t_work, _ = h.shape
    tm = _row_tile(t_work)
    rows = lambda width: pl.BlockSpec((1, tm, width), lambda b, j: (b, j, 0))
    const = lambda shape: pl.BlockSpec(shape, lambda b, j: (0,) * len(shape),
                                       pipeline_mode=pl.Buffered(1))
    return pl.pallas_call(
        _ffn_kernel,
        grid=(bsz, t_work // tm),
        in_specs=[rows(D_MODEL), rows(256), rows(256), rows(512), const((D_MODEL, D_MODEL)),
                  const((1, D_MODEL)), const((D_MODEL, 2 * D_FF)), const((D_FF, D_MODEL))],
        out_specs=rows(D_MODEL),
        out_shape=jax.ShapeDtypeStruct(h.shape, F32),
        compiler_params=pltpu.CompilerParams(
            dimension_semantics=("parallel", "parallel"), vmem_limit_bytes=VMEM_LIMIT),
        name="outproj_ffn",
    )(h, oa, ob, oc, wo, fn, wgu, wd)


def kernel(x, meta_tokens, attn_norm, w_in, dsa_q_norm, dsa_k_norm, fox_q_norm, fox_k_norm,
           fox_f_bias, gla_gate_w2, gla_gate_b, gla_out_norm, w_out, ffn_norm, w_gate_up, w_down):
    bsz, seq, _ = x.shape
    n_tok = N_META + seq
    t_work = _work_len(n_tok)
    meta = jnp.broadcast_to(meta_tokens[None].astype(x.dtype), (bsz, N_META, D_MODEL))
    h = jnp.concatenate([meta, x, jnp.zeros((bsz, t_work - n_tok, D_MODEL), x.dtype)], axis=1)
    cos_t, sin_t = _rope_tables(t_work)
    g256 = _group_matrix()
    k_top = min(TOPK_MAX, seq // 4)
    for l in range(w_in.shape[0]):
        an, w_perm, vecs, w2_pad, sm_bias = _prep_layer_params(
            l, attn_norm, w_in, dsa_q_norm, dsa_k_norm, fox_q_norm, fox_k_norm, fox_f_bias,
            gla_gate_w2, gla_gate_b)
        (dq, dkk, dva, iq, ikk, fq, fk, fva, gq, gk, gv, gg, la, small) = _inproj(
            h, an, w_perm, cos_t, sin_t, g256, vecs, w2_pad, sm_bias)
        small_t, fkb = _fox_prep(small, fk)
        oa = _dsa_attention(dq, dkk, dva, iq, ikk, small_t, k_top)
        ob = _fox_attention(fq, fkb, fva)
        oc = _gla(gq, gk, gv, gg, la, jnp.tile(gla_out_norm[l], N_HEADS)[None, :])
        h = _outproj_ffn(h, oa, ob, oc, w_out[l].astype(BF16), ffn_norm[l][None, :],
                         w_gate_up[l].astype(BF16), w_down[l].astype(BF16))
    return h[:, N_META:n_tok]
```

```python
import functools

import numpy as np
import jax
import jax.numpy as jnp
from jax import lax
from jax.experimental import pallas as pl
from jax.experimental.pallas import tpu as pltpu

F32 = jnp.float32
BF16 = jnp.bfloat16

D_MODEL = 1024
HEAD_DIM = 64
N_META = 16
ROPE_THETA = 500000.0
ROPE_DIM = HEAD_DIM // 4
ROPE_HALF = ROPE_DIM // 2
NEG = -1e30
EPS = 1e-6

N_HEADS = 4
TOPK_MAX = 256
GLA_DV = 128
GLA_RANK = 16
GLA_TAU = 16.0
GLA_CHUNK = 64
GLA_SUB = 16
D_FF = 2816

LOG2E = 1.4426950408889634
BIAS_TERMS = 3
KN_DSA, KN_FOX = 0, 1
NORM_MARGIN = 1.01
SAFE_LOG2_SPAN = 100.0
LANES = 128
SEQ_ALIGN = 512
VMEM_LIMIT = 56 * 1024 * 1024

_SLABS = (("dq", 256), ("dkk", 128), ("dva", 128), ("iq", 256), ("ikk", 128),
          ("fqa", 512), ("fkp", 512), ("fva", 512), ("gq", 256), ("gk", 256),
          ("gv", 512), ("gg", 512), ("small", 128))
_SLAB_OFF = {}
_off = 0
for _name, _width in _SLABS:
    _SLAB_OFF[_name] = (_off, _width)
    _off += _width
N_PROJ = _off
SM_IW, SM_FF, SM_GLR = 0, 4, 8


def _work_len(n_tok):
    return -(-n_tok // SEQ_ALIGN) * SEQ_ALIGN


def _row_tile(t_work):
    for cand in (768, 640, 512):
        if t_work % cand == 0:
            return cand
    raise ValueError(f"unsupported working length {t_work}")


def _dot(a, b):
    return jnp.dot(a, b, preferred_element_type=F32)


def _dot_nt(a, b):
    return lax.dot_general(a, b, (((1,), (1,)), ((), ())), preferred_element_type=F32)


def _dot_tn(a, b):
    return lax.dot_general(a, b, (((0,), (0,)), ((), ())), preferred_element_type=F32)


def _split3(x):
    h1 = x.astype(BF16)
    r1 = x - h1.astype(F32)
    h2 = r1.astype(BF16)
    h3 = (r1 - h2.astype(F32)).astype(BF16)
    return h1, h2, h3


def _log_sigmoid(x):
    return jnp.minimum(x, 0.0) - jnp.log1p(jnp.exp(-jnp.abs(x)))


def _silu(x):
    return x / (1.0 + jnp.exp(-x))


def _group_rms(y, gmat, gain):
    yy = y * y
    hi = yy.astype(BF16)
    lo = (yy - hi.astype(F32)).astype(BF16)
    ss = _dot(hi, gmat) + _dot(lo, gmat)
    return y * lax.rsqrt(ss * (1.0 / HEAD_DIM) + EPS) * gain


def _rope(y, cos, sin):
    width = y.shape[-1]
    lane = lax.broadcasted_iota(jnp.int32, y.shape, 1) % HEAD_DIM
    upper = pltpu.roll(y, width - ROPE_HALF, axis=1)
    lower = pltpu.roll(y, ROPE_HALF, axis=1)
    partner = jnp.where(lane < ROPE_HALF, upper, lower)
    return y * cos + partner * sin


def _inproj_kernel(x_ref, an_ref, w_ref, cos_ref, sin_ref, g256_ref, vec_ref, w2_ref, sm_ref,
                   dq_ref, dkk_ref, dva_ref, iq_ref, ikk_ref, fq_ref, fk_ref, fva_ref,
                   gq_ref, gk_ref, gv_ref, gg_ref, la_ref, small_ref, kn_ref):
    x = x_ref[0]
    ms = jnp.mean(x * x, axis=-1, keepdims=True)
    a = (x * lax.rsqrt(ms + EPS) * an_ref[...]).astype(BF16)

    def proj(name):
        off, width = _SLAB_OFF[name]
        return _dot(a, w_ref[:, off:off + width])

    def ones_in_upper_half(y):
        lane = lax.broadcasted_iota(jnp.int32, y.shape, 1) % LANES
        return jnp.where(lane < HEAD_DIM, y, 1.0)

    cos = cos_ref[...]
    sin = sin_ref[...]
    g256 = g256_ref[...]
    g128 = g256[:LANES, :LANES]
    dqn, dkn, fqn, fkn = vec_ref[0:1, :], vec_ref[1:2, :LANES], vec_ref[2:3, :], vec_ref[3:4, :]
    scale = HEAD_DIM ** -0.5

    def max_sq_norm(k):
        kf = k.astype(F32)
        return jnp.max(_dot((kf * kf).astype(BF16), g128), axis=0, keepdims=True)

    dq_ref[0] = (_rope(_group_rms(proj("dq"), g256, dqn), cos, sin) * (scale * LOG2E)).astype(BF16)
    dkk = _rope(_group_rms(proj("dkk"), g128, dkn), cos[:, :LANES], sin[:, :LANES]).astype(BF16)
    dkk_ref[0] = dkk
    key_norms = [max_sq_norm(dkk)]
    dva_ref[0] = ones_in_upper_half(proj("dva")).astype(BF16)
    iq_ref[0] = (_rope(proj("iq"), cos, sin) * scale).astype(BF16)
    ikk_ref[0] = _rope(proj("ikk"), cos[:, :LANES], sin[:, :LANES]).astype(BF16)
    fq, fk = proj("fqa"), proj("fkp")
    lane = lax.broadcasted_iota(jnp.int32, (fq.shape[0], LANES), 1)
    bias_lanes = (lane >= HEAD_DIM) & (lane < HEAD_DIM + BIAS_TERMS)
    for h in range(N_HEADS):
        hs = slice(h * LANES, (h + 1) * LANES)
        q_h = _group_rms(fq[:, hs], g128, fqn[:, :LANES]) * (scale * LOG2E)
        fq_ref[0, :, hs] = jnp.where(bias_lanes, 1.0, q_h).astype(BF16)
        k_h = _group_rms(fk[:, hs], g128, fkn[:, :LANES]).astype(BF16)
        fk_ref[0, :, hs] = k_h
        key_norms.append(max_sq_norm(k_h))
    kn_ref[0, 0] = jnp.concatenate(key_norms + [jnp.zeros((8 - len(key_norms), LANES), F32)], axis=0)
    fva_ref[0] = ones_in_upper_half(proj("fva")).astype(BF16)
    gq_ref[0] = proj("gq") * scale
    gk_ref[0] = proj("gk")
    gv_ref[0] = proj("gv").astype(BF16)
    gg_ref[0] = proj("gg")

    small = proj("small")
    lane = lax.broadcasted_iota(jnp.int32, small.shape, 1)
    small_ref[0] = jnp.where(lane < SM_FF, small * (N_HEADS ** -0.5),
                             _log_sigmoid(small + sm_ref[0:1, :]))
    gate = _dot(small.astype(BF16), w2_ref[...]) + vec_ref[4:5, :]
    la_ref[0] = _log_sigmoid(gate) * (1.0 / GLA_TAU)


def _inproj(h, an, w_perm, cos_t, sin_t, g256, vecs, w2_pad, sm_bias):
    bsz, t_work, _ = h.shape
    tm = _row_tile(t_work)
    grid = (bsz, t_work // tm)

    def rows(width, dtype):
        return (jax.ShapeDtypeStruct((bsz, t_work, width), dtype),
                pl.BlockSpec((1, tm, width), lambda b, j: (b, j, 0)))

    outs = [rows(256, BF16), rows(128, BF16), rows(128, BF16), rows(256, BF16), rows(128, BF16),
            rows(512, BF16), rows(512, BF16), rows(512, BF16), rows(256, F32), rows(256, F32),
            rows(512, BF16), rows(512, F32), rows(256, F32), rows(128, F32),
            (jax.ShapeDtypeStruct((bsz, t_work // tm, 8, LANES), F32),
             pl.BlockSpec((1, 1, 8, LANES), lambda b, j: (b, j, 0, 0)))]
    const = lambda shape: pl.BlockSpec(shape, lambda b, j: (0,) * len(shape))
    return pl.pallas_call(
        _inproj_kernel,
        grid=grid,
        in_specs=[pl.BlockSpec((1, tm, D_MODEL), lambda b, j: (b, j, 0)),
                  const((1, D_MODEL)), const((D_MODEL, N_PROJ)),
                  pl.BlockSpec((tm, 256), lambda b, j: (j, 0)),
                  pl.BlockSpec((tm, 256), lambda b, j: (j, 0)),
                  const((256, 256)), const((8, 256)), const((LANES, 256)), const((8, LANES))],
        out_specs=[o[1] for o in outs],
        out_shape=[o[0] for o in outs],
        compiler_params=pltpu.CompilerParams(
            dimension_semantics=("parallel", "parallel"), vmem_limit_bytes=VMEM_LIMIT),
        name="inproj",
    )(h, an, w_perm, cos_t, sin_t, g256, vecs, w2_pad, sm_bias)


def _prep_layer_params(l, attn_norm, w_in, dsa_q_norm, dsa_k_norm, fox_q_norm, fox_k_norm,
                       fox_f_bias, gla_gate_w2, gla_gate_b):
    w = w_in[l]
    splits = np.cumsum([256, 64, 64, 256, 4, 64, 256, 256, 256, 4, 256, 256, 512, 512, 16])[:-1]
    (dq, dk, dv, iq, iw, ik, fq, fk, fv, ff, gq, gk, gv, gg, glr) = jnp.split(w, splits, axis=1)
    small = jnp.concatenate(
        [iw, ff, glr, jnp.zeros((D_MODEL, LANES - 4 - 4 - GLA_RANK), w.dtype)], axis=1)
    z64 = jnp.zeros((D_MODEL, HEAD_DIM), w.dtype)
    per_head = lambda t: [part for h in range(N_HEADS)
                          for part in (t[:, h * HEAD_DIM:(h + 1) * HEAD_DIM], z64)]
    w_perm = jnp.concatenate([dq, dk, dk, dv, z64, iq, ik, ik, *per_head(fq), *per_head(fk),
                              *per_head(fv), gq, gk, gv, gg, small], axis=1).astype(BF16)
    tile4 = lambda g: jnp.tile(g, N_HEADS)
    vecs = jnp.zeros((8, 256), F32)
    vecs = vecs.at[0].set(tile4(dsa_q_norm[l])).at[1].set(tile4(dsa_k_norm[l]))
    vecs = vecs.at[2].set(tile4(fox_q_norm[l])).at[3].set(tile4(fox_k_norm[l]))
    vecs = vecs.at[4].set(gla_gate_b[l])
    w2_pad = jnp.zeros((LANES, 256), F32).at[SM_GLR:SM_GLR + GLA_RANK].set(gla_gate_w2[l]).astype(BF16)
    sm_bias = jnp.zeros((8, LANES), F32).at[0, SM_FF:SM_FF + N_HEADS].set(fox_f_bias[l])
    return attn_norm[l][None, :], w_perm, vecs, w2_pad, sm_bias


def _rope_tables(t_work):
    inv = jnp.power(ROPE_THETA, -jnp.arange(ROPE_HALF, dtype=F32) * 2.0 / ROPE_DIM)
    ang = jnp.arange(t_work).astype(F32)[:, None] * inv[None, :]
    cos, sin = jnp.cos(ang), jnp.sin(ang)
    rest = HEAD_DIM - ROPE_DIM
    cos64 = jnp.concatenate([cos, cos, jnp.ones((t_work, rest), F32)], axis=1)
    sin64 = jnp.concatenate([-sin, sin, jnp.zeros((t_work, rest), F32)], axis=1)
    return jnp.tile(cos64, (1, N_HEADS)), jnp.tile(sin64, (1, N_HEADS))


def _group_matrix():
    idx = np.arange(256) // HEAD_DIM
    return jnp.asarray((idx[:, None] == idx[None, :]).astype(np.float32), dtype=BF16)


DSA_TQ = 256
DSA_TK = 512
KEY_NEG_INF = -2139095041
KEY_NEG_ZERO = -1
KEY_NEG_MIN_NORMAL = -8388609
SEARCH_FEW = 4.0
GUIDED_PROBES = 64
MAX_PROBES = GUIDED_PROBES + 40
COUNT_ROWS = 128


def _key_to_f32(key):
    bits = key ^ ((key >> 31) & 0x7FFFFFFF)
    return lax.bitcast_convert_type(bits, F32)


def _f32_to_key(value):
    bits = lax.bitcast_convert_type(value, jnp.int32)
    return bits ^ ((bits >> 31) & 0x7FFFFFFF)


def _head_lane_mask(shape, head):
    lane = lax.broadcasted_iota(jnp.int32, shape, 1)
    return (lane < HEAD_DIM) if head % 2 == 0 else (lane >= HEAD_DIM)


def _masked_heads(slabs):
    return [jnp.where(_head_lane_mask(slabs[h // 2].shape, h), slabs[h // 2],
                      jnp.zeros_like(slabs[h // 2])) for h in range(N_HEADS)]


def _softmax_step(s, m, acc, v_aug):
    m_new = jnp.maximum(m, jnp.max(s, axis=-1, keepdims=True))
    p = jnp.exp2(s - m_new)
    return m_new, acc * jnp.exp2(m - m_new) + _dot(p.astype(BF16), v_aug)


def _normalise_heads(accs):
    outs = [acc / pltpu.roll(acc, HEAD_DIM, axis=1) for acc in accs]
    lane = lax.broadcasted_iota(jnp.int32, outs[0].shape, 1)
    return [jnp.where(lane < HEAD_DIM, outs[2 * p], pltpu.roll(outs[2 * p + 1], HEAD_DIM, axis=1))
            for p in range(2)]


def _dsa_kernel(k_top, dq_ref, kk_ref, va_ref, iq_ref, ik_ref, wt_ref, tri_ref, kn_ref, o_ref,
                s_ref):
    i = pl.program_id(1)
    q0 = i * DSA_TQ
    n_tiles = (q0 + DSA_TQ + DSA_TK - 1) // DSA_TK
    last = n_tiles - 1
    key = lax.broadcasted_iota(jnp.int32, (DSA_TK, DSA_TQ), 0)
    qry = q0 + lax.broadcasted_iota(jnp.int32, (DSA_TK, DSA_TQ), 1)
    head = lambda x, h: x[:, h * DSA_TQ:(h + 1) * DSA_TQ]

    iq_all = jnp.concatenate(
        _masked_heads([iq_ref[0, :, 0:LANES], iq_ref[0, :, LANES:2 * LANES]]), axis=0)
    w_h = [wt_ref[0, SM_IW + h:SM_IW + h + 1, :] for h in range(N_HEADS)]

    def score_tile(j, carry):
        k0 = pl.multiple_of(j * DSA_TK, DSA_TK)
        dots = jnp.maximum(_dot_nt(ik_ref[0, pl.ds(k0, DSA_TK), :], iq_all), 0.0)
        s = w_h[0] * head(dots, 0)
        for h in range(1, N_HEADS):
            s = s + w_h[h] * head(dots, h)
        s = jnp.where(k0 + key <= qry, s, NEG)
        s_ref[pl.ds(k0, DSA_TK), :] = s
        return jnp.maximum(carry, jnp.max(s, axis=0, keepdims=True))

    row_max = lax.fori_loop(0, n_tiles, score_tile, jnp.full((1, DSA_TQ), NEG, F32))

    kf = jnp.float32(k_top)

    def count_above(t):
        tb = jnp.broadcast_to(t, (COUNT_ROWS, DSA_TQ))

        def body(j, acc):
            k0 = pl.multiple_of(j * DSA_TK, DSA_TK)
            for part in range(DSA_TK // COUNT_ROWS):
                s = s_ref[pl.ds(k0 + part * COUNT_ROWS, COUNT_ROWS), :]
                acc = acc + jnp.where(s > tb, 1.0, 0.0)
            return acc

        acc = lax.fori_loop(0, n_tiles, body, jnp.zeros((COUNT_ROWS, DSA_TQ), F32))
        return jnp.sum(acc, axis=0, keepdims=True)

    def midpoint(lo, hi):
        return (lo >> 1) + (hi >> 1) + (lo & hi & 1)

    def converged(lo, hi):
        return (midpoint(lo, hi) == lo) | ((lo >= KEY_NEG_MIN_NORMAL) & (hi <= 0))

    def search_cond(state):
        it, pending = state[0], state[1]
        return (pending > 0) & (it < MAX_PROBES)

    col1 = lambda value, dtype: jnp.full((1, DSA_TQ), value, dtype)

    def search_body(state):
        it, _, lo, hi, clo, chi, wlo, whi, side = state
        f_lo, f_hi = _key_to_f32(lo), _key_to_f32(hi)
        g_lo = (clo - (kf - 0.5)) * wlo
        g_hi = ((kf - 0.5) - chi) * whi
        halve = (clo - chi <= SEARCH_FEW) | (col1(it % 4, jnp.int32) == 3)
        guess = _f32_to_key(f_lo + (f_hi - f_lo) * jnp.where(halve, 0.5, g_lo / (g_lo + g_hi)))
        fixed = col1(jnp.where(it == 0, KEY_NEG_ZERO, KEY_NEG_MIN_NORMAL), jnp.int32)
        early = col1(it, jnp.int32) < 2
        guided = col1(it, jnp.int32) < GUIDED_PROBES
        probe = jnp.where((guess > lo) & (guess < hi) & guided & ~early, guess, midpoint(lo, hi))
        probe = jnp.where(early & (lo < fixed) & (fixed < hi), fixed, probe)
        probe = jnp.where(converged(lo, hi), lo, probe)
        c = count_above(_key_to_f32(probe))
        live = probe != lo
        up = live & (c >= kf)
        down = live & (c <= kf)
        wlo = jnp.where(down & (side < 0), wlo * 0.5, jnp.where(up, 1.0, wlo))
        whi = jnp.where(up & (side > 0), whi * 0.5, jnp.where(down, 1.0, whi))
        side = jnp.where(up, 1, jnp.where(down, -1, side))
        lo, clo = jnp.where(up, probe, lo), jnp.where(up, c, clo)
        hi, chi = jnp.where(down, probe, hi), jnp.where(down, c, chi)
        pending = jnp.max(jnp.where(converged(lo, hi), 0, 1))
        return it + 1, pending, lo, hi, clo, chi, wlo, whi, side

    n_swept = (n_tiles * DSA_TK).astype(F32)
    state = lax.while_loop(
        search_cond, search_body,
        (jnp.int32(0), jnp.int32(1), col1(KEY_NEG_INF, jnp.int32), _f32_to_key(row_max),
         jnp.broadcast_to(n_swept, (1, DSA_TQ)), col1(0.0, F32), col1(1.0, F32), col1(1.0, F32),
         col1(0, jnp.int32)))
    thr = _key_to_f32(state[3])
    n_ties = kf - state[5]

    to_column = lambda r: jnp.broadcast_to(r, (8, DSA_TQ)).T[:, 0:1]
    thr_c, ties_c = to_column(thr), to_column(n_ties)
    q_all = jnp.concatenate(
        _masked_heads([dq_ref[0, :, 0:LANES], dq_ref[0, :, LANES:2 * LANES]]), axis=0)
    tri = tri_ref[...]
    qrow = q0 + lax.broadcasted_iota(jnp.int32, (DSA_TQ, DSA_TK), 0)
    kcol = lax.broadcasted_iota(jnp.int32, (DSA_TQ, DSA_TK), 1)
    n_blocks = DSA_TK // LANES

    reach = _logit_reach(q_all, jnp.max(kn_ref[0], axis=0)[KN_DSA:KN_DSA + 1, 0:1])

    def selected_logits(j, seen, causal):
        k0 = pl.multiple_of(j * DSA_TK, DSA_TK)
        s = s_ref[pl.ds(k0, DSA_TK), :].T
        tie = s == thr_c
        tie_b = jnp.where(tie, 1.0, 0.0).astype(BF16)
        local = [_dot(tie_b[:, b * LANES:(b + 1) * LANES], tri) for b in range(n_blocks)]
        ranks = []
        for b in range(n_blocks):
            ranks.append(local[b] + seen)
            seen = seen + local[b][:, LANES - 1:LANES]
        sel = (s > thr_c) | (tie & (jnp.concatenate(ranks, axis=1) <= ties_c))
        if causal:
            sel = sel & (k0 + kcol <= qrow)
        logits = _dot_nt(q_all, kk_ref[0, pl.ds(k0, DSA_TK), :])
        logits = jnp.where(sel[None], logits.reshape(N_HEADS, DSA_TQ, DSA_TK), NEG)
        return seen, logits.reshape(N_HEADS * DSA_TQ, DSA_TK), va_ref[0, pl.ds(k0, DSA_TK), :]

    def capped(j, carry, causal):
        seen, logits, va_t = selected_logits(j, carry[0], causal)
        return seen, carry[1] + _dot(jnp.exp2(logits - reach).astype(BF16), va_t)

    def online(j, carry, causal):
        seen, logits, va_t = selected_logits(j, carry[0], causal)
        return (seen,) + _softmax_step(logits, carry[1], carry[2], va_t)

    def sweep(step, *stats):
        carry = (jnp.zeros((DSA_TQ, 1), F32),) + stats + (jnp.zeros((N_HEADS * DSA_TQ, LANES), F32),)
        carry = lax.fori_loop(0, last, functools.partial(step, causal=False), carry)
        return step(last, carry, causal=True)[-1]

    acc = lax.cond(2.0 * jnp.max(reach) <= SAFE_LOG2_SPAN,
                   functools.partial(sweep, capped),
                   functools.partial(sweep, online, jnp.full((N_HEADS * DSA_TQ, 1), NEG, F32)))
    o_ref[0] = jnp.concatenate(
        _normalise_heads([acc[h * DSA_TQ:(h + 1) * DSA_TQ] for h in range(N_HEADS)]),
        axis=1).astype(o_ref.dtype)


def _dsa_attention(dq, dkk, dva, iq, ikk, small_t, key_norms, k_top):
    bsz, t_work, _ = dq.shape
    tri = jnp.asarray(np.triu(np.ones((LANES, LANES), np.float32)), dtype=BF16)
    tile = lambda width: pl.BlockSpec((1, DSA_TQ, width), lambda b, i: (b, i, 0))
    full = pl.BlockSpec((1, t_work, LANES), lambda b, i: (b, 0, 0), pipeline_mode=pl.Buffered(1))
    return pl.pallas_call(
        functools.partial(_dsa_kernel, k_top),
        grid=(bsz, t_work // DSA_TQ),
        in_specs=[tile(256), full, full, tile(256), full,
                  pl.BlockSpec((1, 8, DSA_TQ), lambda b, i: (b, 0, i)),
                  pl.BlockSpec((LANES, LANES), lambda b, i: (0, 0)),
                  pl.BlockSpec((1,) + key_norms.shape[1:], lambda b, i: (b, 0, 0, 0))],
        out_specs=tile(256),
        out_shape=jax.ShapeDtypeStruct((bsz, t_work, 256), BF16),
        scratch_shapes=[pltpu.VMEM((t_work, DSA_TQ), F32)],
        compiler_params=pltpu.CompilerParams(
            dimension_semantics=("parallel", "parallel"), vmem_limit_bytes=VMEM_LIMIT),
        name="dsa_attention",
    )(dq, dkk, dva, iq, ikk, small_t, tri, key_norms)


FOX_TQ = 512
FOX_TK = 512
CUM_T = 256


def _fox_prep_kernel(x_ref, tri_ref, place_ref, k_ref, xt_ref, kb_ref, carry_ref):
    @pl.when(pl.program_id(1) == 0)
    def _():
        carry_ref[...] = jnp.zeros_like(carry_ref)

    tri = tri_ref[...]
    x = x_ref[0]
    h1, h2, h3 = _split3(x)
    c = _dot(tri, h1) + _dot(tri, h2) + _dot(tri, h3) + carry_ref[0:1, :]
    carry_ref[...] = jnp.broadcast_to(c[CUM_T - 1:CUM_T, :], carry_ref.shape)
    xt_ref[0] = x.T[0:8, :]
    terms = _split3(c * -LOG2E)
    bias = sum(_dot(terms[t], place_ref[t]) for t in range(BIAS_TERMS))
    lane = lax.broadcasted_iota(jnp.int32, bias.shape, 1) % LANES
    kb_ref[0] = jnp.where((lane >= HEAD_DIM) & (lane < HEAD_DIM + BIAS_TERMS),
                          bias.astype(BF16), k_ref[0])


def _fox_prep(small, fk):
    bsz, t_work, width = fk.shape
    tri = jnp.asarray(np.tril(np.ones((CUM_T, CUM_T), np.float32)), dtype=BF16)
    place = np.zeros((BIAS_TERMS, LANES, width), np.float32)
    for t in range(BIAS_TERMS):
        for h in range(N_HEADS):
            place[t, SM_FF + h, h * LANES + HEAD_DIM + t] = 1.0
    rows = lambda w: pl.BlockSpec((1, CUM_T, w), lambda b, j: (b, j, 0))
    return pl.pallas_call(
        _fox_prep_kernel,
        grid=(bsz, t_work // CUM_T),
        in_specs=[rows(LANES), pl.BlockSpec((CUM_T, CUM_T), lambda b, j: (0, 0)),
                  pl.BlockSpec((BIAS_TERMS, LANES, width), lambda b, j: (0, 0, 0)), rows(width)],
        out_specs=[pl.BlockSpec((1, 8, CUM_T), lambda b, j: (b, 0, j)), rows(width)],
        out_shape=[jax.ShapeDtypeStruct((bsz, 8, t_work), F32),
                   jax.ShapeDtypeStruct(fk.shape, BF16)],
        scratch_shapes=[pltpu.VMEM((8, LANES), F32)],
        compiler_params=pltpu.CompilerParams(dimension_semantics=("parallel", "arbitrary")),
        name="fox_prep",
    )(small, tri, jnp.asarray(place, dtype=BF16), fk)


def _logit_reach(q, kmax_sq):
    qf = q.astype(F32)
    return jnp.sqrt(jnp.sum(qf * qf, axis=-1, keepdims=True) * kmax_sq) * NORM_MARGIN


def _fox_kernel(q_ref, k_ref, v_ref, kn_ref, o_ref):
    i = pl.program_id(1)
    q0 = pl.multiple_of(i * FOX_TQ, FOX_TQ)
    n_full = q0 // FOX_TK
    row = q0 + lax.broadcasted_iota(jnp.int32, (FOX_TQ, FOX_TK), 0)
    col = n_full * FOX_TK + lax.broadcasted_iota(jnp.int32, (FOX_TQ, FOX_TK), 1)
    heads = [slice(h * LANES, (h + 1) * LANES) for h in range(N_HEADS)]

    lane = lax.broadcasted_iota(jnp.int32, (FOX_TQ, LANES), 1)
    bias_lanes = (lane >= HEAD_DIM) & (lane < HEAD_DIM + BIAS_TERMS)
    kmax_sq = jnp.max(kn_ref[0], axis=0)
    caps, span = [], jnp.float32(0.0)
    for h in range(N_HEADS):
        q_h = q_ref[0, :, heads[h]]
        reach = _logit_reach(jnp.where(lane < HEAD_DIM, q_h, jnp.zeros_like(q_h)),
                             kmax_sq[KN_FOX + h:KN_FOX + h + 1, 0:1])
        own = k_ref[0, pl.ds(q0, FOX_TQ), heads[h]].astype(F32)
        caps.append(reach + jnp.sum(jnp.where(bias_lanes, own, 0.0), axis=-1, keepdims=True))
        span = jnp.maximum(span, 2.0 * jnp.max(reach))

    def logits(j, h, diag):
        k0 = pl.multiple_of(j * FOX_TK, FOX_TK)
        s = _dot_nt(q_ref[0, :, heads[h]], k_ref[0, pl.ds(k0, FOX_TK), heads[h]])
        return jnp.where(col <= row, s, NEG) if diag else s

    def values(j, h):
        return v_ref[0, pl.ds(pl.multiple_of(j * FOX_TK, FOX_TK), FOX_TK), heads[h]]

    def capped(j, accs, diag):
        return tuple(accs[h] + _dot(jnp.exp2(logits(j, h, diag) - caps[h]).astype(BF16), values(j, h))
                     for h in range(N_HEADS))

    def online(j, carry, diag):
        return tuple(_softmax_step(logits(j, h, diag), *carry[h], values(j, h))
                     for h in range(N_HEADS))

    def capped_sweep():
        accs = tuple(jnp.zeros((FOX_TQ, LANES), F32) for _ in range(N_HEADS))
        accs = lax.fori_loop(0, n_full, functools.partial(capped, diag=False), accs)
        return capped(n_full, accs, diag=True)

    def online_sweep():
        carry = tuple((jnp.full((FOX_TQ, 1), NEG, F32), jnp.zeros((FOX_TQ, LANES), F32))
                      for _ in range(N_HEADS))
        carry = lax.fori_loop(0, n_full, functools.partial(online, diag=False), carry)
        return tuple(acc for _, acc in online(n_full, carry, diag=True))

    accs = lax.cond(span <= SAFE_LOG2_SPAN, capped_sweep, online_sweep)
    o_ref[0] = jnp.concatenate(_normalise_heads(list(accs)), axis=1).astype(o_ref.dtype)


def _fox_attention(fqa, fkb, fva, key_norms):
    bsz, t_work, width = fqa.shape
    full = pl.BlockSpec((1, t_work, width), lambda b, i: (b, 0, 0), pipeline_mode=pl.Buffered(1))
    return pl.pallas_call(
        _fox_kernel,
        grid=(bsz, t_work // FOX_TQ),
        in_specs=[pl.BlockSpec((1, FOX_TQ, width), lambda b, i: (b, i, 0)), full, full,
                  pl.BlockSpec((1,) + key_norms.shape[1:], lambda b, i: (b, 0, 0, 0))],
        out_specs=pl.BlockSpec((1, FOX_TQ, 2 * LANES), lambda b, i: (b, i, 0)),
        out_shape=jax.ShapeDtypeStruct((bsz, t_work, 2 * LANES), BF16),
        compiler_params=pltpu.CompilerParams(
            dimension_semantics=("parallel", "parallel"), vmem_limit_bytes=VMEM_LIMIT),
        name="fox_attention",
    )(fqa, fkb, fva, key_norms)


GLA_TG = 256
GLA_NSUB = GLA_CHUNK // GLA_SUB


def _gla_kernel(q_ref, k_ref, v_ref, g_ref, la_ref, tri_ref, e_ref, gn_ref, o_ref, st_ref):
    @pl.when(pl.program_id(1) == 0)
    def _():
        st_ref[...] = jnp.zeros_like(st_ref)

    tri = tri_ref[...]
    emat = e_ref[...]
    lane = lax.broadcasted_iota(jnp.int32, (GLA_CHUNK, LANES), 1)
    rowblk = lax.broadcasted_iota(jnp.int32, (GLA_CHUNK, LANES), 0) // GLA_SUB
    tblk = lax.broadcasted_iota(jnp.int32, (GLA_CHUNK, GLA_CHUNK), 0) // GLA_SUB
    sblk = lax.broadcasted_iota(jnp.int32, (GLA_CHUNK, GLA_CHUNK), 1) // GLA_SUB
    trow = lax.broadcasted_iota(jnp.int32, (GLA_SUB, 256), 0)

    def chunk(c, carry):
        r0 = pl.multiple_of(c * GLA_CHUNK, GLA_CHUNK)
        rows = pl.ds(r0, GLA_CHUNK)
        h1, h2, h3 = _split3(la_ref[0, rows, :])
        b = _dot(tri, h1) + _dot(tri, h2) + _dot(tri, h3)
        q = q_ref[0, rows, :]
        k = k_ref[0, rows, :]
        v = v_ref[0, rows, :]
        b_last = b[GLA_CHUNK - 1:GLA_CHUNK, :]
        qd = q * jnp.exp(b)
        kd = (k * jnp.exp(b_last - b)).astype(BF16)
        starts = [jnp.zeros((1, 256), F32)] + [b[GLA_SUB * i - 1:GLA_SUB * i, :]
                                               for i in range(1, GLA_NSUB)]
        bsel = jnp.concatenate([jnp.broadcast_to(s, (GLA_SUB, 256)) for s in starts], axis=0)
        qn = q * jnp.exp(b - bsel)

        diag = []
        for i in range(GLA_NSUB):
            rs = slice(GLA_SUB * i, GLA_SUB * (i + 1))
            b_i, q_i, k_i = b[rs], q[rs], k[rs]
            v_i = v[rs].astype(F32)
            ps = []
            for s in range(GLA_SUB):
                d = jnp.exp(jnp.minimum(b_i - b_i[s:s + 1], 0.0))
                ps.append(jnp.where(trow >= s, q_i * d * k_i[s:s + 1], 0.0).astype(BF16))
            r = _dot(jnp.concatenate(ps, axis=0), emat)
            od = r[0:GLA_SUB] * v_i[0:1]
            for s in range(1, GLA_SUB):
                od = od + r[GLA_SUB * s:GLA_SUB * (s + 1)] * v_i[s:s + 1]
            diag.append(od)
        o_diag = jnp.concatenate(diag, axis=0)

        for slab in range(2):
            ls = slice(slab * LANES, (slab + 1) * LANES)
            qn_s, k_s, b_s = qn[:, ls], k[:, ls], b[:, ls]
            khat = jnp.concatenate(
                [(k_s * jnp.exp(jnp.minimum(starts[i][:, ls] - b_s, 0.0))).astype(BF16)
                 for i in range(1, GLA_NSUB)], axis=1)
            for half in range(2):
                head = 2 * slab + half
                hs = slice(head * GLA_DV, (head + 1) * GLA_DV)
                in_head = (lane < HEAD_DIM) if half == 0 else (lane >= HEAD_DIM)
                qm = jnp.where(in_head, qn_s, 0.0)
                qhat = jnp.concatenate([jnp.where(rowblk == i, qm, 0.0).astype(BF16)
                                        for i in range(1, GLA_NSUB)], axis=1)
                att = jnp.where(sblk < tblk, _dot_nt(qhat, khat), 0.0)
                v_h = v[:, hs]
                st = st_ref[head]
                o = (_dot_nt(jnp.where(in_head, qd[:, ls], 0.0).astype(BF16), st.astype(BF16))
                     + _dot(att.astype(BF16), v_h) + o_diag[:, hs])
                st_ref[head] = st * jnp.exp(b_last[:, ls]) + _dot_tn(v_h, kd[:, ls])
                y = o * lax.rsqrt(jnp.mean(o * o, axis=-1, keepdims=True) + EPS) * gn_ref[:, hs]
                o_ref[0, rows, hs] = (y * _silu(g_ref[0, rows, hs])).astype(o_ref.dtype)
        return carry

    lax.fori_loop(0, GLA_TG // GLA_CHUNK, chunk, 0)


def _gla(gq, gk, gv, gg, la, gain):
    bsz, t_work, _ = gq.shape
    tri = jnp.asarray(np.tril(np.ones((GLA_CHUNK, GLA_CHUNK), np.float32)), dtype=BF16)
    emat = jnp.asarray(
        (np.arange(256)[:, None] // HEAD_DIM == np.arange(512)[None, :] // GLA_DV).astype(np.float32),
        dtype=BF16)
    rows = lambda width: pl.BlockSpec((1, GLA_TG, width), lambda b, j: (b, j, 0))
    const = lambda shape: pl.BlockSpec(shape, lambda b, j: (0,) * len(shape))
    return pl.pallas_call(
        _gla_kernel,
        grid=(bsz, t_work // GLA_TG),
        in_specs=[rows(256), rows(256), rows(512), rows(512), rows(256),
                  const((GLA_CHUNK, GLA_CHUNK)), const((256, 512)), const((1, 512))],
        out_specs=rows(512),
        out_shape=jax.ShapeDtypeStruct((bsz, t_work, 512), BF16),
        scratch_shapes=[pltpu.VMEM((N_HEADS, GLA_DV, LANES), F32)],
        compiler_params=pltpu.CompilerParams(
            dimension_semantics=("parallel", "arbitrary"), vmem_limit_bytes=VMEM_LIMIT),
        name="gla",
    )(gq, gk, gv, gg, la, tri, emat, gain)


FFN_CHUNK = 256


def _ffn_kernel(h_ref, oa_ref, ob_ref, oc_ref, wo_ref, fn_ref, wgu_ref, wd_ref, out_ref):
    h1 = (h_ref[0] + _dot(oa_ref[0], wo_ref[0:256, :]) + _dot(ob_ref[0], wo_ref[256:512, :])
          + _dot(oc_ref[0], wo_ref[512:1024, :]))
    ms = jnp.mean(h1 * h1, axis=-1, keepdims=True)
    f = (h1 * lax.rsqrt(ms + EPS) * fn_ref[...]).astype(BF16)
    out_ref[0] = h1
    for c in range(0, D_FF, FFN_CHUNK):
        gate = _dot(f, wgu_ref[:, c:c + FFN_CHUNK])
        up = _dot(f, wgu_ref[:, D_FF + c:D_FF + c + FFN_CHUNK])
        out_ref[0] += _dot((_silu(gate) * up).astype(BF16), wd_ref[c:c + FFN_CHUNK, :])


def _outproj_ffn(h, oa, ob, oc, wo, fn, wgu, wd):
    bsz, t_work, _ = h.shape
    tm = _row_tile(t_work)
    rows = lambda width: pl.BlockSpec((1, tm, width), lambda b, j: (b, j, 0))
    const = lambda shape: pl.BlockSpec(shape, lambda b, j: (0,) * len(shape),
                                       pipeline_mode=pl.Buffered(1))
    return pl.pallas_call(
        _ffn_kernel,
        grid=(bsz, t_work // tm),
        in_specs=[rows(D_MODEL), rows(256), rows(256), rows(512), const((D_MODEL, D_MODEL)),
                  const((1, D_MODEL)), const((D_MODEL, 2 * D_FF)), const((D_FF, D_MODEL))],
        out_specs=rows(D_MODEL),
        out_shape=jax.ShapeDtypeStruct(h.shape, F32),
        compiler_params=pltpu.CompilerParams(
            dimension_semantics=("parallel", "parallel"), vmem_limit_bytes=VMEM_LIMIT),
        name="outproj_ffn",
    )(h, oa, ob, oc, wo, fn, wgu, wd)


def kernel(x, meta_tokens, attn_norm, w_in, dsa_q_norm, dsa_k_norm, fox_q_norm, fox_k_norm,
           fox_f_bias, gla_gate_w2, gla_gate_b, gla_out_norm, w_out, ffn_norm, w_gate_up, w_down):
    bsz, seq, _ = x.shape
    n_tok = N_META + seq
    t_work = _work_len(n_tok)
    meta = jnp.broadcast_to(meta_tokens[None].astype(x.dtype), (bsz, N_META, D_MODEL))
    h = jnp.concatenate([meta, x, jnp.zeros((bsz, t_work - n_tok, D_MODEL), x.dtype)], axis=1)
    cos_t, sin_t = _rope_tables(t_work)
    g256 = _group_matrix()
    k_top = min(TOPK_MAX, seq // 4)
    for l in range(w_in.shape[0]):
        an, w_perm, vecs, w2_pad, sm_bias = _prep_layer_params(
            l, attn_norm, w_in, dsa_q_norm, dsa_k_norm, fox_q_norm, fox_k_norm, fox_f_bias,
            gla_gate_w2, gla_gate_b)
        (dq, dkk, dva, iq, ikk, fq, fk, fva, gq, gk, gv, gg, la, small, key_norms) = _inproj(
            h, an, w_perm, cos_t, sin_t, g256, vecs, w2_pad, sm_bias)
        small_t, fkb = _fox_prep(small, fk)
        oa = _dsa_attention(dq, dkk, dva, iq, ikk, small_t, key_norms, k_top)
        ob = _fox_attention(fq, fkb, fva, key_norms)
        oc = _gla(gq, gk, gv, gg, la, jnp.tile(gla_out_norm[l], N_HEADS)[None, :])
        h = _outproj_ffn(h, oa, ob, oc, w_out[l].astype(BF16), ffn_norm[l][None, :],
                         w_gate_up[l].astype(BF16), w_down[l].astype(BF16))
    return h[:, N_META:n_tok]
```

```python
import functools

import numpy as np
import jax
import jax.numpy as jnp
from jax import lax
from jax.experimental import pallas as pl
from jax.experimental.pallas import tpu as pltpu

F32 = jnp.float32
BF16 = jnp.bfloat16

D_MODEL = 1024
HEAD_DIM = 64
N_META = 16
ROPE_THETA = 500000.0
ROPE_DIM = HEAD_DIM // 4
ROPE_HALF = ROPE_DIM // 2
NEG = -1e30
EPS = 1e-6

N_HEADS = 4
TOPK_MAX = 256
GLA_DV = 128
GLA_RANK = 16
GLA_TAU = 16.0
GLA_CHUNK = 64
GLA_SUB = 16
D_FF = 2816

LOG2E = 1.4426950408889634
BIAS_TERMS = 3
KN_DSA, KN_FOX = 0, 1
NORM_MARGIN = 1.01
SAFE_LOG2_SPAN = 100.0
LANES = 128
SEQ_ALIGN = 512
VMEM_LIMIT = 56 * 1024 * 1024

_SLABS = (("dq", 256), ("dkk", 128), ("dva", 128), ("iq", 256), ("ikk", 128),
          ("fqa", 512), ("fkp", 512), ("fva", 512), ("gq", 256), ("gk", 256),
          ("gv", 512), ("gg", 512), ("small", 128))
_SLAB_OFF = {}
_off = 0
for _name, _width in _SLABS:
    _SLAB_OFF[_name] = (_off, _width)
    _off += _width
N_PROJ = _off
SM_IW, SM_FF, SM_GLR = 0, 4, 8


def _work_len(n_tok):
    return -(-n_tok // SEQ_ALIGN) * SEQ_ALIGN


def _row_tile(t_work):
    for cand in (768, 640, 512):
        if t_work % cand == 0:
            return cand
    raise ValueError(f"unsupported working length {t_work}")


def _dot(a, b):
    return jnp.dot(a, b, preferred_element_type=F32)


def _dot_nt(a, b):
    return lax.dot_general(a, b, (((1,), (1,)), ((), ())), preferred_element_type=F32)


def _dot_tn(a, b):
    return lax.dot_general(a, b, (((0,), (0,)), ((), ())), preferred_element_type=F32)


def _split3(x):
    h1 = x.astype(BF16)
    r1 = x - h1.astype(F32)
    h2 = r1.astype(BF16)
    h3 = (r1 - h2.astype(F32)).astype(BF16)
    return h1, h2, h3


def _log_sigmoid(x):
    return jnp.minimum(x, 0.0) - jnp.log1p(jnp.exp(-jnp.abs(x)))


def _silu(x):
    return x / (1.0 + jnp.exp(-x))


def _group_rms(y, gmat, gain):
    yy = y * y
    hi = yy.astype(BF16)
    lo = (yy - hi.astype(F32)).astype(BF16)
    ss = _dot(hi, gmat) + _dot(lo, gmat)
    return y * lax.rsqrt(ss * (1.0 / HEAD_DIM) + EPS) * gain


def _rope(y, cos, sin):
    width = y.shape[-1]
    lane = lax.broadcasted_iota(jnp.int32, y.shape, 1) % HEAD_DIM
    upper = pltpu.roll(y, width - ROPE_HALF, axis=1)
    lower = pltpu.roll(y, ROPE_HALF, axis=1)
    partner = jnp.where(lane < ROPE_HALF, upper, lower)
    return y * cos + partner * sin


def _inproj_kernel(x_ref, an_ref, w_ref, cos_ref, sin_ref, g256_ref, vec_ref, w2_ref, sm_ref,
                   dq_ref, dkk_ref, dva_ref, iq_ref, ikk_ref, fq_ref, fk_ref, fva_ref,
                   gq_ref, gk_ref, gv_ref, gg_ref, la_ref, small_ref, kn_ref):
    x = x_ref[0]
    ms = jnp.mean(x * x, axis=-1, keepdims=True)
    a = (x * lax.rsqrt(ms + EPS) * an_ref[...]).astype(BF16)

    def proj(name):
        off, width = _SLAB_OFF[name]
        return _dot(a, w_ref[:, off:off + width])

    def ones_in_upper_half(y):
        lane = lax.broadcasted_iota(jnp.int32, y.shape, 1) % LANES
        return jnp.where(lane < HEAD_DIM, y, 1.0)

    cos = cos_ref[...]
    sin = sin_ref[...]
    g256 = g256_ref[...]
    g128 = g256[:LANES, :LANES]
    dqn, dkn, fqn, fkn = vec_ref[0:1, :], vec_ref[1:2, :LANES], vec_ref[2:3, :], vec_ref[3:4, :]
    scale = HEAD_DIM ** -0.5

    def max_sq_norm(k):
        kf = k.astype(F32)
        return jnp.max(_dot((kf * kf).astype(BF16), g128), axis=0, keepdims=True)

    dq_ref[0] = (_rope(_group_rms(proj("dq"), g256, dqn), cos, sin) * (scale * LOG2E)).astype(BF16)
    dkk = _rope(_group_rms(proj("dkk"), g128, dkn), cos[:, :LANES], sin[:, :LANES]).astype(BF16)
    dkk_ref[0] = dkk
    key_norms = [max_sq_norm(dkk)]
    dva_ref[0] = ones_in_upper_half(proj("dva")).astype(BF16)
    iq_ref[0] = (_rope(proj("iq"), cos, sin) * scale).astype(BF16)
    ikk_ref[0] = _rope(proj("ikk"), cos[:, :LANES], sin[:, :LANES]).astype(BF16)
    fq, fk = proj("fqa"), proj("fkp")
    lane = lax.broadcasted_iota(jnp.int32, (fq.shape[0], LANES), 1)
    bias_lanes = (lane >= HEAD_DIM) & (lane < HEAD_DIM + BIAS_TERMS)
    for h in range(N_HEADS):
        hs = slice(h * LANES, (h + 1) * LANES)
        q_h = _group_rms(fq[:, hs], g128, fqn[:, :LANES]) * (scale * LOG2E)
        fq_ref[0, :, hs] = jnp.where(bias_lanes, 1.0, q_h).astype(BF16)
        k_h = _group_rms(fk[:, hs], g128, fkn[:, :LANES]).astype(BF16)
        fk_ref[0, :, hs] = k_h
        key_norms.append(max_sq_norm(k_h))
    kn_ref[0, 0] = jnp.concatenate(key_norms + [jnp.zeros((8 - len(key_norms), LANES), F32)], axis=0)
    fva_ref[0] = ones_in_upper_half(proj("fva")).astype(BF16)
    gq_ref[0] = proj("gq") * scale
    gk_ref[0] = proj("gk")
    gv_ref[0] = proj("gv").astype(BF16)
    gg_ref[0] = proj("gg")

    small = proj("small")
    lane = lax.broadcasted_iota(jnp.int32, small.shape, 1)
    small_ref[0] = jnp.where(lane < SM_FF, small * (N_HEADS ** -0.5),
                             _log_sigmoid(small + sm_ref[0:1, :]))
    gate = _dot(small.astype(BF16), w2_ref[...]) + vec_ref[4:5, :]
    la_ref[0] = _log_sigmoid(gate) * (1.0 / GLA_TAU)


def _inproj(h, an, w_perm, cos_t, sin_t, g256, vecs, w2_pad, sm_bias):
    bsz, t_work, _ = h.shape
    tm = _row_tile(t_work)
    grid = (bsz, t_work // tm)

    def rows(width, dtype):
        return (jax.ShapeDtypeStruct((bsz, t_work, width), dtype),
                pl.BlockSpec((1, tm, width), lambda b, j: (b, j, 0)))

    outs = [rows(256, BF16), rows(128, BF16), rows(128, BF16), rows(256, BF16), rows(128, BF16),
            rows(512, BF16), rows(512, BF16), rows(512, BF16), rows(256, F32), rows(256, F32),
            rows(512, BF16), rows(512, F32), rows(256, F32), rows(128, F32),
            (jax.ShapeDtypeStruct((bsz, t_work // tm, 8, LANES), F32),
             pl.BlockSpec((1, 1, 8, LANES), lambda b, j: (b, j, 0, 0)))]
    const = lambda shape: pl.BlockSpec(shape, lambda b, j: (0,) * len(shape))
    return pl.pallas_call(
        _inproj_kernel,
        grid=grid,
        in_specs=[pl.BlockSpec((1, tm, D_MODEL), lambda b, j: (b, j, 0)),
                  const((1, D_MODEL)), const((D_MODEL, N_PROJ)),
                  pl.BlockSpec((tm, 256), lambda b, j: (j, 0)),
                  pl.BlockSpec((tm, 256), lambda b, j: (j, 0)),
                  const((256, 256)), const((8, 256)), const((LANES, 256)), const((8, LANES))],
        out_specs=[o[1] for o in outs],
        out_shape=[o[0] for o in outs],
        compiler_params=pltpu.CompilerParams(
            dimension_semantics=("parallel", "parallel"), vmem_limit_bytes=VMEM_LIMIT),
        name="inproj",
    )(h, an, w_perm, cos_t, sin_t, g256, vecs, w2_pad, sm_bias)


def _prep_layer_params(l, attn_norm, w_in, dsa_q_norm, dsa_k_norm, fox_q_norm, fox_k_norm,
                       fox_f_bias, gla_gate_w2, gla_gate_b):
    w = w_in[l]
    splits = np.cumsum([256, 64, 64, 256, 4, 64, 256, 256, 256, 4, 256, 256, 512, 512, 16])[:-1]
    (dq, dk, dv, iq, iw, ik, fq, fk, fv, ff, gq, gk, gv, gg, glr) = jnp.split(w, splits, axis=1)
    small = jnp.concatenate(
        [iw, ff, glr, jnp.zeros((D_MODEL, LANES - 4 - 4 - GLA_RANK), w.dtype)], axis=1)
    z64 = jnp.zeros((D_MODEL, HEAD_DIM), w.dtype)
    per_head = lambda t: [part for h in range(N_HEADS)
                          for part in (t[:, h * HEAD_DIM:(h + 1) * HEAD_DIM], z64)]
    w_perm = jnp.concatenate([dq, dk, dk, dv, z64, iq, ik, ik, *per_head(fq), *per_head(fk),
                              *per_head(fv), gq, gk, gv, gg, small], axis=1).astype(BF16)
    tile4 = lambda g: jnp.tile(g, N_HEADS)
    vecs = jnp.zeros((8, 256), F32)
    vecs = vecs.at[0].set(tile4(dsa_q_norm[l])).at[1].set(tile4(dsa_k_norm[l]))
    vecs = vecs.at[2].set(tile4(fox_q_norm[l])).at[3].set(tile4(fox_k_norm[l]))
    vecs = vecs.at[4].set(gla_gate_b[l])
    w2_pad = jnp.zeros((LANES, 256), F32).at[SM_GLR:SM_GLR + GLA_RANK].set(gla_gate_w2[l]).astype(BF16)
    sm_bias = jnp.zeros((8, LANES), F32).at[0, SM_FF:SM_FF + N_HEADS].set(fox_f_bias[l])
    return attn_norm[l][None, :], w_perm, vecs, w2_pad, sm_bias


def _rope_tables(t_work):
    inv = jnp.power(ROPE_THETA, -jnp.arange(ROPE_HALF, dtype=F32) * 2.0 / ROPE_DIM)
    ang = jnp.arange(t_work).astype(F32)[:, None] * inv[None, :]
    cos, sin = jnp.cos(ang), jnp.sin(ang)
    rest = HEAD_DIM - ROPE_DIM
    cos64 = jnp.concatenate([cos, cos, jnp.ones((t_work, rest), F32)], axis=1)
    sin64 = jnp.concatenate([-sin, sin, jnp.zeros((t_work, rest), F32)], axis=1)
    return jnp.tile(cos64, (1, N_HEADS)), jnp.tile(sin64, (1, N_HEADS))


def _group_matrix():
    idx = np.arange(256) // HEAD_DIM
    return jnp.asarray((idx[:, None] == idx[None, :]).astype(np.float32), dtype=BF16)


DSA_TQ = 256
DSA_TK = 512
KEY_NEG_INF = -2139095041
KEY_NEG_ZERO = -1
KEY_NEG_MIN_NORMAL = -8388609
SEARCH_FEW = 4.0
UNCHECKED_PROBES = 16
GUIDED_PROBES = 64
MAX_PROBES = GUIDED_PROBES + 40
COUNT_ROWS = 128


def _key_to_f32(key):
    bits = key ^ ((key >> 31) & 0x7FFFFFFF)
    return lax.bitcast_convert_type(bits, F32)


def _f32_to_key(value):
    bits = lax.bitcast_convert_type(value, jnp.int32)
    return bits ^ ((bits >> 31) & 0x7FFFFFFF)


def _head_lane_mask(shape, head):
    lane = lax.broadcasted_iota(jnp.int32, shape, 1)
    return (lane < HEAD_DIM) if head % 2 == 0 else (lane >= HEAD_DIM)


def _masked_heads(slabs):
    return [jnp.where(_head_lane_mask(slabs[h // 2].shape, h), slabs[h // 2],
                      jnp.zeros_like(slabs[h // 2])) for h in range(N_HEADS)]


def _softmax_step(s, m, acc, v_aug):
    m_new = jnp.maximum(m, jnp.max(s, axis=-1, keepdims=True))
    p = jnp.exp2(s - m_new)
    return m_new, acc * jnp.exp2(m - m_new) + _dot(p.astype(BF16), v_aug)


def _normalise_heads(accs):
    outs = [acc / pltpu.roll(acc, HEAD_DIM, axis=1) for acc in accs]
    lane = lax.broadcasted_iota(jnp.int32, outs[0].shape, 1)
    return [jnp.where(lane < HEAD_DIM, outs[2 * p], pltpu.roll(outs[2 * p + 1], HEAD_DIM, axis=1))
            for p in range(2)]


def _dsa_kernel(k_top, dq_ref, kk_ref, va_ref, iq_ref, ik_ref, wt_ref, tri_ref, kn_ref, o_ref,
                s_ref):
    i = pl.program_id(1)
    q0 = i * DSA_TQ
    n_tiles = (q0 + DSA_TQ + DSA_TK - 1) // DSA_TK
    last = n_tiles - 1
    key = lax.broadcasted_iota(jnp.int32, (DSA_TK, DSA_TQ), 0)
    qry = q0 + lax.broadcasted_iota(jnp.int32, (DSA_TK, DSA_TQ), 1)
    head = lambda x, h: x[:, h * DSA_TQ:(h + 1) * DSA_TQ]

    iq_all = jnp.concatenate(
        _masked_heads([iq_ref[0, :, 0:LANES], iq_ref[0, :, LANES:2 * LANES]]), axis=0)
    w_h = [wt_ref[0, SM_IW + h:SM_IW + h + 1, :] for h in range(N_HEADS)]

    def score_tile(j, carry):
        k0 = pl.multiple_of(j * DSA_TK, DSA_TK)
        dots = jnp.maximum(_dot_nt(ik_ref[0, pl.ds(k0, DSA_TK), :], iq_all), 0.0)
        s = w_h[0] * head(dots, 0)
        for h in range(1, N_HEADS):
            s = s + w_h[h] * head(dots, h)
        s = jnp.where(k0 + key <= qry, s, NEG)
        s_ref[pl.ds(k0, DSA_TK), :] = s
        return jnp.maximum(carry, jnp.max(s, axis=0, keepdims=True))

    row_max = lax.fori_loop(0, n_tiles, score_tile, jnp.full((1, DSA_TQ), NEG, F32))

    kf = jnp.float32(k_top)

    def count_above(t):
        tb = jnp.broadcast_to(t, (COUNT_ROWS, DSA_TQ))

        def body(j, acc):
            k0 = pl.multiple_of(j * DSA_TK, DSA_TK)
            for part in range(DSA_TK // COUNT_ROWS):
                s = s_ref[pl.ds(k0 + part * COUNT_ROWS, COUNT_ROWS), :]
                acc = acc + jnp.where(s > tb, 1.0, 0.0)
            return acc

        acc = lax.fori_loop(0, n_tiles, body, jnp.zeros((COUNT_ROWS, DSA_TQ), F32))
        return jnp.sum(acc, axis=0, keepdims=True)

    def midpoint(lo, hi):
        return (lo >> 1) + (hi >> 1) + (lo & hi & 1)

    def converged(lo, hi):
        return (midpoint(lo, hi) == lo) | ((lo >= KEY_NEG_MIN_NORMAL) & (hi <= 0))

    col1 = lambda value, dtype: jnp.full((1, DSA_TQ), value, dtype)

    def probe_once(it, state):
        lo, hi, clo, chi, wlo, whi, side = state
        f_lo, f_hi = _key_to_f32(lo), _key_to_f32(hi)
        g_lo = (clo - (kf - 0.5)) * wlo
        g_hi = ((kf - 0.5) - chi) * whi
        halve = (clo - chi <= SEARCH_FEW) | (col1(it % 4, jnp.int32) == 3)
        guess = _f32_to_key(f_lo + (f_hi - f_lo) * jnp.where(halve, 0.5, g_lo / (g_lo + g_hi)))
        fixed = col1(jnp.where(it == 0, KEY_NEG_ZERO, KEY_NEG_MIN_NORMAL), jnp.int32)
        early = col1(it, jnp.int32) < 2
        guided = col1(it, jnp.int32) < GUIDED_PROBES
        probe = jnp.where((guess > lo) & (guess < hi) & guided & ~early, guess, midpoint(lo, hi))
        probe = jnp.where(early & (lo < fixed) & (fixed < hi), fixed, probe)
        probe = jnp.where(converged(lo, hi), lo, probe)
        c = count_above(_key_to_f32(probe))
        live = probe != lo
        up = live & (c >= kf)
        down = live & (c <= kf)
        wlo = jnp.where(down & (side < 0), wlo * 0.5, jnp.where(up, 1.0, wlo))
        whi = jnp.where(up & (side > 0), whi * 0.5, jnp.where(down, 1.0, whi))
        side = jnp.where(up, 1, jnp.where(down, -1, side))
        lo, clo = jnp.where(up, probe, lo), jnp.where(up, c, clo)
        hi, chi = jnp.where(down, probe, hi), jnp.where(down, c, chi)
        return lo, hi, clo, chi, wlo, whi, side

    def search_cond(carry):
        it, pending = carry[0], carry[1]
        return (pending > 0) & (it < MAX_PROBES)

    def search_body(carry):
        it, state = carry[0], carry[2]
        state = probe_once(it + 1, probe_once(it, state))
        return it + 2, jnp.max(jnp.where(converged(state[0], state[1]), 0, 1)), state

    n_swept = (n_tiles * DSA_TK).astype(F32)
    state = (col1(KEY_NEG_INF, jnp.int32), _f32_to_key(row_max),
             jnp.broadcast_to(n_swept, (1, DSA_TQ)), col1(0.0, F32), col1(1.0, F32), col1(1.0, F32),
             col1(0, jnp.int32))
    state = lax.fori_loop(0, UNCHECKED_PROBES, probe_once, state)
    state = lax.while_loop(search_cond, search_body,
                           (jnp.int32(UNCHECKED_PROBES), jnp.int32(1), state))[2]
    thr = _key_to_f32(state[1])
    n_ties = kf - state[3]

    to_column = lambda r: jnp.broadcast_to(r, (8, DSA_TQ)).T[:, 0:1]
    thr_c, ties_c = to_column(thr), to_column(n_ties)
    q_all = jnp.concatenate(
        _masked_heads([dq_ref[0, :, 0:LANES], dq_ref[0, :, LANES:2 * LANES]]), axis=0)
    tri = tri_ref[...]
    tri_lo = jnp.where(lax.broadcasted_iota(jnp.int32, (LANES, LANES), 0)
                       >= lax.broadcasted_iota(jnp.int32, (LANES, LANES), 1), 1.0, 0.0).astype(BF16)
    qrow = q0 + lax.broadcasted_iota(jnp.int32, (DSA_TQ, DSA_TK), 0)
    kcol = lax.broadcasted_iota(jnp.int32, (DSA_TQ, DSA_TK), 1)
    n_blocks = DSA_TK // LANES

    reach = _logit_reach(q_all, jnp.max(kn_ref[0], axis=0)[KN_DSA:KN_DSA + 1, 0:1])

    def selected_logits(j, seen, causal):
        k0 = pl.multiple_of(j * DSA_TK, DSA_TK)
        s = s_ref[pl.ds(k0, DSA_TK), :].T
        tie = s == thr_c
        tie_b = jnp.where(tie, 1.0, 0.0).astype(BF16)
        local = [_dot(tie_b[:, b * LANES:(b + 1) * LANES], tri) for b in range(n_blocks)]
        ranks = []
        for b in range(n_blocks):
            ranks.append(local[b] + seen)
            seen = seen + local[b][:, LANES - 1:LANES]
        sel = (s > thr_c) | (tie & (jnp.concatenate(ranks, axis=1) <= ties_c))
        if causal:
            sel = sel & (k0 + kcol <= qrow)
        logits = _dot_nt(q_all, kk_ref[0, pl.ds(k0, DSA_TK), :])
        logits = jnp.where(sel[None], logits.reshape(N_HEADS, DSA_TQ, DSA_TK), NEG)
        return seen, logits.reshape(N_HEADS * DSA_TQ, DSA_TK), va_ref[0, pl.ds(k0, DSA_TK), :]

    def capped(j, carry, causal):
        seen, acc = carry
        k0 = pl.multiple_of(j * DSA_TK, DSA_TK)
        s = s_ref[pl.ds(k0, DSA_TK), :]
        tie = s == thr
        tie_b = jnp.where(tie, 1.0, 0.0).astype(BF16)
        ranks = []
        for b in range(n_blocks):
            local = _dot(tri_lo, tie_b[b * LANES:(b + 1) * LANES, :])
            ranks.append(local + seen)
            seen = seen + local[LANES - 1:LANES, :]
        sel = (s > thr) | (tie & (jnp.concatenate(ranks, axis=0) <= n_ties))
        if causal:
            sel = sel & (k0 + key <= qry)
        keep = jnp.where(sel, 1.0, 0.0).astype(BF16).T
        logits = _dot_nt(q_all, kk_ref[0, pl.ds(k0, DSA_TK), :])
        p = jnp.exp2(logits - reach).astype(BF16).reshape(N_HEADS, DSA_TQ, DSA_TK) * keep[None]
        return seen, acc + _dot(p.reshape(N_HEADS * DSA_TQ, DSA_TK), va_ref[0, pl.ds(k0, DSA_TK), :])

    def online(j, carry, causal):
        seen, logits, va_t = selected_logits(j, carry[0], causal)
        return (seen,) + _softmax_step(logits, carry[1], carry[2], va_t)

    def sweep(step, *stats):
        carry = stats + (jnp.zeros((N_HEADS * DSA_TQ, LANES), F32),)
        carry = lax.fori_loop(0, last, functools.partial(step, causal=False), carry)
        return step(last, carry, causal=True)[-1]

    acc = lax.cond(2.0 * jnp.max(reach) <= SAFE_LOG2_SPAN,
                   functools.partial(sweep, capped, jnp.zeros((1, DSA_TQ), F32)),
                   functools.partial(sweep, online, jnp.zeros((DSA_TQ, 1), F32),
                                     jnp.full((N_HEADS * DSA_TQ, 1), NEG, F32)))
    o_ref[0] = jnp.concatenate(
        _normalise_heads([acc[h * DSA_TQ:(h + 1) * DSA_TQ] for h in range(N_HEADS)]),
        axis=1).astype(o_ref.dtype)


def _dsa_attention(dq, dkk, dva, iq, ikk, small_t, key_norms, k_top):
    bsz, t_work, _ = dq.shape
    tri = jnp.asarray(np.triu(np.ones((LANES, LANES), np.float32)), dtype=BF16)
    tile = lambda width: pl.BlockSpec((1, DSA_TQ, width), lambda b, i: (b, i, 0))
    full = pl.BlockSpec((1, t_work, LANES), lambda b, i: (b, 0, 0), pipeline_mode=pl.Buffered(1))
    return pl.pallas_call(
        functools.partial(_dsa_kernel, k_top),
        grid=(bsz, t_work // DSA_TQ),
        in_specs=[tile(256), full, full, tile(256), full,
                  pl.BlockSpec((1, 8, DSA_TQ), lambda b, i: (b, 0, i)),
                  pl.BlockSpec((LANES, LANES), lambda b, i: (0, 0)),
                  pl.BlockSpec((1,) + key_norms.shape[1:], lambda b, i: (b, 0, 0, 0))],
        out_specs=tile(256),
        out_shape=jax.ShapeDtypeStruct((bsz, t_work, 256), BF16),
        scratch_shapes=[pltpu.VMEM((t_work, DSA_TQ), F32)],
        compiler_params=pltpu.CompilerParams(
            dimension_semantics=("parallel", "parallel"), vmem_limit_bytes=VMEM_LIMIT),
        name="dsa_attention",
    )(dq, dkk, dva, iq, ikk, small_t, tri, key_norms)


FOX_TQ = 512
FOX_TK = 512
CUM_T = 256


def _fox_prep_kernel(x_ref, tri_ref, place_ref, k_ref, xt_ref, kb_ref, carry_ref):
    @pl.when(pl.program_id(1) == 0)
    def _():
        carry_ref[...] = jnp.zeros_like(carry_ref)

    tri = tri_ref[...]
    x = x_ref[0]
    h1, h2, h3 = _split3(x)
    c = _dot(tri, h1) + _dot(tri, h2) + _dot(tri, h3) + carry_ref[0:1, :]
    carry_ref[...] = jnp.broadcast_to(c[CUM_T - 1:CUM_T, :], carry_ref.shape)
    xt_ref[0] = x.T[0:8, :]
    terms = _split3(c * -LOG2E)
    bias = sum(_dot(terms[t], place_ref[t]) for t in range(BIAS_TERMS))
    lane = lax.broadcasted_iota(jnp.int32, bias.shape, 1) % LANES
    kb_ref[0] = jnp.where((lane >= HEAD_DIM) & (lane < HEAD_DIM + BIAS_TERMS),
                          bias.astype(BF16), k_ref[0])


def _fox_prep(small, fk):
    bsz, t_work, width = fk.shape
    tri = jnp.asarray(np.tril(np.ones((CUM_T, CUM_T), np.float32)), dtype=BF16)
    place = np.zeros((BIAS_TERMS, LANES, width), np.float32)
    for t in range(BIAS_TERMS):
        for h in range(N_HEADS):
            place[t, SM_FF + h, h * LANES + HEAD_DIM + t] = 1.0
    rows = lambda w: pl.BlockSpec((1, CUM_T, w), lambda b, j: (b, j, 0))
    return pl.pallas_call(
        _fox_prep_kernel,
        grid=(bsz, t_work // CUM_T),
        in_specs=[rows(LANES), pl.BlockSpec((CUM_T, CUM_T), lambda b, j: (0, 0)),
                  pl.BlockSpec((BIAS_TERMS, LANES, width), lambda b, j: (0, 0, 0)), rows(width)],
        out_specs=[pl.BlockSpec((1, 8, CUM_T), lambda b, j: (b, 0, j)), rows(width)],
        out_shape=[jax.ShapeDtypeStruct((bsz, 8, t_work), F32),
                   jax.ShapeDtypeStruct(fk.shape, BF16)],
        scratch_shapes=[pltpu.VMEM((8, LANES), F32)],
        compiler_params=pltpu.CompilerParams(dimension_semantics=("parallel", "arbitrary")),
        name="fox_prep",
    )(small, tri, jnp.asarray(place, dtype=BF16), fk)


def _logit_reach(q, kmax_sq):
    qf = q.astype(F32)
    return jnp.sqrt(jnp.sum(qf * qf, axis=-1, keepdims=True) * kmax_sq) * NORM_MARGIN


def _fox_kernel(q_ref, k_ref, v_ref, kn_ref, o_ref):
    i = pl.program_id(1)
    q0 = pl.multiple_of(i * FOX_TQ, FOX_TQ)
    n_full = q0 // FOX_TK
    row = q0 + lax.broadcasted_iota(jnp.int32, (FOX_TQ, FOX_TK), 0)
    col = n_full * FOX_TK + lax.broadcasted_iota(jnp.int32, (FOX_TQ, FOX_TK), 1)
    heads = [slice(h * LANES, (h + 1) * LANES) for h in range(N_HEADS)]

    lane = lax.broadcasted_iota(jnp.int32, (FOX_TQ, LANES), 1)
    bias_lanes = (lane >= HEAD_DIM) & (lane < HEAD_DIM + BIAS_TERMS)
    kmax_sq = jnp.max(kn_ref[0], axis=0)
    caps, span = [], jnp.float32(0.0)
    for h in range(N_HEADS):
        q_h = q_ref[0, :, heads[h]]
        reach = _logit_reach(jnp.where(lane < HEAD_DIM, q_h, jnp.zeros_like(q_h)),
                             kmax_sq[KN_FOX + h:KN_FOX + h + 1, 0:1])
        own = k_ref[0, pl.ds(q0, FOX_TQ), heads[h]].astype(F32)
        caps.append(reach + jnp.sum(jnp.where(bias_lanes, own, 0.0), axis=-1, keepdims=True))
        span = jnp.maximum(span, 2.0 * jnp.max(reach))

    def logits(j, h, diag):
        k0 = pl.multiple_of(j * FOX_TK, FOX_TK)
        s = _dot_nt(q_ref[0, :, heads[h]], k_ref[0, pl.ds(k0, FOX_TK), heads[h]])
        return jnp.where(col <= row, s, NEG) if diag else s

    def values(j, h):
        return v_ref[0, pl.ds(pl.multiple_of(j * FOX_TK, FOX_TK), FOX_TK), heads[h]]

    def capped(j, accs, diag):
        return tuple(accs[h] + _dot(jnp.exp2(logits(j, h, diag) - caps[h]).astype(BF16), values(j, h))
                     for h in range(N_HEADS))

    def online(j, carry, diag):
        return tuple(_softmax_step(logits(j, h, diag), *carry[h], values(j, h))
                     for h in range(N_HEADS))

    def capped_sweep():
        accs = tuple(jnp.zeros((FOX_TQ, LANES), F32) for _ in range(N_HEADS))
        accs = lax.fori_loop(0, n_full, functools.partial(capped, diag=False), accs)
        return capped(n_full, accs, diag=True)

    def online_sweep():
        carry = tuple((jnp.full((FOX_TQ, 1), NEG, F32), jnp.zeros((FOX_TQ, LANES), F32))
                      for _ in range(N_HEADS))
        carry = lax.fori_loop(0, n_full, functools.partial(online, diag=False), carry)
        return tuple(acc for _, acc in online(n_full, carry, diag=True))

    accs = lax.cond(span <= SAFE_LOG2_SPAN, capped_sweep, online_sweep)
    o_ref[0] = jnp.concatenate(_normalise_heads(list(accs)), axis=1).astype(o_ref.dtype)


def _fox_attention(fqa, fkb, fva, key_norms):
    bsz, t_work, width = fqa.shape
    full = pl.BlockSpec((1, t_work, width), lambda b, i: (b, 0, 0), pipeline_mode=pl.Buffered(1))
    return pl.pallas_call(
        _fox_kernel,
        grid=(bsz, t_work // FOX_TQ),
        in_specs=[pl.BlockSpec((1, FOX_TQ, width), lambda b, i: (b, i, 0)), full, full,
                  pl.BlockSpec((1,) + key_norms.shape[1:], lambda b, i: (b, 0, 0, 0))],
        out_specs=pl.BlockSpec((1, FOX_TQ, 2 * LANES), lambda b, i: (b, i, 0)),
        out_shape=jax.ShapeDtypeStruct((bsz, t_work, 2 * LANES), BF16),
        compiler_params=pltpu.CompilerParams(
            dimension_semantics=("parallel", "parallel"), vmem_limit_bytes=VMEM_LIMIT),
        name="fox_attention",
    )(fqa, fkb, fva, key_norms)


GLA_TG = 256
GLA_NSUB = GLA_CHUNK // GLA_SUB


def _gla_kernel(q_ref, k_ref, v_ref, g_ref, la_ref, tri_ref, e_ref, gn_ref, o_ref, st_ref):
    @pl.when(pl.program_id(1) == 0)
    def _():
        st_ref[...] = jnp.zeros_like(st_ref)

    tri = tri_ref[...]
    emat = e_ref[...]
    lane = lax.broadcasted_iota(jnp.int32, (GLA_CHUNK, LANES), 1)
    rowblk = lax.broadcasted_iota(jnp.int32, (GLA_CHUNK, LANES), 0) // GLA_SUB
    tblk = lax.broadcasted_iota(jnp.int32, (GLA_CHUNK, GLA_CHUNK), 0) // GLA_SUB
    sblk = lax.broadcasted_iota(jnp.int32, (GLA_CHUNK, GLA_CHUNK), 1) // GLA_SUB
    trow = lax.broadcasted_iota(jnp.int32, (GLA_SUB, 256), 0)

    def chunk(c, carry):
        r0 = pl.multiple_of(c * GLA_CHUNK, GLA_CHUNK)
        rows = pl.ds(r0, GLA_CHUNK)
        h1, h2, h3 = _split3(la_ref[0, rows, :])
        b = _dot(tri, h1) + _dot(tri, h2) + _dot(tri, h3)
        q = q_ref[0, rows, :]
        k = k_ref[0, rows, :]
        v = v_ref[0, rows, :]
        b_last = b[GLA_CHUNK - 1:GLA_CHUNK, :]
        qd = q * jnp.exp(b)
        kd = (k * jnp.exp(b_last - b)).astype(BF16)
        starts = [jnp.zeros((1, 256), F32)] + [b[GLA_SUB * i - 1:GLA_SUB * i, :]
                                               for i in range(1, GLA_NSUB)]
        bsel = jnp.concatenate([jnp.broadcast_to(s, (GLA_SUB, 256)) for s in starts], axis=0)
        qn = q * jnp.exp(b - bsel)

        diag = []
        for i in range(GLA_NSUB):
            rs = slice(GLA_SUB * i, GLA_SUB * (i + 1))
            b_i, q_i, k_i = b[rs], q[rs], k[rs]
            v_i = v[rs].astype(F32)
            ps = []
            for s in range(GLA_SUB):
                d = jnp.exp(jnp.minimum(b_i - b_i[s:s + 1], 0.0))
                ps.append(jnp.where(trow >= s, q_i * d * k_i[s:s + 1], 0.0).astype(BF16))
            r = _dot(jnp.concatenate(ps, axis=0), emat)
            od = r[0:GLA_SUB] * v_i[0:1]
            for s in range(1, GLA_SUB):
                od = od + r[GLA_SUB * s:GLA_SUB * (s + 1)] * v_i[s:s + 1]
            diag.append(od)
        o_diag = jnp.concatenate(diag, axis=0)

        for slab in range(2):
            ls = slice(slab * LANES, (slab + 1) * LANES)
            qn_s, k_s, b_s = qn[:, ls], k[:, ls], b[:, ls]
            khat = jnp.concatenate(
                [(k_s * jnp.exp(jnp.minimum(starts[i][:, ls] - b_s, 0.0))).astype(BF16)
                 for i in range(1, GLA_NSUB)], axis=1)
            for half in range(2):
                head = 2 * slab + half
                hs = slice(head * GLA_DV, (head + 1) * GLA_DV)
                in_head = (lane < HEAD_DIM) if half == 0 else (lane >= HEAD_DIM)
                qm = jnp.where(in_head, qn_s, 0.0)
                qhat = jnp.concatenate([jnp.where(rowblk == i, qm, 0.0).astype(BF16)
                                        for i in range(1, GLA_NSUB)], axis=1)
                att = jnp.where(sblk < tblk, _dot_nt(qhat, khat), 0.0)
                v_h = v[:, hs]
                st = st_ref[head]
                o = (_dot_nt(jnp.where(in_head, qd[:, ls], 0.0).astype(BF16), st.astype(BF16))
                     + _dot(att.astype(BF16), v_h) + o_diag[:, hs])
                st_ref[head] = st * jnp.exp(b_last[:, ls]) + _dot_tn(v_h, kd[:, ls])
                y = o * lax.rsqrt(jnp.mean(o * o, axis=-1, keepdims=True) + EPS) * gn_ref[:, hs]
                o_ref[0, rows, hs] = (y * _silu(g_ref[0, rows, hs])).astype(o_ref.dtype)
        return carry

    lax.fori_loop(0, GLA_TG // GLA_CHUNK, chunk, 0)


def _gla(gq, gk, gv, gg, la, gain):
    bsz, t_work, _ = gq.shape
    tri = jnp.asarray(np.tril(np.ones((GLA_CHUNK, GLA_CHUNK), np.float32)), dtype=BF16)
    emat = jnp.asarray(
        (np.arange(256)[:, None] // HEAD_DIM == np.arange(512)[None, :] // GLA_DV).astype(np.float32),
        dtype=BF16)
    rows = lambda width: pl.BlockSpec((1, GLA_TG, width), lambda b, j: (b, j, 0))
    const = lambda shape: pl.BlockSpec(shape, lambda b, j: (0,) * len(shape))
    return pl.pallas_call(
        _gla_kernel,
        grid=(bsz, t_work // GLA_TG),
        in_specs=[rows(256), rows(256), rows(512), rows(512), rows(256),
                  const((GLA_CHUNK, GLA_CHUNK)), const((256, 512)), const((1, 512))],
        out_specs=rows(512),
        out_shape=jax.ShapeDtypeStruct((bsz, t_work, 512), BF16),
        scratch_shapes=[pltpu.VMEM((N_HEADS, GLA_DV, LANES), F32)],
        compiler_params=pltpu.CompilerParams(
            dimension_semantics=("parallel", "arbitrary"), vmem_limit_bytes=VMEM_LIMIT),
        name="gla",
    )(gq, gk, gv, gg, la, tri, emat, gain)


FFN_CHUNK = 256


def _ffn_kernel(h_ref, oa_ref, ob_ref, oc_ref, wo_ref, fn_ref, wgu_ref, wd_ref, out_ref):
    h1 = (h_ref[0] + _dot(oa_ref[0], wo_ref[0:256, :]) + _dot(ob_ref[0], wo_ref[256:512, :])
          + _dot(oc_ref[0], wo_ref[512:1024, :]))
    ms = jnp.mean(h1 * h1, axis=-1, keepdims=True)
    f = (h1 * lax.rsqrt(ms + EPS) * fn_ref[...]).astype(BF16)
    out_ref[0] = h1
    for c in range(0, D_FF, FFN_CHUNK):
        gate = _dot(f, wgu_ref[:, c:c + FFN_CHUNK])
        up = _dot(f, wgu_ref[:, D_FF + c:D_FF + c + FFN_CHUNK])
        out_ref[0] += _dot((_silu(gate) * up).astype(BF16), wd_ref[c:c + FFN_CHUNK, :])


def _outproj_ffn(h, oa, ob, oc, wo, fn, wgu, wd):
    bsz, t_work, _ = h.shape
    tm = _row_tile(t_work)
    rows = lambda width: pl.BlockSpec((1, tm, width), lambda b, j: (b, j, 0))
    const = lambda shape: pl.BlockSpec(shape, lambda b, j: (0,) * len(shape),
                                       pipeline_mode=pl.Buffered(1))
    return pl.pallas_call(
        _ffn_kernel,
        grid=(bsz, t_work // tm),
        in_specs=[rows(D_MODEL), rows(256), rows(256), rows(512), const((D_MODEL, D_MODEL)),
                  const((1, D_MODEL)), const((D_MODEL, 2 * D_FF)), const((D_FF, D_MODEL))],
        out_specs=rows(D_MODEL),
        out_shape=jax.ShapeDtypeStruct(h.shape, F32),
        compiler_params=pltpu.CompilerParams(
            dimension_semantics=("parallel", "parallel"), vmem_limit_bytes=VMEM_LIMIT),
        name="outproj_ffn",
    )(h, oa, ob, oc, wo, fn, wgu, wd)


def kernel(x, meta_tokens, attn_norm, w_in, dsa_q_norm, dsa_k_norm, fox_q_norm, fox_k_norm,
           fox_f_bias, gla_gate_w2, gla_gate_b, gla_out_norm, w_out, ffn_norm, w_gate_up, w_down):
    bsz, seq, _ = x.shape
    n_tok = N_META + seq
    t_work = _work_len(n_tok)
    meta = jnp.broadcast_to(meta_tokens[None].astype(x.dtype), (bsz, N_META, D_MODEL))
    h = jnp.concatenate([meta, x, jnp.zeros((bsz, t_work - n_tok, D_MODEL), x.dtype)], axis=1)
    cos_t, sin_t = _rope_tables(t_work)
    g256 = _group_matrix()
    k_top = min(TOPK_MAX, seq // 4)
    for l in range(w_in.shape[0]):
        an, w_perm, vecs, w2_pad, sm_bias = _prep_layer_params(
            l, attn_norm, w_in, dsa_q_norm, dsa_k_norm, fox_q_norm, fox_k_norm, fox_f_bias,
            gla_gate_w2, gla_gate_b)
        (dq, dkk, dva, iq, ikk, fq, fk, fva, gq, gk, gv, gg, la, small, key_norms) = _inproj(
            h, an, w_perm, cos_t, sin_t, g256, vecs, w2_pad, sm_bias)
        small_t, fkb = _fox_prep(small, fk)
        oa = _dsa_attention(dq, dkk, dva, iq, ikk, small_t, key_norms, k_top)
        ob = _fox_attention(fq, fkb, fva, key_norms)
        oc = _gla(gq, gk, gv, gg, la, jnp.tile(gla_out_norm[l], N_HEADS)[None, :])
        h = _outproj_ffn(h, oa, ob, oc, w_out[l].astype(BF16), ffn_norm[l][None, :],
                         w_gate_up[l].astype(BF16), w_down[l].astype(BF16))
    return h[:, N_META:n_tok]
```

```python
import functools

import numpy as np
import jax
import jax.numpy as jnp
from jax import lax
from jax.experimental import pallas as pl
from jax.experimental.pallas import tpu as pltpu

F32 = jnp.float32
BF16 = jnp.bfloat16

D_MODEL = 1024
HEAD_DIM = 64
N_META = 16
ROPE_THETA = 500000.0
ROPE_DIM = HEAD_DIM // 4
ROPE_HALF = ROPE_DIM // 2
NEG = -1e30
EPS = 1e-6

N_HEADS = 4
TOPK_MAX = 256
GLA_DV = 128
GLA_RANK = 16
GLA_TAU = 16.0
GLA_CHUNK = 64
GLA_SUB = 16
D_FF = 2816

LOG2E = 1.4426950408889634
BIAS_TERMS = 3
KN_DSA, KN_FOX = 0, 1
NORM_MARGIN = 1.01
SAFE_LOG2_SPAN = 100.0
LANES = 128
SEQ_ALIGN = 512
VMEM_LIMIT = 56 * 1024 * 1024

_SLABS = (("dq", 256), ("dkk", 128), ("dva", 128), ("iq", 256), ("ikk", 128),
          ("fqa", 512), ("fkp", 512), ("fva", 512), ("gq", 256), ("gk", 256),
          ("gv", 512), ("gg", 512), ("small", 128))
_SLAB_OFF = {}
_off = 0
for _name, _width in _SLABS:
    _SLAB_OFF[_name] = (_off, _width)
    _off += _width
N_PROJ = _off
SM_IW, SM_FF, SM_GLR = 0, 4, 8


def _work_len(n_tok):
    return -(-n_tok // SEQ_ALIGN) * SEQ_ALIGN


def _row_tile(t_work):
    for cand in (768, 640, 512):
        if t_work % cand == 0:
            return cand
    raise ValueError(f"unsupported working length {t_work}")


def _dot(a, b):
    return jnp.dot(a, b, preferred_element_type=F32)


def _dot_nt(a, b):
    return lax.dot_general(a, b, (((1,), (1,)), ((), ())), preferred_element_type=F32)


def _dot_tn(a, b):
    return lax.dot_general(a, b, (((0,), (0,)), ((), ())), preferred_element_type=F32)


def _split3(x):
    h1 = x.astype(BF16)
    r1 = x - h1.astype(F32)
    h2 = r1.astype(BF16)
    h3 = (r1 - h2.astype(F32)).astype(BF16)
    return h1, h2, h3


def _log_sigmoid(x):
    return jnp.minimum(x, 0.0) - jnp.log1p(jnp.exp(-jnp.abs(x)))


def _silu(x):
    return x / (1.0 + jnp.exp(-x))


def _group_rms(y, gmat, gain):
    yy = y * y
    hi = yy.astype(BF16)
    lo = (yy - hi.astype(F32)).astype(BF16)
    ss = _dot(hi, gmat) + _dot(lo, gmat)
    return y * lax.rsqrt(ss * (1.0 / HEAD_DIM) + EPS) * gain


def _rope(y, cos, sin):
    width = y.shape[-1]
    lane = lax.broadcasted_iota(jnp.int32, y.shape, 1) % HEAD_DIM
    upper = pltpu.roll(y, width - ROPE_HALF, axis=1)
    lower = pltpu.roll(y, ROPE_HALF, axis=1)
    partner = jnp.where(lane < ROPE_HALF, upper, lower)
    return y * cos + partner * sin


def _inproj_kernel(x_ref, an_ref, w_ref, cos_ref, sin_ref, g256_ref, vec_ref, w2_ref, sm_ref,
                   dq_ref, dkk_ref, dva_ref, iq_ref, ikk_ref, fq_ref, fk_ref, fva_ref,
                   gq_ref, gk_ref, gv_ref, gg_ref, la_ref, small_ref, kn_ref):
    x = x_ref[0]
    ms = jnp.mean(x * x, axis=-1, keepdims=True)
    a = (x * lax.rsqrt(ms + EPS) * an_ref[...]).astype(BF16)

    def proj(name):
        off, width = _SLAB_OFF[name]
        return _dot(a, w_ref[:, off:off + width])

    def ones_in_upper_half(y):
        lane = lax.broadcasted_iota(jnp.int32, y.shape, 1) % LANES
        return jnp.where(lane < HEAD_DIM, y, 1.0)

    cos = cos_ref[...]
    sin = sin_ref[...]
    g256 = g256_ref[...]
    g128 = g256[:LANES, :LANES]
    dqn, dkn, fqn, fkn = vec_ref[0:1, :], vec_ref[1:2, :LANES], vec_ref[2:3, :], vec_ref[3:4, :]
    scale = HEAD_DIM ** -0.5

    def max_sq_norm(k):
        kf = k.astype(F32)
        return jnp.max(_dot((kf * kf).astype(BF16), g128), axis=0, keepdims=True)

    dq_ref[0] = (_rope(_group_rms(proj("dq"), g256, dqn), cos, sin) * (scale * LOG2E)).astype(BF16)
    dkk = _rope(_group_rms(proj("dkk"), g128, dkn), cos[:, :LANES], sin[:, :LANES]).astype(BF16)
    dkk_ref[0] = dkk
    key_norms = [max_sq_norm(dkk)]
    dva_ref[0] = ones_in_upper_half(proj("dva")).astype(BF16)
    iq_ref[0] = (_rope(proj("iq"), cos, sin) * scale).astype(BF16)
    ikk_ref[0] = _rope(proj("ikk"), cos[:, :LANES], sin[:, :LANES]).astype(BF16)
    fq, fk = proj("fqa"), proj("fkp")
    lane = lax.broadcasted_iota(jnp.int32, (fq.shape[0], LANES), 1)
    bias_lanes = (lane >= HEAD_DIM) & (lane < HEAD_DIM + BIAS_TERMS)
    for h in range(N_HEADS):
        hs = slice(h * LANES, (h + 1) * LANES)
        q_h = _group_rms(fq[:, hs], g128, fqn[:, :LANES]) * (scale * LOG2E)
        fq_ref[0, :, hs] = jnp.where(bias_lanes, 1.0, q_h).astype(BF16)
        k_h = _group_rms(fk[:, hs], g128, fkn[:, :LANES]).astype(BF16)
        fk_ref[0, :, hs] = k_h
        key_norms.append(max_sq_norm(k_h))
    kn_ref[0, 0] = jnp.concatenate(key_norms + [jnp.zeros((8 - len(key_norms), LANES), F32)], axis=0)
    fva_ref[0] = ones_in_upper_half(proj("fva")).astype(BF16)
    gq_ref[0] = proj("gq") * scale
    gk_ref[0] = proj("gk")
    gv_ref[0] = proj("gv").astype(BF16)
    gg_ref[0] = proj("gg")

    small = proj("small")
    lane = lax.broadcasted_iota(jnp.int32, small.shape, 1)
    small_ref[0] = jnp.where(lane < SM_FF, small * (N_HEADS ** -0.5),
                             _log_sigmoid(small + sm_ref[0:1, :]))
    gate = _dot(small.astype(BF16), w2_ref[...]) + vec_ref[4:5, :]
    la_ref[0] = _log_sigmoid(gate) * (1.0 / GLA_TAU)


def _inproj(h, an, w_perm, cos_t, sin_t, g256, vecs, w2_pad, sm_bias):
    bsz, t_work, _ = h.shape
    tm = _row_tile(t_work)
    grid = (bsz, t_work // tm)

    def rows(width, dtype):
        return (jax.ShapeDtypeStruct((bsz, t_work, width), dtype),
                pl.BlockSpec((1, tm, width), lambda b, j: (b, j, 0)))

    outs = [rows(256, BF16), rows(128, BF16), rows(128, BF16), rows(256, BF16), rows(128, BF16),
            rows(512, BF16), rows(512, BF16), rows(512, BF16), rows(256, F32), rows(256, F32),
            rows(512, BF16), rows(512, F32), rows(256, F32), rows(128, F32),
            (jax.ShapeDtypeStruct((bsz, t_work // tm, 8, LANES), F32),
             pl.BlockSpec((1, 1, 8, LANES), lambda b, j: (b, j, 0, 0)))]
    const = lambda shape: pl.BlockSpec(shape, lambda b, j: (0,) * len(shape))
    return pl.pallas_call(
        _inproj_kernel,
        grid=grid,
        in_specs=[pl.BlockSpec((1, tm, D_MODEL), lambda b, j: (b, j, 0)),
                  const((1, D_MODEL)), const((D_MODEL, N_PROJ)),
                  pl.BlockSpec((tm, 256), lambda b, j: (j, 0)),
                  pl.BlockSpec((tm, 256), lambda b, j: (j, 0)),
                  const((256, 256)), const((8, 256)), const((LANES, 256)), const((8, LANES))],
        out_specs=[o[1] for o in outs],
        out_shape=[o[0] for o in outs],
        compiler_params=pltpu.CompilerParams(
            dimension_semantics=("parallel", "parallel"), vmem_limit_bytes=VMEM_LIMIT),
        name="inproj",
    )(h, an, w_perm, cos_t, sin_t, g256, vecs, w2_pad, sm_bias)


def _prep_layer_params(l, attn_norm, w_in, dsa_q_norm, dsa_k_norm, fox_q_norm, fox_k_norm,
                       fox_f_bias, gla_gate_w2, gla_gate_b):
    w = w_in[l]
    splits = np.cumsum([256, 64, 64, 256, 4, 64, 256, 256, 256, 4, 256, 256, 512, 512, 16])[:-1]
    (dq, dk, dv, iq, iw, ik, fq, fk, fv, ff, gq, gk, gv, gg, glr) = jnp.split(w, splits, axis=1)
    small = jnp.concatenate(
        [iw, ff, glr, jnp.zeros((D_MODEL, LANES - 4 - 4 - GLA_RANK), w.dtype)], axis=1)
    z64 = jnp.zeros((D_MODEL, HEAD_DIM), w.dtype)
    per_head = lambda t: [part for h in range(N_HEADS)
                          for part in (t[:, h * HEAD_DIM:(h + 1) * HEAD_DIM], z64)]
    w_perm = jnp.concatenate([dq, dk, dk, dv, z64, iq, ik, ik, *per_head(fq), *per_head(fk),
                              *per_head(fv), gq, gk, gv, gg, small], axis=1).astype(BF16)
    tile4 = lambda g: jnp.tile(g, N_HEADS)
    vecs = jnp.zeros((8, 256), F32)
    vecs = vecs.at[0].set(tile4(dsa_q_norm[l])).at[1].set(tile4(dsa_k_norm[l]))
    vecs = vecs.at[2].set(tile4(fox_q_norm[l])).at[3].set(tile4(fox_k_norm[l]))
    vecs = vecs.at[4].set(gla_gate_b[l])
    w2_pad = jnp.zeros((LANES, 256), F32).at[SM_GLR:SM_GLR + GLA_RANK].set(gla_gate_w2[l]).astype(BF16)
    sm_bias = jnp.zeros((8, LANES), F32).at[0, SM_FF:SM_FF + N_HEADS].set(fox_f_bias[l])
    return attn_norm[l][None, :], w_perm, vecs, w2_pad, sm_bias


def _rope_tables(t_work):
    inv = jnp.power(ROPE_THETA, -jnp.arange(ROPE_HALF, dtype=F32) * 2.0 / ROPE_DIM)
    ang = jnp.arange(t_work).astype(F32)[:, None] * inv[None, :]
    cos, sin = jnp.cos(ang), jnp.sin(ang)
    rest = HEAD_DIM - ROPE_DIM
    cos64 = jnp.concatenate([cos, cos, jnp.ones((t_work, rest), F32)], axis=1)
    sin64 = jnp.concatenate([-sin, sin, jnp.zeros((t_work, rest), F32)], axis=1)
    return jnp.tile(cos64, (1, N_HEADS)), jnp.tile(sin64, (1, N_HEADS))


def _group_matrix():
    idx = np.arange(256) // HEAD_DIM
    return jnp.asarray((idx[:, None] == idx[None, :]).astype(np.float32), dtype=BF16)


DSA_TQ = 256
DSA_TK = 512
KEY_NEG_INF = -2139095041
KEY_NEG_ZERO = -1
KEY_NEG_MIN_NORMAL = -8388609
SEARCH_FEW = 4.0
SEARCH_MANY = 16.0
UNCHECKED_PROBES = 14
GUIDED_PROBES = 64
MAX_PROBES = GUIDED_PROBES + 40
COUNT_ROWS = 128


def _key_to_f32(key):
    bits = key ^ ((key >> 31) & 0x7FFFFFFF)
    return lax.bitcast_convert_type(bits, F32)


def _f32_to_key(value):
    bits = lax.bitcast_convert_type(value, jnp.int32)
    return bits ^ ((bits >> 31) & 0x7FFFFFFF)


def _head_lane_mask(shape, head):
    lane = lax.broadcasted_iota(jnp.int32, shape, 1)
    return (lane < HEAD_DIM) if head % 2 == 0 else (lane >= HEAD_DIM)


def _masked_heads(slabs):
    return [jnp.where(_head_lane_mask(slabs[h // 2].shape, h), slabs[h // 2],
                      jnp.zeros_like(slabs[h // 2])) for h in range(N_HEADS)]


def _softmax_step(s, m, acc, v_aug):
    m_new = jnp.maximum(m, jnp.max(s, axis=-1, keepdims=True))
    p = jnp.exp2(s - m_new)
    return m_new, acc * jnp.exp2(m - m_new) + _dot(p.astype(BF16), v_aug)


def _normalise_heads(accs):
    outs = [acc / pltpu.roll(acc, HEAD_DIM, axis=1) for acc in accs]
    lane = lax.broadcasted_iota(jnp.int32, outs[0].shape, 1)
    return [jnp.where(lane < HEAD_DIM, outs[2 * p], pltpu.roll(outs[2 * p + 1], HEAD_DIM, axis=1))
            for p in range(2)]


def _dsa_kernel(k_top, dq_ref, kk_ref, va_ref, iq_ref, ik_ref, wt_ref, tri_ref, kn_ref, o_ref,
                s_ref):
    i = pl.program_id(1)
    q0 = i * DSA_TQ
    n_tiles = (q0 + DSA_TQ + DSA_TK - 1) // DSA_TK
    last = n_tiles - 1
    key = lax.broadcasted_iota(jnp.int32, (DSA_TK, DSA_TQ), 0)
    qry = q0 + lax.broadcasted_iota(jnp.int32, (DSA_TK, DSA_TQ), 1)
    head = lambda x, h: x[:, h * DSA_TQ:(h + 1) * DSA_TQ]

    iq_all = jnp.concatenate(
        _masked_heads([iq_ref[0, :, 0:LANES], iq_ref[0, :, LANES:2 * LANES]]), axis=0)
    w_h = [wt_ref[0, SM_IW + h:SM_IW + h + 1, :] for h in range(N_HEADS)]

    def score_tile(j, carry):
        k0 = pl.multiple_of(j * DSA_TK, DSA_TK)
        dots = jnp.maximum(_dot_nt(ik_ref[0, pl.ds(k0, DSA_TK), :], iq_all), 0.0)
        s = w_h[0] * head(dots, 0)
        for h in range(1, N_HEADS):
            s = s + w_h[h] * head(dots, h)
        s = jnp.where(k0 + key <= qry, s, NEG)
        s_ref[pl.ds(k0, DSA_TK), :] = s
        return jnp.maximum(carry, jnp.max(s, axis=0, keepdims=True))

    row_max = lax.fori_loop(0, n_tiles, score_tile, jnp.full((1, DSA_TQ), NEG, F32))

    kf = jnp.float32(k_top)

    def count_above(t):
        tb = jnp.broadcast_to(t, (COUNT_ROWS, DSA_TQ))

        def body(j, acc):
            k0 = pl.multiple_of(j * DSA_TK, DSA_TK)
            for part in range(DSA_TK // COUNT_ROWS):
                s = s_ref[pl.ds(k0 + part * COUNT_ROWS, COUNT_ROWS), :]
                acc = acc + jnp.where(s > tb, 1.0, 0.0)
            return acc

        acc = lax.fori_loop(0, n_tiles, body, jnp.zeros((COUNT_ROWS, DSA_TQ), F32))
        return jnp.sum(acc, axis=0, keepdims=True)

    def midpoint(lo, hi):
        return (lo >> 1) + (hi >> 1) + (lo & hi & 1)

    def converged(lo, hi):
        return (midpoint(lo, hi) == lo) | ((lo >= KEY_NEG_MIN_NORMAL) & (hi <= 0))

    col1 = lambda value, dtype: jnp.full((1, DSA_TQ), value, dtype)

    def probe_once(it, state):
        lo, hi, clo, chi, wlo, whi, side = state
        f_lo, f_hi = _key_to_f32(lo), _key_to_f32(hi)
        target = kf - 0.5
        log_count = lambda c: jnp.log2(jnp.maximum(c, 0.25))
        many = clo - chi > SEARCH_MANY
        g_lo = jnp.where(many, log_count(clo) - np.log2(k_top - 0.5), clo - target) * wlo
        g_hi = jnp.where(many, np.log2(k_top - 0.5) - log_count(chi), target - chi) * whi
        halve = (clo - chi <= SEARCH_FEW) | (col1(it % 8, jnp.int32) == 7)
        guess = _f32_to_key(f_lo + (f_hi - f_lo) * jnp.where(halve, 0.5, g_lo / (g_lo + g_hi)))
        fixed = col1(jnp.where(it == 0, KEY_NEG_ZERO, KEY_NEG_MIN_NORMAL), jnp.int32)
        early = col1(it, jnp.int32) < 2
        guided = col1(it, jnp.int32) < GUIDED_PROBES
        probe = jnp.where((guess > lo) & (guess < hi) & guided & ~early, guess, midpoint(lo, hi))
        probe = jnp.where(early & (lo < fixed) & (fixed < hi), fixed, probe)
        probe = jnp.where(converged(lo, hi), lo, probe)
        c = count_above(_key_to_f32(probe))
        live = probe != lo
        up = live & (c >= kf)
        down = live & (c <= kf)
        wlo = jnp.where(down & (side < 0), wlo * 0.5, jnp.where(up, 1.0, wlo))
        whi = jnp.where(up & (side > 0), whi * 0.5, jnp.where(down, 1.0, whi))
        side = jnp.where(up, 1, jnp.where(down, -1, side))
        lo, clo = jnp.where(up, probe, lo), jnp.where(up, c, clo)
        hi, chi = jnp.where(down, probe, hi), jnp.where(down, c, chi)
        return lo, hi, clo, chi, wlo, whi, side

    def search_cond(carry):
        it, pending = carry[0], carry[1]
        return (pending > 0) & (it < MAX_PROBES)

    def search_body(carry):
        it, state = carry[0], carry[2]
        state = probe_once(it + 1, probe_once(it, state))
        return it + 2, jnp.max(jnp.where(converged(state[0], state[1]), 0, 1)), state

    n_swept = (n_tiles * DSA_TK).astype(F32)
    state = (col1(KEY_NEG_INF, jnp.int32), _f32_to_key(row_max),
             jnp.broadcast_to(n_swept, (1, DSA_TQ)), col1(0.0, F32), col1(1.0, F32), col1(1.0, F32),
             col1(0, jnp.int32))
    state = lax.fori_loop(0, UNCHECKED_PROBES, probe_once, state)
    state = lax.while_loop(search_cond, search_body,
                           (jnp.int32(UNCHECKED_PROBES), jnp.int32(1), state))[2]
    thr = _key_to_f32(state[1])
    n_ties = kf - state[3]

    to_column = lambda r: jnp.broadcast_to(r, (8, DSA_TQ)).T[:, 0:1]
    thr_c, ties_c = to_column(thr), to_column(n_ties)
    q_all = jnp.concatenate(
        _masked_heads([dq_ref[0, :, 0:LANES], dq_ref[0, :, LANES:2 * LANES]]), axis=0)
    tri = tri_ref[...]
    tri_lo = jnp.where(lax.broadcasted_iota(jnp.int32, (LANES, LANES), 0)
                       >= lax.broadcasted_iota(jnp.int32, (LANES, LANES), 1), 1.0, 0.0).astype(BF16)
    qrow = q0 + lax.broadcasted_iota(jnp.int32, (DSA_TQ, DSA_TK), 0)
    kcol = lax.broadcasted_iota(jnp.int32, (DSA_TQ, DSA_TK), 1)
    n_blocks = DSA_TK // LANES

    reach = _logit_reach(q_all, jnp.max(kn_ref[0], axis=0)[KN_DSA:KN_DSA + 1, 0:1])

    def selected_logits(j, seen, causal):
        k0 = pl.multiple_of(j * DSA_TK, DSA_TK)
        s = s_ref[pl.ds(k0, DSA_TK), :].T
        tie = s == thr_c
        tie_b = jnp.where(tie, 1.0, 0.0).astype(BF16)
        local = [_dot(tie_b[:, b * LANES:(b + 1) * LANES], tri) for b in range(n_blocks)]
        ranks = []
        for b in range(n_blocks):
            ranks.append(local[b] + seen)
            seen = seen + local[b][:, LANES - 1:LANES]
        sel = (s > thr_c) | (tie & (jnp.concatenate(ranks, axis=1) <= ties_c))
        if causal:
            sel = sel & (k0 + kcol <= qrow)
        logits = _dot_nt(q_all, kk_ref[0, pl.ds(k0, DSA_TK), :])
        logits = jnp.where(sel[None], logits.reshape(N_HEADS, DSA_TQ, DSA_TK), NEG)
        return seen, logits.reshape(N_HEADS * DSA_TQ, DSA_TK), va_ref[0, pl.ds(k0, DSA_TK), :]

    def capped(j, carry, causal):
        seen, acc = carry
        k0 = pl.multiple_of(j * DSA_TK, DSA_TK)
        s = s_ref[pl.ds(k0, DSA_TK), :]
        tie = s == thr
        tie_b = jnp.where(tie, 1.0, 0.0).astype(BF16)
        ranks = []
        for b in range(n_blocks):
            local = _dot(tri_lo, tie_b[b * LANES:(b + 1) * LANES, :])
            ranks.append(local + seen)
            seen = seen + local[LANES - 1:LANES, :]
        sel = (s > thr) | (tie & (jnp.concatenate(ranks, axis=0) <= n_ties))
        if causal:
            sel = sel & (k0 + key <= qry)
        keep = jnp.where(sel, 1.0, 0.0).astype(BF16).T
        logits = _dot_nt(q_all, kk_ref[0, pl.ds(k0, DSA_TK), :])
        p = jnp.exp2(logits - reach).astype(BF16).reshape(N_HEADS, DSA_TQ, DSA_TK) * keep[None]
        return seen, acc + _dot(p.reshape(N_HEADS * DSA_TQ, DSA_TK), va_ref[0, pl.ds(k0, DSA_TK), :])

    def online(j, carry, causal):
        seen, logits, va_t = selected_logits(j, carry[0], causal)
        return (seen,) + _softmax_step(logits, carry[1], carry[2], va_t)

    def sweep(step, *stats):
        carry = stats + (jnp.zeros((N_HEADS * DSA_TQ, LANES), F32),)
        carry = lax.fori_loop(0, last, functools.partial(step, causal=False), carry)
        return step(last, carry, causal=True)[-1]

    acc = lax.cond(2.0 * jnp.max(reach) <= SAFE_LOG2_SPAN,
                   functools.partial(sweep, capped, jnp.zeros((1, DSA_TQ), F32)),
                   functools.partial(sweep, online, jnp.zeros((DSA_TQ, 1), F32),
                                     jnp.full((N_HEADS * DSA_TQ, 1), NEG, F32)))
    o_ref[0] = jnp.concatenate(
        _normalise_heads([acc[h * DSA_TQ:(h + 1) * DSA_TQ] for h in range(N_HEADS)]),
        axis=1).astype(o_ref.dtype)


def _dsa_attention(dq, dkk, dva, iq, ikk, small_t, key_norms, k_top):
    bsz, t_work, _ = dq.shape
    tri = jnp.asarray(np.triu(np.ones((LANES, LANES), np.float32)), dtype=BF16)
    tile = lambda width: pl.BlockSpec((1, DSA_TQ, width), lambda b, i: (b, i, 0))
    full = pl.BlockSpec((1, t_work, LANES), lambda b, i: (b, 0, 0), pipeline_mode=pl.Buffered(1))
    return pl.pallas_call(
        functools.partial(_dsa_kernel, k_top),
        grid=(bsz, t_work // DSA_TQ),
        in_specs=[tile(256), full, full, tile(256), full,
                  pl.BlockSpec((1, 8, DSA_TQ), lambda b, i: (b, 0, i)),
                  pl.BlockSpec((LANES, LANES), lambda b, i: (0, 0)),
                  pl.BlockSpec((1,) + key_norms.shape[1:], lambda b, i: (b, 0, 0, 0))],
        out_specs=tile(256),
        out_shape=jax.ShapeDtypeStruct((bsz, t_work, 256), BF16),
        scratch_shapes=[pltpu.VMEM((t_work, DSA_TQ), F32)],
        compiler_params=pltpu.CompilerParams(
            dimension_semantics=("parallel", "parallel"), vmem_limit_bytes=VMEM_LIMIT),
        name="dsa_attention",
    )(dq, dkk, dva, iq, ikk, small_t, tri, key_norms)


FOX_TQ = 512
FOX_TK = 512
CUM_T = 256


def _fox_prep_kernel(x_ref, tri_ref, place_ref, k_ref, xt_ref, kb_ref, carry_ref):
    @pl.when(pl.program_id(1) == 0)
    def _():
        carry_ref[...] = jnp.zeros_like(carry_ref)

    tri = tri_ref[...]
    x = x_ref[0]
    h1, h2, h3 = _split3(x)
    c = _dot(tri, h1) + _dot(tri, h2) + _dot(tri, h3) + carry_ref[0:1, :]
    carry_ref[...] = jnp.broadcast_to(c[CUM_T - 1:CUM_T, :], carry_ref.shape)
    xt_ref[0] = x.T[0:8, :]
    terms = _split3(c * -LOG2E)
    bias = sum(_dot(terms[t], place_ref[t]) for t in range(BIAS_TERMS))
    lane = lax.broadcasted_iota(jnp.int32, bias.shape, 1) % LANES
    kb_ref[0] = jnp.where((lane >= HEAD_DIM) & (lane < HEAD_DIM + BIAS_TERMS),
                          bias.astype(BF16), k_ref[0])


def _fox_prep(small, fk):
    bsz, t_work, width = fk.shape
    tri = jnp.asarray(np.tril(np.ones((CUM_T, CUM_T), np.float32)), dtype=BF16)
    place = np.zeros((BIAS_TERMS, LANES, width), np.float32)
    for t in range(BIAS_TERMS):
        for h in range(N_HEADS):
            place[t, SM_FF + h, h * LANES + HEAD_DIM + t] = 1.0
    rows = lambda w: pl.BlockSpec((1, CUM_T, w), lambda b, j: (b, j, 0))
    return pl.pallas_call(
        _fox_prep_kernel,
        grid=(bsz, t_work // CUM_T),
        in_specs=[rows(LANES), pl.BlockSpec((CUM_T, CUM_T), lambda b, j: (0, 0)),
                  pl.BlockSpec((BIAS_TERMS, LANES, width), lambda b, j: (0, 0, 0)), rows(width)],
        out_specs=[pl.BlockSpec((1, 8, CUM_T), lambda b, j: (b, 0, j)), rows(width)],
        out_shape=[jax.ShapeDtypeStruct((bsz, 8, t_work), F32),
                   jax.ShapeDtypeStruct(fk.shape, BF16)],
        scratch_shapes=[pltpu.VMEM((8, LANES), F32)],
        compiler_params=pltpu.CompilerParams(dimension_semantics=("parallel", "arbitrary")),
        name="fox_prep",
    )(small, tri, jnp.asarray(place, dtype=BF16), fk)


def _logit_reach(q, kmax_sq):
    qf = q.astype(F32)
    return jnp.sqrt(jnp.sum(qf * qf, axis=-1, keepdims=True) * kmax_sq) * NORM_MARGIN


def _fox_kernel(q_ref, k_ref, v_ref, kn_ref, o_ref):
    i = pl.program_id(1)
    q0 = pl.multiple_of(i * FOX_TQ, FOX_TQ)
    n_full = q0 // FOX_TK
    row = q0 + lax.broadcasted_iota(jnp.int32, (FOX_TQ, FOX_TK), 0)
    col = n_full * FOX_TK + lax.broadcasted_iota(jnp.int32, (FOX_TQ, FOX_TK), 1)
    heads = [slice(h * LANES, (h + 1) * LANES) for h in range(N_HEADS)]

    lane = lax.broadcasted_iota(jnp.int32, (FOX_TQ, LANES), 1)
    bias_lanes = (lane >= HEAD_DIM) & (lane < HEAD_DIM + BIAS_TERMS)
    kmax_sq = jnp.max(kn_ref[0], axis=0)
    caps, span = [], jnp.float32(0.0)
    for h in range(N_HEADS):
        q_h = q_ref[0, :, heads[h]]
        reach = _logit_reach(jnp.where(lane < HEAD_DIM, q_h, jnp.zeros_like(q_h)),
                             kmax_sq[KN_FOX + h:KN_FOX + h + 1, 0:1])
        own = k_ref[0, pl.ds(q0, FOX_TQ), heads[h]].astype(F32)
        caps.append(reach + jnp.sum(jnp.where(bias_lanes, own, 0.0), axis=-1, keepdims=True))
        span = jnp.maximum(span, 2.0 * jnp.max(reach))

    def logits(j, h, diag):
        k0 = pl.multiple_of(j * FOX_TK, FOX_TK)
        s = _dot_nt(q_ref[0, :, heads[h]], k_ref[0, pl.ds(k0, FOX_TK), heads[h]])
        return jnp.where(col <= row, s, NEG) if diag else s

    def values(j, h):
        return v_ref[0, pl.ds(pl.multiple_of(j * FOX_TK, FOX_TK), FOX_TK), heads[h]]

    def capped(j, accs, diag):
        return tuple(accs[h] + _dot(jnp.exp2(logits(j, h, diag) - caps[h]).astype(BF16), values(j, h))
                     for h in range(N_HEADS))

    def online(j, carry, diag):
        return tuple(_softmax_step(logits(j, h, diag), *carry[h], values(j, h))
                     for h in range(N_HEADS))

    def capped_sweep():
        accs = tuple(jnp.zeros((FOX_TQ, LANES), F32) for _ in range(N_HEADS))
        accs = lax.fori_loop(0, n_full, functools.partial(capped, diag=False), accs)
        return capped(n_full, accs, diag=True)

    def online_sweep():
        carry = tuple((jnp.full((FOX_TQ, 1), NEG, F32), jnp.zeros((FOX_TQ, LANES), F32))
                      for _ in range(N_HEADS))
        carry = lax.fori_loop(0, n_full, functools.partial(online, diag=False), carry)
        return tuple(acc for _, acc in online(n_full, carry, diag=True))

    accs = lax.cond(span <= SAFE_LOG2_SPAN, capped_sweep, online_sweep)
    o_ref[0] = jnp.concatenate(_normalise_heads(list(accs)), axis=1).astype(o_ref.dtype)


def _fox_attention(fqa, fkb, fva, key_norms):
    bsz, t_work, width = fqa.shape
    full = pl.BlockSpec((1, t_work, width), lambda b, i: (b, 0, 0), pipeline_mode=pl.Buffered(1))
    return pl.pallas_call(
        _fox_kernel,
        grid=(bsz, t_work // FOX_TQ),
        in_specs=[pl.BlockSpec((1, FOX_TQ, width), lambda b, i: (b, i, 0)), full, full,
                  pl.BlockSpec((1,) + key_norms.shape[1:], lambda b, i: (b, 0, 0, 0))],
        out_specs=pl.BlockSpec((1, FOX_TQ, 2 * LANES), lambda b, i: (b, i, 0)),
        out_shape=jax.ShapeDtypeStruct((bsz, t_work, 2 * LANES), BF16),
        compiler_params=pltpu.CompilerParams(
            dimension_semantics=("parallel", "parallel"), vmem_limit_bytes=VMEM_LIMIT),
        name="fox_attention",
    )(fqa, fkb, fva, key_norms)


GLA_TG = 256
GLA_NSUB = GLA_CHUNK // GLA_SUB


def _gla_kernel(q_ref, k_ref, v_ref, g_ref, la_ref, tri_ref, e_ref, gn_ref, o_ref, st_ref):
    @pl.when(pl.program_id(1) == 0)
    def _():
        st_ref[...] = jnp.zeros_like(st_ref)

    tri = tri_ref[...]
    emat = e_ref[...]
    lane = lax.broadcasted_iota(jnp.int32, (GLA_CHUNK, LANES), 1)
    rowblk = lax.broadcasted_iota(jnp.int32, (GLA_CHUNK, LANES), 0) // GLA_SUB
    tblk = lax.broadcasted_iota(jnp.int32, (GLA_CHUNK, GLA_CHUNK), 0) // GLA_SUB
    sblk = lax.broadcasted_iota(jnp.int32, (GLA_CHUNK, GLA_CHUNK), 1) // GLA_SUB
    trow = lax.broadcasted_iota(jnp.int32, (GLA_SUB, 256), 0)

    def chunk(c, carry):
        r0 = pl.multiple_of(c * GLA_CHUNK, GLA_CHUNK)
        rows = pl.ds(r0, GLA_CHUNK)
        h1, h2, h3 = _split3(la_ref[0, rows, :])
        b = _dot(tri, h1) + _dot(tri, h2) + _dot(tri, h3)
        q = q_ref[0, rows, :]
        k = k_ref[0, rows, :]
        v = v_ref[0, rows, :]
        b_last = b[GLA_CHUNK - 1:GLA_CHUNK, :]
        qd = q * jnp.exp(b)
        kd = (k * jnp.exp(b_last - b)).astype(BF16)
        starts = [jnp.zeros((1, 256), F32)] + [b[GLA_SUB * i - 1:GLA_SUB * i, :]
                                               for i in range(1, GLA_NSUB)]
        bsel = jnp.concatenate([jnp.broadcast_to(s, (GLA_SUB, 256)) for s in starts], axis=0)
        qn = q * jnp.exp(b - bsel)

        diag = []
        for i in range(GLA_NSUB):
            rs = slice(GLA_SUB * i, GLA_SUB * (i + 1))
            b_i, q_i, k_i = b[rs], q[rs], k[rs]
            v_i = v[rs].astype(F32)
            ps = []
            for s in range(GLA_SUB):
                d = jnp.exp(jnp.minimum(b_i - b_i[s:s + 1], 0.0))
                ps.append(jnp.where(trow >= s, q_i * d * k_i[s:s + 1], 0.0).astype(BF16))
            r = _dot(jnp.concatenate(ps, axis=0), emat)
            od = r[0:GLA_SUB] * v_i[0:1]
            for s in range(1, GLA_SUB):
                od = od + r[GLA_SUB * s:GLA_SUB * (s + 1)] * v_i[s:s + 1]
            diag.append(od)
        o_diag = jnp.concatenate(diag, axis=0)

        for slab in range(2):
            ls = slice(slab * LANES, (slab + 1) * LANES)
            qn_s, k_s, b_s = qn[:, ls], k[:, ls], b[:, ls]
            khat = jnp.concatenate(
                [(k_s * jnp.exp(jnp.minimum(starts[i][:, ls] - b_s, 0.0))).astype(BF16)
                 for i in range(1, GLA_NSUB)], axis=1)
            for half in range(2):
                head = 2 * slab + half
                hs = slice(head * GLA_DV, (head + 1) * GLA_DV)
                in_head = (lane < HEAD_DIM) if half == 0 else (lane >= HEAD_DIM)
                qm = jnp.where(in_head, qn_s, 0.0)
                qhat = jnp.concatenate([jnp.where(rowblk == i, qm, 0.0).astype(BF16)
                                        for i in range(1, GLA_NSUB)], axis=1)
                att = jnp.where(sblk < tblk, _dot_nt(qhat, khat), 0.0)
                v_h = v[:, hs]
                st = st_ref[head]
                o = (_dot_nt(jnp.where(in_head, qd[:, ls], 0.0).astype(BF16), st.astype(BF16))
                     + _dot(att.astype(BF16), v_h) + o_diag[:, hs])
                st_ref[head] = st * jnp.exp(b_last[:, ls]) + _dot_tn(v_h, kd[:, ls])
                y = o * lax.rsqrt(jnp.mean(o * o, axis=-1, keepdims=True) + EPS) * gn_ref[:, hs]
                o_ref[0, rows, hs] = (y * _silu(g_ref[0, rows, hs])).astype(o_ref.dtype)
        return carry

    lax.fori_loop(0, GLA_TG // GLA_CHUNK, chunk, 0)


def _gla(gq, gk, gv, gg, la, gain):
    bsz, t_work, _ = gq.shape
    tri = jnp.asarray(np.tril(np.ones((GLA_CHUNK, GLA_CHUNK), np.float32)), dtype=BF16)
    emat = jnp.asarray(
        (np.arange(256)[:, None] // HEAD_DIM == np.arange(512)[None, :] // GLA_DV).astype(np.float32),
        dtype=BF16)
    rows = lambda width: pl.BlockSpec((1, GLA_TG, width), lambda b, j: (b, j, 0))
    const = lambda shape: pl.BlockSpec(shape, lambda b, j: (0,) * len(shape))
    return pl.pallas_call(
        _gla_kernel,
        grid=(bsz, t_work // GLA_TG),
        in_specs=[rows(256), rows(256), rows(512), rows(512), rows(256),
                  const((GLA_CHUNK, GLA_CHUNK)), const((256, 512)), const((1, 512))],
        out_specs=rows(512),
        out_shape=jax.ShapeDtypeStruct((bsz, t_work, 512), BF16),
        scratch_shapes=[pltpu.VMEM((N_HEADS, GLA_DV, LANES), F32)],
        compiler_params=pltpu.CompilerParams(
            dimension_semantics=("parallel", "arbitrary"), vmem_limit_bytes=VMEM_LIMIT),
        name="gla",
    )(gq, gk, gv, gg, la, tri, emat, gain)


FFN_CHUNK = 256


def _ffn_kernel(h_ref, oa_ref, ob_ref, oc_ref, wo_ref, fn_ref, wgu_ref, wd_ref, out_ref):
    h1 = (h_ref[0] + _dot(oa_ref[0], wo_ref[0:256, :]) + _dot(ob_ref[0], wo_ref[256:512, :])
          + _dot(oc_ref[0], wo_ref[512:1024, :]))
    ms = jnp.mean(h1 * h1, axis=-1, keepdims=True)
    f = (h1 * lax.rsqrt(ms + EPS) * fn_ref[...]).astype(BF16)
    out_ref[0] = h1
    for c in range(0, D_FF, FFN_CHUNK):
        gate = _dot(f, wgu_ref[:, c:c + FFN_CHUNK])
        up = _dot(f, wgu_ref[:, D_FF + c:D_FF + c + FFN_CHUNK])
        out_ref[0] += _dot((_silu(gate) * up).astype(BF16), wd_ref[c:c + FFN_CHUNK, :])


def _outproj_ffn(h, oa, ob, oc, wo, fn, wgu, wd):
    bsz, t_work, _ = h.shape
    tm = _row_tile(t_work)
    rows = lambda width: pl.BlockSpec((1, tm, width), lambda b, j: (b, j, 0))
    const = lambda shape: pl.BlockSpec(shape, lambda b, j: (0,) * len(shape),
                                       pipeline_mode=pl.Buffered(1))
    return pl.pallas_call(
        _ffn_kernel,
        grid=(bsz, t_work // tm),
        in_specs=[rows(D_MODEL), rows(256), rows(256), rows(512), const((D_MODEL, D_MODEL)),
                  const((1, D_MODEL)), const((D_MODEL, 2 * D_FF)), const((D_FF, D_MODEL))],
        out_specs=rows(D_MODEL),
        out_shape=jax.ShapeDtypeStruct(h.shape, F32),
        compiler_params=pltpu.CompilerParams(
            dimension_semantics=("parallel", "parallel"), vmem_limit_bytes=VMEM_LIMIT),
        name="outproj_ffn",
    )(h, oa, ob, oc, wo, fn, wgu, wd)


def kernel(x, meta_tokens, attn_norm, w_in, dsa_q_norm, dsa_k_norm, fox_q_norm, fox_k_norm,
           fox_f_bias, gla_gate_w2, gla_gate_b, gla_out_norm, w_out, ffn_norm, w_gate_up, w_down):
    bsz, seq, _ = x.shape
    n_tok = N_META + seq
    t_work = _work_len(n_tok)
    meta = jnp.broadcast_to(meta_tokens[None].astype(x.dtype), (bsz, N_META, D_MODEL))
    h = jnp.concatenate([meta, x, jnp.zeros((bsz, t_work - n_tok, D_MODEL), x.dtype)], axis=1)
    cos_t, sin_t = _rope_tables(t_work)
    g256 = _group_matrix()
    k_top = min(TOPK_MAX, seq // 4)
    for l in range(w_in.shape[0]):
        an, w_perm, vecs, w2_pad, sm_bias = _prep_layer_params(
            l, attn_norm, w_in, dsa_q_norm, dsa_k_norm, fox_q_norm, fox_k_norm, fox_f_bias,
            gla_gate_w2, gla_gate_b)
        (dq, dkk, dva, iq, ikk, fq, fk, fva, gq, gk, gv, gg, la, small, key_norms) = _inproj(
            h, an, w_perm, cos_t, sin_t, g256, vecs, w2_pad, sm_bias)
        small_t, fkb = _fox_prep(small, fk)
        oa = _dsa_attention(dq, dkk, dva, iq, ikk, small_t, key_norms, k_top)
        ob = _fox_attention(fq, fkb, fva, key_norms)
        oc = _gla(gq, gk, gv, gg, la, jnp.tile(gla_out_norm[l], N_HEADS)[None, :])
        h = _outproj_ffn(h, oa, ob, oc, w_out[l].astype(BF16), ffn_norm[l][None, :],
                         w_gate_up[l].astype(BF16), w_down[l].astype(BF16))
    return h[:, N_META:n_tok]
```

```python
import functools

import numpy as np
import jax
import jax.numpy as jnp
from jax import lax
from jax.experimental import pallas as pl
from jax.experimental.pallas import tpu as pltpu

F32 = jnp.float32
BF16 = jnp.bfloat16

D_MODEL = 1024
HEAD_DIM = 64
N_META = 16
ROPE_THETA = 500000.0
ROPE_DIM = HEAD_DIM // 4
ROPE_HALF = ROPE_DIM // 2
NEG = -1e30
EPS = 1e-6

N_HEADS = 4
TOPK_MAX = 256
GLA_DV = 128
GLA_RANK = 16
GLA_TAU = 16.0
GLA_CHUNK = 64
GLA_SUB = 16
D_FF = 2816

LOG2E = 1.4426950408889634
BIAS_TERMS = 3
KN_DSA, KN_FOX = 0, 1
NORM_MARGIN = 1.01
SAFE_LOG2_SPAN = 100.0
LANES = 128
SEQ_ALIGN = 512
VMEM_LIMIT = 56 * 1024 * 1024

_SLABS = (("dq", 256), ("dkk", 128), ("dva", 128), ("iq", 256), ("ikk", 128),
          ("fqa", 512), ("fkp", 512), ("fva", 512), ("gq", 256), ("gk", 256),
          ("gv", 512), ("gg", 512), ("small", 128))
_SLAB_OFF = {}
_off = 0
for _name, _width in _SLABS:
    _SLAB_OFF[_name] = (_off, _width)
    _off += _width
N_PROJ = _off
SM_IW, SM_FF, SM_GLR = 0, 4, 8


def _work_len(n_tok):
    return -(-n_tok // SEQ_ALIGN) * SEQ_ALIGN


def _row_tile(t_work):
    for cand in (768, 640, 512):
        if t_work % cand == 0:
            return cand
    raise ValueError(f"unsupported working length {t_work}")


def _dot(a, b):
    return jnp.dot(a, b, preferred_element_type=F32)


def _dot_nt(a, b):
    return lax.dot_general(a, b, (((1,), (1,)), ((), ())), preferred_element_type=F32)


def _dot_tn(a, b):
    return lax.dot_general(a, b, (((0,), (0,)), ((), ())), preferred_element_type=F32)


def _split3(x):
    h1 = x.astype(BF16)
    r1 = x - h1.astype(F32)
    h2 = r1.astype(BF16)
    h3 = (r1 - h2.astype(F32)).astype(BF16)
    return h1, h2, h3


def _log_sigmoid(x):
    return jnp.minimum(x, 0.0) - jnp.log1p(jnp.exp(-jnp.abs(x)))


def _silu(x):
    return x / (1.0 + jnp.exp(-x))


def _group_rms(y, gmat, gain):
    yy = y * y
    hi = yy.astype(BF16)
    lo = (yy - hi.astype(F32)).astype(BF16)
    ss = _dot(hi, gmat) + _dot(lo, gmat)
    return y * lax.rsqrt(ss * (1.0 / HEAD_DIM) + EPS) * gain


def _rope(y, cos, sin):
    width = y.shape[-1]
    lane = lax.broadcasted_iota(jnp.int32, y.shape, 1) % HEAD_DIM
    upper = pltpu.roll(y, width - ROPE_HALF, axis=1)
    lower = pltpu.roll(y, ROPE_HALF, axis=1)
    partner = jnp.where(lane < ROPE_HALF, upper, lower)
    return y * cos + partner * sin


def _inproj_kernel(x_ref, an_ref, w_ref, cos_ref, sin_ref, g256_ref, vec_ref, w2_ref, sm_ref,
                   dq_ref, dkk_ref, dva_ref, iq_ref, ikk_ref, fq_ref, fk_ref, fva_ref,
                   gq_ref, gk_ref, gv_ref, gg_ref, la_ref, small_ref, kn_ref):
    x = x_ref[0]
    ms = jnp.mean(x * x, axis=-1, keepdims=True)
    a = (x * lax.rsqrt(ms + EPS) * an_ref[...]).astype(BF16)

    def proj(name):
        off, width = _SLAB_OFF[name]
        return _dot(a, w_ref[:, off:off + width])

    def ones_in_upper_half(y):
        lane = lax.broadcasted_iota(jnp.int32, y.shape, 1) % LANES
        return jnp.where(lane < HEAD_DIM, y, 1.0)

    cos = cos_ref[...]
    sin = sin_ref[...]
    g256 = g256_ref[...]
    g128 = g256[:LANES, :LANES]
    dqn, dkn, fqn, fkn = vec_ref[0:1, :], vec_ref[1:2, :LANES], vec_ref[2:3, :], vec_ref[3:4, :]
    scale = HEAD_DIM ** -0.5

    def max_sq_norm(k):
        kf = k.astype(F32)
        return jnp.max(_dot((kf * kf).astype(BF16), g128), axis=0, keepdims=True)

    dq_ref[0] = (_rope(_group_rms(proj("dq"), g256, dqn), cos, sin) * (scale * LOG2E)).astype(BF16)
    dkk = _rope(_group_rms(proj("dkk"), g128, dkn), cos[:, :LANES], sin[:, :LANES]).astype(BF16)
    dkk_ref[0] = dkk
    key_norms = [max_sq_norm(dkk)]
    dva_ref[0] = ones_in_upper_half(proj("dva")).astype(BF16)
    iq_ref[0] = (_rope(proj("iq"), cos, sin) * scale).astype(BF16)
    ikk_ref[0] = _rope(proj("ikk"), cos[:, :LANES], sin[:, :LANES]).astype(BF16)
    fq, fk = proj("fqa"), proj("fkp")
    lane = lax.broadcasted_iota(jnp.int32, (fq.shape[0], LANES), 1)
    bias_lanes = (lane >= HEAD_DIM) & (lane < HEAD_DIM + BIAS_TERMS)
    for h in range(N_HEADS):
        hs = slice(h * LANES, (h + 1) * LANES)
        q_h = _group_rms(fq[:, hs], g128, fqn[:, :LANES]) * (scale * LOG2E)
        fq_ref[0, :, hs] = jnp.where(bias_lanes, 1.0, q_h).astype(BF16)
        k_h = _group_rms(fk[:, hs], g128, fkn[:, :LANES]).astype(BF16)
        fk_ref[0, :, hs] = k_h
        key_norms.append(max_sq_norm(k_h))
    kn_ref[0, 0] = jnp.concatenate(key_norms + [jnp.zeros((8 - len(key_norms), LANES), F32)], axis=0)
    fva_ref[0] = ones_in_upper_half(proj("fva")).astype(BF16)
    gq_ref[0] = proj("gq") * scale
    gk_ref[0] = proj("gk")
    gv_ref[0] = proj("gv").astype(BF16)
    gg_ref[0] = proj("gg")

    small = proj("small")
    lane = lax.broadcasted_iota(jnp.int32, small.shape, 1)
    small_ref[0] = jnp.where(lane < SM_FF, small * (N_HEADS ** -0.5),
                             _log_sigmoid(small + sm_ref[0:1, :]))
    gate = _dot(small.astype(BF16), w2_ref[...]) + vec_ref[4:5, :]
    la_ref[0] = _log_sigmoid(gate) * (1.0 / GLA_TAU)


def _inproj(h, an, w_perm, cos_t, sin_t, g256, vecs, w2_pad, sm_bias):
    bsz, t_work, _ = h.shape
    tm = _row_tile(t_work)
    grid = (bsz, t_work // tm)

    def rows(width, dtype):
        return (jax.ShapeDtypeStruct((bsz, t_work, width), dtype),
                pl.BlockSpec((1, tm, width), lambda b, j: (b, j, 0)))

    outs = [rows(256, BF16), rows(128, BF16), rows(128, BF16), rows(256, BF16), rows(128, BF16),
            rows(512, BF16), rows(512, BF16), rows(512, BF16), rows(256, F32), rows(256, F32),
            rows(512, BF16), rows(512, F32), rows(256, F32), rows(128, F32),
            (jax.ShapeDtypeStruct((bsz, t_work // tm, 8, LANES), F32),
             pl.BlockSpec((1, 1, 8, LANES), lambda b, j: (b, j, 0, 0)))]
    const = lambda shape: pl.BlockSpec(shape, lambda b, j: (0,) * len(shape))
    return pl.pallas_call(
        _inproj_kernel,
        grid=grid,
        in_specs=[pl.BlockSpec((1, tm, D_MODEL), lambda b, j: (b, j, 0)),
                  const((1, D_MODEL)), const((D_MODEL, N_PROJ)),
                  pl.BlockSpec((tm, 256), lambda b, j: (j, 0)),
                  pl.BlockSpec((tm, 256), lambda b, j: (j, 0)),
                  const((256, 256)), const((8, 256)), const((LANES, 256)), const((8, LANES))],
        out_specs=[o[1] for o in outs],
        out_shape=[o[0] for o in outs],
        compiler_params=pltpu.CompilerParams(
            dimension_semantics=("parallel", "parallel"), vmem_limit_bytes=VMEM_LIMIT),
        name="inproj",
    )(h, an, w_perm, cos_t, sin_t, g256, vecs, w2_pad, sm_bias)


def _prep_layer_params(l, attn_norm, w_in, dsa_q_norm, dsa_k_norm, fox_q_norm, fox_k_norm,
                       fox_f_bias, gla_gate_w2, gla_gate_b):
    w = w_in[l]
    splits = np.cumsum([256, 64, 64, 256, 4, 64, 256, 256, 256, 4, 256, 256, 512, 512, 16])[:-1]
    (dq, dk, dv, iq, iw, ik, fq, fk, fv, ff, gq, gk, gv, gg, glr) = jnp.split(w, splits, axis=1)
    small = jnp.concatenate(
        [iw, ff, glr, jnp.zeros((D_MODEL, LANES - 4 - 4 - GLA_RANK), w.dtype)], axis=1)
    z64 = jnp.zeros((D_MODEL, HEAD_DIM), w.dtype)
    per_head = lambda t: [part for h in range(N_HEADS)
                          for part in (t[:, h * HEAD_DIM:(h + 1) * HEAD_DIM], z64)]
    w_perm = jnp.concatenate([dq, dk, dk, dv, z64, iq, ik, ik, *per_head(fq), *per_head(fk),
                              *per_head(fv), gq, gk, gv, gg, small], axis=1).astype(BF16)
    tile4 = lambda g: jnp.tile(g, N_HEADS)
    vecs = jnp.zeros((8, 256), F32)
    vecs = vecs.at[0].set(tile4(dsa_q_norm[l])).at[1].set(tile4(dsa_k_norm[l]))
    vecs = vecs.at[2].set(tile4(fox_q_norm[l])).at[3].set(tile4(fox_k_norm[l]))
    vecs = vecs.at[4].set(gla_gate_b[l])
    w2_pad = jnp.zeros((LANES, 256), F32).at[SM_GLR:SM_GLR + GLA_RANK].set(gla_gate_w2[l]).astype(BF16)
    sm_bias = jnp.zeros((8, LANES), F32).at[0, SM_FF:SM_FF + N_HEADS].set(fox_f_bias[l])
    return attn_norm[l][None, :], w_perm, vecs, w2_pad, sm_bias


def _rope_tables(t_work):
    inv = jnp.power(ROPE_THETA, -jnp.arange(ROPE_HALF, dtype=F32) * 2.0 / ROPE_DIM)
    ang = jnp.arange(t_work).astype(F32)[:, None] * inv[None, :]
    cos, sin = jnp.cos(ang), jnp.sin(ang)
    rest = HEAD_DIM - ROPE_DIM
    cos64 = jnp.concatenate([cos, cos, jnp.ones((t_work, rest), F32)], axis=1)
    sin64 = jnp.concatenate([-sin, sin, jnp.zeros((t_work, rest), F32)], axis=1)
    return jnp.tile(cos64, (1, N_HEADS)), jnp.tile(sin64, (1, N_HEADS))


def _group_matrix():
    idx = np.arange(256) // HEAD_DIM
    return jnp.asarray((idx[:, None] == idx[None, :]).astype(np.float32), dtype=BF16)


DSA_TQ = 256
DSA_TK = 512
KEY_NEG_INF = -2139095041
KEY_NEG_ZERO = -1
KEY_NEG_MIN_NORMAL = -8388609
NEG_MIN_NORMAL = -1.1754943508222875e-38
SEARCH_FEW = 4.0
SEARCH_MANY = 16.0
UNCHECKED_PROBES = 12
GUIDED_PROBES = 64
MAX_PROBES = GUIDED_PROBES + 40
COUNT_ROWS = 128


def _key_to_f32(key):
    bits = key ^ ((key >> 31) & 0x7FFFFFFF)
    return lax.bitcast_convert_type(bits, F32)


def _f32_to_key(value):
    bits = lax.bitcast_convert_type(value, jnp.int32)
    return bits ^ ((bits >> 31) & 0x7FFFFFFF)


def _head_lane_mask(shape, head):
    lane = lax.broadcasted_iota(jnp.int32, shape, 1)
    return (lane < HEAD_DIM) if head % 2 == 0 else (lane >= HEAD_DIM)


def _masked_heads(slabs):
    return [jnp.where(_head_lane_mask(slabs[h // 2].shape, h), slabs[h // 2],
                      jnp.zeros_like(slabs[h // 2])) for h in range(N_HEADS)]


def _softmax_step(s, m, acc, v_aug):
    m_new = jnp.maximum(m, jnp.max(s, axis=-1, keepdims=True))
    p = jnp.exp2(s - m_new)
    return m_new, acc * jnp.exp2(m - m_new) + _dot(p.astype(BF16), v_aug)


def _normalise_heads(accs):
    outs = [acc / pltpu.roll(acc, HEAD_DIM, axis=1) for acc in accs]
    lane = lax.broadcasted_iota(jnp.int32, outs[0].shape, 1)
    return [jnp.where(lane < HEAD_DIM, outs[2 * p], pltpu.roll(outs[2 * p + 1], HEAD_DIM, axis=1))
            for p in range(2)]


def _dsa_kernel(k_top, dq_ref, kk_ref, va_ref, iq_ref, ik_ref, wt_ref, tri_ref, kn_ref, o_ref,
                s_ref):
    i = pl.program_id(1)
    q0 = i * DSA_TQ
    n_tiles = (q0 + DSA_TQ + DSA_TK - 1) // DSA_TK
    last = n_tiles - 1
    key = lax.broadcasted_iota(jnp.int32, (DSA_TK, DSA_TQ), 0)
    qry = q0 + lax.broadcasted_iota(jnp.int32, (DSA_TK, DSA_TQ), 1)
    head = lambda x, h: x[:, h * DSA_TQ:(h + 1) * DSA_TQ]

    iq_all = jnp.concatenate(
        _masked_heads([iq_ref[0, :, 0:LANES], iq_ref[0, :, LANES:2 * LANES]]), axis=0)
    w_h = [wt_ref[0, SM_IW + h:SM_IW + h + 1, :] for h in range(N_HEADS)]

    def score_tile(j, carry):
        k0 = pl.multiple_of(j * DSA_TK, DSA_TK)
        dots = jnp.maximum(_dot_nt(ik_ref[0, pl.ds(k0, DSA_TK), :], iq_all), 0.0)
        s = w_h[0] * head(dots, 0)
        for h in range(1, N_HEADS):
            s = s + w_h[h] * head(dots, h)
        s = jnp.where(k0 + key <= qry, s, NEG)
        s_ref[pl.ds(k0, DSA_TK), :] = s
        row_max, above_zero, above_neg = carry
        for part in range(DSA_TK // COUNT_ROWS):
            chunk = s[part * COUNT_ROWS:(part + 1) * COUNT_ROWS]
            above_zero = above_zero + jnp.where(chunk > 0.0, 1.0, 0.0)
            above_neg = above_neg + jnp.where(chunk > NEG_MIN_NORMAL, 1.0, 0.0)
        return jnp.maximum(row_max, jnp.max(s, axis=0, keepdims=True)), above_zero, above_neg

    zeros = jnp.zeros((COUNT_ROWS, DSA_TQ), F32)
    row_max, above_zero, above_neg = lax.fori_loop(
        0, n_tiles, score_tile, (jnp.full((1, DSA_TQ), NEG, F32), zeros, zeros))

    kf = jnp.float32(k_top)

    def count_above(t):
        tb = jnp.broadcast_to(t, (COUNT_ROWS, DSA_TQ))

        def body(j, acc):
            k0 = pl.multiple_of(j * DSA_TK, DSA_TK)
            for part in range(DSA_TK // COUNT_ROWS):
                s = s_ref[pl.ds(k0 + part * COUNT_ROWS, COUNT_ROWS), :]
                acc = acc + jnp.where(s > tb, 1.0, 0.0)
            return acc

        acc = lax.fori_loop(0, n_tiles, body, jnp.zeros((COUNT_ROWS, DSA_TQ), F32))
        return jnp.sum(acc, axis=0, keepdims=True)

    def midpoint(lo, hi):
        return (lo >> 1) + (hi >> 1) + (lo & hi & 1)

    def converged(lo, hi):
        return (midpoint(lo, hi) == lo) | ((lo >= KEY_NEG_MIN_NORMAL) & (hi <= 0))

    col1 = lambda value, dtype: jnp.full((1, DSA_TQ), value, dtype)

    def absorb(state, probe, c):
        lo, hi, clo, chi, wlo, whi, side = state
        live = probe != lo
        up = live & (c >= kf)
        down = live & (c <= kf)
        wlo = jnp.where(down & (side < 0), wlo * 0.5, jnp.where(up, 1.0, wlo))
        whi = jnp.where(up & (side > 0), whi * 0.5, jnp.where(down, 1.0, whi))
        side = jnp.where(up, 1, jnp.where(down, -1, side))
        lo, clo = jnp.where(up, probe, lo), jnp.where(up, c, clo)
        hi, chi = jnp.where(down, probe, hi), jnp.where(down, c, chi)
        return lo, hi, clo, chi, wlo, whi, side

    def probe_once(it, state):
        lo, hi, clo, chi, wlo, whi, side = state
        f_lo, f_hi = _key_to_f32(lo), _key_to_f32(hi)
        target = kf - 0.5
        log_count = lambda c: jnp.log2(jnp.maximum(c, 0.25))
        many = clo - chi > SEARCH_MANY
        g_lo = jnp.where(many, log_count(clo) - np.log2(k_top - 0.5), clo - target) * wlo
        g_hi = jnp.where(many, np.log2(k_top - 0.5) - log_count(chi), target - chi) * whi
        halve = (clo - chi <= SEARCH_FEW) | (col1(it % 8, jnp.int32) == 7)
        guess = _f32_to_key(f_lo + (f_hi - f_lo) * jnp.where(halve, 0.5, g_lo / (g_lo + g_hi)))
        guided = col1(it, jnp.int32) < GUIDED_PROBES
        probe = jnp.where((guess > lo) & (guess < hi) & guided, guess, midpoint(lo, hi))
        probe = jnp.where(converged(lo, hi), lo, probe)
        return absorb(state, probe, count_above(_key_to_f32(probe)))

    def search_cond(carry):
        it, pending = carry[0], carry[1]
        return (pending > 0) & (it < MAX_PROBES)

    def search_body(carry):
        it, state = carry[0], carry[2]
        state = probe_once(it + 1, probe_once(it, state))
        return it + 2, jnp.max(jnp.where(converged(state[0], state[1]), 0, 1)), state

    n_swept = (n_tiles * DSA_TK).astype(F32)
    state = (col1(KEY_NEG_INF, jnp.int32), _f32_to_key(row_max),
             jnp.broadcast_to(n_swept, (1, DSA_TQ)), col1(0.0, F32), col1(1.0, F32), col1(1.0, F32),
             col1(0, jnp.int32))
    for fixed, counts in ((KEY_NEG_ZERO, above_zero), (KEY_NEG_MIN_NORMAL, above_neg)):
        inside = (state[0] < fixed) & (fixed < state[1])
        state = absorb(state, jnp.where(inside, fixed, state[0]),
                       jnp.sum(counts, axis=0, keepdims=True))
    state = lax.fori_loop(0, UNCHECKED_PROBES, probe_once, state)
    state = lax.while_loop(search_cond, search_body,
                           (jnp.int32(UNCHECKED_PROBES), jnp.int32(1), state))[2]
    thr = _key_to_f32(state[1])
    n_ties = kf - state[3]

    to_column = lambda r: jnp.broadcast_to(r, (8, DSA_TQ)).T[:, 0:1]
    thr_c, ties_c = to_column(thr), to_column(n_ties)
    q_all = jnp.concatenate(
        _masked_heads([dq_ref[0, :, 0:LANES], dq_ref[0, :, LANES:2 * LANES]]), axis=0)
    tri = tri_ref[...]
    tri_lo = jnp.where(lax.broadcasted_iota(jnp.int32, (LANES, LANES), 0)
                       >= lax.broadcasted_iota(jnp.int32, (LANES, LANES), 1), 1.0, 0.0).astype(BF16)
    qrow = q0 + lax.broadcasted_iota(jnp.int32, (DSA_TQ, DSA_TK), 0)
    kcol = lax.broadcasted_iota(jnp.int32, (DSA_TQ, DSA_TK), 1)
    n_blocks = DSA_TK // LANES

    reach = _logit_reach(q_all, jnp.max(kn_ref[0], axis=0)[KN_DSA:KN_DSA + 1, 0:1])

    def selected_logits(j, seen, causal):
        k0 = pl.multiple_of(j * DSA_TK, DSA_TK)
        s = s_ref[pl.ds(k0, DSA_TK), :].T
        tie = s == thr_c
        tie_b = jnp.where(tie, 1.0, 0.0).astype(BF16)
        local = [_dot(tie_b[:, b * LANES:(b + 1) * LANES], tri) for b in range(n_blocks)]
        ranks = []
        for b in range(n_blocks):
            ranks.append(local[b] + seen)
            seen = seen + local[b][:, LANES - 1:LANES]
        sel = (s > thr_c) | (tie & (jnp.concatenate(ranks, axis=1) <= ties_c))
        if causal:
            sel = sel & (k0 + kcol <= qrow)
        logits = _dot_nt(q_all, kk_ref[0, pl.ds(k0, DSA_TK), :])
        logits = jnp.where(sel[None], logits.reshape(N_HEADS, DSA_TQ, DSA_TK), NEG)
        return seen, logits.reshape(N_HEADS * DSA_TQ, DSA_TK), va_ref[0, pl.ds(k0, DSA_TK), :]

    def capped(j, carry, causal):
        seen, acc = carry
        k0 = pl.multiple_of(j * DSA_TK, DSA_TK)
        s = s_ref[pl.ds(k0, DSA_TK), :]
        tie = s == thr
        tie_b = jnp.where(tie, 1.0, 0.0).astype(BF16)
        ranks = []
        for b in range(n_blocks):
            local = _dot(tri_lo, tie_b[b * LANES:(b + 1) * LANES, :])
            ranks.append(local + seen)
            seen = seen + local[LANES - 1:LANES, :]
        sel = (s > thr) | (tie & (jnp.concatenate(ranks, axis=0) <= n_ties))
        if causal:
            sel = sel & (k0 + key <= qry)
        keep = jnp.where(sel, 1.0, 0.0).astype(BF16).T
        logits = _dot_nt(q_all, kk_ref[0, pl.ds(k0, DSA_TK), :])
        p = jnp.exp2(logits - reach).astype(BF16).reshape(N_HEADS, DSA_TQ, DSA_TK) * keep[None]
        return seen, acc + _dot(p.reshape(N_HEADS * DSA_TQ, DSA_TK), va_ref[0, pl.ds(k0, DSA_TK), :])

    def online(j, carry, causal):
        seen, logits, va_t = selected_logits(j, carry[0], causal)
        return (seen,) + _softmax_step(logits, carry[1], carry[2], va_t)

    def sweep(step, *stats):
        carry = stats + (jnp.zeros((N_HEADS * DSA_TQ, LANES), F32),)
        carry = lax.fori_loop(0, last, functools.partial(step, causal=False), carry)
        return step(last, carry, causal=True)[-1]

    acc = lax.cond(2.0 * jnp.max(reach) <= SAFE_LOG2_SPAN,
                   functools.partial(sweep, capped, jnp.zeros((1, DSA_TQ), F32)),
                   functools.partial(sweep, online, jnp.zeros((DSA_TQ, 1), F32),
                                     jnp.full((N_HEADS * DSA_TQ, 1), NEG, F32)))
    o_ref[0] = jnp.concatenate(
        _normalise_heads([acc[h * DSA_TQ:(h + 1) * DSA_TQ] for h in range(N_HEADS)]),
        axis=1).astype(o_ref.dtype)


def _dsa_attention(dq, dkk, dva, iq, ikk, small_t, key_norms, k_top):
    bsz, t_work, _ = dq.shape
    tri = jnp.asarray(np.triu(np.ones((LANES, LANES), np.float32)), dtype=BF16)
    tile = lambda width: pl.BlockSpec((1, DSA_TQ, width), lambda b, i: (b, i, 0))
    full = pl.BlockSpec((1, t_work, LANES), lambda b, i: (b, 0, 0), pipeline_mode=pl.Buffered(1))
    return pl.pallas_call(
        functools.partial(_dsa_kernel, k_top),
        grid=(bsz, t_work // DSA_TQ),
        in_specs=[tile(256), full, full, tile(256), full,
                  pl.BlockSpec((1, 8, DSA_TQ), lambda b, i: (b, 0, i)),
                  pl.BlockSpec((LANES, LANES), lambda b, i: (0, 0)),
                  pl.BlockSpec((1,) + key_norms.shape[1:], lambda b, i: (b, 0, 0, 0))],
        out_specs=tile(256),
        out_shape=jax.ShapeDtypeStruct((bsz, t_work, 256), BF16),
        scratch_shapes=[pltpu.VMEM((t_work, DSA_TQ), F32)],
        compiler_params=pltpu.CompilerParams(
            dimension_semantics=("parallel", "parallel"), vmem_limit_bytes=VMEM_LIMIT),
        name="dsa_attention",
    )(dq, dkk, dva, iq, ikk, small_t, tri, key_norms)


FOX_TQ = 512
FOX_TK = 512
CUM_T = 256


def _fox_prep_kernel(x_ref, tri_ref, place_ref, k_ref, xt_ref, kb_ref, carry_ref):
    @pl.when(pl.program_id(1) == 0)
    def _():
        carry_ref[...] = jnp.zeros_like(carry_ref)

    tri = tri_ref[...]
    x = x_ref[0]
    h1, h2, h3 = _split3(x)
    c = _dot(tri, h1) + _dot(tri, h2) + _dot(tri, h3) + carry_ref[0:1, :]
    carry_ref[...] = jnp.broadcast_to(c[CUM_T - 1:CUM_T, :], carry_ref.shape)
    xt_ref[0] = x.T[0:8, :]
    terms = _split3(c * -LOG2E)
    bias = sum(_dot(terms[t], place_ref[t]) for t in range(BIAS_TERMS))
    lane = lax.broadcasted_iota(jnp.int32, bias.shape, 1) % LANES
    kb_ref[0] = jnp.where((lane >= HEAD_DIM) & (lane < HEAD_DIM + BIAS_TERMS),
                          bias.astype(BF16), k_ref[0])


def _fox_prep(small, fk):
    bsz, t_work, width = fk.shape
    tri = jnp.asarray(np.tril(np.ones((CUM_T, CUM_T), np.float32)), dtype=BF16)
    place = np.zeros((BIAS_TERMS, LANES, width), np.float32)
    for t in range(BIAS_TERMS):
        for h in range(N_HEADS):
            place[t, SM_FF + h, h * LANES + HEAD_DIM + t] = 1.0
    rows = lambda w: pl.BlockSpec((1, CUM_T, w), lambda b, j: (b, j, 0))
    return pl.pallas_call(
        _fox_prep_kernel,
        grid=(bsz, t_work // CUM_T),
        in_specs=[rows(LANES), pl.BlockSpec((CUM_T, CUM_T), lambda b, j: (0, 0)),
                  pl.BlockSpec((BIAS_TERMS, LANES, width), lambda b, j: (0, 0, 0)), rows(width)],
        out_specs=[pl.BlockSpec((1, 8, CUM_T), lambda b, j: (b, 0, j)), rows(width)],
        out_shape=[jax.ShapeDtypeStruct((bsz, 8, t_work), F32),
                   jax.ShapeDtypeStruct(fk.shape, BF16)],
        scratch_shapes=[pltpu.VMEM((8, LANES), F32)],
        compiler_params=pltpu.CompilerParams(dimension_semantics=("parallel", "arbitrary")),
        name="fox_prep",
    )(small, tri, jnp.asarray(place, dtype=BF16), fk)


def _logit_reach(q, kmax_sq):
    qf = q.astype(F32)
    return jnp.sqrt(jnp.sum(qf * qf, axis=-1, keepdims=True) * kmax_sq) * NORM_MARGIN


def _fox_kernel(q_ref, k_ref, v_ref, kn_ref, o_ref):
    i = pl.program_id(1)
    q0 = pl.multiple_of(i * FOX_TQ, FOX_TQ)
    n_full = q0 // FOX_TK
    row = q0 + lax.broadcasted_iota(jnp.int32, (FOX_TQ, FOX_TK), 0)
    col = n_full * FOX_TK + lax.broadcasted_iota(jnp.int32, (FOX_TQ, FOX_TK), 1)
    heads = [slice(h * LANES, (h + 1) * LANES) for h in range(N_HEADS)]

    lane = lax.broadcasted_iota(jnp.int32, (FOX_TQ, LANES), 1)
    bias_lanes = (lane >= HEAD_DIM) & (lane < HEAD_DIM + BIAS_TERMS)
    kmax_sq = jnp.max(kn_ref[0], axis=0)
    caps, span = [], jnp.float32(0.0)
    for h in range(N_HEADS):
        q_h = q_ref[0, :, heads[h]]
        reach = _logit_reach(jnp.where(lane < HEAD_DIM, q_h, jnp.zeros_like(q_h)),
                             kmax_sq[KN_FOX + h:KN_FOX + h + 1, 0:1])
        own = k_ref[0, pl.ds(q0, FOX_TQ), heads[h]].astype(F32)
        caps.append(reach + jnp.sum(jnp.where(bias_lanes, own, 0.0), axis=-1, keepdims=True))
        span = jnp.maximum(span, 2.0 * jnp.max(reach))

    def logits(j, h, diag):
        k0 = pl.multiple_of(j * FOX_TK, FOX_TK)
        s = _dot_nt(q_ref[0, :, heads[h]], k_ref[0, pl.ds(k0, FOX_TK), heads[h]])
        return jnp.where(col <= row, s, NEG) if diag else s

    def values(j, h):
        return v_ref[0, pl.ds(pl.multiple_of(j * FOX_TK, FOX_TK), FOX_TK), heads[h]]

    def capped(j, accs, diag):
        return tuple(accs[h] + _dot(jnp.exp2(logits(j, h, diag) - caps[h]).astype(BF16), values(j, h))
                     for h in range(N_HEADS))

    def online(j, carry, diag):
        return tuple(_softmax_step(logits(j, h, diag), *carry[h], values(j, h))
                     for h in range(N_HEADS))

    def capped_sweep():
        accs = tuple(jnp.zeros((FOX_TQ, LANES), F32) for _ in range(N_HEADS))
        accs = lax.fori_loop(0, n_full, functools.partial(capped, diag=False), accs)
        return capped(n_full, accs, diag=True)

    def online_sweep():
        carry = tuple((jnp.full((FOX_TQ, 1), NEG, F32), jnp.zeros((FOX_TQ, LANES), F32))
                      for _ in range(N_HEADS))
        carry = lax.fori_loop(0, n_full, functools.partial(online, diag=False), carry)
        return tuple(acc for _, acc in online(n_full, carry, diag=True))

    accs = lax.cond(span <= SAFE_LOG2_SPAN, capped_sweep, online_sweep)
    o_ref[0] = jnp.concatenate(_normalise_heads(list(accs)), axis=1).astype(o_ref.dtype)


def _fox_attention(fqa, fkb, fva, key_norms):
    bsz, t_work, width = fqa.shape
    full = pl.BlockSpec((1, t_work, width), lambda b, i: (b, 0, 0), pipeline_mode=pl.Buffered(1))
    return pl.pallas_call(
        _fox_kernel,
        grid=(bsz, t_work // FOX_TQ),
        in_specs=[pl.BlockSpec((1, FOX_TQ, width), lambda b, i: (b, i, 0)), full, full,
                  pl.BlockSpec((1,) + key_norms.shape[1:], lambda b, i: (b, 0, 0, 0))],
        out_specs=pl.BlockSpec((1, FOX_TQ, 2 * LANES), lambda b, i: (b, i, 0)),
        out_shape=jax.ShapeDtypeStruct((bsz, t_work, 2 * LANES), BF16),
        compiler_params=pltpu.CompilerParams(
            dimension_semantics=("parallel", "parallel"), vmem_limit_bytes=VMEM_LIMIT),
        name="fox_attention",
    )(fqa, fkb, fva, key_norms)


GLA_TG = 256
GLA_NSUB = GLA_CHUNK // GLA_SUB


def _gla_kernel(q_ref, k_ref, v_ref, g_ref, la_ref, tri_ref, e_ref, gn_ref, o_ref, st_ref):
    @pl.when(pl.program_id(1) == 0)
    def _():
        st_ref[...] = jnp.zeros_like(st_ref)

    tri = tri_ref[...]
    emat = e_ref[...]
    lane = lax.broadcasted_iota(jnp.int32, (GLA_CHUNK, LANES), 1)
    rowblk = lax.broadcasted_iota(jnp.int32, (GLA_CHUNK, LANES), 0) // GLA_SUB
    tblk = lax.broadcasted_iota(jnp.int32, (GLA_CHUNK, GLA_CHUNK), 0) // GLA_SUB
    sblk = lax.broadcasted_iota(jnp.int32, (GLA_CHUNK, GLA_CHUNK), 1) // GLA_SUB
    trow = lax.broadcasted_iota(jnp.int32, (GLA_SUB, 256), 0)

    def chunk(c, carry):
        r0 = pl.multiple_of(c * GLA_CHUNK, GLA_CHUNK)
        rows = pl.ds(r0, GLA_CHUNK)
        h1, h2, h3 = _split3(la_ref[0, rows, :])
        b = _dot(tri, h1) + _dot(tri, h2) + _dot(tri, h3)
        q = q_ref[0, rows, :]
        k = k_ref[0, rows, :]
        v = v_ref[0, rows, :]
        b_last = b[GLA_CHUNK - 1:GLA_CHUNK, :]
        qd = q * jnp.exp(b)
        kd = (k * jnp.exp(b_last - b)).astype(BF16)
        starts = [jnp.zeros((1, 256), F32)] + [b[GLA_SUB * i - 1:GLA_SUB * i, :]
                                               for i in range(1, GLA_NSUB)]
        bsel = jnp.concatenate([jnp.broadcast_to(s, (GLA_SUB, 256)) for s in starts], axis=0)
        qn = q * jnp.exp(b - bsel)

        diag = []
        for i in range(GLA_NSUB):
            rs = slice(GLA_SUB * i, GLA_SUB * (i + 1))
            b_i, q_i, k_i = b[rs], q[rs], k[rs]
            v_i = v[rs].astype(F32)
            ps = []
            for s in range(GLA_SUB):
                d = jnp.exp(jnp.minimum(b_i - b_i[s:s + 1], 0.0))
                ps.append(jnp.where(trow >= s, q_i * d * k_i[s:s + 1], 0.0).astype(BF16))
            r = _dot(jnp.concatenate(ps, axis=0), emat)
            od = r[0:GLA_SUB] * v_i[0:1]
            for s in range(1, GLA_SUB):
                od = od + r[GLA_SUB * s:GLA_SUB * (s + 1)] * v_i[s:s + 1]
            diag.append(od)
        o_diag = jnp.concatenate(diag, axis=0)

        for slab in range(2):
            ls = slice(slab * LANES, (slab + 1) * LANES)
            qn_s, k_s, b_s = qn[:, ls], k[:, ls], b[:, ls]
            khat = jnp.concatenate(
                [(k_s * jnp.exp(jnp.minimum(starts[i][:, ls] - b_s, 0.0))).astype(BF16)
                 for i in range(1, GLA_NSUB)], axis=1)
            for half in range(2):
                head = 2 * slab + half
                hs = slice(head * GLA_DV, (head + 1) * GLA_DV)
                in_head = (lane < HEAD_DIM) if half == 0 else (lane >= HEAD_DIM)
                qm = jnp.where(in_head, qn_s, 0.0)
                qhat = jnp.concatenate([jnp.where(rowblk == i, qm, 0.0).astype(BF16)
                                        for i in range(1, GLA_NSUB)], axis=1)
                att = jnp.where(sblk < tblk, _dot_nt(qhat, khat), 0.0)
                v_h = v[:, hs]
                st = st_ref[head]
                o = (_dot_nt(jnp.where(in_head, qd[:, ls], 0.0).astype(BF16), st.astype(BF16))
                     + _dot(att.astype(BF16), v_h) + o_diag[:, hs])
                st_ref[head] = st * jnp.exp(b_last[:, ls]) + _dot_tn(v_h, kd[:, ls])
                y = o * lax.rsqrt(jnp.mean(o * o, axis=-1, keepdims=True) + EPS) * gn_ref[:, hs]
                o_ref[0, rows, hs] = (y * _silu(g_ref[0, rows, hs])).astype(o_ref.dtype)
        return carry

    lax.fori_loop(0, GLA_TG // GLA_CHUNK, chunk, 0)


def _gla(gq, gk, gv, gg, la, gain):
    bsz, t_work, _ = gq.shape
    tri = jnp.asarray(np.tril(np.ones((GLA_CHUNK, GLA_CHUNK), np.float32)), dtype=BF16)
    emat = jnp.asarray(
        (np.arange(256)[:, None] // HEAD_DIM == np.arange(512)[None, :] // GLA_DV).astype(np.float32),
        dtype=BF16)
    rows = lambda width: pl.BlockSpec((1, GLA_TG, width), lambda b, j: (b, j, 0))
    const = lambda shape: pl.BlockSpec(shape, lambda b, j: (0,) * len(shape))
    return pl.pallas_call(
        _gla_kernel,
        grid=(bsz, t_work // GLA_TG),
        in_specs=[rows(256), rows(256), rows(512), rows(512), rows(256),
                  const((GLA_CHUNK, GLA_CHUNK)), const((256, 512)), const((1, 512))],
        out_specs=rows(512),
        out_shape=jax.ShapeDtypeStruct((bsz, t_work, 512), BF16),
        scratch_shapes=[pltpu.VMEM((N_HEADS, GLA_DV, LANES), F32)],
        compiler_params=pltpu.CompilerParams(
            dimension_semantics=("parallel", "arbitrary"), vmem_limit_bytes=VMEM_LIMIT),
        name="gla",
    )(gq, gk, gv, gg, la, tri, emat, gain)


FFN_CHUNK = 256


def _ffn_kernel(h_ref, oa_ref, ob_ref, oc_ref, wo_ref, fn_ref, wgu_ref, wd_ref, out_ref):
    h1 = (h_ref[0] + _dot(oa_ref[0], wo_ref[0:256, :]) + _dot(ob_ref[0], wo_ref[256:512, :])
          + _dot(oc_ref[0], wo_ref[512:1024, :]))
    ms = jnp.mean(h1 * h1, axis=-1, keepdims=True)
    f = (h1 * lax.rsqrt(ms + EPS) * fn_ref[...]).astype(BF16)
    out_ref[0] = h1
    for c in range(0, D_FF, FFN_CHUNK):
        gate = _dot(f, wgu_ref[:, c:c + FFN_CHUNK])
        up = _dot(f, wgu_ref[:, D_FF + c:D_FF + c + FFN_CHUNK])
        out_ref[0] += _dot((_silu(gate) * up).astype(BF16), wd_ref[c:c + FFN_CHUNK, :])


def _outproj_ffn(h, oa, ob, oc, wo, fn, wgu, wd):
    bsz, t_work, _ = h.shape
    tm = _row_tile(t_work)
    rows = lambda width: pl.BlockSpec((1, tm, width), lambda b, j: (b, j, 0))
    const = lambda shape: pl.BlockSpec(shape, lambda b, j: (0,) * len(shape),
                                       pipeline_mode=pl.Buffered(1))
    return pl.pallas_call(
        _ffn_kernel,
        grid=(bsz, t_work // tm),
        in_specs=[rows(D_MODEL), rows(256), rows(256), rows(512), const((D_MODEL, D_MODEL)),
                  const((1, D_MODEL)), const((D_MODEL, 2 * D_FF)), const((D_FF, D_MODEL))],
        out_specs=rows(D_MODEL),
        out_shape=jax.ShapeDtypeStruct(h.shape, F32),
        compiler_params=pltpu.CompilerParams(
            dimension_semantics=("parallel", "parallel"), vmem_limit_bytes=VMEM_LIMIT),
        name="outproj_ffn",
    )(h, oa, ob, oc, wo, fn, wgu, wd)


def kernel(x, meta_tokens, attn_norm, w_in, dsa_q_norm, dsa_k_norm, fox_q_norm, fox_k_norm,
           fox_f_bias, gla_gate_w2, gla_gate_b, gla_out_norm, w_out, ffn_norm, w_gate_up, w_down):
    bsz, seq, _ = x.shape
    n_tok = N_META + seq
    t_work = _work_len(n_tok)
    meta = jnp.broadcast_to(meta_tokens[None].astype(x.dtype), (bsz, N_META, D_MODEL))
    h = jnp.concatenate([meta, x, jnp.zeros((bsz, t_work - n_tok, D_MODEL), x.dtype)], axis=1)
    cos_t, sin_t = _rope_tables(t_work)
    g256 = _group_matrix()
    k_top = min(TOPK_MAX, seq // 4)
    for l in range(w_in.shape[0]):
        an, w_perm, vecs, w2_pad, sm_bias = _prep_layer_params(
            l, attn_norm, w_in, dsa_q_norm, dsa_k_norm, fox_q_norm, fox_k_norm, fox_f_bias,
            gla_gate_w2, gla_gate_b)
        (dq, dkk, dva, iq, ikk, fq, fk, fva, gq, gk, gv, gg, la, small, key_norms) = _inproj(
            h, an, w_perm, cos_t, sin_t, g256, vecs, w2_pad, sm_bias)
        small_t, fkb = _fox_prep(small, fk)
        oa = _dsa_attention(dq, dkk, dva, iq, ikk, small_t, key_norms, k_top)
        ob = _fox_attention(fq, fkb, fva, key_norms)
        oc = _gla(gq, gk, gv, gg, la, jnp.tile(gla_out_norm[l], N_HEADS)[None, :])
        h = _outproj_ffn(h, oa, ob, oc, w_out[l].astype(BF16), ffn_norm[l][None, :],
                         w_gate_up[l].astype(BF16), w_down[l].astype(BF16))
    return h[:, N_META:n_tok]
```

```python
import functools

import numpy as np
import jax
import jax.numpy as jnp
from jax import lax
from jax.experimental import pallas as pl
from jax.experimental.pallas import tpu as pltpu

F32 = jnp.float32
BF16 = jnp.bfloat16

D_MODEL = 1024
HEAD_DIM = 64
N_META = 16
ROPE_THETA = 500000.0
ROPE_DIM = HEAD_DIM // 4
ROPE_HALF = ROPE_DIM // 2
NEG = -1e30
EPS = 1e-6

N_HEADS = 4
TOPK_MAX = 256
GLA_DV = 128
GLA_RANK = 16
GLA_TAU = 16.0
GLA_CHUNK = 64
GLA_SUB = 16
D_FF = 2816

LOG2E = 1.4426950408889634
BIAS_TERMS = 3
KN_DSA, KN_FOX = 0, 1
NORM_MARGIN = 1.01
SAFE_LOG2_SPAN = 100.0
LANES = 128
SEQ_ALIGN = 512
VMEM_LIMIT = 56 * 1024 * 1024

_SLABS = (("dq", 256), ("dkk", 128), ("dva", 128), ("iq", 256), ("ikk", 128),
          ("fqa", 512), ("fkp", 512), ("fva", 512), ("gq", 256), ("gk", 256),
          ("gv", 512), ("gg", 512), ("small", 128))
_SLAB_OFF = {}
_off = 0
for _name, _width in _SLABS:
    _SLAB_OFF[_name] = (_off, _width)
    _off += _width
N_PROJ = _off
SM_IW, SM_FF, SM_GLR = 0, 4, 8


def _work_len(n_tok):
    return -(-n_tok // SEQ_ALIGN) * SEQ_ALIGN


def _row_tile(t_work):
    for cand in (768, 640, 512):
        if t_work % cand == 0:
            return cand
    raise ValueError(f"unsupported working length {t_work}")


def _dot(a, b):
    return jnp.dot(a, b, preferred_element_type=F32)


def _dot_nt(a, b):
    return lax.dot_general(a, b, (((1,), (1,)), ((), ())), preferred_element_type=F32)


def _dot_tn(a, b):
    return lax.dot_general(a, b, (((0,), (0,)), ((), ())), preferred_element_type=F32)


def _split3(x):
    h1 = x.astype(BF16)
    r1 = x - h1.astype(F32)
    h2 = r1.astype(BF16)
    h3 = (r1 - h2.astype(F32)).astype(BF16)
    return h1, h2, h3


def _log_sigmoid(x):
    return jnp.minimum(x, 0.0) - jnp.log1p(jnp.exp(-jnp.abs(x)))


def _silu(x):
    return x / (1.0 + jnp.exp(-x))


def _group_rms(y, gmat, gain):
    yy = y * y
    hi = yy.astype(BF16)
    lo = (yy - hi.astype(F32)).astype(BF16)
    ss = _dot(hi, gmat) + _dot(lo, gmat)
    return y * lax.rsqrt(ss * (1.0 / HEAD_DIM) + EPS) * gain


def _rope(y, cos, sin):
    width = y.shape[-1]
    lane = lax.broadcasted_iota(jnp.int32, y.shape, 1) % HEAD_DIM
    upper = pltpu.roll(y, width - ROPE_HALF, axis=1)
    lower = pltpu.roll(y, ROPE_HALF, axis=1)
    partner = jnp.where(lane < ROPE_HALF, upper, lower)
    return y * cos + partner * sin


def _inproj_kernel(x_ref, an_ref, w_ref, cos_ref, sin_ref, g256_ref, vec_ref, w2_ref, sm_ref,
                   dq_ref, dkk_ref, dva_ref, iq_ref, ikk_ref, fq_ref, fk_ref, fva_ref,
                   gq_ref, gk_ref, gv_ref, gg_ref, la_ref, small_ref, kn_ref):
    x = x_ref[0]
    ms = jnp.mean(x * x, axis=-1, keepdims=True)
    a = (x * lax.rsqrt(ms + EPS) * an_ref[...]).astype(BF16)

    def proj(name):
        off, width = _SLAB_OFF[name]
        return _dot(a, w_ref[:, off:off + width])

    def ones_in_upper_half(y):
        lane = lax.broadcasted_iota(jnp.int32, y.shape, 1) % LANES
        return jnp.where(lane < HEAD_DIM, y, 1.0)

    cos = cos_ref[...]
    sin = sin_ref[...]
    g256 = g256_ref[...]
    g128 = g256[:LANES, :LANES]
    dqn, dkn, fqn, fkn = vec_ref[0:1, :], vec_ref[1:2, :LANES], vec_ref[2:3, :], vec_ref[3:4, :]
    scale = HEAD_DIM ** -0.5

    def max_sq_norm(k):
        kf = k.astype(F32)
        return jnp.max(_dot((kf * kf).astype(BF16), g128), axis=0, keepdims=True)

    dq_ref[0] = (_rope(_group_rms(proj("dq"), g256, dqn), cos, sin) * (scale * LOG2E)).astype(BF16)
    dkk = _rope(_group_rms(proj("dkk"), g128, dkn), cos[:, :LANES], sin[:, :LANES]).astype(BF16)
    dkk_ref[0] = dkk
    key_norms = [max_sq_norm(dkk)]
    dva_ref[0] = ones_in_upper_half(proj("dva")).astype(BF16)
    iq_ref[0] = (_rope(proj("iq"), cos, sin) * scale).astype(BF16)
    ikk_ref[0] = _rope(proj("ikk"), cos[:, :LANES], sin[:, :LANES]).astype(BF16)
    fq, fk = proj("fqa"), proj("fkp")
    lane = lax.broadcasted_iota(jnp.int32, (fq.shape[0], LANES), 1)
    bias_lanes = (lane >= HEAD_DIM) & (lane < HEAD_DIM + BIAS_TERMS)
    for h in range(N_HEADS):
        hs = slice(h * LANES, (h + 1) * LANES)
        q_h = _group_rms(fq[:, hs], g128, fqn[:, :LANES]) * (scale * LOG2E)
        fq_ref[0, :, hs] = jnp.where(bias_lanes, 1.0, q_h).astype(BF16)
        k_h = _group_rms(fk[:, hs], g128, fkn[:, :LANES]).astype(BF16)
        fk_ref[0, :, hs] = k_h
        key_norms.append(max_sq_norm(k_h))
    kn_ref[0, 0] = jnp.concatenate(key_norms + [jnp.zeros((8 - len(key_norms), LANES), F32)], axis=0)
    fva_ref[0] = ones_in_upper_half(proj("fva")).astype(BF16)
    gq_ref[0] = proj("gq") * scale
    gk_ref[0] = proj("gk")
    gv_ref[0] = proj("gv").astype(BF16)
    gg_ref[0] = proj("gg")

    small = proj("small")
    lane = lax.broadcasted_iota(jnp.int32, small.shape, 1)
    small_ref[0] = jnp.where(lane < SM_FF, small * (N_HEADS ** -0.5),
                             _log_sigmoid(small + sm_ref[0:1, :]))
    gate = _dot(small.astype(BF16), w2_ref[...]) + vec_ref[4:5, :]
    la_ref[0] = _log_sigmoid(gate) * (1.0 / GLA_TAU)


def _inproj(h, an, w_perm, cos_t, sin_t, g256, vecs, w2_pad, sm_bias):
    bsz, t_work, _ = h.shape
    tm = _row_tile(t_work)
    grid = (bsz, t_work // tm)

    def rows(width, dtype):
        return (jax.ShapeDtypeStruct((bsz, t_work, width), dtype),
                pl.BlockSpec((1, tm, width), lambda b, j: (b, j, 0)))

    outs = [rows(256, BF16), rows(128, BF16), rows(128, BF16), rows(256, BF16), rows(128, BF16),
            rows(512, BF16), rows(512, BF16), rows(512, BF16), rows(256, F32), rows(256, F32),
            rows(512, BF16), rows(512, F32), rows(256, F32), rows(128, F32),
            (jax.ShapeDtypeStruct((bsz, t_work // tm, 8, LANES), F32),
             pl.BlockSpec((1, 1, 8, LANES), lambda b, j: (b, j, 0, 0)))]
    const = lambda shape: pl.BlockSpec(shape, lambda b, j: (0,) * len(shape))
    return pl.pallas_call(
        _inproj_kernel,
        grid=grid,
        in_specs=[pl.BlockSpec((1, tm, D_MODEL), lambda b, j: (b, j, 0)),
                  const((1, D_MODEL)), const((D_MODEL, N_PROJ)),
                  pl.BlockSpec((tm, 256), lambda b, j: (j, 0)),
                  pl.BlockSpec((tm, 256), lambda b, j: (j, 0)),
                  const((256, 256)), const((8, 256)), const((LANES, 256)), const((8, LANES))],
        out_specs=[o[1] for o in outs],
        out_shape=[o[0] for o in outs],
        compiler_params=pltpu.CompilerParams(
            dimension_semantics=("parallel", "parallel"), vmem_limit_bytes=VMEM_LIMIT),
        name="inproj",
    )(h, an, w_perm, cos_t, sin_t, g256, vecs, w2_pad, sm_bias)


def _prep_layer_params(l, attn_norm, w_in, dsa_q_norm, dsa_k_norm, fox_q_norm, fox_k_norm,
                       fox_f_bias, gla_gate_w2, gla_gate_b):
    w = w_in[l]
    splits = np.cumsum([256, 64, 64, 256, 4, 64, 256, 256, 256, 4, 256, 256, 512, 512, 16])[:-1]
    (dq, dk, dv, iq, iw, ik, fq, fk, fv, ff, gq, gk, gv, gg, glr) = jnp.split(w, splits, axis=1)
    small = jnp.concatenate(
        [iw, ff, glr, jnp.zeros((D_MODEL, LANES - 4 - 4 - GLA_RANK), w.dtype)], axis=1)
    z64 = jnp.zeros((D_MODEL, HEAD_DIM), w.dtype)
    per_head = lambda t: [part for h in range(N_HEADS)
                          for part in (t[:, h * HEAD_DIM:(h + 1) * HEAD_DIM], z64)]
    w_perm = jnp.concatenate([dq, dk, dk, dv, z64, iq, ik, ik, *per_head(fq), *per_head(fk),
                              *per_head(fv), gq, gk, gv, gg, small], axis=1).astype(BF16)
    tile4 = lambda g: jnp.tile(g, N_HEADS)
    vecs = jnp.zeros((8, 256), F32)
    vecs = vecs.at[0].set(tile4(dsa_q_norm[l])).at[1].set(tile4(dsa_k_norm[l]))
    vecs = vecs.at[2].set(tile4(fox_q_norm[l])).at[3].set(tile4(fox_k_norm[l]))
    vecs = vecs.at[4].set(gla_gate_b[l])
    w2_pad = jnp.zeros((LANES, 256), F32).at[SM_GLR:SM_GLR + GLA_RANK].set(gla_gate_w2[l]).astype(BF16)
    sm_bias = jnp.zeros((8, LANES), F32).at[0, SM_FF:SM_FF + N_HEADS].set(fox_f_bias[l])
    return attn_norm[l][None, :], w_perm, vecs, w2_pad, sm_bias


def _rope_tables(t_work):
    inv = jnp.power(ROPE_THETA, -jnp.arange(ROPE_HALF, dtype=F32) * 2.0 / ROPE_DIM)
    ang = jnp.arange(t_work).astype(F32)[:, None] * inv[None, :]
    cos, sin = jnp.cos(ang), jnp.sin(ang)
    rest = HEAD_DIM - ROPE_DIM
    cos64 = jnp.concatenate([cos, cos, jnp.ones((t_work, rest), F32)], axis=1)
    sin64 = jnp.concatenate([-sin, sin, jnp.zeros((t_work, rest), F32)], axis=1)
    return jnp.tile(cos64, (1, N_HEADS)), jnp.tile(sin64, (1, N_HEADS))


def _group_matrix():
    idx = np.arange(256) // HEAD_DIM
    return jnp.asarray((idx[:, None] == idx[None, :]).astype(np.float32), dtype=BF16)


DSA_TQ = 256
DSA_TK = 512
KEY_NEG_INF = -2139095041
KEY_NEG_ZERO = -1
KEY_NEG_MIN_NORMAL = -8388609
NEG_MIN_NORMAL = -1.1754943508222875e-38
SEARCH_FEW = 4.0
SEARCH_MANY = 16.0
UNCHECKED_PROBES = 12
GUIDED_PROBES = 64
MAX_PROBES = GUIDED_PROBES + 40
COUNT_ROWS = 128


def _key_to_f32(key):
    bits = key ^ ((key >> 31) & 0x7FFFFFFF)
    return lax.bitcast_convert_type(bits, F32)


def _f32_to_key(value):
    bits = lax.bitcast_convert_type(value, jnp.int32)
    return bits ^ ((bits >> 31) & 0x7FFFFFFF)


def _head_lane_mask(shape, head):
    lane = lax.broadcasted_iota(jnp.int32, shape, 1)
    return (lane < HEAD_DIM) if head % 2 == 0 else (lane >= HEAD_DIM)


def _masked_heads(slabs):
    return [jnp.where(_head_lane_mask(slabs[h // 2].shape, h), slabs[h // 2],
                      jnp.zeros_like(slabs[h // 2])) for h in range(N_HEADS)]


def _softmax_step(s, m, acc, v_aug):
    m_new = jnp.maximum(m, jnp.max(s, axis=-1, keepdims=True))
    p = jnp.exp2(s - m_new)
    return m_new, acc * jnp.exp2(m - m_new) + _dot(p.astype(BF16), v_aug)


def _normalise_heads(accs):
    outs = [acc / pltpu.roll(acc, HEAD_DIM, axis=1) for acc in accs]
    lane = lax.broadcasted_iota(jnp.int32, outs[0].shape, 1)
    return [jnp.where(lane < HEAD_DIM, outs[2 * p], pltpu.roll(outs[2 * p + 1], HEAD_DIM, axis=1))
            for p in range(2)]


def _dsa_kernel(k_top, dq_ref, kk_ref, va_ref, iq_ref, ik_ref, wt_ref, tri_ref, kn_ref, o_ref,
                s_ref):
    i = pl.program_id(1)
    q0 = i * DSA_TQ
    n_tiles = (q0 + DSA_TQ + DSA_TK - 1) // DSA_TK
    last = n_tiles - 1
    key = lax.broadcasted_iota(jnp.int32, (DSA_TK, DSA_TQ), 0)
    qry = q0 + lax.broadcasted_iota(jnp.int32, (DSA_TK, DSA_TQ), 1)
    head = lambda x, h: x[:, h * DSA_TQ:(h + 1) * DSA_TQ]

    iq_all = jnp.concatenate(
        _masked_heads([iq_ref[0, :, 0:LANES], iq_ref[0, :, LANES:2 * LANES]]), axis=0)
    w_h = [wt_ref[0, SM_IW + h:SM_IW + h + 1, :] for h in range(N_HEADS)]

    def score_tile(j, carry, causal):
        k0 = pl.multiple_of(j * DSA_TK, DSA_TK)
        dots = jnp.maximum(_dot_nt(ik_ref[0, pl.ds(k0, DSA_TK), :], iq_all), 0.0)
        s = w_h[0] * head(dots, 0)
        for h in range(1, N_HEADS):
            s = s + w_h[h] * head(dots, h)
        if causal:
            s = jnp.where(k0 + key <= qry, s, NEG)
        s_ref[pl.ds(k0, DSA_TK), :] = s
        row_max, above_zero, above_neg = carry
        for part in range(DSA_TK // COUNT_ROWS):
            chunk = s[part * COUNT_ROWS:(part + 1) * COUNT_ROWS]
            above_zero = above_zero + jnp.where(chunk > 0.0, 1.0, 0.0)
            above_neg = above_neg + jnp.where(chunk > NEG_MIN_NORMAL, 1.0, 0.0)
        return jnp.maximum(row_max, jnp.max(s, axis=0, keepdims=True)), above_zero, above_neg

    zeros = jnp.zeros((COUNT_ROWS, DSA_TQ), F32)
    row_max, above_zero, above_neg = score_tile(last, lax.fori_loop(
        0, last, functools.partial(score_tile, causal=False),
        (jnp.full((1, DSA_TQ), NEG, F32), zeros, zeros)), causal=True)

    kf = jnp.float32(k_top)

    def count_above(t):
        tb = jnp.broadcast_to(t, (COUNT_ROWS, DSA_TQ))

        def body(j, acc, span):
            k0 = pl.multiple_of(j * span, span)
            for part in range(span // COUNT_ROWS):
                s = s_ref[pl.ds(k0 + part * COUNT_ROWS, COUNT_ROWS), :]
                acc = acc + jnp.where(s > tb, 1.0, 0.0)
            return acc

        acc = lax.fori_loop(0, n_tiles // 2, functools.partial(body, span=2 * DSA_TK),
                            jnp.zeros((COUNT_ROWS, DSA_TQ), F32))
        acc = lax.fori_loop(2 * (n_tiles // 2), n_tiles, functools.partial(body, span=DSA_TK), acc)
        return jnp.sum(acc, axis=0, keepdims=True)

    def midpoint(lo, hi):
        return (lo >> 1) + (hi >> 1) + (lo & hi & 1)

    def converged(lo, hi):
        return (midpoint(lo, hi) == lo) | ((lo >= KEY_NEG_MIN_NORMAL) & (hi <= 0))

    col1 = lambda value, dtype: jnp.full((1, DSA_TQ), value, dtype)

    def absorb(state, probe, c):
        lo, hi, clo, chi, wlo, whi, side = state
        live = probe != lo
        up = live & (c >= kf)
        down = live & (c <= kf)
        wlo = jnp.where(down & (side < 0), wlo * 0.5, jnp.where(up, 1.0, wlo))
        whi = jnp.where(up & (side > 0), whi * 0.5, jnp.where(down, 1.0, whi))
        side = jnp.where(up, 1, jnp.where(down, -1, side))
        lo, clo = jnp.where(up, probe, lo), jnp.where(up, c, clo)
        hi, chi = jnp.where(down, probe, hi), jnp.where(down, c, chi)
        return lo, hi, clo, chi, wlo, whi, side

    def probe_once(it, state):
        lo, hi, clo, chi, wlo, whi, side = state
        f_lo, f_hi = _key_to_f32(lo), _key_to_f32(hi)
        target = kf - 0.5
        log_count = lambda c: jnp.log2(jnp.maximum(c, 0.25))
        many = clo - chi > SEARCH_MANY
        g_lo = jnp.where(many, log_count(clo) - np.log2(k_top - 0.5), clo - target) * wlo
        g_hi = jnp.where(many, np.log2(k_top - 0.5) - log_count(chi), target - chi) * whi
        halve = (clo - chi <= SEARCH_FEW) | (col1(it % 8, jnp.int32) == 7)
        guess = _f32_to_key(f_lo + (f_hi - f_lo) * jnp.where(halve, 0.5, g_lo / (g_lo + g_hi)))
        guided = col1(it, jnp.int32) < GUIDED_PROBES
        probe = jnp.where((guess > lo) & (guess < hi) & guided, guess, midpoint(lo, hi))
        probe = jnp.where(converged(lo, hi), lo, probe)
        return absorb(state, probe, count_above(_key_to_f32(probe)))

    def search_cond(carry):
        it, pending = carry[0], carry[1]
        return (pending > 0) & (it < MAX_PROBES)

    def search_body(carry):
        it, state = carry[0], carry[2]
        state = probe_once(it + 1, probe_once(it, state))
        return it + 2, jnp.max(jnp.where(converged(state[0], state[1]), 0, 1)), state

    n_swept = (n_tiles * DSA_TK).astype(F32)
    state = (col1(KEY_NEG_INF, jnp.int32), _f32_to_key(row_max),
             jnp.broadcast_to(n_swept, (1, DSA_TQ)), col1(0.0, F32), col1(1.0, F32), col1(1.0, F32),
             col1(0, jnp.int32))
    for fixed, counts in ((KEY_NEG_ZERO, above_zero), (KEY_NEG_MIN_NORMAL, above_neg)):
        inside = (state[0] < fixed) & (fixed < state[1])
        state = absorb(state, jnp.where(inside, fixed, state[0]),
                       jnp.sum(counts, axis=0, keepdims=True))
    state = lax.fori_loop(0, UNCHECKED_PROBES, probe_once, state)
    state = lax.while_loop(search_cond, search_body,
                           (jnp.int32(UNCHECKED_PROBES), jnp.int32(1), state))[2]
    thr = _key_to_f32(state[1])
    n_ties = kf - state[3]

    to_column = lambda r: jnp.broadcast_to(r, (8, DSA_TQ)).T[:, 0:1]
    thr_c, ties_c = to_column(thr), to_column(n_ties)
    q_all = jnp.concatenate(
        _masked_heads([dq_ref[0, :, 0:LANES], dq_ref[0, :, LANES:2 * LANES]]), axis=0)
    tri = tri_ref[...]
    tri_lo = jnp.where(lax.broadcasted_iota(jnp.int32, (LANES, LANES), 0)
                       >= lax.broadcasted_iota(jnp.int32, (LANES, LANES), 1), 1.0, 0.0).astype(BF16)
    qrow = q0 + lax.broadcasted_iota(jnp.int32, (DSA_TQ, DSA_TK), 0)
    kcol = lax.broadcasted_iota(jnp.int32, (DSA_TQ, DSA_TK), 1)
    n_blocks = DSA_TK // LANES

    reach = _logit_reach(q_all, jnp.max(kn_ref[0], axis=0)[KN_DSA:KN_DSA + 1, 0:1])

    def selected_logits(j, seen, causal):
        k0 = pl.multiple_of(j * DSA_TK, DSA_TK)
        s = s_ref[pl.ds(k0, DSA_TK), :].T
        tie = s == thr_c
        tie_b = jnp.where(tie, 1.0, 0.0).astype(BF16)
        local = [_dot(tie_b[:, b * LANES:(b + 1) * LANES], tri) for b in range(n_blocks)]
        ranks = []
        for b in range(n_blocks):
            ranks.append(local[b] + seen)
            seen = seen + local[b][:, LANES - 1:LANES]
        sel = (s > thr_c) | (tie & (jnp.concatenate(ranks, axis=1) <= ties_c))
        if causal:
            sel = sel & (k0 + kcol <= qrow)
        logits = _dot_nt(q_all, kk_ref[0, pl.ds(k0, DSA_TK), :])
        logits = jnp.where(sel[None], logits.reshape(N_HEADS, DSA_TQ, DSA_TK), NEG)
        return seen, logits.reshape(N_HEADS * DSA_TQ, DSA_TK), va_ref[0, pl.ds(k0, DSA_TK), :]

    def capped(j, carry, causal):
        seen, acc = carry
        k0 = pl.multiple_of(j * DSA_TK, DSA_TK)
        s = s_ref[pl.ds(k0, DSA_TK), :]
        tie = s == thr
        tie_b = jnp.where(tie, 1.0, 0.0).astype(BF16)
        ranks = []
        for b in range(n_blocks):
            local = _dot(tri_lo, tie_b[b * LANES:(b + 1) * LANES, :])
            ranks.append(local + seen)
            seen = seen + local[LANES - 1:LANES, :]
        sel = (s > thr) | (tie & (jnp.concatenate(ranks, axis=0) <= n_ties))
        if causal:
            sel = sel & (k0 + key <= qry)
        keep = jnp.where(sel, 1.0, 0.0).astype(BF16).T
        logits = _dot_nt(q_all, kk_ref[0, pl.ds(k0, DSA_TK), :])
        p = jnp.exp2(logits - reach).astype(BF16).reshape(N_HEADS, DSA_TQ, DSA_TK) * keep[None]
        return seen, acc + _dot(p.reshape(N_HEADS * DSA_TQ, DSA_TK), va_ref[0, pl.ds(k0, DSA_TK), :])

    def online(j, carry, causal):
        seen, logits, va_t = selected_logits(j, carry[0], causal)
        return (seen,) + _softmax_step(logits, carry[1], carry[2], va_t)

    def sweep(step, *stats):
        carry = stats + (jnp.zeros((N_HEADS * DSA_TQ, LANES), F32),)
        carry = lax.fori_loop(0, last, functools.partial(step, causal=False), carry)
        return step(last, carry, causal=True)[-1]

    acc = lax.cond(2.0 * jnp.max(reach) <= SAFE_LOG2_SPAN,
                   functools.partial(sweep, capped, jnp.zeros((1, DSA_TQ), F32)),
                   functools.partial(sweep, online, jnp.zeros((DSA_TQ, 1), F32),
                                     jnp.full((N_HEADS * DSA_TQ, 1), NEG, F32)))
    o_ref[0] = jnp.concatenate(
        _normalise_heads([acc[h * DSA_TQ:(h + 1) * DSA_TQ] for h in range(N_HEADS)]),
        axis=1).astype(o_ref.dtype)


def _dsa_attention(dq, dkk, dva, iq, ikk, small_t, key_norms, k_top):
    bsz, t_work, _ = dq.shape
    tri = jnp.asarray(np.triu(np.ones((LANES, LANES), np.float32)), dtype=BF16)
    tile = lambda width: pl.BlockSpec((1, DSA_TQ, width), lambda b, i: (b, i, 0))
    full = pl.BlockSpec((1, t_work, LANES), lambda b, i: (b, 0, 0), pipeline_mode=pl.Buffered(1))
    return pl.pallas_call(
        functools.partial(_dsa_kernel, k_top),
        grid=(bsz, t_work // DSA_TQ),
        in_specs=[tile(256), full, full, tile(256), full,
                  pl.BlockSpec((1, 8, DSA_TQ), lambda b, i: (b, 0, i)),
                  pl.BlockSpec((LANES, LANES), lambda b, i: (0, 0)),
                  pl.BlockSpec((1,) + key_norms.shape[1:], lambda b, i: (b, 0, 0, 0))],
        out_specs=tile(256),
        out_shape=jax.ShapeDtypeStruct((bsz, t_work, 256), BF16),
        scratch_shapes=[pltpu.VMEM((t_work, DSA_TQ), F32)],
        compiler_params=pltpu.CompilerParams(
            dimension_semantics=("parallel", "parallel"), vmem_limit_bytes=VMEM_LIMIT),
        name="dsa_attention",
    )(dq, dkk, dva, iq, ikk, small_t, tri, key_norms)


FOX_TQ = 512
FOX_TK = 512
CUM_T = 256


def _fox_prep_kernel(x_ref, tri_ref, place_ref, k_ref, xt_ref, kb_ref, carry_ref):
    @pl.when(pl.program_id(1) == 0)
    def _():
        carry_ref[...] = jnp.zeros_like(carry_ref)

    tri = tri_ref[...]
    x = x_ref[0]
    h1, h2, h3 = _split3(x)
    c = _dot(tri, h1) + _dot(tri, h2) + _dot(tri, h3) + carry_ref[0:1, :]
    carry_ref[...] = jnp.broadcast_to(c[CUM_T - 1:CUM_T, :], carry_ref.shape)
    xt_ref[0] = x.T[0:8, :]
    terms = _split3(c * -LOG2E)
    bias = sum(_dot(terms[t], place_ref[t]) for t in range(BIAS_TERMS))
    lane = lax.broadcasted_iota(jnp.int32, bias.shape, 1) % LANES
    kb_ref[0] = jnp.where((lane >= HEAD_DIM) & (lane < HEAD_DIM + BIAS_TERMS),
                          bias.astype(BF16), k_ref[0])


def _fox_prep(small, fk):
    bsz, t_work, width = fk.shape
    tri = jnp.asarray(np.tril(np.ones((CUM_T, CUM_T), np.float32)), dtype=BF16)
    place = np.zeros((BIAS_TERMS, LANES, width), np.float32)
    for t in range(BIAS_TERMS):
        for h in range(N_HEADS):
            place[t, SM_FF + h, h * LANES + HEAD_DIM + t] = 1.0
    rows = lambda w: pl.BlockSpec((1, CUM_T, w), lambda b, j: (b, j, 0))
    return pl.pallas_call(
        _fox_prep_kernel,
        grid=(bsz, t_work // CUM_T),
        in_specs=[rows(LANES), pl.BlockSpec((CUM_T, CUM_T), lambda b, j: (0, 0)),
                  pl.BlockSpec((BIAS_TERMS, LANES, width), lambda b, j: (0, 0, 0)), rows(width)],
        out_specs=[pl.BlockSpec((1, 8, CUM_T), lambda b, j: (b, 0, j)), rows(width)],
        out_shape=[jax.ShapeDtypeStruct((bsz, 8, t_work), F32),
                   jax.ShapeDtypeStruct(fk.shape, BF16)],
        scratch_shapes=[pltpu.VMEM((8, LANES), F32)],
        compiler_params=pltpu.CompilerParams(dimension_semantics=("parallel", "arbitrary")),
        name="fox_prep",
    )(small, tri, jnp.asarray(place, dtype=BF16), fk)


def _logit_reach(q, kmax_sq):
    qf = q.astype(F32)
    return jnp.sqrt(jnp.sum(qf * qf, axis=-1, keepdims=True) * kmax_sq) * NORM_MARGIN


def _fox_kernel(q_ref, k_ref, v_ref, kn_ref, o_ref):
    i = pl.program_id(1)
    q0 = pl.multiple_of(i * FOX_TQ, FOX_TQ)
    n_full = q0 // FOX_TK
    row = q0 + lax.broadcasted_iota(jnp.int32, (FOX_TQ, FOX_TK), 0)
    col = n_full * FOX_TK + lax.broadcasted_iota(jnp.int32, (FOX_TQ, FOX_TK), 1)
    heads = [slice(h * LANES, (h + 1) * LANES) for h in range(N_HEADS)]

    lane = lax.broadcasted_iota(jnp.int32, (FOX_TQ, LANES), 1)
    bias_lanes = (lane >= HEAD_DIM) & (lane < HEAD_DIM + BIAS_TERMS)
    kmax_sq = jnp.max(kn_ref[0], axis=0)
    caps, span = [], jnp.float32(0.0)
    for h in range(N_HEADS):
        q_h = q_ref[0, :, heads[h]]
        reach = _logit_reach(jnp.where(lane < HEAD_DIM, q_h, jnp.zeros_like(q_h)),
                             kmax_sq[KN_FOX + h:KN_FOX + h + 1, 0:1])
        own = k_ref[0, pl.ds(q0, FOX_TQ), heads[h]].astype(F32)
        caps.append(reach + jnp.sum(jnp.where(bias_lanes, own, 0.0), axis=-1, keepdims=True))
        span = jnp.maximum(span, 2.0 * jnp.max(reach))

    def logits(j, h, diag):
        k0 = pl.multiple_of(j * FOX_TK, FOX_TK)
        s = _dot_nt(q_ref[0, :, heads[h]], k_ref[0, pl.ds(k0, FOX_TK), heads[h]])
        return jnp.where(col <= row, s, NEG) if diag else s

    def values(j, h):
        return v_ref[0, pl.ds(pl.multiple_of(j * FOX_TK, FOX_TK), FOX_TK), heads[h]]

    def capped(j, accs, diag):
        return tuple(accs[h] + _dot(jnp.exp2(logits(j, h, diag) - caps[h]).astype(BF16), values(j, h))
                     for h in range(N_HEADS))

    def online(j, carry, diag):
        return tuple(_softmax_step(logits(j, h, diag), *carry[h], values(j, h))
                     for h in range(N_HEADS))

    def capped_sweep():
        accs = tuple(jnp.zeros((FOX_TQ, LANES), F32) for _ in range(N_HEADS))
        accs = lax.fori_loop(0, n_full, functools.partial(capped, diag=False), accs)
        return capped(n_full, accs, diag=True)

    def online_sweep():
        carry = tuple((jnp.full((FOX_TQ, 1), NEG, F32), jnp.zeros((FOX_TQ, LANES), F32))
                      for _ in range(N_HEADS))
        carry = lax.fori_loop(0, n_full, functools.partial(online, diag=False), carry)
        return tuple(acc for _, acc in online(n_full, carry, diag=True))

    accs = lax.cond(span <= SAFE_LOG2_SPAN, capped_sweep, online_sweep)
    o_ref[0] = jnp.concatenate(_normalise_heads(list(accs)), axis=1).astype(o_ref.dtype)


def _fox_attention(fqa, fkb, fva, key_norms):
    bsz, t_work, width = fqa.shape
    full = pl.BlockSpec((1, t_work, width), lambda b, i: (b, 0, 0), pipeline_mode=pl.Buffered(1))
    return pl.pallas_call(
        _fox_kernel,
        grid=(bsz, t_work // FOX_TQ),
        in_specs=[pl.BlockSpec((1, FOX_TQ, width), lambda b, i: (b, i, 0)), full, full,
                  pl.BlockSpec((1,) + key_norms.shape[1:], lambda b, i: (b, 0, 0, 0))],
        out_specs=pl.BlockSpec((1, FOX_TQ, 2 * LANES), lambda b, i: (b, i, 0)),
        out_shape=jax.ShapeDtypeStruct((bsz, t_work, 2 * LANES), BF16),
        compiler_params=pltpu.CompilerParams(
            dimension_semantics=("parallel", "parallel"), vmem_limit_bytes=VMEM_LIMIT),
        name="fox_attention",
    )(fqa, fkb, fva, key_norms)


GLA_TG = 256
GLA_NSUB = GLA_CHUNK // GLA_SUB


def _gla_kernel(q_ref, k_ref, v_ref, g_ref, la_ref, tri_ref, e_ref, gn_ref, o_ref, st_ref):
    @pl.when(pl.program_id(1) == 0)
    def _():
        st_ref[...] = jnp.zeros_like(st_ref)

    tri = tri_ref[...]
    emat = e_ref[...]
    lane = lax.broadcasted_iota(jnp.int32, (GLA_CHUNK, LANES), 1)
    rowblk = lax.broadcasted_iota(jnp.int32, (GLA_CHUNK, LANES), 0) // GLA_SUB
    tblk = lax.broadcasted_iota(jnp.int32, (GLA_CHUNK, GLA_CHUNK), 0) // GLA_SUB
    sblk = lax.broadcasted_iota(jnp.int32, (GLA_CHUNK, GLA_CHUNK), 1) // GLA_SUB
    trow = lax.broadcasted_iota(jnp.int32, (GLA_SUB, 256), 0)

    def chunk(c, carry):
        r0 = pl.multiple_of(c * GLA_CHUNK, GLA_CHUNK)
        rows = pl.ds(r0, GLA_CHUNK)
        h1, h2, h3 = _split3(la_ref[0, rows, :])
        b = _dot(tri, h1) + _dot(tri, h2) + _dot(tri, h3)
        q = q_ref[0, rows, :]
        k = k_ref[0, rows, :]
        v = v_ref[0, rows, :]
        b_last = b[GLA_CHUNK - 1:GLA_CHUNK, :]
        qd = q * jnp.exp(b)
        kd = (k * jnp.exp(b_last - b)).astype(BF16)
        starts = [jnp.zeros((1, 256), F32)] + [b[GLA_SUB * i - 1:GLA_SUB * i, :]
                                               for i in range(1, GLA_NSUB)]
        bsel = jnp.concatenate([jnp.broadcast_to(s, (GLA_SUB, 256)) for s in starts], axis=0)
        qn = q * jnp.exp(b - bsel)

        diag = []
        for i in range(GLA_NSUB):
            rs = slice(GLA_SUB * i, GLA_SUB * (i + 1))
            b_i, q_i, k_i = b[rs], q[rs], k[rs]
            v_i = v[rs].astype(F32)
            ps = []
            for s in range(GLA_SUB):
                d = jnp.exp(jnp.minimum(b_i - b_i[s:s + 1], 0.0))
                ps.append(jnp.where(trow >= s, q_i * d * k_i[s:s + 1], 0.0).astype(BF16))
            r = _dot(jnp.concatenate(ps, axis=0), emat)
            od = r[0:GLA_SUB] * v_i[0:1]
            for s in range(1, GLA_SUB):
                od = od + r[GLA_SUB * s:GLA_SUB * (s + 1)] * v_i[s:s + 1]
            diag.append(od)
        o_diag = jnp.concatenate(diag, axis=0)

        for slab in range(2):
            ls = slice(slab * LANES, (slab + 1) * LANES)
            qn_s, k_s, b_s = qn[:, ls], k[:, ls], b[:, ls]
            khat = jnp.concatenate(
                [(k_s * jnp.exp(jnp.minimum(starts[i][:, ls] - b_s, 0.0))).astype(BF16)
                 for i in range(1, GLA_NSUB)], axis=1)
            for half in range(2):
                head = 2 * slab + half
                hs = slice(head * GLA_DV, (head + 1) * GLA_DV)
                in_head = (lane < HEAD_DIM) if half == 0 else (lane >= HEAD_DIM)
                qm = jnp.where(in_head, qn_s, 0.0)
                qhat = jnp.concatenate([jnp.where(rowblk == i, qm, 0.0).astype(BF16)
                                        for i in range(1, GLA_NSUB)], axis=1)
                att = jnp.where(sblk < tblk, _dot_nt(qhat, khat), 0.0)
                v_h = v[:, hs]
                st = st_ref[head]
                o = (_dot_nt(jnp.where(in_head, qd[:, ls], 0.0).astype(BF16), st.astype(BF16))
                     + _dot(att.astype(BF16), v_h) + o_diag[:, hs])
                st_ref[head] = st * jnp.exp(b_last[:, ls]) + _dot_tn(v_h, kd[:, ls])
                y = o * lax.rsqrt(jnp.mean(o * o, axis=-1, keepdims=True) + EPS) * gn_ref[:, hs]
                o_ref[0, rows, hs] = (y * _silu(g_ref[0, rows, hs])).astype(o_ref.dtype)
        return carry

    lax.fori_loop(0, GLA_TG // GLA_CHUNK, chunk, 0)


def _gla(gq, gk, gv, gg, la, gain):
    bsz, t_work, _ = gq.shape
    tri = jnp.asarray(np.tril(np.ones((GLA_CHUNK, GLA_CHUNK), np.float32)), dtype=BF16)
    emat = jnp.asarray(
        (np.arange(256)[:, None] // HEAD_DIM == np.arange(512)[None, :] // GLA_DV).astype(np.float32),
        dtype=BF16)
    rows = lambda width: pl.BlockSpec((1, GLA_TG, width), lambda b, j: (b, j, 0))
    const = lambda shape: pl.BlockSpec(shape, lambda b, j: (0,) * len(shape))
    return pl.pallas_call(
        _gla_kernel,
        grid=(bsz, t_work // GLA_TG),
        in_specs=[rows(256), rows(256), rows(512), rows(512), rows(256),
                  const((GLA_CHUNK, GLA_CHUNK)), const((256, 512)), const((1, 512))],
        out_specs=rows(512),
        out_shape=jax.ShapeDtypeStruct((bsz, t_work, 512), BF16),
        scratch_shapes=[pltpu.VMEM((N_HEADS, GLA_DV, LANES), F32)],
        compiler_params=pltpu.CompilerParams(
            dimension_semantics=("parallel", "arbitrary"), vmem_limit_bytes=VMEM_LIMIT),
        name="gla",
    )(gq, gk, gv, gg, la, tri, emat, gain)


FFN_CHUNK = 256


def _ffn_kernel(h_ref, oa_ref, ob_ref, oc_ref, wo_ref, fn_ref, wgu_ref, wd_ref, out_ref):
    h1 = (h_ref[0] + _dot(oa_ref[0], wo_ref[0:256, :]) + _dot(ob_ref[0], wo_ref[256:512, :])
          + _dot(oc_ref[0], wo_ref[512:1024, :]))
    ms = jnp.mean(h1 * h1, axis=-1, keepdims=True)
    f = (h1 * lax.rsqrt(ms + EPS) * fn_ref[...]).astype(BF16)
    out_ref[0] = h1
    for c in range(0, D_FF, FFN_CHUNK):
        gate = _dot(f, wgu_ref[:, c:c + FFN_CHUNK])
        up = _dot(f, wgu_ref[:, D_FF + c:D_FF + c + FFN_CHUNK])
        out_ref[0] += _dot((_silu(gate) * up).astype(BF16), wd_ref[c:c + FFN_CHUNK, :])


def _outproj_ffn(h, oa, ob, oc, wo, fn, wgu, wd):
    bsz, t_work, _ = h.shape
    tm = _row_tile(t_work)
    rows = lambda width: pl.BlockSpec((1, tm, width), lambda b, j: (b, j, 0))
    const = lambda shape: pl.BlockSpec(shape, lambda b, j: (0,) * len(shape),
                                       pipeline_mode=pl.Buffered(1))
    return pl.pallas_call(
        _ffn_kernel,
        grid=(bsz, t_work // tm),
        in_specs=[rows(D_MODEL), rows(256), rows(256), rows(512), const((D_MODEL, D_MODEL)),
                  const((1, D_MODEL)), const((D_MODEL, 2 * D_FF)), const((D_FF, D_MODEL))],
        out_specs=rows(D_MODEL),
        out_shape=jax.ShapeDtypeStruct(h.shape, F32),
        compiler_params=pltpu.CompilerParams(
            dimension_semantics=("parallel", "parallel"), vmem_limit_bytes=VMEM_LIMIT),
        name="outproj_ffn",
    )(h, oa, ob, oc, wo, fn, wgu, wd)


def kernel(x, meta_tokens, attn_norm, w_in, dsa_q_norm, dsa_k_norm, fox_q_norm, fox_k_norm,
           fox_f_bias, gla_gate_w2, gla_gate_b, gla_out_norm, w_out, ffn_norm, w_gate_up, w_down):
    bsz, seq, _ = x.shape
    n_tok = N_META + seq
    t_work = _work_len(n_tok)
    meta = jnp.broadcast_to(meta_tokens[None].astype(x.dtype), (bsz, N_META, D_MODEL))
    h = jnp.concatenate([meta, x, jnp.zeros((bsz, t_work - n_tok, D_MODEL), x.dtype)], axis=1)
    cos_t, sin_t = _rope_tables(t_work)
    g256 = _group_matrix()
    k_top = min(TOPK_MAX, seq // 4)
    for l in range(w_in.shape[0]):
        an, w_perm, vecs, w2_pad, sm_bias = _prep_layer_params(
            l, attn_norm, w_in, dsa_q_norm, dsa_k_norm, fox_q_norm, fox_k_norm, fox_f_bias,
            gla_gate_w2, gla_gate_b)
        (dq, dkk, dva, iq, ikk, fq, fk, fva, gq, gk, gv, gg, la, small, key_norms) = _inproj(
            h, an, w_perm, cos_t, sin_t, g256, vecs, w2_pad, sm_bias)
        small_t, fkb = _fox_prep(small, fk)
        oa = _dsa_attention(dq, dkk, dva, iq, ikk, small_t, key_norms, k_top)
        ob = _fox_attention(fq, fkb, fva, key_norms)
        oc = _gla(gq, gk, gv, gg, la, jnp.tile(gla_out_norm[l], N_HEADS)[None, :])
        h = _outproj_ffn(h, oa, ob, oc, w_out[l].astype(BF16), ffn_norm[l][None, :],
                         w_gate_up[l].astype(BF16), w_down[l].astype(BF16))
    return h[:, N_META:n_tok]
```

```python
import functools

import numpy as np
import jax
import jax.numpy as jnp
from jax import lax
from jax.experimental import pallas as pl
from jax.experimental.pallas import tpu as pltpu

F32 = jnp.float32
BF16 = jnp.bfloat16

D_MODEL = 1024
HEAD_DIM = 64
N_META = 16
ROPE_THETA = 500000.0
ROPE_DIM = HEAD_DIM // 4
ROPE_HALF = ROPE_DIM // 2
NEG = -1e30
EPS = 1e-6

N_HEADS = 4
TOPK_MAX = 256
GLA_DV = 128
GLA_RANK = 16
GLA_TAU = 16.0
GLA_CHUNK = 64
GLA_SUB = 16
D_FF = 2816

LOG2E = 1.4426950408889634
BIAS_TERMS = 3
KN_DSA, KN_FOX = 0, 1
NORM_MARGIN = 1.01
SAFE_LOG2_SPAN = 100.0
LANES = 128
SEQ_ALIGN = 512
VMEM_LIMIT = 56 * 1024 * 1024

_SLABS = (("dq", 256), ("dkk", 128), ("dva", 128), ("iq", 256), ("ikk", 128),
          ("fq", 256), ("fk", 256), ("fv", 256), ("gq", 256), ("gk", 256),
          ("gv", 512), ("gg", 512), ("small", 128))
_SLAB_OFF = {}
_off = 0
for _name, _width in _SLABS:
    _SLAB_OFF[_name] = (_off, _width)
    _off += _width
N_PROJ = _off
SM_IW, SM_FF, SM_GLR = 0, 4, 8


def _work_len(n_tok):
    return -(-n_tok // SEQ_ALIGN) * SEQ_ALIGN


def _row_tile(t_work):
    for cand in (768, 640, 512):
        if t_work % cand == 0:
            return cand
    raise ValueError(f"unsupported working length {t_work}")


def _dot(a, b):
    return jnp.dot(a, b, preferred_element_type=F32)


def _dot_nt(a, b):
    return lax.dot_general(a, b, (((1,), (1,)), ((), ())), preferred_element_type=F32)


def _dot_tn(a, b):
    return lax.dot_general(a, b, (((0,), (0,)), ((), ())), preferred_element_type=F32)


def _split3(x):
    h1 = x.astype(BF16)
    r1 = x - h1.astype(F32)
    h2 = r1.astype(BF16)
    h3 = (r1 - h2.astype(F32)).astype(BF16)
    return h1, h2, h3


def _log_sigmoid(x):
    return jnp.minimum(x, 0.0) - jnp.log1p(jnp.exp(-jnp.abs(x)))


def _silu(x):
    return x / (1.0 + jnp.exp(-x))


def _group_rms(y, gmat, gain):
    yy = y * y
    hi = yy.astype(BF16)
    lo = (yy - hi.astype(F32)).astype(BF16)
    ss = _dot(hi, gmat) + _dot(lo, gmat)
    return y * lax.rsqrt(ss * (1.0 / HEAD_DIM) + EPS) * gain


def _rope(y, cos, sin):
    width = y.shape[-1]
    lane = lax.broadcasted_iota(jnp.int32, y.shape, 1) % HEAD_DIM
    upper = pltpu.roll(y, width - ROPE_HALF, axis=1)
    lower = pltpu.roll(y, ROPE_HALF, axis=1)
    partner = jnp.where(lane < ROPE_HALF, upper, lower)
    return y * cos + partner * sin


def _inproj_kernel(x_ref, an_ref, w_ref, cos_ref, sin_ref, g256_ref, vec_ref, w2_ref, sm_ref,
                   dq_ref, dkk_ref, dva_ref, iq_ref, ikk_ref, fq_ref, fk_ref, fva_ref,
                   gq_ref, gk_ref, gv_ref, gg_ref, la_ref, small_ref, kn_ref):
    x = x_ref[0]
    ms = jnp.mean(x * x, axis=-1, keepdims=True)
    a = (x * lax.rsqrt(ms + EPS) * an_ref[...]).astype(BF16)

    def proj(name):
        off, width = _SLAB_OFF[name]
        return _dot(a, w_ref[:, off:off + width])

    def ones_in_upper_half(y):
        lane = lax.broadcasted_iota(jnp.int32, y.shape, 1) % LANES
        return jnp.where(lane < HEAD_DIM, y, 1.0)

    cos = cos_ref[...]
    sin = sin_ref[...]
    g256 = g256_ref[...]
    g128 = g256[:LANES, :LANES]
    dqn, dkn, fqn, fkn = vec_ref[0:1, :], vec_ref[1:2, :LANES], vec_ref[2:3, :], vec_ref[3:4, :]
    scale = HEAD_DIM ** -0.5

    def max_sq_norm(k):
        kf = k.astype(F32)
        return jnp.max(_dot((kf * kf).astype(BF16), g128), axis=0, keepdims=True)

    dq_ref[0] = (_rope(_group_rms(proj("dq"), g256, dqn), cos, sin) * (scale * LOG2E)).astype(BF16)
    dkk = _rope(_group_rms(proj("dkk"), g128, dkn), cos[:, :LANES], sin[:, :LANES]).astype(BF16)
    dkk_ref[0] = dkk
    key_norms = [max_sq_norm(dkk)]
    dva_ref[0] = ones_in_upper_half(proj("dva")).astype(BF16)
    iq_ref[0] = (_rope(proj("iq"), cos, sin) * scale).astype(BF16)
    ikk_ref[0] = _rope(proj("ikk"), cos[:, :LANES], sin[:, :LANES]).astype(BF16)
    fq, fk, fv = proj("fq"), proj("fk"), proj("fv")
    lane = lax.broadcasted_iota(jnp.int32, (fq.shape[0], LANES), 1)
    in_head = lane < HEAD_DIM
    q_pad = jnp.where(lane < HEAD_DIM + BIAS_TERMS, 1.0, 0.0)
    for pair in range(N_HEADS // 2):
        ps = slice(pair * LANES, (pair + 1) * LANES)
        q_pair = _group_rms(fq[:, ps], g128, fqn[:, :LANES]) * (scale * LOG2E)
        k_pair = _group_rms(fk[:, ps], g128, fkn[:, :LANES])
        v_pair = fv[:, ps]
        for odd in range(2):
            hs = slice((2 * pair + odd) * LANES, (2 * pair + odd + 1) * LANES)
            down = (lambda y: pltpu.roll(y, HEAD_DIM, axis=1)) if odd else (lambda y: y)
            fq_ref[0, :, hs] = jnp.where(in_head, down(q_pair), q_pad).astype(BF16)
            k_h = jnp.where(in_head, down(k_pair), 0.0).astype(BF16)
            fk_ref[0, :, hs] = k_h
            key_norms.append(max_sq_norm(k_h))
            fva_ref[0, :, hs] = jnp.where(in_head, down(v_pair), 1.0).astype(BF16)
    kn_ref[0, 0] = jnp.concatenate(key_norms + [jnp.zeros((8 - len(key_norms), LANES), F32)], axis=0)
    gq_ref[0] = proj("gq") * scale
    gk_ref[0] = proj("gk")
    gv_ref[0] = proj("gv").astype(BF16)
    gg_ref[0] = proj("gg")

    small = proj("small")
    lane = lax.broadcasted_iota(jnp.int32, small.shape, 1)
    small_ref[0] = jnp.where(lane < SM_FF, small * (N_HEADS ** -0.5),
                             _log_sigmoid(small + sm_ref[0:1, :]))
    gate = _dot(small.astype(BF16), w2_ref[...]) + vec_ref[4:5, :]
    la_ref[0] = _log_sigmoid(gate) * (1.0 / GLA_TAU)


def _inproj(h, an, w_perm, cos_t, sin_t, g256, vecs, w2_pad, sm_bias):
    bsz, t_work, _ = h.shape
    tm = _row_tile(t_work)
    grid = (bsz, t_work // tm)

    def rows(width, dtype):
        return (jax.ShapeDtypeStruct((bsz, t_work, width), dtype),
                pl.BlockSpec((1, tm, width), lambda b, j: (b, j, 0)))

    outs = [rows(256, BF16), rows(128, BF16), rows(128, BF16), rows(256, BF16), rows(128, BF16),
            rows(512, BF16), rows(512, BF16), rows(512, BF16), rows(256, F32), rows(256, F32),
            rows(512, BF16), rows(512, F32), rows(256, F32), rows(128, F32),
            (jax.ShapeDtypeStruct((bsz, t_work // tm, 8, LANES), F32),
             pl.BlockSpec((1, 1, 8, LANES), lambda b, j: (b, j, 0, 0)))]
    const = lambda shape: pl.BlockSpec(shape, lambda b, j: (0,) * len(shape))
    return pl.pallas_call(
        _inproj_kernel,
        grid=grid,
        in_specs=[pl.BlockSpec((1, tm, D_MODEL), lambda b, j: (b, j, 0)),
                  const((1, D_MODEL)), const((D_MODEL, N_PROJ)),
                  pl.BlockSpec((tm, 256), lambda b, j: (j, 0)),
                  pl.BlockSpec((tm, 256), lambda b, j: (j, 0)),
                  const((256, 256)), const((8, 256)), const((LANES, 256)), const((8, LANES))],
        out_specs=[o[1] for o in outs],
        out_shape=[o[0] for o in outs],
        compiler_params=pltpu.CompilerParams(
            dimension_semantics=("parallel", "parallel"), vmem_limit_bytes=VMEM_LIMIT),
        name="inproj",
    )(h, an, w_perm, cos_t, sin_t, g256, vecs, w2_pad, sm_bias)


def _prep_layer_params(l, attn_norm, w_in, dsa_q_norm, dsa_k_norm, fox_q_norm, fox_k_norm,
                       fox_f_bias, gla_gate_w2, gla_gate_b):
    w = w_in[l]
    splits = np.cumsum([256, 64, 64, 256, 4, 64, 256, 256, 256, 4, 256, 256, 512, 512, 16])[:-1]
    (dq, dk, dv, iq, iw, ik, fq, fk, fv, ff, gq, gk, gv, gg, glr) = jnp.split(w, splits, axis=1)
    small = jnp.concatenate(
        [iw, ff, glr, jnp.zeros((D_MODEL, LANES - 4 - 4 - GLA_RANK), w.dtype)], axis=1)
    z64 = jnp.zeros((D_MODEL, HEAD_DIM), w.dtype)
    w_perm = jnp.concatenate([dq, dk, dk, dv, z64, iq, ik, ik, fq, fk, fv, gq, gk, gv, gg, small],
                             axis=1).astype(BF16)
    tile4 = lambda g: jnp.tile(g, N_HEADS)
    vecs = jnp.zeros((8, 256), F32)
    vecs = vecs.at[0].set(tile4(dsa_q_norm[l])).at[1].set(tile4(dsa_k_norm[l]))
    vecs = vecs.at[2].set(tile4(fox_q_norm[l])).at[3].set(tile4(fox_k_norm[l]))
    vecs = vecs.at[4].set(gla_gate_b[l])
    w2_pad = jnp.zeros((LANES, 256), F32).at[SM_GLR:SM_GLR + GLA_RANK].set(gla_gate_w2[l]).astype(BF16)
    sm_bias = jnp.zeros((8, LANES), F32).at[0, SM_FF:SM_FF + N_HEADS].set(fox_f_bias[l])
    return attn_norm[l][None, :], w_perm, vecs, w2_pad, sm_bias


def _rope_tables(t_work):
    inv = jnp.power(ROPE_THETA, -jnp.arange(ROPE_HALF, dtype=F32) * 2.0 / ROPE_DIM)
    ang = jnp.arange(t_work).astype(F32)[:, None] * inv[None, :]
    cos, sin = jnp.cos(ang), jnp.sin(ang)
    rest = HEAD_DIM - ROPE_DIM
    cos64 = jnp.concatenate([cos, cos, jnp.ones((t_work, rest), F32)], axis=1)
    sin64 = jnp.concatenate([-sin, sin, jnp.zeros((t_work, rest), F32)], axis=1)
    return jnp.tile(cos64, (1, N_HEADS)), jnp.tile(sin64, (1, N_HEADS))


def _group_matrix():
    idx = np.arange(256) // HEAD_DIM
    return jnp.asarray((idx[:, None] == idx[None, :]).astype(np.float32), dtype=BF16)


DSA_TQ = 256
DSA_TK = 512
KEY_NEG_INF = -2139095041
KEY_NEG_ZERO = -1
KEY_NEG_MIN_NORMAL = -8388609
NEG_MIN_NORMAL = -1.1754943508222875e-38
SEARCH_FEW = 4.0
SEARCH_MANY = 16.0
UNCHECKED_PROBES = 12
GUIDED_PROBES = 64
MAX_PROBES = GUIDED_PROBES + 40
COUNT_ROWS = 128


def _key_to_f32(key):
    bits = key ^ ((key >> 31) & 0x7FFFFFFF)
    return lax.bitcast_convert_type(bits, F32)


def _f32_to_key(value):
    bits = lax.bitcast_convert_type(value, jnp.int32)
    return bits ^ ((bits >> 31) & 0x7FFFFFFF)


def _head_lane_mask(shape, head):
    lane = lax.broadcasted_iota(jnp.int32, shape, 1)
    return (lane < HEAD_DIM) if head % 2 == 0 else (lane >= HEAD_DIM)


def _masked_heads(slabs):
    return [jnp.where(_head_lane_mask(slabs[h // 2].shape, h), slabs[h // 2],
                      jnp.zeros_like(slabs[h // 2])) for h in range(N_HEADS)]


def _softmax_step(s, m, acc, v_aug):
    m_new = jnp.maximum(m, jnp.max(s, axis=-1, keepdims=True))
    p = jnp.exp2(s - m_new)
    return m_new, acc * jnp.exp2(m - m_new) + _dot(p.astype(BF16), v_aug)


def _normalise_heads(accs):
    outs = [acc / pltpu.roll(acc, HEAD_DIM, axis=1) for acc in accs]
    lane = lax.broadcasted_iota(jnp.int32, outs[0].shape, 1)
    return [jnp.where(lane < HEAD_DIM, outs[2 * p], pltpu.roll(outs[2 * p + 1], HEAD_DIM, axis=1))
            for p in range(2)]


def _dsa_kernel(k_top, dq_ref, kk_ref, va_ref, iq_ref, ik_ref, wt_ref, tri_ref, kn_ref, o_ref,
                s_ref):
    i = pl.program_id(1)
    q0 = i * DSA_TQ
    n_tiles = (q0 + DSA_TQ + DSA_TK - 1) // DSA_TK
    last = n_tiles - 1
    key = lax.broadcasted_iota(jnp.int32, (DSA_TK, DSA_TQ), 0)
    qry = q0 + lax.broadcasted_iota(jnp.int32, (DSA_TK, DSA_TQ), 1)
    head = lambda x, h: x[:, h * DSA_TQ:(h + 1) * DSA_TQ]

    iq_all = jnp.concatenate(
        _masked_heads([iq_ref[0, :, 0:LANES], iq_ref[0, :, LANES:2 * LANES]]), axis=0)
    w_h = [wt_ref[0, SM_IW + h:SM_IW + h + 1, :] for h in range(N_HEADS)]

    def score_tile(j, carry, causal):
        k0 = pl.multiple_of(j * DSA_TK, DSA_TK)
        dots = jnp.maximum(_dot_nt(ik_ref[0, pl.ds(k0, DSA_TK), :], iq_all), 0.0)
        s = w_h[0] * head(dots, 0)
        for h in range(1, N_HEADS):
            s = s + w_h[h] * head(dots, h)
        if causal:
            s = jnp.where(k0 + key <= qry, s, NEG)
        s_ref[pl.ds(k0, DSA_TK), :] = s
        row_max, above_zero, above_neg = carry
        for part in range(DSA_TK // COUNT_ROWS):
            chunk = s[part * COUNT_ROWS:(part + 1) * COUNT_ROWS]
            above_zero = above_zero + jnp.where(chunk > 0.0, 1.0, 0.0)
            above_neg = above_neg + jnp.where(chunk > NEG_MIN_NORMAL, 1.0, 0.0)
        return jnp.maximum(row_max, jnp.max(s, axis=0, keepdims=True)), above_zero, above_neg

    zeros = jnp.zeros((COUNT_ROWS, DSA_TQ), F32)
    row_max, above_zero, above_neg = score_tile(last, lax.fori_loop(
        0, last, functools.partial(score_tile, causal=False),
        (jnp.full((1, DSA_TQ), NEG, F32), zeros, zeros)), causal=True)

    kf = jnp.float32(k_top)

    def count_above(t):
        tb = jnp.broadcast_to(t, (COUNT_ROWS, DSA_TQ))

        def body(j, acc, span):
            k0 = pl.multiple_of(j * span, span)
            for part in range(span // COUNT_ROWS):
                s = s_ref[pl.ds(k0 + part * COUNT_ROWS, COUNT_ROWS), :]
                acc = acc + jnp.where(s > tb, 1.0, 0.0)
            return acc

        acc = lax.fori_loop(0, n_tiles // 2, functools.partial(body, span=2 * DSA_TK),
                            jnp.zeros((COUNT_ROWS, DSA_TQ), F32))
        acc = lax.fori_loop(2 * (n_tiles // 2), n_tiles, functools.partial(body, span=DSA_TK), acc)
        return jnp.sum(acc, axis=0, keepdims=True)

    def midpoint(lo, hi):
        return (lo >> 1) + (hi >> 1) + (lo & hi & 1)

    def converged(lo, hi):
        return (midpoint(lo, hi) == lo) | ((lo >= KEY_NEG_MIN_NORMAL) & (hi <= 0))

    col1 = lambda value, dtype: jnp.full((1, DSA_TQ), value, dtype)

    def absorb(state, probe, c):
        lo, hi, clo, chi, wlo, whi, side = state
        live = probe != lo
        up = live & (c >= kf)
        down = live & (c <= kf)
        wlo = jnp.where(down & (side < 0), wlo * 0.5, jnp.where(up, 1.0, wlo))
        whi = jnp.where(up & (side > 0), whi * 0.5, jnp.where(down, 1.0, whi))
        side = jnp.where(up, 1, jnp.where(down, -1, side))
        lo, clo = jnp.where(up, probe, lo), jnp.where(up, c, clo)
        hi, chi = jnp.where(down, probe, hi), jnp.where(down, c, chi)
        return lo, hi, clo, chi, wlo, whi, side

    def probe_once(it, state):
        lo, hi, clo, chi, wlo, whi, side = state
        f_lo, f_hi = _key_to_f32(lo), _key_to_f32(hi)
        target = kf - 0.5
        log_count = lambda c: jnp.log2(jnp.maximum(c, 0.25))
        many = clo - chi > SEARCH_MANY
        g_lo = jnp.where(many, log_count(clo) - np.log2(k_top - 0.5), clo - target) * wlo
        g_hi = jnp.where(many, np.log2(k_top - 0.5) - log_count(chi), target - chi) * whi
        halve = (clo - chi <= SEARCH_FEW) | (col1(it % 8, jnp.int32) == 7)
        guess = _f32_to_key(f_lo + (f_hi - f_lo) * jnp.where(halve, 0.5, g_lo / (g_lo + g_hi)))
        guided = col1(it, jnp.int32) < GUIDED_PROBES
        probe = jnp.where((guess > lo) & (guess < hi) & guided, guess, midpoint(lo, hi))
        probe = jnp.where(converged(lo, hi), lo, probe)
        return absorb(state, probe, count_above(_key_to_f32(probe)))

    def search_cond(carry):
        it, pending = carry[0], carry[1]
        return (pending > 0) & (it < MAX_PROBES)

    def search_body(carry):
        it, state = carry[0], carry[2]
        state = probe_once(it + 1, probe_once(it, state))
        return it + 2, jnp.max(jnp.where(converged(state[0], state[1]), 0, 1)), state

    n_swept = (n_tiles * DSA_TK).astype(F32)
    state = (col1(KEY_NEG_INF, jnp.int32), _f32_to_key(row_max),
             jnp.broadcast_to(n_swept, (1, DSA_TQ)), col1(0.0, F32), col1(1.0, F32), col1(1.0, F32),
             col1(0, jnp.int32))
    for fixed, counts in ((KEY_NEG_ZERO, above_zero), (KEY_NEG_MIN_NORMAL, above_neg)):
        inside = (state[0] < fixed) & (fixed < state[1])
        state = absorb(state, jnp.where(inside, fixed, state[0]),
                       jnp.sum(counts, axis=0, keepdims=True))
    state = lax.fori_loop(0, UNCHECKED_PROBES, probe_once, state)
    state = lax.while_loop(search_cond, search_body,
                           (jnp.int32(UNCHECKED_PROBES), jnp.int32(1), state))[2]
    thr = _key_to_f32(state[1])
    n_ties = kf - state[3]

    to_column = lambda r: jnp.broadcast_to(r, (8, DSA_TQ)).T[:, 0:1]
    thr_c, ties_c = to_column(thr), to_column(n_ties)
    q_all = jnp.concatenate(
        _masked_heads([dq_ref[0, :, 0:LANES], dq_ref[0, :, LANES:2 * LANES]]), axis=0)
    tri = tri_ref[...]
    tri_lo = jnp.where(lax.broadcasted_iota(jnp.int32, (LANES, LANES), 0)
                       >= lax.broadcasted_iota(jnp.int32, (LANES, LANES), 1), 1.0, 0.0).astype(BF16)
    qrow = q0 + lax.broadcasted_iota(jnp.int32, (DSA_TQ, DSA_TK), 0)
    kcol = lax.broadcasted_iota(jnp.int32, (DSA_TQ, DSA_TK), 1)
    n_blocks = DSA_TK // LANES

    reach = _logit_reach(q_all, jnp.max(kn_ref[0], axis=0)[KN_DSA:KN_DSA + 1, 0:1])

    def selected_logits(j, seen, causal):
        k0 = pl.multiple_of(j * DSA_TK, DSA_TK)
        s = s_ref[pl.ds(k0, DSA_TK), :].T
        tie = s == thr_c
        tie_b = jnp.where(tie, 1.0, 0.0).astype(BF16)
        local = [_dot(tie_b[:, b * LANES:(b + 1) * LANES], tri) for b in range(n_blocks)]
        ranks = []
        for b in range(n_blocks):
            ranks.append(local[b] + seen)
            seen = seen + local[b][:, LANES - 1:LANES]
        sel = (s > thr_c) | (tie & (jnp.concatenate(ranks, axis=1) <= ties_c))
        if causal:
            sel = sel & (k0 + kcol <= qrow)
        logits = _dot_nt(q_all, kk_ref[0, pl.ds(k0, DSA_TK), :])
        logits = jnp.where(sel[None], logits.reshape(N_HEADS, DSA_TQ, DSA_TK), NEG)
        return seen, logits.reshape(N_HEADS * DSA_TQ, DSA_TK), va_ref[0, pl.ds(k0, DSA_TK), :]

    def capped(j, carry, causal):
        seen, acc = carry
        k0 = pl.multiple_of(j * DSA_TK, DSA_TK)
        s = s_ref[pl.ds(k0, DSA_TK), :]
        tie = s == thr
        tie_b = jnp.where(tie, 1.0, 0.0).astype(BF16)
        ranks = []
        for b in range(n_blocks):
            local = _dot(tri_lo, tie_b[b * LANES:(b + 1) * LANES, :])
            ranks.append(local + seen)
            seen = seen + local[LANES - 1:LANES, :]
        sel = (s > thr) | (tie & (jnp.concatenate(ranks, axis=0) <= n_ties))
        if causal:
            sel = sel & (k0 + key <= qry)
        keep = jnp.where(sel, 1.0, 0.0).astype(BF16).T
        logits = _dot_nt(q_all, kk_ref[0, pl.ds(k0, DSA_TK), :])
        p = jnp.exp2(logits - reach).astype(BF16).reshape(N_HEADS, DSA_TQ, DSA_TK) * keep[None]
        return seen, acc + _dot(p.reshape(N_HEADS * DSA_TQ, DSA_TK), va_ref[0, pl.ds(k0, DSA_TK), :])

    def online(j, carry, causal):
        seen, logits, va_t = selected_logits(j, carry[0], causal)
        return (seen,) + _softmax_step(logits, carry[1], carry[2], va_t)

    def sweep(step, *stats):
        carry = stats + (jnp.zeros((N_HEADS * DSA_TQ, LANES), F32),)
        carry = lax.fori_loop(0, last, functools.partial(step, causal=False), carry)
        return step(last, carry, causal=True)[-1]

    acc = lax.cond(2.0 * jnp.max(reach) <= SAFE_LOG2_SPAN,
                   functools.partial(sweep, capped, jnp.zeros((1, DSA_TQ), F32)),
                   functools.partial(sweep, online, jnp.zeros((DSA_TQ, 1), F32),
                                     jnp.full((N_HEADS * DSA_TQ, 1), NEG, F32)))
    o_ref[0] = jnp.concatenate(
        _normalise_heads([acc[h * DSA_TQ:(h + 1) * DSA_TQ] for h in range(N_HEADS)]),
        axis=1).astype(o_ref.dtype)


def _dsa_attention(dq, dkk, dva, iq, ikk, small_t, key_norms, k_top):
    bsz, t_work, _ = dq.shape
    tri = jnp.asarray(np.triu(np.ones((LANES, LANES), np.float32)), dtype=BF16)
    tile = lambda width: pl.BlockSpec((1, DSA_TQ, width), lambda b, i: (b, i, 0))
    full = pl.BlockSpec((1, t_work, LANES), lambda b, i: (b, 0, 0), pipeline_mode=pl.Buffered(1))
    return pl.pallas_call(
        functools.partial(_dsa_kernel, k_top),
        grid=(bsz, t_work // DSA_TQ),
        in_specs=[tile(256), full, full, tile(256), full,
                  pl.BlockSpec((1, 8, DSA_TQ), lambda b, i: (b, 0, i)),
                  pl.BlockSpec((LANES, LANES), lambda b, i: (0, 0)),
                  pl.BlockSpec((1,) + key_norms.shape[1:], lambda b, i: (b, 0, 0, 0))],
        out_specs=tile(256),
        out_shape=jax.ShapeDtypeStruct((bsz, t_work, 256), BF16),
        scratch_shapes=[pltpu.VMEM((t_work, DSA_TQ), F32)],
        compiler_params=pltpu.CompilerParams(
            dimension_semantics=("parallel", "parallel"), vmem_limit_bytes=VMEM_LIMIT),
        name="dsa_attention",
    )(dq, dkk, dva, iq, ikk, small_t, tri, key_norms)


FOX_TQ = 512
FOX_TK = 512
CUM_T = 256


def _fox_prep_kernel(x_ref, tri_ref, place_ref, k_ref, xt_ref, kb_ref, carry_ref):
    @pl.when(pl.program_id(1) == 0)
    def _():
        carry_ref[...] = jnp.zeros_like(carry_ref)

    tri = tri_ref[...]
    x = x_ref[0]
    h1, h2, h3 = _split3(x)
    c = _dot(tri, h1) + _dot(tri, h2) + _dot(tri, h3) + carry_ref[0:1, :]
    carry_ref[...] = jnp.broadcast_to(c[CUM_T - 1:CUM_T, :], carry_ref.shape)
    xt_ref[0] = x.T[0:8, :]
    terms = _split3(c * -LOG2E)
    bias = sum(_dot(terms[t], place_ref[t]) for t in range(BIAS_TERMS))
    lane = lax.broadcasted_iota(jnp.int32, bias.shape, 1) % LANES
    kb_ref[0] = jnp.where((lane >= HEAD_DIM) & (lane < HEAD_DIM + BIAS_TERMS),
                          bias.astype(BF16), k_ref[0])


def _fox_prep(small, fk):
    bsz, t_work, width = fk.shape
    tri = jnp.asarray(np.tril(np.ones((CUM_T, CUM_T), np.float32)), dtype=BF16)
    place = np.zeros((BIAS_TERMS, LANES, width), np.float32)
    for t in range(BIAS_TERMS):
        for h in range(N_HEADS):
            place[t, SM_FF + h, h * LANES + HEAD_DIM + t] = 1.0
    rows = lambda w: pl.BlockSpec((1, CUM_T, w), lambda b, j: (b, j, 0))
    return pl.pallas_call(
        _fox_prep_kernel,
        grid=(bsz, t_work // CUM_T),
        in_specs=[rows(LANES), pl.BlockSpec((CUM_T, CUM_T), lambda b, j: (0, 0)),
                  pl.BlockSpec((BIAS_TERMS, LANES, width), lambda b, j: (0, 0, 0)), rows(width)],
        out_specs=[pl.BlockSpec((1, 8, CUM_T), lambda b, j: (b, 0, j)), rows(width)],
        out_shape=[jax.ShapeDtypeStruct((bsz, 8, t_work), F32),
                   jax.ShapeDtypeStruct(fk.shape, BF16)],
        scratch_shapes=[pltpu.VMEM((8, LANES), F32)],
        compiler_params=pltpu.CompilerParams(dimension_semantics=("parallel", "arbitrary")),
        name="fox_prep",
    )(small, tri, jnp.asarray(place, dtype=BF16), fk)


def _logit_reach(q, kmax_sq):
    qf = q.astype(F32)
    return jnp.sqrt(jnp.sum(qf * qf, axis=-1, keepdims=True) * kmax_sq) * NORM_MARGIN


def _fox_kernel(q_ref, k_ref, v_ref, kn_ref, o_ref):
    i = pl.program_id(1)
    q0 = pl.multiple_of(i * FOX_TQ, FOX_TQ)
    n_full = q0 // FOX_TK
    row = q0 + lax.broadcasted_iota(jnp.int32, (FOX_TQ, FOX_TK), 0)
    col = n_full * FOX_TK + lax.broadcasted_iota(jnp.int32, (FOX_TQ, FOX_TK), 1)
    heads = [slice(h * LANES, (h + 1) * LANES) for h in range(N_HEADS)]

    lane = lax.broadcasted_iota(jnp.int32, (FOX_TQ, LANES), 1)
    bias_lanes = (lane >= HEAD_DIM) & (lane < HEAD_DIM + BIAS_TERMS)
    kmax_sq = jnp.max(kn_ref[0], axis=0)
    caps, span = [], jnp.float32(0.0)
    for h in range(N_HEADS):
        q_h = q_ref[0, :, heads[h]]
        reach = _logit_reach(jnp.where(lane < HEAD_DIM, q_h, jnp.zeros_like(q_h)),
                             kmax_sq[KN_FOX + h:KN_FOX + h + 1, 0:1])
        own = k_ref[0, pl.ds(q0, FOX_TQ), heads[h]].astype(F32)
        caps.append(reach + jnp.sum(jnp.where(bias_lanes, own, 0.0), axis=-1, keepdims=True))
        span = jnp.maximum(span, 2.0 * jnp.max(reach))

    def logits(j, h, diag):
        k0 = pl.multiple_of(j * FOX_TK, FOX_TK)
        s = _dot_nt(q_ref[0, :, heads[h]], k_ref[0, pl.ds(k0, FOX_TK), heads[h]])
        return jnp.where(col <= row, s, NEG) if diag else s

    def values(j, h):
        return v_ref[0, pl.ds(pl.multiple_of(j * FOX_TK, FOX_TK), FOX_TK), heads[h]]

    def capped(j, accs, diag):
        return tuple(accs[h] + _dot(jnp.exp2(logits(j, h, diag) - caps[h]).astype(BF16), values(j, h))
                     for h in range(N_HEADS))

    def online(j, carry, diag):
        return tuple(_softmax_step(logits(j, h, diag), *carry[h], values(j, h))
                     for h in range(N_HEADS))

    def capped_sweep():
        accs = tuple(jnp.zeros((FOX_TQ, LANES), F32) for _ in range(N_HEADS))
        accs = lax.fori_loop(0, n_full, functools.partial(capped, diag=False), accs)
        return capped(n_full, accs, diag=True)

    def online_sweep():
        carry = tuple((jnp.full((FOX_TQ, 1), NEG, F32), jnp.zeros((FOX_TQ, LANES), F32))
                      for _ in range(N_HEADS))
        carry = lax.fori_loop(0, n_full, functools.partial(online, diag=False), carry)
        return tuple(acc for _, acc in online(n_full, carry, diag=True))

    accs = lax.cond(span <= SAFE_LOG2_SPAN, capped_sweep, online_sweep)
    o_ref[0] = jnp.concatenate(_normalise_heads(list(accs)), axis=1).astype(o_ref.dtype)


def _fox_attention(fqa, fkb, fva, key_norms):
    bsz, t_work, width = fqa.shape
    full = pl.BlockSpec((1, t_work, width), lambda b, i: (b, 0, 0), pipeline_mode=pl.Buffered(1))
    return pl.pallas_call(
        _fox_kernel,
        grid=(bsz, t_work // FOX_TQ),
        in_specs=[pl.BlockSpec((1, FOX_TQ, width), lambda b, i: (b, i, 0)), full, full,
                  pl.BlockSpec((1,) + key_norms.shape[1:], lambda b, i: (b, 0, 0, 0))],
        out_specs=pl.BlockSpec((1, FOX_TQ, 2 * LANES), lambda b, i: (b, i, 0)),
        out_shape=jax.ShapeDtypeStruct((bsz, t_work, 2 * LANES), BF16),
        compiler_params=pltpu.CompilerParams(
            dimension_semantics=("parallel", "parallel"), vmem_limit_bytes=VMEM_LIMIT),
        name="fox_attention",
    )(fqa, fkb, fva, key_norms)


GLA_TG = 256
GLA_NSUB = GLA_CHUNK // GLA_SUB


def _gla_kernel(q_ref, k_ref, v_ref, g_ref, la_ref, tri_ref, e_ref, gn_ref, o_ref, st_ref):
    @pl.when(pl.program_id(1) == 0)
    def _():
        st_ref[...] = jnp.zeros_like(st_ref)

    tri = tri_ref[...]
    emat = e_ref[...]
    lane = lax.broadcasted_iota(jnp.int32, (GLA_CHUNK, LANES), 1)
    rowblk = lax.broadcasted_iota(jnp.int32, (GLA_CHUNK, LANES), 0) // GLA_SUB
    tblk = lax.broadcasted_iota(jnp.int32, (GLA_CHUNK, GLA_CHUNK), 0) // GLA_SUB
    sblk = lax.broadcasted_iota(jnp.int32, (GLA_CHUNK, GLA_CHUNK), 1) // GLA_SUB
    trow = lax.broadcasted_iota(jnp.int32, (GLA_SUB, 256), 0)

    def chunk(c, carry):
        r0 = pl.multiple_of(c * GLA_CHUNK, GLA_CHUNK)
        rows = pl.ds(r0, GLA_CHUNK)
        h1, h2, h3 = _split3(la_ref[0, rows, :])
        b = _dot(tri, h1) + _dot(tri, h2) + _dot(tri, h3)
        q = q_ref[0, rows, :]
        k = k_ref[0, rows, :]
        v = v_ref[0, rows, :]
        b_last = b[GLA_CHUNK - 1:GLA_CHUNK, :]
        qd = q * jnp.exp(b)
        kd = (k * jnp.exp(b_last - b)).astype(BF16)
        starts = [jnp.zeros((1, 256), F32)] + [b[GLA_SUB * i - 1:GLA_SUB * i, :]
                                               for i in range(1, GLA_NSUB)]
        bsel = jnp.concatenate([jnp.broadcast_to(s, (GLA_SUB, 256)) for s in starts], axis=0)
        qn = q * jnp.exp(b - bsel)

        diag = []
        for i in range(GLA_NSUB):
            rs = slice(GLA_SUB * i, GLA_SUB * (i + 1))
            b_i, q_i, k_i = b[rs], q[rs], k[rs]
            v_i = v[rs].astype(F32)
            ps = []
            for s in range(GLA_SUB):
                d = jnp.exp(jnp.minimum(b_i - b_i[s:s + 1], 0.0))
                ps.append(jnp.where(trow >= s, q_i * d * k_i[s:s + 1], 0.0).astype(BF16))
            r = _dot(jnp.concatenate(ps, axis=0), emat)
            od = r[0:GLA_SUB] * v_i[0:1]
            for s in range(1, GLA_SUB):
                od = od + r[GLA_SUB * s:GLA_SUB * (s + 1)] * v_i[s:s + 1]
            diag.append(od)
        o_diag = jnp.concatenate(diag, axis=0)

        for slab in range(2):
            ls = slice(slab * LANES, (slab + 1) * LANES)
            qn_s, k_s, b_s = qn[:, ls], k[:, ls], b[:, ls]
            khat = jnp.concatenate(
                [(k_s * jnp.exp(jnp.minimum(starts[i][:, ls] - b_s, 0.0))).astype(BF16)
                 for i in range(1, GLA_NSUB)], axis=1)
            for half in range(2):
                head = 2 * slab + half
                hs = slice(head * GLA_DV, (head + 1) * GLA_DV)
                in_head = (lane < HEAD_DIM) if half == 0 else (lane >= HEAD_DIM)
                qm = jnp.where(in_head, qn_s, 0.0)
                qhat = jnp.concatenate([jnp.where(rowblk == i, qm, 0.0).astype(BF16)
                                        for i in range(1, GLA_NSUB)], axis=1)
                att = jnp.where(sblk < tblk, _dot_nt(qhat, khat), 0.0)
                v_h = v[:, hs]
                st = st_ref[head]
                o = (_dot_nt(jnp.where(in_head, qd[:, ls], 0.0).astype(BF16), st.astype(BF16))
                     + _dot(att.astype(BF16), v_h) + o_diag[:, hs])
                st_ref[head] = st * jnp.exp(b_last[:, ls]) + _dot_tn(v_h, kd[:, ls])
                y = o * lax.rsqrt(jnp.mean(o * o, axis=-1, keepdims=True) + EPS) * gn_ref[:, hs]
                o_ref[0, rows, hs] = (y * _silu(g_ref[0, rows, hs])).astype(o_ref.dtype)
        return carry

    lax.fori_loop(0, GLA_TG // GLA_CHUNK, chunk, 0)


def _gla(gq, gk, gv, gg, la, gain):
    bsz, t_work, _ = gq.shape
    tri = jnp.asarray(np.tril(np.ones((GLA_CHUNK, GLA_CHUNK), np.float32)), dtype=BF16)
    emat = jnp.asarray(
        (np.arange(256)[:, None] // HEAD_DIM == np.arange(512)[None, :] // GLA_DV).astype(np.float32),
        dtype=BF16)
    rows = lambda width: pl.BlockSpec((1, GLA_TG, width), lambda b, j: (b, j, 0))
    const = lambda shape: pl.BlockSpec(shape, lambda b, j: (0,) * len(shape))
    return pl.pallas_call(
        _gla_kernel,
        grid=(bsz, t_work // GLA_TG),
        in_specs=[rows(256), rows(256), rows(512), rows(512), rows(256),
                  const((GLA_CHUNK, GLA_CHUNK)), const((256, 512)), const((1, 512))],
        out_specs=rows(512),
        out_shape=jax.ShapeDtypeStruct((bsz, t_work, 512), BF16),
        scratch_shapes=[pltpu.VMEM((N_HEADS, GLA_DV, LANES), F32)],
        compiler_params=pltpu.CompilerParams(
            dimension_semantics=("parallel", "arbitrary"), vmem_limit_bytes=VMEM_LIMIT),
        name="gla",
    )(gq, gk, gv, gg, la, tri, emat, gain)


FFN_CHUNK = 256


def _ffn_kernel(h_ref, oa_ref, ob_ref, oc_ref, wo_ref, fn_ref, wgu_ref, wd_ref, out_ref):
    h1 = (h_ref[0] + _dot(oa_ref[0], wo_ref[0:256, :]) + _dot(ob_ref[0], wo_ref[256:512, :])
          + _dot(oc_ref[0], wo_ref[512:1024, :]))
    ms = jnp.mean(h1 * h1, axis=-1, keepdims=True)
    f = (h1 * lax.rsqrt(ms + EPS) * fn_ref[...]).astype(BF16)
    out_ref[0] = h1
    for c in range(0, D_FF, FFN_CHUNK):
        gate = _dot(f, wgu_ref[:, c:c + FFN_CHUNK])
        up = _dot(f, wgu_ref[:, D_FF + c:D_FF + c + FFN_CHUNK])
        out_ref[0] += _dot((_silu(gate) * up).astype(BF16), wd_ref[c:c + FFN_CHUNK, :])


def _outproj_ffn(h, oa, ob, oc, wo, fn, wgu, wd):
    bsz, t_work, _ = h.shape
    tm = _row_tile(t_work)
    rows = lambda width: pl.BlockSpec((1, tm, width), lambda b, j: (b, j, 0))
    const = lambda shape: pl.BlockSpec(shape, lambda b, j: (0,) * len(shape),
                                       pipeline_mode=pl.Buffered(1))
    return pl.pallas_call(
        _ffn_kernel,
        grid=(bsz, t_work // tm),
        in_specs=[rows(D_MODEL), rows(256), rows(256), rows(512), const((D_MODEL, D_MODEL)),
                  const((1, D_MODEL)), const((D_MODEL, 2 * D_FF)), const((D_FF, D_MODEL))],
        out_specs=rows(D_MODEL),
        out_shape=jax.ShapeDtypeStruct(h.shape, F32),
        compiler_params=pltpu.CompilerParams(
            dimension_semantics=("parallel", "parallel"), vmem_limit_bytes=VMEM_LIMIT),
        name="outproj_ffn",
    )(h, oa, ob, oc, wo, fn, wgu, wd)


def kernel(x, meta_tokens, attn_norm, w_in, dsa_q_norm, dsa_k_norm, fox_q_norm, fox_k_norm,
           fox_f_bias, gla_gate_w2, gla_gate_b, gla_out_norm, w_out, ffn_norm, w_gate_up, w_down):
    bsz, seq, _ = x.shape
    n_tok = N_META + seq
    t_work = _work_len(n_tok)
    meta = jnp.broadcast_to(meta_tokens[None].astype(x.dtype), (bsz, N_META, D_MODEL))
    h = jnp.concatenate([meta, x, jnp.zeros((bsz, t_work - n_tok, D_MODEL), x.dtype)], axis=1)
    cos_t, sin_t = _rope_tables(t_work)
    g256 = _group_matrix()
    k_top = min(TOPK_MAX, seq // 4)
    for l in range(w_in.shape[0]):
        an, w_perm, vecs, w2_pad, sm_bias = _prep_layer_params(
            l, attn_norm, w_in, dsa_q_norm, dsa_k_norm, fox_q_norm, fox_k_norm, fox_f_bias,
            gla_gate_w2, gla_gate_b)
        (dq, dkk, dva, iq, ikk, fq, fk, fva, gq, gk, gv, gg, la, small, key_norms) = _inproj(
            h, an, w_perm, cos_t, sin_t, g256, vecs, w2_pad, sm_bias)
        small_t, fkb = _fox_prep(small, fk)
        oa = _dsa_attention(dq, dkk, dva, iq, ikk, small_t, key_norms, k_top)
        ob = _fox_attention(fq, fkb, fva, key_norms)
        oc = _gla(gq, gk, gv, gg, la, jnp.tile(gla_out_norm[l], N_HEADS)[None, :])
        h = _outproj_ffn(h, oa, ob, oc, w_out[l].astype(BF16), ffn_norm[l][None, :],
                         w_gate_up[l].astype(BF16), w_down[l].astype(BF16))
    return h[:, N_META:n_tok]
```

```python
import functools

import numpy as np
import jax
import jax.numpy as jnp
from jax import lax
from jax.experimental import pallas as pl
from jax.experimental.pallas import tpu as pltpu

F32 = jnp.float32
BF16 = jnp.bfloat16

D_MODEL = 1024
HEAD_DIM = 64
N_META = 16
ROPE_THETA = 500000.0
ROPE_DIM = HEAD_DIM // 4
ROPE_HALF = ROPE_DIM // 2
NEG = -1e30
EPS = 1e-6

N_HEADS = 4
TOPK_MAX = 256
GLA_DV = 128
GLA_RANK = 16
GLA_TAU = 16.0
GLA_CHUNK = 64
GLA_SUB = 16
D_FF = 2816

LOG2E = 1.4426950408889634
BIAS_TERMS = 3
KN_DSA, KN_FOX = 0, 1
NORM_MARGIN = 1.01
SAFE_LOG2_SPAN = 100.0
LANES = 128
SEQ_ALIGN = 512
VMEM_LIMIT = 56 * 1024 * 1024

_SLABS = (("dq", 256), ("dkk", 128), ("dva", 128), ("iq", 256), ("ikk", 128),
          ("fq", 256), ("fk", 256), ("fv", 256), ("gq", 256), ("gk", 256),
          ("gv", 512), ("gg", 512), ("small", 128))
_SLAB_OFF = {}
_off = 0
for _name, _width in _SLABS:
    _SLAB_OFF[_name] = (_off, _width)
    _off += _width
N_PROJ = _off
SM_IW, SM_FF, SM_GLR = 0, 4, 8


def _work_len(n_tok):
    return -(-n_tok // SEQ_ALIGN) * SEQ_ALIGN


def _row_tile(t_work):
    for cand in (768, 640, 512):
        if t_work % cand == 0:
            return cand
    raise ValueError(f"unsupported working length {t_work}")


def _dot(a, b):
    return jnp.dot(a, b, preferred_element_type=F32)


def _dot_nt(a, b):
    return lax.dot_general(a, b, (((1,), (1,)), ((), ())), preferred_element_type=F32)


def _dot_tn(a, b):
    return lax.dot_general(a, b, (((0,), (0,)), ((), ())), preferred_element_type=F32)


def _split3(x):
    h1 = x.astype(BF16)
    r1 = x - h1.astype(F32)
    h2 = r1.astype(BF16)
    h3 = (r1 - h2.astype(F32)).astype(BF16)
    return h1, h2, h3


def _log_sigmoid(x):
    return jnp.minimum(x, 0.0) - jnp.log1p(jnp.exp(-jnp.abs(x)))


def _silu(x):
    return x / (1.0 + jnp.exp(-x))


def _group_rms(y, gmat, gain):
    yy = y * y
    hi = yy.astype(BF16)
    lo = (yy - hi.astype(F32)).astype(BF16)
    ss = _dot(hi, gmat) + _dot(lo, gmat)
    return y * lax.rsqrt(ss * (1.0 / HEAD_DIM) + EPS) * gain


def _rope(y, cos, sin):
    width = y.shape[-1]
    lane = lax.broadcasted_iota(jnp.int32, y.shape, 1) % HEAD_DIM
    upper = pltpu.roll(y, width - ROPE_HALF, axis=1)
    lower = pltpu.roll(y, ROPE_HALF, axis=1)
    partner = jnp.where(lane < ROPE_HALF, upper, lower)
    return y * cos + partner * sin


def _inproj_kernel(x_ref, an_ref, w_ref, cos_ref, sin_ref, g256_ref, vec_ref, w2_ref, sm_ref,
                   dq_ref, dkk_ref, dva_ref, iq_ref, ikk_ref, fq_ref, fk_ref, fva_ref,
                   gq_ref, gk_ref, gv_ref, gg_ref, la_ref, small_ref, kn_ref):
    x = x_ref[0]
    ms = jnp.mean(x * x, axis=-1, keepdims=True)
    a = (x * lax.rsqrt(ms + EPS) * an_ref[...]).astype(BF16)

    def proj(name):
        off, width = _SLAB_OFF[name]
        return _dot(a, w_ref[:, off:off + width])

    def ones_in_upper_half(y):
        lane = lax.broadcasted_iota(jnp.int32, y.shape, 1) % LANES
        return jnp.where(lane < HEAD_DIM, y, 1.0)

    cos = cos_ref[...]
    sin = sin_ref[...]
    g256 = g256_ref[...]
    g128 = g256[:LANES, :LANES]
    dqn, dkn, fqn, fkn = vec_ref[0:1, :], vec_ref[1:2, :LANES], vec_ref[2:3, :], vec_ref[3:4, :]
    scale = HEAD_DIM ** -0.5

    def max_sq_norm(k):
        kf = k.astype(F32)
        return jnp.max(_dot((kf * kf).astype(BF16), g128), axis=0, keepdims=True)

    dq_ref[0] = (_rope(_group_rms(proj("dq"), g256, dqn), cos, sin) * (scale * LOG2E)).astype(BF16)
    dkk = _rope(_group_rms(proj("dkk"), g128, dkn), cos[:, :LANES], sin[:, :LANES]).astype(BF16)
    dkk_ref[0] = dkk
    key_norms = [max_sq_norm(dkk)]
    dva_ref[0] = ones_in_upper_half(proj("dva")).astype(BF16)
    iq_ref[0] = (_rope(proj("iq"), cos, sin) * scale).astype(BF16)
    ikk_ref[0] = _rope(proj("ikk"), cos[:, :LANES], sin[:, :LANES]).astype(BF16)
    fq, fk, fv = proj("fq"), proj("fk"), proj("fv")
    lane = lax.broadcasted_iota(jnp.int32, (fq.shape[0], LANES), 1)
    in_head = lane < HEAD_DIM
    q_pad = jnp.where(lane < HEAD_DIM + BIAS_TERMS, 1.0, 0.0)
    for pair in range(N_HEADS // 2):
        ps = slice(pair * LANES, (pair + 1) * LANES)
        q_pair = _group_rms(fq[:, ps], g128, fqn[:, :LANES]) * (scale * LOG2E)
        k_pair = _group_rms(fk[:, ps], g128, fkn[:, :LANES])
        v_pair = fv[:, ps]
        for odd in range(2):
            hs = slice((2 * pair + odd) * LANES, (2 * pair + odd + 1) * LANES)
            down = (lambda y: pltpu.roll(y, HEAD_DIM, axis=1)) if odd else (lambda y: y)
            fq_ref[0, :, hs] = jnp.where(in_head, down(q_pair), q_pad).astype(BF16)
            k_h = jnp.where(in_head, down(k_pair), 0.0).astype(BF16)
            fk_ref[0, :, hs] = k_h
            key_norms.append(max_sq_norm(k_h))
            fva_ref[0, :, hs] = jnp.where(in_head, down(v_pair), 1.0).astype(BF16)
    kn_ref[0, 0] = jnp.concatenate(key_norms + [jnp.zeros((8 - len(key_norms), LANES), F32)], axis=0)
    gq_ref[0] = proj("gq") * scale
    gk_ref[0] = proj("gk")
    gv_ref[0] = proj("gv").astype(BF16)
    gg_ref[0] = proj("gg")

    small = proj("small")
    lane = lax.broadcasted_iota(jnp.int32, small.shape, 1)
    small_ref[0] = jnp.where(lane < SM_FF, small * (N_HEADS ** -0.5),
                             _log_sigmoid(small + sm_ref[0:1, :]))
    gate = _dot(small.astype(BF16), w2_ref[...]) + vec_ref[4:5, :]
    la_ref[0] = _log_sigmoid(gate) * (LOG2E / GLA_TAU)


def _inproj(h, an, w_perm, cos_t, sin_t, g256, vecs, w2_pad, sm_bias):
    bsz, t_work, _ = h.shape
    tm = _row_tile(t_work)
    grid = (bsz, t_work // tm)

    def rows(width, dtype):
        return (jax.ShapeDtypeStruct((bsz, t_work, width), dtype),
                pl.BlockSpec((1, tm, width), lambda b, j: (b, j, 0)))

    outs = [rows(256, BF16), rows(128, BF16), rows(128, BF16), rows(256, BF16), rows(128, BF16),
            rows(512, BF16), rows(512, BF16), rows(512, BF16), rows(256, F32), rows(256, F32),
            rows(512, BF16), rows(512, F32), rows(256, F32), rows(128, F32),
            (jax.ShapeDtypeStruct((bsz, t_work // tm, 8, LANES), F32),
             pl.BlockSpec((1, 1, 8, LANES), lambda b, j: (b, j, 0, 0)))]
    const = lambda shape: pl.BlockSpec(shape, lambda b, j: (0,) * len(shape))
    return pl.pallas_call(
        _inproj_kernel,
        grid=grid,
        in_specs=[pl.BlockSpec((1, tm, D_MODEL), lambda b, j: (b, j, 0)),
                  const((1, D_MODEL)), const((D_MODEL, N_PROJ)),
                  pl.BlockSpec((tm, 256), lambda b, j: (j, 0)),
                  pl.BlockSpec((tm, 256), lambda b, j: (j, 0)),
                  const((256, 256)), const((8, 256)), const((LANES, 256)), const((8, LANES))],
        out_specs=[o[1] for o in outs],
        out_shape=[o[0] for o in outs],
        compiler_params=pltpu.CompilerParams(
            dimension_semantics=("parallel", "parallel"), vmem_limit_bytes=VMEM_LIMIT),
        name="inproj",
    )(h, an, w_perm, cos_t, sin_t, g256, vecs, w2_pad, sm_bias)


def _prep_layer_params(l, attn_norm, w_in, dsa_q_norm, dsa_k_norm, fox_q_norm, fox_k_norm,
                       fox_f_bias, gla_gate_w2, gla_gate_b):
    w = w_in[l]
    splits = np.cumsum([256, 64, 64, 256, 4, 64, 256, 256, 256, 4, 256, 256, 512, 512, 16])[:-1]
    (dq, dk, dv, iq, iw, ik, fq, fk, fv, ff, gq, gk, gv, gg, glr) = jnp.split(w, splits, axis=1)
    small = jnp.concatenate(
        [iw, ff, glr, jnp.zeros((D_MODEL, LANES - 4 - 4 - GLA_RANK), w.dtype)], axis=1)
    z64 = jnp.zeros((D_MODEL, HEAD_DIM), w.dtype)
    w_perm = jnp.concatenate([dq, dk, dk, dv, z64, iq, ik, ik, fq, fk, fv, gq, gk, gv, gg, small],
                             axis=1).astype(BF16)
    tile4 = lambda g: jnp.tile(g, N_HEADS)
    vecs = jnp.zeros((8, 256), F32)
    vecs = vecs.at[0].set(tile4(dsa_q_norm[l])).at[1].set(tile4(dsa_k_norm[l]))
    vecs = vecs.at[2].set(tile4(fox_q_norm[l])).at[3].set(tile4(fox_k_norm[l]))
    vecs = vecs.at[4].set(gla_gate_b[l])
    w2_pad = jnp.zeros((LANES, 256), F32).at[SM_GLR:SM_GLR + GLA_RANK].set(gla_gate_w2[l]).astype(BF16)
    sm_bias = jnp.zeros((8, LANES), F32).at[0, SM_FF:SM_FF + N_HEADS].set(fox_f_bias[l])
    return attn_norm[l][None, :], w_perm, vecs, w2_pad, sm_bias


def _rope_tables(t_work):
    inv = jnp.power(ROPE_THETA, -jnp.arange(ROPE_HALF, dtype=F32) * 2.0 / ROPE_DIM)
    ang = jnp.arange(t_work).astype(F32)[:, None] * inv[None, :]
    cos, sin = jnp.cos(ang), jnp.sin(ang)
    rest = HEAD_DIM - ROPE_DIM
    cos64 = jnp.concatenate([cos, cos, jnp.ones((t_work, rest), F32)], axis=1)
    sin64 = jnp.concatenate([-sin, sin, jnp.zeros((t_work, rest), F32)], axis=1)
    return jnp.tile(cos64, (1, N_HEADS)), jnp.tile(sin64, (1, N_HEADS))


def _group_matrix():
    idx = np.arange(256) // HEAD_DIM
    return jnp.asarray((idx[:, None] == idx[None, :]).astype(np.float32), dtype=BF16)


DSA_TQ = 256
DSA_TK = 512
KEY_NEG_INF = -2139095041
KEY_NEG_ZERO = -1
KEY_NEG_MIN_NORMAL = -8388609
NEG_MIN_NORMAL = -1.1754943508222875e-38
SEARCH_FEW = 4.0
SEARCH_MANY = 16.0
UNCHECKED_PROBES = 12
GUIDED_PROBES = 64
MAX_PROBES = GUIDED_PROBES + 40
COUNT_ROWS = 128


def _key_to_f32(key):
    bits = key ^ ((key >> 31) & 0x7FFFFFFF)
    return lax.bitcast_convert_type(bits, F32)


def _f32_to_key(value):
    bits = lax.bitcast_convert_type(value, jnp.int32)
    return bits ^ ((bits >> 31) & 0x7FFFFFFF)


def _head_lane_mask(shape, head):
    lane = lax.broadcasted_iota(jnp.int32, shape, 1)
    return (lane < HEAD_DIM) if head % 2 == 0 else (lane >= HEAD_DIM)


def _masked_heads(slabs):
    return [jnp.where(_head_lane_mask(slabs[h // 2].shape, h), slabs[h // 2],
                      jnp.zeros_like(slabs[h // 2])) for h in range(N_HEADS)]


def _softmax_step(s, m, acc, v_aug):
    m_new = jnp.maximum(m, jnp.max(s, axis=-1, keepdims=True))
    p = jnp.exp2(s - m_new)
    return m_new, acc * jnp.exp2(m - m_new) + _dot(p.astype(BF16), v_aug)


def _normalise_heads(accs):
    outs = [acc / pltpu.roll(acc, HEAD_DIM, axis=1) for acc in accs]
    lane = lax.broadcasted_iota(jnp.int32, outs[0].shape, 1)
    return [jnp.where(lane < HEAD_DIM, outs[2 * p], pltpu.roll(outs[2 * p + 1], HEAD_DIM, axis=1))
            for p in range(2)]


def _dsa_kernel(k_top, dq_ref, kk_ref, va_ref, iq_ref, ik_ref, wt_ref, tri_ref, kn_ref, o_ref,
                s_ref):
    i = pl.program_id(1)
    q0 = i * DSA_TQ
    n_tiles = (q0 + DSA_TQ + DSA_TK - 1) // DSA_TK
    last = n_tiles - 1
    key = lax.broadcasted_iota(jnp.int32, (DSA_TK, DSA_TQ), 0)
    qry = q0 + lax.broadcasted_iota(jnp.int32, (DSA_TK, DSA_TQ), 1)
    head = lambda x, h: x[:, h * DSA_TQ:(h + 1) * DSA_TQ]

    iq_all = jnp.concatenate(
        _masked_heads([iq_ref[0, :, 0:LANES], iq_ref[0, :, LANES:2 * LANES]]), axis=0)
    w_h = [wt_ref[0, SM_IW + h:SM_IW + h + 1, :] for h in range(N_HEADS)]

    def score_tile(j, carry, causal):
        k0 = pl.multiple_of(j * DSA_TK, DSA_TK)
        dots = jnp.maximum(_dot_nt(ik_ref[0, pl.ds(k0, DSA_TK), :], iq_all), 0.0)
        s = w_h[0] * head(dots, 0)
        for h in range(1, N_HEADS):
            s = s + w_h[h] * head(dots, h)
        if causal:
            s = jnp.where(k0 + key <= qry, s, NEG)
        s_ref[pl.ds(k0, DSA_TK), :] = s
        row_max, above_zero, above_neg = carry
        for part in range(DSA_TK // COUNT_ROWS):
            chunk = s[part * COUNT_ROWS:(part + 1) * COUNT_ROWS]
            above_zero = above_zero + jnp.where(chunk > 0.0, 1.0, 0.0)
            above_neg = above_neg + jnp.where(chunk > NEG_MIN_NORMAL, 1.0, 0.0)
        return jnp.maximum(row_max, jnp.max(s, axis=0, keepdims=True)), above_zero, above_neg

    zeros = jnp.zeros((COUNT_ROWS, DSA_TQ), F32)
    row_max, above_zero, above_neg = score_tile(last, lax.fori_loop(
        0, last, functools.partial(score_tile, causal=False),
        (jnp.full((1, DSA_TQ), NEG, F32), zeros, zeros)), causal=True)

    kf = jnp.float32(k_top)

    def count_above(t):
        tb = jnp.broadcast_to(t, (COUNT_ROWS, DSA_TQ))

        def body(j, acc, span):
            k0 = pl.multiple_of(j * span, span)
            for part in range(span // COUNT_ROWS):
                s = s_ref[pl.ds(k0 + part * COUNT_ROWS, COUNT_ROWS), :]
                acc = acc + jnp.where(s > tb, 1.0, 0.0)
            return acc

        acc = lax.fori_loop(0, n_tiles // 2, functools.partial(body, span=2 * DSA_TK),
                            jnp.zeros((COUNT_ROWS, DSA_TQ), F32))
        acc = lax.fori_loop(2 * (n_tiles // 2), n_tiles, functools.partial(body, span=DSA_TK), acc)
        return jnp.sum(acc, axis=0, keepdims=True)

    def midpoint(lo, hi):
        return (lo >> 1) + (hi >> 1) + (lo & hi & 1)

    def converged(lo, hi):
        return (midpoint(lo, hi) == lo) | ((lo >= KEY_NEG_MIN_NORMAL) & (hi <= 0))

    col1 = lambda value, dtype: jnp.full((1, DSA_TQ), value, dtype)

    def absorb(state, probe, c):
        lo, hi, clo, chi, wlo, whi, side = state
        live = probe != lo
        up = live & (c >= kf)
        down = live & (c <= kf)
        wlo = jnp.where(down & (side < 0), wlo * 0.5, jnp.where(up, 1.0, wlo))
        whi = jnp.where(up & (side > 0), whi * 0.5, jnp.where(down, 1.0, whi))
        side = jnp.where(up, 1, jnp.where(down, -1, side))
        lo, clo = jnp.where(up, probe, lo), jnp.where(up, c, clo)
        hi, chi = jnp.where(down, probe, hi), jnp.where(down, c, chi)
        return lo, hi, clo, chi, wlo, whi, side

    def probe_once(it, state):
        lo, hi, clo, chi, wlo, whi, side = state
        f_lo, f_hi = _key_to_f32(lo), _key_to_f32(hi)
        target = kf - 0.5
        log_count = lambda c: jnp.log2(jnp.maximum(c, 0.25))
        many = clo - chi > SEARCH_MANY
        g_lo = jnp.where(many, log_count(clo) - np.log2(k_top - 0.5), clo - target) * wlo
        g_hi = jnp.where(many, np.log2(k_top - 0.5) - log_count(chi), target - chi) * whi
        halve = (clo - chi <= SEARCH_FEW) | (col1(it % 8, jnp.int32) == 7)
        guess = _f32_to_key(f_lo + (f_hi - f_lo) * jnp.where(halve, 0.5, g_lo / (g_lo + g_hi)))
        guided = col1(it, jnp.int32) < GUIDED_PROBES
        probe = jnp.where((guess > lo) & (guess < hi) & guided, guess, midpoint(lo, hi))
        probe = jnp.where(converged(lo, hi), lo, probe)
        return absorb(state, probe, count_above(_key_to_f32(probe)))

    def search_cond(carry):
        it, pending = carry[0], carry[1]
        return (pending > 0) & (it < MAX_PROBES)

    def search_body(carry):
        it, state = carry[0], carry[2]
        state = probe_once(it + 1, probe_once(it, state))
        return it + 2, jnp.max(jnp.where(converged(state[0], state[1]), 0, 1)), state

    n_swept = (n_tiles * DSA_TK).astype(F32)
    state = (col1(KEY_NEG_INF, jnp.int32), _f32_to_key(row_max),
             jnp.broadcast_to(n_swept, (1, DSA_TQ)), col1(0.0, F32), col1(1.0, F32), col1(1.0, F32),
             col1(0, jnp.int32))
    for fixed, counts in ((KEY_NEG_ZERO, above_zero), (KEY_NEG_MIN_NORMAL, above_neg)):
        inside = (state[0] < fixed) & (fixed < state[1])
        state = absorb(state, jnp.where(inside, fixed, state[0]),
                       jnp.sum(counts, axis=0, keepdims=True))
    state = lax.fori_loop(0, UNCHECKED_PROBES, probe_once, state)
    state = lax.while_loop(search_cond, search_body,
                           (jnp.int32(UNCHECKED_PROBES), jnp.int32(1), state))[2]
    thr = _key_to_f32(state[1])
    n_ties = kf - state[3]

    to_column = lambda r: jnp.broadcast_to(r, (8, DSA_TQ)).T[:, 0:1]
    thr_c, ties_c = to_column(thr), to_column(n_ties)
    q_all = jnp.concatenate(
        _masked_heads([dq_ref[0, :, 0:LANES], dq_ref[0, :, LANES:2 * LANES]]), axis=0)
    tri = tri_ref[...]
    tri_lo = jnp.where(lax.broadcasted_iota(jnp.int32, (LANES, LANES), 0)
                       >= lax.broadcasted_iota(jnp.int32, (LANES, LANES), 1), 1.0, 0.0).astype(BF16)
    qrow = q0 + lax.broadcasted_iota(jnp.int32, (DSA_TQ, DSA_TK), 0)
    kcol = lax.broadcasted_iota(jnp.int32, (DSA_TQ, DSA_TK), 1)
    n_blocks = DSA_TK // LANES

    reach = _logit_reach(q_all, jnp.max(kn_ref[0], axis=0)[KN_DSA:KN_DSA + 1, 0:1])

    def selected_logits(j, seen, causal):
        k0 = pl.multiple_of(j * DSA_TK, DSA_TK)
        s = s_ref[pl.ds(k0, DSA_TK), :].T
        tie = s == thr_c
        tie_b = jnp.where(tie, 1.0, 0.0).astype(BF16)
        local = [_dot(tie_b[:, b * LANES:(b + 1) * LANES], tri) for b in range(n_blocks)]
        ranks = []
        for b in range(n_blocks):
            ranks.append(local[b] + seen)
            seen = seen + local[b][:, LANES - 1:LANES]
        sel = (s > thr_c) | (tie & (jnp.concatenate(ranks, axis=1) <= ties_c))
        if causal:
            sel = sel & (k0 + kcol <= qrow)
        logits = _dot_nt(q_all, kk_ref[0, pl.ds(k0, DSA_TK), :])
        logits = jnp.where(sel[None], logits.reshape(N_HEADS, DSA_TQ, DSA_TK), NEG)
        return seen, logits.reshape(N_HEADS * DSA_TQ, DSA_TK), va_ref[0, pl.ds(k0, DSA_TK), :]

    def capped(j, carry, causal):
        seen, acc = carry
        k0 = pl.multiple_of(j * DSA_TK, DSA_TK)
        s = s_ref[pl.ds(k0, DSA_TK), :]
        tie = s == thr
        tie_b = jnp.where(tie, 1.0, 0.0).astype(BF16)
        ranks = []
        for b in range(n_blocks):
            local = _dot(tri_lo, tie_b[b * LANES:(b + 1) * LANES, :])
            ranks.append(local + seen)
            seen = seen + local[LANES - 1:LANES, :]
        sel = (s > thr) | (tie & (jnp.concatenate(ranks, axis=0) <= n_ties))
        if causal:
            sel = sel & (k0 + key <= qry)
        keep = jnp.where(sel, 1.0, 0.0).astype(BF16).T
        logits = _dot_nt(q_all, kk_ref[0, pl.ds(k0, DSA_TK), :])
        p = jnp.exp2(logits - reach).astype(BF16).reshape(N_HEADS, DSA_TQ, DSA_TK) * keep[None]
        return seen, acc + _dot(p.reshape(N_HEADS * DSA_TQ, DSA_TK), va_ref[0, pl.ds(k0, DSA_TK), :])

    def online(j, carry, causal):
        seen, logits, va_t = selected_logits(j, carry[0], causal)
        return (seen,) + _softmax_step(logits, carry[1], carry[2], va_t)

    def sweep(step, *stats):
        carry = stats + (jnp.zeros((N_HEADS * DSA_TQ, LANES), F32),)
        carry = lax.fori_loop(0, last, functools.partial(step, causal=False), carry)
        return step(last, carry, causal=True)[-1]

    acc = lax.cond(2.0 * jnp.max(reach) <= SAFE_LOG2_SPAN,
                   functools.partial(sweep, capped, jnp.zeros((1, DSA_TQ), F32)),
                   functools.partial(sweep, online, jnp.zeros((DSA_TQ, 1), F32),
                                     jnp.full((N_HEADS * DSA_TQ, 1), NEG, F32)))
    o_ref[0] = jnp.concatenate(
        _normalise_heads([acc[h * DSA_TQ:(h + 1) * DSA_TQ] for h in range(N_HEADS)]),
        axis=1).astype(o_ref.dtype)


def _dsa_attention(dq, dkk, dva, iq, ikk, small_t, key_norms, k_top):
    bsz, t_work, _ = dq.shape
    tri = jnp.asarray(np.triu(np.ones((LANES, LANES), np.float32)), dtype=BF16)
    tile = lambda width: pl.BlockSpec((1, DSA_TQ, width), lambda b, i: (b, i, 0))
    full = pl.BlockSpec((1, t_work, LANES), lambda b, i: (b, 0, 0), pipeline_mode=pl.Buffered(1))
    return pl.pallas_call(
        functools.partial(_dsa_kernel, k_top),
        grid=(bsz, t_work // DSA_TQ),
        in_specs=[tile(256), full, full, tile(256), full,
                  pl.BlockSpec((1, 8, DSA_TQ), lambda b, i: (b, 0, i)),
                  pl.BlockSpec((LANES, LANES), lambda b, i: (0, 0)),
                  pl.BlockSpec((1,) + key_norms.shape[1:], lambda b, i: (b, 0, 0, 0))],
        out_specs=tile(256),
        out_shape=jax.ShapeDtypeStruct((bsz, t_work, 256), BF16),
        scratch_shapes=[pltpu.VMEM((t_work, DSA_TQ), F32)],
        compiler_params=pltpu.CompilerParams(
            dimension_semantics=("parallel", "parallel"), vmem_limit_bytes=VMEM_LIMIT),
        name="dsa_attention",
    )(dq, dkk, dva, iq, ikk, small_t, tri, key_norms)


FOX_TQ = 512
FOX_TK = 512
CUM_T = 256


def _fox_prep_kernel(x_ref, tri_ref, place_ref, k_ref, xt_ref, kb_ref, carry_ref):
    @pl.when(pl.program_id(1) == 0)
    def _():
        carry_ref[...] = jnp.zeros_like(carry_ref)

    tri = tri_ref[...]
    x = x_ref[0]
    h1, h2, h3 = _split3(x)
    c = _dot(tri, h1) + _dot(tri, h2) + _dot(tri, h3) + carry_ref[0:1, :]
    carry_ref[...] = jnp.broadcast_to(c[CUM_T - 1:CUM_T, :], carry_ref.shape)
    xt_ref[0] = x.T[0:8, :]
    terms = _split3(c * -LOG2E)
    bias = sum(_dot(terms[t], place_ref[t]) for t in range(BIAS_TERMS))
    lane = lax.broadcasted_iota(jnp.int32, bias.shape, 1) % LANES
    kb_ref[0] = jnp.where((lane >= HEAD_DIM) & (lane < HEAD_DIM + BIAS_TERMS),
                          bias.astype(BF16), k_ref[0])


def _fox_prep(small, fk):
    bsz, t_work, width = fk.shape
    tri = jnp.asarray(np.tril(np.ones((CUM_T, CUM_T), np.float32)), dtype=BF16)
    place = np.zeros((BIAS_TERMS, LANES, width), np.float32)
    for t in range(BIAS_TERMS):
        for h in range(N_HEADS):
            place[t, SM_FF + h, h * LANES + HEAD_DIM + t] = 1.0
    rows = lambda w: pl.BlockSpec((1, CUM_T, w), lambda b, j: (b, j, 0))
    return pl.pallas_call(
        _fox_prep_kernel,
        grid=(bsz, t_work // CUM_T),
        in_specs=[rows(LANES), pl.BlockSpec((CUM_T, CUM_T), lambda b, j: (0, 0)),
                  pl.BlockSpec((BIAS_TERMS, LANES, width), lambda b, j: (0, 0, 0)), rows(width)],
        out_specs=[pl.BlockSpec((1, 8, CUM_T), lambda b, j: (b, 0, j)), rows(width)],
        out_shape=[jax.ShapeDtypeStruct((bsz, 8, t_work), F32),
                   jax.ShapeDtypeStruct(fk.shape, BF16)],
        scratch_shapes=[pltpu.VMEM((8, LANES), F32)],
        compiler_params=pltpu.CompilerParams(dimension_semantics=("parallel", "arbitrary")),
        name="fox_prep",
    )(small, tri, jnp.asarray(place, dtype=BF16), fk)


def _logit_reach(q, kmax_sq):
    qf = q.astype(F32)
    return jnp.sqrt(jnp.sum(qf * qf, axis=-1, keepdims=True) * kmax_sq) * NORM_MARGIN


def _fox_kernel(q_ref, k_ref, v_ref, kn_ref, o_ref):
    i = pl.program_id(1)
    q0 = pl.multiple_of(i * FOX_TQ, FOX_TQ)
    n_full = q0 // FOX_TK
    row = q0 + lax.broadcasted_iota(jnp.int32, (FOX_TQ, FOX_TK), 0)
    col = n_full * FOX_TK + lax.broadcasted_iota(jnp.int32, (FOX_TQ, FOX_TK), 1)
    heads = [slice(h * LANES, (h + 1) * LANES) for h in range(N_HEADS)]

    lane = lax.broadcasted_iota(jnp.int32, (FOX_TQ, LANES), 1)
    bias_lanes = (lane >= HEAD_DIM) & (lane < HEAD_DIM + BIAS_TERMS)
    kmax_sq = jnp.max(kn_ref[0], axis=0)
    caps, span = [], jnp.float32(0.0)
    for h in range(N_HEADS):
        q_h = q_ref[0, :, heads[h]]
        reach = _logit_reach(jnp.where(lane < HEAD_DIM, q_h, jnp.zeros_like(q_h)),
                             kmax_sq[KN_FOX + h:KN_FOX + h + 1, 0:1])
        own = k_ref[0, pl.ds(q0, FOX_TQ), heads[h]].astype(F32)
        caps.append(reach + jnp.sum(jnp.where(bias_lanes, own, 0.0), axis=-1, keepdims=True))
        span = jnp.maximum(span, 2.0 * jnp.max(reach))

    def logits(j, h, diag):
        k0 = pl.multiple_of(j * FOX_TK, FOX_TK)
        s = _dot_nt(q_ref[0, :, heads[h]], k_ref[0, pl.ds(k0, FOX_TK), heads[h]])
        return jnp.where(col <= row, s, NEG) if diag else s

    def values(j, h):
        return v_ref[0, pl.ds(pl.multiple_of(j * FOX_TK, FOX_TK), FOX_TK), heads[h]]

    def capped(j, accs, diag):
        return tuple(accs[h] + _dot(jnp.exp2(logits(j, h, diag) - caps[h]).astype(BF16), values(j, h))
                     for h in range(N_HEADS))

    def online(j, carry, diag):
        return tuple(_softmax_step(logits(j, h, diag), *carry[h], values(j, h))
                     for h in range(N_HEADS))

    def capped_sweep():
        accs = tuple(jnp.zeros((FOX_TQ, LANES), F32) for _ in range(N_HEADS))
        accs = lax.fori_loop(0, n_full, functools.partial(capped, diag=False), accs)
        return capped(n_full, accs, diag=True)

    def online_sweep():
        carry = tuple((jnp.full((FOX_TQ, 1), NEG, F32), jnp.zeros((FOX_TQ, LANES), F32))
                      for _ in range(N_HEADS))
        carry = lax.fori_loop(0, n_full, functools.partial(online, diag=False), carry)
        return tuple(acc for _, acc in online(n_full, carry, diag=True))

    accs = lax.cond(span <= SAFE_LOG2_SPAN, capped_sweep, online_sweep)
    o_ref[0] = jnp.concatenate(_normalise_heads(list(accs)), axis=1).astype(o_ref.dtype)


def _fox_attention(fqa, fkb, fva, key_norms):
    bsz, t_work, width = fqa.shape
    full = pl.BlockSpec((1, t_work, width), lambda b, i: (b, 0, 0), pipeline_mode=pl.Buffered(1))
    return pl.pallas_call(
        _fox_kernel,
        grid=(bsz, t_work // FOX_TQ),
        in_specs=[pl.BlockSpec((1, FOX_TQ, width), lambda b, i: (b, i, 0)), full, full,
                  pl.BlockSpec((1,) + key_norms.shape[1:], lambda b, i: (b, 0, 0, 0))],
        out_specs=pl.BlockSpec((1, FOX_TQ, 2 * LANES), lambda b, i: (b, i, 0)),
        out_shape=jax.ShapeDtypeStruct((bsz, t_work, 2 * LANES), BF16),
        compiler_params=pltpu.CompilerParams(
            dimension_semantics=("parallel", "parallel"), vmem_limit_bytes=VMEM_LIMIT),
        name="fox_attention",
    )(fqa, fkb, fva, key_norms)


GLA_TG = 256
GLA_NSUB = GLA_CHUNK // GLA_SUB


def _gla_kernel(q_ref, k_ref, v_ref, g_ref, la_ref, tri_ref, e_ref, gn_ref, o_ref, st_ref):
    @pl.when(pl.program_id(1) == 0)
    def _():
        st_ref[...] = jnp.zeros_like(st_ref)

    tri = tri_ref[...]
    emat = e_ref[...]
    lane = lax.broadcasted_iota(jnp.int32, (GLA_CHUNK, LANES), 1)
    rowblk = lax.broadcasted_iota(jnp.int32, (GLA_CHUNK, LANES), 0) // GLA_SUB
    tblk = lax.broadcasted_iota(jnp.int32, (GLA_CHUNK, GLA_CHUNK), 0) // GLA_SUB
    sblk = lax.broadcasted_iota(jnp.int32, (GLA_CHUNK, GLA_CHUNK), 1) // GLA_SUB
    trow = lax.broadcasted_iota(jnp.int32, (GLA_SUB, 256), 0)

    def chunk(c, carry):
        r0 = pl.multiple_of(c * GLA_CHUNK, GLA_CHUNK)
        rows = pl.ds(r0, GLA_CHUNK)
        h1, h2, h3 = _split3(la_ref[0, rows, :])
        b = _dot(tri, h1) + _dot(tri, h2) + _dot(tri, h3)
        q = q_ref[0, rows, :]
        k = k_ref[0, rows, :]
        v = v_ref[0, rows, :]
        b_last = b[GLA_CHUNK - 1:GLA_CHUNK, :]
        qd = q * jnp.exp2(b)
        kd = (k * jnp.exp2(b_last - b)).astype(BF16)
        starts = [jnp.zeros((1, 256), F32)] + [b[GLA_SUB * i - 1:GLA_SUB * i, :]
                                               for i in range(1, GLA_NSUB)]
        bsel = jnp.concatenate([jnp.broadcast_to(s, (GLA_SUB, 256)) for s in starts], axis=0)
        qn = q * jnp.exp2(b - bsel)

        diag = []
        for i in range(GLA_NSUB):
            rs = slice(GLA_SUB * i, GLA_SUB * (i + 1))
            b_i, q_i, k_i = b[rs], q[rs], k[rs]
            v_i = v[rs].astype(F32)
            ps = []
            for s in range(GLA_SUB):
                d = jnp.exp2(jnp.minimum(b_i - b_i[s:s + 1], 0.0))
                ps.append(jnp.where(trow >= s, q_i * d * k_i[s:s + 1], 0.0).astype(BF16))
            r = _dot(jnp.concatenate(ps, axis=0), emat)
            od = r[0:GLA_SUB] * v_i[0:1]
            for s in range(1, GLA_SUB):
                od = od + r[GLA_SUB * s:GLA_SUB * (s + 1)] * v_i[s:s + 1]
            diag.append(od)
        o_diag = jnp.concatenate(diag, axis=0)

        for slab in range(2):
            ls = slice(slab * LANES, (slab + 1) * LANES)
            qn_s, k_s, b_s = qn[:, ls], k[:, ls], b[:, ls]
            khat = jnp.concatenate(
                [(k_s * jnp.exp2(jnp.minimum(starts[i][:, ls] - b_s, 0.0))).astype(BF16)
                 for i in range(1, GLA_NSUB)], axis=1)
            for half in range(2):
                head = 2 * slab + half
                hs = slice(head * GLA_DV, (head + 1) * GLA_DV)
                in_head = (lane < HEAD_DIM) if half == 0 else (lane >= HEAD_DIM)
                qm = jnp.where(in_head, qn_s, 0.0)
                qhat = jnp.concatenate([jnp.where(rowblk == i, qm, 0.0).astype(BF16)
                                        for i in range(1, GLA_NSUB)], axis=1)
                att = jnp.where(sblk < tblk, _dot_nt(qhat, khat), 0.0)
                v_h = v[:, hs]
                st = st_ref[head]
                o = (_dot_nt(jnp.where(in_head, qd[:, ls], 0.0).astype(BF16), st.astype(BF16))
                     + _dot(att.astype(BF16), v_h) + o_diag[:, hs])
                st_ref[head] = st * jnp.exp2(b_last[:, ls]) + _dot_tn(v_h, kd[:, ls])
                y = o * lax.rsqrt(jnp.mean(o * o, axis=-1, keepdims=True) + EPS) * gn_ref[:, hs]
                o_ref[0, rows, hs] = (y * _silu(g_ref[0, rows, hs])).astype(o_ref.dtype)
        return carry

    lax.fori_loop(0, GLA_TG // GLA_CHUNK, chunk, 0)


def _gla(gq, gk, gv, gg, la, gain):
    bsz, t_work, _ = gq.shape
    tri = jnp.asarray(np.tril(np.ones((GLA_CHUNK, GLA_CHUNK), np.float32)), dtype=BF16)
    emat = jnp.asarray(
        (np.arange(256)[:, None] // HEAD_DIM == np.arange(512)[None, :] // GLA_DV).astype(np.float32),
        dtype=BF16)
    rows = lambda width: pl.BlockSpec((1, GLA_TG, width), lambda b, j: (b, j, 0))
    const = lambda shape: pl.BlockSpec(shape, lambda b, j: (0,) * len(shape))
    return pl.pallas_call(
        _gla_kernel,
        grid=(bsz, t_work // GLA_TG),
        in_specs=[rows(256), rows(256), rows(512), rows(512), rows(256),
                  const((GLA_CHUNK, GLA_CHUNK)), const((256, 512)), const((1, 512))],
        out_specs=rows(512),
        out_shape=jax.ShapeDtypeStruct((bsz, t_work, 512), BF16),
        scratch_shapes=[pltpu.VMEM((N_HEADS, GLA_DV, LANES), F32)],
        compiler_params=pltpu.CompilerParams(
            dimension_semantics=("parallel", "arbitrary"), vmem_limit_bytes=VMEM_LIMIT),
        name="gla",
    )(gq, gk, gv, gg, la, tri, emat, gain)


FFN_CHUNK = 256


def _ffn_kernel(h_ref, oa_ref, ob_ref, oc_ref, wo_ref, fn_ref, wgu_ref, wd_ref, out_ref):
    h1 = (h_ref[0] + _dot(oa_ref[0], wo_ref[0:256, :]) + _dot(ob_ref[0], wo_ref[256:512, :])
          + _dot(oc_ref[0], wo_ref[512:1024, :]))
    ms = jnp.mean(h1 * h1, axis=-1, keepdims=True)
    f = (h1 * lax.rsqrt(ms + EPS) * fn_ref[...]).astype(BF16)
    out_ref[0] = h1
    for c in range(0, D_FF, FFN_CHUNK):
        gate = _dot(f, wgu_ref[:, c:c + FFN_CHUNK])
        up = _dot(f, wgu_ref[:, D_FF + c:D_FF + c + FFN_CHUNK])
        out_ref[0] += _dot((_silu(gate) * up).astype(BF16), wd_ref[c:c + FFN_CHUNK, :])


def _outproj_ffn(h, oa, ob, oc, wo, fn, wgu, wd):
    bsz, t_work, _ = h.shape
    tm = _row_tile(t_work)
    rows = lambda width: pl.BlockSpec((1, tm, width), lambda b, j: (b, j, 0))
    const = lambda shape: pl.BlockSpec(shape, lambda b, j: (0,) * len(shape),
                                       pipeline_mode=pl.Buffered(1))
    return pl.pallas_call(
        _ffn_kernel,
        grid=(bsz, t_work // tm),
        in_specs=[rows(D_MODEL), rows(256), rows(256), rows(512), const((D_MODEL, D_MODEL)),
                  const((1, D_MODEL)), const((D_MODEL, 2 * D_FF)), const((D_FF, D_MODEL))],
        out_specs=rows(D_MODEL),
        out_shape=jax.ShapeDtypeStruct(h.shape, F32),
        compiler_params=pltpu.CompilerParams(
            dimension_semantics=("parallel", "parallel"), vmem_limit_bytes=VMEM_LIMIT),
        name="outproj_ffn",
    )(h, oa, ob, oc, wo, fn, wgu, wd)


def kernel(x, meta_tokens, attn_norm, w_in, dsa_q_norm, dsa_k_norm, fox_q_norm, fox_k_norm,
           fox_f_bias, gla_gate_w2, gla_gate_b, gla_out_norm, w_out, ffn_norm, w_gate_up, w_down):
    bsz, seq, _ = x.shape
    n_tok = N_META + seq
    t_work = _work_len(n_tok)
    meta = jnp.broadcast_to(meta_tokens[None].astype(x.dtype), (bsz, N_META, D_MODEL))
    h = jnp.concatenate([meta, x, jnp.zeros((bsz, t_work - n_tok, D_MODEL), x.dtype)], axis=1)
    cos_t, sin_t = _rope_tables(t_work)
    g256 = _group_matrix()
    k_top = min(TOPK_MAX, seq // 4)
    for l in range(w_in.shape[0]):
        an, w_perm, vecs, w2_pad, sm_bias = _prep_layer_params(
            l, attn_norm, w_in, dsa_q_norm, dsa_k_norm, fox_q_norm, fox_k_norm, fox_f_bias,
            gla_gate_w2, gla_gate_b)
        (dq, dkk, dva, iq, ikk, fq, fk, fva, gq, gk, gv, gg, la, small, key_norms) = _inproj(
            h, an, w_perm, cos_t, sin_t, g256, vecs, w2_pad, sm_bias)
        small_t, fkb = _fox_prep(small, fk)
        oa = _dsa_attention(dq, dkk, dva, iq, ikk, small_t, key_norms, k_top)
        ob = _fox_attention(fq, fkb, fva, key_norms)
        oc = _gla(gq, gk, gv, gg, la, jnp.tile(gla_out_norm[l], N_HEADS)[None, :])
        h = _outproj_ffn(h, oa, ob, oc, w_out[l].astype(BF16), ffn_norm[l][None, :],
                         w_gate_up[l].astype(BF16), w_down[l].astype(BF16))
    return h[:, N_META:n_tok]
```

```python
import functools

import numpy as np
import jax
import jax.numpy as jnp
from jax import lax
from jax.experimental import pallas as pl
from jax.experimental.pallas import tpu as pltpu

F32 = jnp.float32
BF16 = jnp.bfloat16

D_MODEL = 1024
HEAD_DIM = 64
N_META = 16
ROPE_THETA = 500000.0
ROPE_DIM = HEAD_DIM // 4
ROPE_HALF = ROPE_DIM // 2
NEG = -1e30
EPS = 1e-6

N_HEADS = 4
TOPK_MAX = 256
GLA_DV = 128
GLA_RANK = 16
GLA_TAU = 16.0
GLA_CHUNK = 64
GLA_SUB = 16
D_FF = 2816

LOG2E = 1.4426950408889634
BIAS_TERMS = 3
KN_DSA, KN_FOX = 0, 1
NORM_MARGIN = 1.01
SAFE_LOG2_SPAN = 100.0
LANES = 128
SEQ_ALIGN = 512
VMEM_LIMIT = 56 * 1024 * 1024

_SLABS = (("dq", 256), ("dkk", 128), ("dva", 128), ("iq", 256), ("ikk", 128),
          ("fq", 256), ("fk", 256), ("fv", 256), ("gq", 256), ("gk", 256),
          ("gv", 512), ("gg", 512), ("small", 128))
_SLAB_OFF = {}
_off = 0
for _name, _width in _SLABS:
    _SLAB_OFF[_name] = (_off, _width)
    _off += _width
N_PROJ = _off
SM_IW, SM_FF, SM_GLR = 0, 4, 8


def _work_len(n_tok):
    return -(-n_tok // SEQ_ALIGN) * SEQ_ALIGN


def _row_tile(t_work):
    for cand in (768, 640, 512):
        if t_work % cand == 0:
            return cand
    raise ValueError(f"unsupported working length {t_work}")


def _dot(a, b):
    return jnp.dot(a, b, preferred_element_type=F32)


def _dot_nt(a, b):
    return lax.dot_general(a, b, (((1,), (1,)), ((), ())), preferred_element_type=F32)


def _dot_tn(a, b):
    return lax.dot_general(a, b, (((0,), (0,)), ((), ())), preferred_element_type=F32)


def _split3(x):
    h1 = x.astype(BF16)
    r1 = x - h1.astype(F32)
    h2 = r1.astype(BF16)
    h3 = (r1 - h2.astype(F32)).astype(BF16)
    return h1, h2, h3


def _log_sigmoid(x):
    return jnp.minimum(x, 0.0) - jnp.log1p(jnp.exp(-jnp.abs(x)))


def _silu(x):
    return x / (1.0 + jnp.exp(-x))


def _group_rms(y, gmat, gain):
    yy = y * y
    hi = yy.astype(BF16)
    lo = (yy - hi.astype(F32)).astype(BF16)
    ss = _dot(hi, gmat) + _dot(lo, gmat)
    return y * lax.rsqrt(ss * (1.0 / HEAD_DIM) + EPS) * gain


def _rope(y, cos, sin):
    width = y.shape[-1]
    lane = lax.broadcasted_iota(jnp.int32, y.shape, 1) % HEAD_DIM
    upper = pltpu.roll(y, width - ROPE_HALF, axis=1)
    lower = pltpu.roll(y, ROPE_HALF, axis=1)
    partner = jnp.where(lane < ROPE_HALF, upper, lower)
    return y * cos + partner * sin


def _inproj_kernel(x_ref, an_ref, w_ref, cos_ref, sin_ref, g256_ref, vec_ref, w2_ref, sm_ref,
                   dq_ref, dkk_ref, dva_ref, iq_ref, ikk_ref, fq_ref, fk_ref, fva_ref,
                   gq_ref, gk_ref, gv_ref, gg_ref, la_ref, small_ref, kn_ref):
    x = x_ref[0]
    ms = jnp.mean(x * x, axis=-1, keepdims=True)
    a = (x * lax.rsqrt(ms + EPS) * an_ref[...]).astype(BF16)

    def proj(name):
        off, width = _SLAB_OFF[name]
        return _dot(a, w_ref[:, off:off + width])

    def ones_in_upper_half(y):
        lane = lax.broadcasted_iota(jnp.int32, y.shape, 1) % LANES
        return jnp.where(lane < HEAD_DIM, y, 1.0)

    cos = cos_ref[...]
    sin = sin_ref[...]
    g256 = g256_ref[...]
    g128 = g256[:LANES, :LANES]
    dqn, dkn, fqn, fkn = vec_ref[0:1, :], vec_ref[1:2, :LANES], vec_ref[2:3, :], vec_ref[3:4, :]
    scale = HEAD_DIM ** -0.5

    def max_sq_norm(k):
        kf = k.astype(F32)
        return jnp.max(_dot((kf * kf).astype(BF16), g128), axis=0, keepdims=True)

    dq_ref[0] = (_rope(_group_rms(proj("dq"), g256, dqn), cos, sin) * (scale * LOG2E)).astype(BF16)
    dkk = _rope(_group_rms(proj("dkk"), g128, dkn), cos[:, :LANES], sin[:, :LANES]).astype(BF16)
    dkk_ref[0] = dkk
    key_norms = [max_sq_norm(dkk)]
    dva_ref[0] = ones_in_upper_half(proj("dva")).astype(BF16)
    iq_ref[0] = (_rope(proj("iq"), cos, sin) * scale).astype(BF16)
    ikk_ref[0] = _rope(proj("ikk"), cos[:, :LANES], sin[:, :LANES]).astype(BF16)
    fq, fk, fv = proj("fq"), proj("fk"), proj("fv")
    lane = lax.broadcasted_iota(jnp.int32, (fq.shape[0], LANES), 1)
    in_head = lane < HEAD_DIM
    q_pad = jnp.where(lane < HEAD_DIM + BIAS_TERMS, 1.0, 0.0)
    for pair in range(N_HEADS // 2):
        ps = slice(pair * LANES, (pair + 1) * LANES)
        q_pair = _group_rms(fq[:, ps], g128, fqn[:, :LANES]) * (scale * LOG2E)
        k_pair = _group_rms(fk[:, ps], g128, fkn[:, :LANES])
        v_pair = fv[:, ps]
        for odd in range(2):
            hs = slice((2 * pair + odd) * LANES, (2 * pair + odd + 1) * LANES)
            down = (lambda y: pltpu.roll(y, HEAD_DIM, axis=1)) if odd else (lambda y: y)
            fq_ref[0, :, hs] = jnp.where(in_head, down(q_pair), q_pad).astype(BF16)
            k_h = jnp.where(in_head, down(k_pair), 0.0).astype(BF16)
            fk_ref[0, :, hs] = k_h
            key_norms.append(max_sq_norm(k_h))
            fva_ref[0, :, hs] = jnp.where(in_head, down(v_pair), 1.0).astype(BF16)
    kn_ref[0, 0] = jnp.concatenate(key_norms + [jnp.zeros((8 - len(key_norms), LANES), F32)], axis=0)
    gq_ref[0] = proj("gq") * scale
    gk_ref[0] = proj("gk")
    gv_ref[0] = proj("gv").astype(BF16)
    gg_ref[0] = proj("gg")

    small = proj("small")
    lane = lax.broadcasted_iota(jnp.int32, small.shape, 1)
    small_ref[0] = jnp.where(lane < SM_FF, small * (N_HEADS ** -0.5),
                             _log_sigmoid(small + sm_ref[0:1, :]))
    gate = _dot(small.astype(BF16), w2_ref[...]) + vec_ref[4:5, :]
    la_ref[0] = _log_sigmoid(gate) * (LOG2E / GLA_TAU)


def _inproj(h, an, w_perm, cos_t, sin_t, g256, vecs, w2_pad, sm_bias):
    bsz, t_work, _ = h.shape
    tm = _row_tile(t_work)
    grid = (bsz, t_work // tm)

    def rows(width, dtype):
        return (jax.ShapeDtypeStruct((bsz, t_work, width), dtype),
                pl.BlockSpec((1, tm, width), lambda b, j: (b, j, 0)))

    outs = [rows(256, BF16), rows(128, BF16), rows(128, BF16), rows(256, BF16), rows(128, BF16),
            rows(512, BF16), rows(512, BF16), rows(512, BF16), rows(256, F32), rows(256, F32),
            rows(512, BF16), rows(512, F32), rows(256, F32), rows(128, F32),
            (jax.ShapeDtypeStruct((bsz, t_work // tm, 8, LANES), F32),
             pl.BlockSpec((1, 1, 8, LANES), lambda b, j: (b, j, 0, 0)))]
    const = lambda shape: pl.BlockSpec(shape, lambda b, j: (0,) * len(shape))
    return pl.pallas_call(
        _inproj_kernel,
        grid=grid,
        in_specs=[pl.BlockSpec((1, tm, D_MODEL), lambda b, j: (b, j, 0)),
                  const((1, D_MODEL)), const((D_MODEL, N_PROJ)),
                  pl.BlockSpec((tm, 256), lambda b, j: (j, 0)),
                  pl.BlockSpec((tm, 256), lambda b, j: (j, 0)),
                  const((256, 256)), const((8, 256)), const((LANES, 256)), const((8, LANES))],
        out_specs=[o[1] for o in outs],
        out_shape=[o[0] for o in outs],
        compiler_params=pltpu.CompilerParams(
            dimension_semantics=("parallel", "parallel"), vmem_limit_bytes=VMEM_LIMIT),
        name="inproj",
    )(h, an, w_perm, cos_t, sin_t, g256, vecs, w2_pad, sm_bias)


def _prep_layer_params(l, attn_norm, w_in, dsa_q_norm, dsa_k_norm, fox_q_norm, fox_k_norm,
                       fox_f_bias, gla_gate_w2, gla_gate_b):
    w = w_in[l]
    splits = np.cumsum([256, 64, 64, 256, 4, 64, 256, 256, 256, 4, 256, 256, 512, 512, 16])[:-1]
    (dq, dk, dv, iq, iw, ik, fq, fk, fv, ff, gq, gk, gv, gg, glr) = jnp.split(w, splits, axis=1)
    small = jnp.concatenate(
        [iw, ff, glr, jnp.zeros((D_MODEL, LANES - 4 - 4 - GLA_RANK), w.dtype)], axis=1)
    z64 = jnp.zeros((D_MODEL, HEAD_DIM), w.dtype)
    w_perm = jnp.concatenate([dq, dk, dk, dv, z64, iq, ik, ik, fq, fk, fv, gq, gk, gv, gg, small],
                             axis=1).astype(BF16)
    tile4 = lambda g: jnp.tile(g, N_HEADS)
    vecs = jnp.zeros((8, 256), F32)
    vecs = vecs.at[0].set(tile4(dsa_q_norm[l])).at[1].set(tile4(dsa_k_norm[l]))
    vecs = vecs.at[2].set(tile4(fox_q_norm[l])).at[3].set(tile4(fox_k_norm[l]))
    vecs = vecs.at[4].set(gla_gate_b[l])
    w2_pad = jnp.zeros((LANES, 256), F32).at[SM_GLR:SM_GLR + GLA_RANK].set(gla_gate_w2[l]).astype(BF16)
    sm_bias = jnp.zeros((8, LANES), F32).at[0, SM_FF:SM_FF + N_HEADS].set(fox_f_bias[l])
    return attn_norm[l][None, :], w_perm, vecs, w2_pad, sm_bias


def _rope_tables(t_work):
    inv = jnp.power(ROPE_THETA, -jnp.arange(ROPE_HALF, dtype=F32) * 2.0 / ROPE_DIM)
    ang = jnp.arange(t_work).astype(F32)[:, None] * inv[None, :]
    cos, sin = jnp.cos(ang), jnp.sin(ang)
    rest = HEAD_DIM - ROPE_DIM
    cos64 = jnp.concatenate([cos, cos, jnp.ones((t_work, rest), F32)], axis=1)
    sin64 = jnp.concatenate([-sin, sin, jnp.zeros((t_work, rest), F32)], axis=1)
    return jnp.tile(cos64, (1, N_HEADS)), jnp.tile(sin64, (1, N_HEADS))


def _group_matrix():
    idx = np.arange(256) // HEAD_DIM
    return jnp.asarray((idx[:, None] == idx[None, :]).astype(np.float32), dtype=BF16)


DSA_TQ = 256
DSA_TK = 512
KEY_NEG_INF = -2139095041
KEY_NEG_ZERO = -1
KEY_NEG_MIN_NORMAL = -8388609
NEG_MIN_NORMAL = -1.1754943508222875e-38
SEARCH_FEW = 4.0
SEARCH_MANY = 16.0
UNCHECKED_PROBES = 12
GUIDED_PROBES = 64
MAX_PROBES = GUIDED_PROBES + 40
COUNT_ROWS = 128


def _key_to_f32(key):
    bits = key ^ ((key >> 31) & 0x7FFFFFFF)
    return lax.bitcast_convert_type(bits, F32)


def _f32_to_key(value):
    bits = lax.bitcast_convert_type(value, jnp.int32)
    return bits ^ ((bits >> 31) & 0x7FFFFFFF)


def _head_lane_mask(shape, head):
    lane = lax.broadcasted_iota(jnp.int32, shape, 1)
    return (lane < HEAD_DIM) if head % 2 == 0 else (lane >= HEAD_DIM)


def _masked_heads(slabs):
    return [jnp.where(_head_lane_mask(slabs[h // 2].shape, h), slabs[h // 2],
                      jnp.zeros_like(slabs[h // 2])) for h in range(N_HEADS)]


def _softmax_step(s, m, acc, v_aug):
    m_new = jnp.maximum(m, jnp.max(s, axis=-1, keepdims=True))
    p = jnp.exp2(s - m_new)
    return m_new, acc * jnp.exp2(m - m_new) + _dot(p.astype(BF16), v_aug)


def _normalise_heads(accs):
    outs = [acc / pltpu.roll(acc, HEAD_DIM, axis=1) for acc in accs]
    lane = lax.broadcasted_iota(jnp.int32, outs[0].shape, 1)
    return [jnp.where(lane < HEAD_DIM, outs[2 * p], pltpu.roll(outs[2 * p + 1], HEAD_DIM, axis=1))
            for p in range(2)]


def _dsa_kernel(k_top, dq_ref, kk_ref, va_ref, iq_ref, ik_ref, wt_ref, tri_ref, kn_ref, o_ref,
                s_ref):
    i = pl.program_id(1)
    q0 = i * DSA_TQ
    n_tiles = (q0 + DSA_TQ + DSA_TK - 1) // DSA_TK
    last = n_tiles - 1
    key = lax.broadcasted_iota(jnp.int32, (DSA_TK, DSA_TQ), 0)
    qry = q0 + lax.broadcasted_iota(jnp.int32, (DSA_TK, DSA_TQ), 1)
    head = lambda x, h: x[:, h * DSA_TQ:(h + 1) * DSA_TQ]

    iq_all = jnp.concatenate(
        _masked_heads([iq_ref[0, :, 0:LANES], iq_ref[0, :, LANES:2 * LANES]]), axis=0)
    w_h = [wt_ref[0, SM_IW + h:SM_IW + h + 1, :] for h in range(N_HEADS)]

    def score_tile(j, carry, causal):
        k0 = pl.multiple_of(j * DSA_TK, DSA_TK)
        dots = jnp.maximum(_dot_nt(ik_ref[0, pl.ds(k0, DSA_TK), :], iq_all), 0.0)
        s = w_h[0] * head(dots, 0)
        for h in range(1, N_HEADS):
            s = s + w_h[h] * head(dots, h)
        if causal:
            s = jnp.where(k0 + key <= qry, s, NEG)
        s_ref[pl.ds(k0, DSA_TK), :] = s
        row_max, above_zero, above_neg = carry
        for part in range(DSA_TK // COUNT_ROWS):
            chunk = s[part * COUNT_ROWS:(part + 1) * COUNT_ROWS]
            above_zero = above_zero + jnp.where(chunk > 0.0, 1.0, 0.0)
            above_neg = above_neg + jnp.where(chunk > NEG_MIN_NORMAL, 1.0, 0.0)
        return jnp.maximum(row_max, jnp.max(s, axis=0, keepdims=True)), above_zero, above_neg

    zeros = jnp.zeros((COUNT_ROWS, DSA_TQ), F32)
    row_max, above_zero, above_neg = score_tile(last, lax.fori_loop(
        0, last, functools.partial(score_tile, causal=False),
        (jnp.full((1, DSA_TQ), NEG, F32), zeros, zeros)), causal=True)

    kf = jnp.float32(k_top)

    def count_above(t):
        tb = jnp.broadcast_to(t, (COUNT_ROWS, DSA_TQ))

        def body(j, acc, span):
            k0 = pl.multiple_of(j * span, span)
            for part in range(span // COUNT_ROWS):
                s = s_ref[pl.ds(k0 + part * COUNT_ROWS, COUNT_ROWS), :]
                acc = acc + jnp.where(s > tb, 1.0, 0.0)
            return acc

        acc = lax.fori_loop(0, n_tiles // 2, functools.partial(body, span=2 * DSA_TK),
                            jnp.zeros((COUNT_ROWS, DSA_TQ), F32))
        acc = lax.fori_loop(2 * (n_tiles // 2), n_tiles, functools.partial(body, span=DSA_TK), acc)
        return jnp.sum(acc, axis=0, keepdims=True)

    def midpoint(lo, hi):
        return (lo >> 1) + (hi >> 1) + (lo & hi & 1)

    def converged(lo, hi):
        return (midpoint(lo, hi) == lo) | ((lo >= KEY_NEG_MIN_NORMAL) & (hi <= 0))

    col1 = lambda value, dtype: jnp.full((1, DSA_TQ), value, dtype)

    def absorb(state, probe, c):
        lo, hi, clo, chi, wlo, whi, side = state
        live = probe != lo
        up = live & (c >= kf)
        down = live & (c <= kf)
        wlo = jnp.where(down & (side < 0), wlo * 0.5, jnp.where(up, 1.0, wlo))
        whi = jnp.where(up & (side > 0), whi * 0.5, jnp.where(down, 1.0, whi))
        side = jnp.where(up, 1, jnp.where(down, -1, side))
        lo, clo = jnp.where(up, probe, lo), jnp.where(up, c, clo)
        hi, chi = jnp.where(down, probe, hi), jnp.where(down, c, chi)
        return lo, hi, clo, chi, wlo, whi, side

    def probe_once(it, state):
        lo, hi, clo, chi, wlo, whi, side = state
        f_lo, f_hi = _key_to_f32(lo), _key_to_f32(hi)
        target = kf - 0.5
        log_count = lambda c: jnp.log2(jnp.maximum(c, 0.25))
        many = clo - chi > SEARCH_MANY
        g_lo = jnp.where(many, log_count(clo) - np.log2(k_top - 0.5), clo - target) * wlo
        g_hi = jnp.where(many, np.log2(k_top - 0.5) - log_count(chi), target - chi) * whi
        halve = (clo - chi <= SEARCH_FEW) | (col1(it % 8, jnp.int32) == 7)
        guess = _f32_to_key(f_lo + (f_hi - f_lo) * jnp.where(halve, 0.5, g_lo / (g_lo + g_hi)))
        guided = col1(it, jnp.int32) < GUIDED_PROBES
        probe = jnp.where((guess > lo) & (guess < hi) & guided, guess, midpoint(lo, hi))
        probe = jnp.where(converged(lo, hi), lo, probe)
        return absorb(state, probe, count_above(_key_to_f32(probe)))

    def search_cond(carry):
        it, pending = carry[0], carry[1]
        return (pending > 0) & (it < MAX_PROBES)

    def search_body(carry):
        it, state = carry[0], carry[2]
        state = probe_once(it + 1, probe_once(it, state))
        return it + 2, jnp.max(jnp.where(converged(state[0], state[1]), 0, 1)), state

    n_swept = (n_tiles * DSA_TK).astype(F32)
    state = (col1(KEY_NEG_INF, jnp.int32), _f32_to_key(row_max),
             jnp.broadcast_to(n_swept, (1, DSA_TQ)), col1(0.0, F32), col1(1.0, F32), col1(1.0, F32),
             col1(0, jnp.int32))
    for fixed, counts in ((KEY_NEG_ZERO, above_zero), (KEY_NEG_MIN_NORMAL, above_neg)):
        inside = (state[0] < fixed) & (fixed < state[1])
        state = absorb(state, jnp.where(inside, fixed, state[0]),
                       jnp.sum(counts, axis=0, keepdims=True))
    state = lax.fori_loop(0, UNCHECKED_PROBES, probe_once, state)
    state = lax.while_loop(search_cond, search_body,
                           (jnp.int32(UNCHECKED_PROBES), jnp.int32(1), state))[2]
    thr = _key_to_f32(state[1])
    n_ties = kf - state[3]

    to_column = lambda r: jnp.broadcast_to(r, (8, DSA_TQ)).T[:, 0:1]
    thr_c, ties_c = to_column(thr), to_column(n_ties)
    q_all = jnp.concatenate(
        _masked_heads([dq_ref[0, :, 0:LANES], dq_ref[0, :, LANES:2 * LANES]]), axis=0)
    tri = tri_ref[...]
    tri_lo = jnp.where(lax.broadcasted_iota(jnp.int32, (LANES, LANES), 0)
                       >= lax.broadcasted_iota(jnp.int32, (LANES, LANES), 1), 1.0, 0.0).astype(BF16)
    qrow = q0 + lax.broadcasted_iota(jnp.int32, (DSA_TQ, DSA_TK), 0)
    kcol = lax.broadcasted_iota(jnp.int32, (DSA_TQ, DSA_TK), 1)
    n_blocks = DSA_TK // LANES

    reach = _logit_reach(q_all, jnp.max(kn_ref[0], axis=0)[KN_DSA:KN_DSA + 1, 0:1])

    def selected_logits(j, seen, causal):
        k0 = pl.multiple_of(j * DSA_TK, DSA_TK)
        s = s_ref[pl.ds(k0, DSA_TK), :].T
        tie = s == thr_c
        tie_b = jnp.where(tie, 1.0, 0.0).astype(BF16)
        local = [_dot(tie_b[:, b * LANES:(b + 1) * LANES], tri) for b in range(n_blocks)]
        ranks = []
        for b in range(n_blocks):
            ranks.append(local[b] + seen)
            seen = seen + local[b][:, LANES - 1:LANES]
        sel = (s > thr_c) | (tie & (jnp.concatenate(ranks, axis=1) <= ties_c))
        if causal:
            sel = sel & (k0 + kcol <= qrow)
        logits = _dot_nt(q_all, kk_ref[0, pl.ds(k0, DSA_TK), :])
        logits = jnp.where(sel[None], logits.reshape(N_HEADS, DSA_TQ, DSA_TK), NEG)
        return seen, logits.reshape(N_HEADS * DSA_TQ, DSA_TK), va_ref[0, pl.ds(k0, DSA_TK), :]

    def capped(j, carry, causal):
        seen, acc = carry
        k0 = pl.multiple_of(j * DSA_TK, DSA_TK)
        s = s_ref[pl.ds(k0, DSA_TK), :]
        tie = s == thr
        tie_b = jnp.where(tie, 1.0, 0.0).astype(BF16)
        ranks = []
        for b in range(n_blocks):
            local = _dot(tri_lo, tie_b[b * LANES:(b + 1) * LANES, :])
            ranks.append(local + seen)
            seen = seen + local[LANES - 1:LANES, :]
        sel = (s > thr) | (tie & (jnp.concatenate(ranks, axis=0) <= n_ties))
        if causal:
            sel = sel & (k0 + key <= qry)
        keep = jnp.where(sel, 1.0, 0.0).astype(BF16).T
        logits = _dot_nt(q_all, kk_ref[0, pl.ds(k0, DSA_TK), :])
        p = jnp.exp2(logits - reach).astype(BF16).reshape(N_HEADS, DSA_TQ, DSA_TK) * keep[None]
        return seen, acc + _dot(p.reshape(N_HEADS * DSA_TQ, DSA_TK), va_ref[0, pl.ds(k0, DSA_TK), :])

    def online(j, carry, causal):
        seen, logits, va_t = selected_logits(j, carry[0], causal)
        return (seen,) + _softmax_step(logits, carry[1], carry[2], va_t)

    def sweep(step, *stats):
        carry = stats + (jnp.zeros((N_HEADS * DSA_TQ, LANES), F32),)
        carry = lax.fori_loop(0, last, functools.partial(step, causal=False), carry)
        return step(last, carry, causal=True)[-1]

    acc = lax.cond(2.0 * jnp.max(reach) <= SAFE_LOG2_SPAN,
                   functools.partial(sweep, capped, jnp.zeros((1, DSA_TQ), F32)),
                   functools.partial(sweep, online, jnp.zeros((DSA_TQ, 1), F32),
                                     jnp.full((N_HEADS * DSA_TQ, 1), NEG, F32)))
    o_ref[0] = jnp.concatenate(
        _normalise_heads([acc[h * DSA_TQ:(h + 1) * DSA_TQ] for h in range(N_HEADS)]),
        axis=1).astype(o_ref.dtype)


def _dsa_attention(dq, dkk, dva, iq, ikk, small_t, key_norms, k_top):
    bsz, t_work, _ = dq.shape
    tri = jnp.asarray(np.triu(np.ones((LANES, LANES), np.float32)), dtype=BF16)
    tile = lambda width: pl.BlockSpec((1, DSA_TQ, width), lambda b, i: (b, i, 0))
    full = pl.BlockSpec((1, t_work, LANES), lambda b, i: (b, 0, 0), pipeline_mode=pl.Buffered(1))
    return pl.pallas_call(
        functools.partial(_dsa_kernel, k_top),
        grid=(bsz, t_work // DSA_TQ),
        in_specs=[tile(256), full, full, tile(256), full,
                  pl.BlockSpec((1, 8, DSA_TQ), lambda b, i: (b, 0, i)),
                  pl.BlockSpec((LANES, LANES), lambda b, i: (0, 0)),
                  pl.BlockSpec((1,) + key_norms.shape[1:], lambda b, i: (b, 0, 0, 0))],
        out_specs=tile(256),
        out_shape=jax.ShapeDtypeStruct((bsz, t_work, 256), BF16),
        scratch_shapes=[pltpu.VMEM((t_work, DSA_TQ), F32)],
        compiler_params=pltpu.CompilerParams(
            dimension_semantics=("parallel", "parallel"), vmem_limit_bytes=VMEM_LIMIT),
        name="dsa_attention",
    )(dq, dkk, dva, iq, ikk, small_t, tri, key_norms)


FOX_TQ = 512
FOX_TK = 512
CUM_T = 256


def _fox_prep_kernel(x_ref, tri_ref, place_ref, k_ref, xt_ref, kb_ref, carry_ref):
    @pl.when(pl.program_id(1) == 0)
    def _():
        carry_ref[...] = jnp.zeros_like(carry_ref)

    tri = tri_ref[...]
    x = x_ref[0]
    h1, h2, h3 = _split3(x)
    c = _dot(tri, h1) + _dot(tri, h2) + _dot(tri, h3) + carry_ref[0:1, :]
    carry_ref[...] = jnp.broadcast_to(c[CUM_T - 1:CUM_T, :], carry_ref.shape)
    xt_ref[0] = x.T[0:8, :]
    terms = _split3(c * -LOG2E)
    bias = sum(_dot(terms[t], place_ref[t]) for t in range(BIAS_TERMS))
    lane = lax.broadcasted_iota(jnp.int32, bias.shape, 1) % LANES
    kb_ref[0] = jnp.where((lane >= HEAD_DIM) & (lane < HEAD_DIM + BIAS_TERMS),
                          bias.astype(BF16), k_ref[0])


def _fox_prep(small, fk):
    bsz, t_work, width = fk.shape
    tri = jnp.asarray(np.tril(np.ones((CUM_T, CUM_T), np.float32)), dtype=BF16)
    place = np.zeros((BIAS_TERMS, LANES, width), np.float32)
    for t in range(BIAS_TERMS):
        for h in range(N_HEADS):
            place[t, SM_FF + h, h * LANES + HEAD_DIM + t] = 1.0
    rows = lambda w: pl.BlockSpec((1, CUM_T, w), lambda b, j: (b, j, 0))
    return pl.pallas_call(
        _fox_prep_kernel,
        grid=(bsz, t_work // CUM_T),
        in_specs=[rows(LANES), pl.BlockSpec((CUM_T, CUM_T), lambda b, j: (0, 0)),
                  pl.BlockSpec((BIAS_TERMS, LANES, width), lambda b, j: (0, 0, 0)), rows(width)],
        out_specs=[pl.BlockSpec((1, 8, CUM_T), lambda b, j: (b, 0, j)), rows(width)],
        out_shape=[jax.ShapeDtypeStruct((bsz, 8, t_work), F32),
                   jax.ShapeDtypeStruct(fk.shape, BF16)],
        scratch_shapes=[pltpu.VMEM((8, LANES), F32)],
        compiler_params=pltpu.CompilerParams(dimension_semantics=("parallel", "arbitrary")),
        name="fox_prep",
    )(small, tri, jnp.asarray(place, dtype=BF16), fk)


def _logit_reach(q, kmax_sq):
    qf = q.astype(F32)
    return jnp.sqrt(jnp.sum(qf * qf, axis=-1, keepdims=True) * kmax_sq) * NORM_MARGIN


def _fox_kernel(q_ref, k_ref, v_ref, kn_ref, o_ref):
    i = pl.program_id(1)
    q0 = pl.multiple_of(i * FOX_TQ, FOX_TQ)
    n_full = q0 // FOX_TK
    row = q0 + lax.broadcasted_iota(jnp.int32, (FOX_TQ, FOX_TK), 0)
    col = n_full * FOX_TK + lax.broadcasted_iota(jnp.int32, (FOX_TQ, FOX_TK), 1)
    heads = [slice(h * LANES, (h + 1) * LANES) for h in range(N_HEADS)]

    lane = lax.broadcasted_iota(jnp.int32, (FOX_TQ, LANES), 1)
    bias_lanes = (lane >= HEAD_DIM) & (lane < HEAD_DIM + BIAS_TERMS)
    kmax_sq = jnp.max(kn_ref[0], axis=0)
    caps, span = [], jnp.float32(0.0)
    for h in range(N_HEADS):
        q_h = q_ref[0, :, heads[h]]
        reach = _logit_reach(jnp.where(lane < HEAD_DIM, q_h, jnp.zeros_like(q_h)),
                             kmax_sq[KN_FOX + h:KN_FOX + h + 1, 0:1])
        own = k_ref[0, pl.ds(q0, FOX_TQ), heads[h]].astype(F32)
        caps.append(reach + jnp.sum(jnp.where(bias_lanes, own, 0.0), axis=-1, keepdims=True))
        span = jnp.maximum(span, 2.0 * jnp.max(reach))

    def logits(j, h, diag):
        k0 = pl.multiple_of(j * FOX_TK, FOX_TK)
        s = _dot_nt(q_ref[0, :, heads[h]], k_ref[0, pl.ds(k0, FOX_TK), heads[h]])
        return jnp.where(col <= row, s, NEG) if diag else s

    def values(j, h):
        return v_ref[0, pl.ds(pl.multiple_of(j * FOX_TK, FOX_TK), FOX_TK), heads[h]]

    def capped(j, accs, diag):
        return tuple(accs[h] + _dot(jnp.exp2(logits(j, h, diag) - caps[h]).astype(BF16), values(j, h))
                     for h in range(N_HEADS))

    def online(j, carry, diag):
        return tuple(_softmax_step(logits(j, h, diag), *carry[h], values(j, h))
                     for h in range(N_HEADS))

    def capped_sweep():
        accs = tuple(jnp.zeros((FOX_TQ, LANES), F32) for _ in range(N_HEADS))
        accs = lax.fori_loop(0, n_full, functools.partial(capped, diag=False), accs)
        return capped(n_full, accs, diag=True)

    def online_sweep():
        carry = tuple((jnp.full((FOX_TQ, 1), NEG, F32), jnp.zeros((FOX_TQ, LANES), F32))
                      for _ in range(N_HEADS))
        carry = lax.fori_loop(0, n_full, functools.partial(online, diag=False), carry)
        return tuple(acc for _, acc in online(n_full, carry, diag=True))

    accs = lax.cond(span <= SAFE_LOG2_SPAN, capped_sweep, online_sweep)
    o_ref[0] = jnp.concatenate(_normalise_heads(list(accs)), axis=1).astype(o_ref.dtype)


def _fox_attention(fqa, fkb, fva, key_norms):
    bsz, t_work, width = fqa.shape
    full = pl.BlockSpec((1, t_work, width), lambda b, i: (b, 0, 0), pipeline_mode=pl.Buffered(1))
    return pl.pallas_call(
        _fox_kernel,
        grid=(bsz, t_work // FOX_TQ),
        in_specs=[pl.BlockSpec((1, FOX_TQ, width), lambda b, i: (b, i, 0)), full, full,
                  pl.BlockSpec((1,) + key_norms.shape[1:], lambda b, i: (b, 0, 0, 0))],
        out_specs=pl.BlockSpec((1, FOX_TQ, 2 * LANES), lambda b, i: (b, i, 0)),
        out_shape=jax.ShapeDtypeStruct((bsz, t_work, 2 * LANES), BF16),
        compiler_params=pltpu.CompilerParams(
            dimension_semantics=("parallel", "parallel"), vmem_limit_bytes=VMEM_LIMIT),
        name="fox_attention",
    )(fqa, fkb, fva, key_norms)


GLA_NSUB = GLA_CHUNK // GLA_SUB


def _gla_kernel(q_ref, k_ref, v_ref, g_ref, la_ref, tri_ref, e_ref, gn_ref, o_ref, st_ref):
    @pl.when(pl.program_id(1) == 0)
    def _():
        st_ref[...] = jnp.zeros_like(st_ref)

    tri = tri_ref[...]
    emat = e_ref[...]
    lane = lax.broadcasted_iota(jnp.int32, (GLA_CHUNK, LANES), 1)
    rowblk = lax.broadcasted_iota(jnp.int32, (GLA_CHUNK, LANES), 0) // GLA_SUB
    tblk = lax.broadcasted_iota(jnp.int32, (GLA_CHUNK, GLA_CHUNK), 0) // GLA_SUB
    sblk = lax.broadcasted_iota(jnp.int32, (GLA_CHUNK, GLA_CHUNK), 1) // GLA_SUB
    trow = lax.broadcasted_iota(jnp.int32, (GLA_SUB, 256), 0)

    def chunk(c, carry):
        r0 = pl.multiple_of(c * GLA_CHUNK, GLA_CHUNK)
        rows = pl.ds(r0, GLA_CHUNK)
        h1, h2, h3 = _split3(la_ref[0, rows, :])
        b = _dot(tri, h1) + _dot(tri, h2) + _dot(tri, h3)
        q = q_ref[0, rows, :]
        k = k_ref[0, rows, :]
        v = v_ref[0, rows, :]
        b_last = b[GLA_CHUNK - 1:GLA_CHUNK, :]
        qd = q * jnp.exp2(b)
        kd = (k * jnp.exp2(b_last - b)).astype(BF16)
        starts = [jnp.zeros((1, 256), F32)] + [b[GLA_SUB * i - 1:GLA_SUB * i, :]
                                               for i in range(1, GLA_NSUB)]
        bsel = jnp.concatenate([jnp.broadcast_to(s, (GLA_SUB, 256)) for s in starts], axis=0)
        qn = q * jnp.exp2(b - bsel)

        diag = []
        for i in range(GLA_NSUB):
            rs = slice(GLA_SUB * i, GLA_SUB * (i + 1))
            b_i, q_i, k_i = b[rs], q[rs], k[rs]
            v_i = v[rs].astype(F32)
            ps = []
            for s in range(GLA_SUB):
                d = jnp.exp2(jnp.minimum(b_i - b_i[s:s + 1], 0.0))
                ps.append(jnp.where(trow >= s, q_i * d * k_i[s:s + 1], 0.0).astype(BF16))
            r = _dot(jnp.concatenate(ps, axis=0), emat)
            od = r[0:GLA_SUB] * v_i[0:1]
            for s in range(1, GLA_SUB):
                od = od + r[GLA_SUB * s:GLA_SUB * (s + 1)] * v_i[s:s + 1]
            diag.append(od)
        o_diag = jnp.concatenate(diag, axis=0)

        for slab in range(2):
            ls = slice(slab * LANES, (slab + 1) * LANES)
            qn_s, k_s, b_s = qn[:, ls], k[:, ls], b[:, ls]
            khat = jnp.concatenate(
                [(k_s * jnp.exp2(jnp.minimum(starts[i][:, ls] - b_s, 0.0))).astype(BF16)
                 for i in range(1, GLA_NSUB)], axis=1)
            for half in range(2):
                head = 2 * slab + half
                hs = slice(head * GLA_DV, (head + 1) * GLA_DV)
                in_head = (lane < HEAD_DIM) if half == 0 else (lane >= HEAD_DIM)
                qm = jnp.where(in_head, qn_s, 0.0)
                qhat = jnp.concatenate([jnp.where(rowblk == i, qm, 0.0).astype(BF16)
                                        for i in range(1, GLA_NSUB)], axis=1)
                att = jnp.where(sblk < tblk, _dot_nt(qhat, khat), 0.0)
                v_h = v[:, hs]
                st = st_ref[head]
                o = (_dot_nt(jnp.where(in_head, qd[:, ls], 0.0).astype(BF16), st.astype(BF16))
                     + _dot(att.astype(BF16), v_h) + o_diag[:, hs])
                st_ref[head] = st * jnp.exp2(b_last[:, ls]) + _dot_tn(v_h, kd[:, ls])
                y = o * lax.rsqrt(jnp.mean(o * o, axis=-1, keepdims=True) + EPS) * gn_ref[:, hs]
                o_ref[0, rows, hs] = (y * _silu(g_ref[0, rows, hs])).astype(o_ref.dtype)
        return carry

    lax.fori_loop(0, q_ref.shape[1] // GLA_CHUNK, chunk, 0)


def _gla(gq, gk, gv, gg, la, gain):
    bsz, t_work, _ = gq.shape
    tg = _row_tile(t_work)
    tri = jnp.asarray(np.tril(np.ones((GLA_CHUNK, GLA_CHUNK), np.float32)), dtype=BF16)
    emat = jnp.asarray(
        (np.arange(256)[:, None] // HEAD_DIM == np.arange(512)[None, :] // GLA_DV).astype(np.float32),
        dtype=BF16)
    rows = lambda width: pl.BlockSpec((1, tg, width), lambda b, j: (b, j, 0))
    const = lambda shape: pl.BlockSpec(shape, lambda b, j: (0,) * len(shape))
    return pl.pallas_call(
        _gla_kernel,
        grid=(bsz, t_work // tg),
        in_specs=[rows(256), rows(256), rows(512), rows(512), rows(256),
                  const((GLA_CHUNK, GLA_CHUNK)), const((256, 512)), const((1, 512))],
        out_specs=rows(512),
        out_shape=jax.ShapeDtypeStruct((bsz, t_work, 512), BF16),
        scratch_shapes=[pltpu.VMEM((N_HEADS, GLA_DV, LANES), F32)],
        compiler_params=pltpu.CompilerParams(
            dimension_semantics=("parallel", "arbitrary"), vmem_limit_bytes=VMEM_LIMIT),
        name="gla",
    )(gq, gk, gv, gg, la, tri, emat, gain)


FFN_CHUNK = 256


def _ffn_kernel(h_ref, oa_ref, ob_ref, oc_ref, wo_ref, fn_ref, wgu_ref, wd_ref, out_ref):
    h1 = (h_ref[0] + _dot(oa_ref[0], wo_ref[0:256, :]) + _dot(ob_ref[0], wo_ref[256:512, :])
          + _dot(oc_ref[0], wo_ref[512:1024, :]))
    ms = jnp.mean(h1 * h1, axis=-1, keepdims=True)
    f = (h1 * lax.rsqrt(ms + EPS) * fn_ref[...]).astype(BF16)
    out_ref[0] = h1
    for c in range(0, D_FF, FFN_CHUNK):
        gate = _dot(f, wgu_ref[:, c:c + FFN_CHUNK])
        up = _dot(f, wgu_ref[:, D_FF + c:D_FF + c + FFN_CHUNK])
        out_ref[0] += _dot((_silu(gate) * up).astype(BF16), wd_ref[c:c + FFN_CHUNK, :])


def _outproj_ffn(h, oa, ob, oc, wo, fn, wgu, wd):
    bsz, t_work, _ = h.shape
    tm = _row_tile(t_work)
    rows = lambda width: pl.BlockSpec((1, tm, width), lambda b, j: (b, j, 0))
    const = lambda shape: pl.BlockSpec(shape, lambda b, j: (0,) * len(shape),
                                       pipeline_mode=pl.Buffered(1))
    return pl.pallas_call(
        _ffn_kernel,
        grid=(bsz, t_work // tm),
        in_specs=[rows(D_MODEL), rows(256), rows(256), rows(512), const((D_MODEL, D_MODEL)),
                  const((1, D_MODEL)), const((D_MODEL, 2 * D_FF)), const((D_FF, D_MODEL))],
        out_specs=rows(D_MODEL),
        out_shape=jax.ShapeDtypeStruct(h.shape, F32),
        compiler_params=pltpu.CompilerParams(
            dimension_semantics=("parallel", "parallel"), vmem_limit_bytes=VMEM_LIMIT),
        name="outproj_ffn",
    )(h, oa, ob, oc, wo, fn, wgu, wd)


def kernel(x, meta_tokens, attn_norm, w_in, dsa_q_norm, dsa_k_norm, fox_q_norm, fox_k_norm,
           fox_f_bias, gla_gate_w2, gla_gate_b, gla_out_norm, w_out, ffn_norm, w_gate_up, w_down):
    bsz, seq, _ = x.shape
    n_tok = N_META + seq
    t_work = _work_len(n_tok)
    meta = jnp.broadcast_to(meta_tokens[None].astype(x.dtype), (bsz, N_META, D_MODEL))
    h = jnp.concatenate([meta, x, jnp.zeros((bsz, t_work - n_tok, D_MODEL), x.dtype)], axis=1)
    cos_t, sin_t = _rope_tables(t_work)
    g256 = _group_matrix()
    k_top = min(TOPK_MAX, seq // 4)
    for l in range(w_in.shape[0]):
        an, w_perm, vecs, w2_pad, sm_bias = _prep_layer_params(
            l, attn_norm, w_in, dsa_q_norm, dsa_k_norm, fox_q_norm, fox_k_norm, fox_f_bias,
            gla_gate_w2, gla_gate_b)
        (dq, dkk, dva, iq, ikk, fq, fk, fva, gq, gk, gv, gg, la, small, key_norms) = _inproj(
            h, an, w_perm, cos_t, sin_t, g256, vecs, w2_pad, sm_bias)
        small_t, fkb = _fox_prep(small, fk)
        oa = _dsa_attention(dq, dkk, dva, iq, ikk, small_t, key_norms, k_top)
        ob = _fox_attention(fq, fkb, fva, key_norms)
        oc = _gla(gq, gk, gv, gg, la, jnp.tile(gla_out_norm[l], N_HEADS)[None, :])
        h = _outproj_ffn(h, oa, ob, oc, w_out[l].astype(BF16), ffn_norm[l][None, :],
                         w_gate_up[l].astype(BF16), w_down[l].astype(BF16))
    return h[:, N_META:n_tok]
```

```python
import functools

import numpy as np
import jax
import jax.numpy as jnp
from jax import lax
from jax.experimental import pallas as pl
from jax.experimental.pallas import tpu as pltpu

F32 = jnp.float32
BF16 = jnp.bfloat16

D_MODEL = 1024
HEAD_DIM = 64
N_META = 16
ROPE_THETA = 500000.0
ROPE_DIM = HEAD_DIM // 4
ROPE_HALF = ROPE_DIM // 2
NEG = -1e30
EPS = 1e-6

N_HEADS = 4
TOPK_MAX = 256
GLA_DV = 128
GLA_RANK = 16
GLA_TAU = 16.0
GLA_CHUNK = 64
GLA_SUB = 16
D_FF = 2816

LOG2E = 1.4426950408889634
BIAS_TERMS = 3
KN_DSA, KN_FOX = 0, 1
NORM_MARGIN = 1.01
SAFE_LOG2_SPAN = 100.0
LANES = 128
SEQ_ALIGN = 512
VMEM_LIMIT = 56 * 1024 * 1024

_SLABS = (("dq", 256), ("dkk", 128), ("dva", 128), ("iq", 256), ("ikk", 128),
          ("fq", 256), ("fk", 256), ("fv", 256), ("gq", 256), ("gk", 256),
          ("gv", 512), ("gg", 512), ("small", 128))
_SLAB_OFF = {}
_off = 0
for _name, _width in _SLABS:
    _SLAB_OFF[_name] = (_off, _width)
    _off += _width
N_PROJ = _off
SM_IW, SM_FF, SM_GLR = 0, 4, 8


def _work_len(n_tok):
    return -(-n_tok // SEQ_ALIGN) * SEQ_ALIGN


def _row_tile(t_work):
    for cand in (768, 640, 512):
        if t_work % cand == 0:
            return cand
    raise ValueError(f"unsupported working length {t_work}")


def _dot(a, b):
    return jnp.dot(a, b, preferred_element_type=F32)


def _dot_nt(a, b):
    return lax.dot_general(a, b, (((1,), (1,)), ((), ())), preferred_element_type=F32)


def _dot_tn(a, b):
    return lax.dot_general(a, b, (((0,), (0,)), ((), ())), preferred_element_type=F32)


def _split3(x):
    h1 = x.astype(BF16)
    r1 = x - h1.astype(F32)
    h2 = r1.astype(BF16)
    h3 = (r1 - h2.astype(F32)).astype(BF16)
    return h1, h2, h3


def _log_sigmoid(x):
    return jnp.minimum(x, 0.0) - jnp.log1p(jnp.exp(-jnp.abs(x)))


def _silu(x):
    return x / (1.0 + jnp.exp(-x))


def _group_rms(y, gmat, gain):
    yy = y * y
    hi = yy.astype(BF16)
    lo = (yy - hi.astype(F32)).astype(BF16)
    ss = _dot(hi, gmat) + _dot(lo, gmat)
    return y * lax.rsqrt(ss * (1.0 / HEAD_DIM) + EPS) * gain


def _rope(y, cos, sin):
    width = y.shape[-1]
    lane = lax.broadcasted_iota(jnp.int32, y.shape, 1) % HEAD_DIM
    upper = pltpu.roll(y, width - ROPE_HALF, axis=1)
    lower = pltpu.roll(y, ROPE_HALF, axis=1)
    partner = jnp.where(lane < ROPE_HALF, upper, lower)
    return y * cos + partner * sin


def _inproj_kernel(x_ref, an_ref, w_ref, cos_ref, sin_ref, g256_ref, vec_ref, w2_ref, sm_ref,
                   dq_ref, dkk_ref, dva_ref, iq_ref, ikk_ref, fq_ref, fk_ref, fva_ref,
                   gq_ref, gk_ref, gv_ref, gg_ref, la_ref, small_ref, kn_ref):
    x = x_ref[0]
    ms = jnp.mean(x * x, axis=-1, keepdims=True)
    a = (x * lax.rsqrt(ms + EPS) * an_ref[...]).astype(BF16)

    def proj(name):
        off, width = _SLAB_OFF[name]
        return _dot(a, w_ref[:, off:off + width])

    def ones_in_upper_half(y):
        lane = lax.broadcasted_iota(jnp.int32, y.shape, 1) % LANES
        return jnp.where(lane < HEAD_DIM, y, 1.0)

    cos = cos_ref[...]
    sin = sin_ref[...]
    g256 = g256_ref[...]
    g128 = g256[:LANES, :LANES]
    dqn, dkn, fqn, fkn = vec_ref[0:1, :], vec_ref[1:2, :LANES], vec_ref[2:3, :], vec_ref[3:4, :]
    scale = HEAD_DIM ** -0.5

    def max_sq_norm(k):
        kf = k.astype(F32)
        return jnp.max(_dot((kf * kf).astype(BF16), g128), axis=0, keepdims=True)

    dq_ref[0] = (_rope(_group_rms(proj("dq"), g256, dqn), cos, sin) * (scale * LOG2E)).astype(BF16)
    dkk = _rope(_group_rms(proj("dkk"), g128, dkn), cos[:, :LANES], sin[:, :LANES]).astype(BF16)
    dkk_ref[0] = dkk
    key_norms = [max_sq_norm(dkk)]
    dva_ref[0] = ones_in_upper_half(proj("dva")).astype(BF16)
    iq_ref[0] = (_rope(proj("iq"), cos, sin) * scale).astype(BF16)
    ikk_ref[0] = _rope(proj("ikk"), cos[:, :LANES], sin[:, :LANES]).astype(BF16)
    fq, fk, fv = proj("fq"), proj("fk"), proj("fv")
    lane = lax.broadcasted_iota(jnp.int32, (fq.shape[0], LANES), 1)
    in_head = lane < HEAD_DIM
    q_pad = jnp.where(lane < HEAD_DIM + BIAS_TERMS, 1.0, 0.0)
    for pair in range(N_HEADS // 2):
        ps = slice(pair * LANES, (pair + 1) * LANES)
        q_pair = _group_rms(fq[:, ps], g128, fqn[:, :LANES]) * (scale * LOG2E)
        k_pair = _group_rms(fk[:, ps], g128, fkn[:, :LANES])
        v_pair = fv[:, ps]
        for odd in range(2):
            hs = slice((2 * pair + odd) * LANES, (2 * pair + odd + 1) * LANES)
            down = (lambda y: pltpu.roll(y, HEAD_DIM, axis=1)) if odd else (lambda y: y)
            fq_ref[0, :, hs] = jnp.where(in_head, down(q_pair), q_pad).astype(BF16)
            k_h = jnp.where(in_head, down(k_pair), 0.0).astype(BF16)
            fk_ref[0, :, hs] = k_h
            key_norms.append(max_sq_norm(k_h))
            fva_ref[0, :, hs] = jnp.where(in_head, down(v_pair), 1.0).astype(BF16)
    kn_ref[0, 0] = jnp.concatenate(key_norms + [jnp.zeros((8 - len(key_norms), LANES), F32)], axis=0)
    gq_ref[0] = proj("gq") * scale
    gk_ref[0] = proj("gk")
    gv_ref[0] = proj("gv").astype(BF16)
    gg_ref[0] = proj("gg")

    small = proj("small")
    lane = lax.broadcasted_iota(jnp.int32, small.shape, 1)
    small_ref[0] = jnp.where(lane < SM_FF, small * (N_HEADS ** -0.5),
                             _log_sigmoid(small + sm_ref[0:1, :]))
    gate = _dot(small.astype(BF16), w2_ref[...]) + vec_ref[4:5, :]
    la_ref[0] = _log_sigmoid(gate) * (LOG2E / GLA_TAU)


def _inproj(h, an, w_perm, cos_t, sin_t, g256, vecs, w2_pad, sm_bias):
    bsz, t_work, _ = h.shape
    tm = _row_tile(t_work)
    grid = (bsz, t_work // tm)

    def rows(width, dtype):
        return (jax.ShapeDtypeStruct((bsz, t_work, width), dtype),
                pl.BlockSpec((1, tm, width), lambda b, j: (b, j, 0)))

    outs = [rows(256, BF16), rows(128, BF16), rows(128, BF16), rows(256, BF16), rows(128, BF16),
            rows(512, BF16), rows(512, BF16), rows(512, BF16), rows(256, F32), rows(256, F32),
            rows(512, BF16), rows(512, F32), rows(256, F32), rows(128, F32),
            (jax.ShapeDtypeStruct((bsz, t_work // tm, 8, LANES), F32),
             pl.BlockSpec((1, 1, 8, LANES), lambda b, j: (b, j, 0, 0)))]
    const = lambda shape: pl.BlockSpec(shape, lambda b, j: (0,) * len(shape))
    return pl.pallas_call(
        _inproj_kernel,
        grid=grid,
        in_specs=[pl.BlockSpec((1, tm, D_MODEL), lambda b, j: (b, j, 0)),
                  const((1, D_MODEL)), const((D_MODEL, N_PROJ)),
                  pl.BlockSpec((tm, 256), lambda b, j: (j, 0)),
                  pl.BlockSpec((tm, 256), lambda b, j: (j, 0)),
                  const((256, 256)), const((8, 256)), const((LANES, 256)), const((8, LANES))],
        out_specs=[o[1] for o in outs],
        out_shape=[o[0] for o in outs],
        compiler_params=pltpu.CompilerParams(
            dimension_semantics=("parallel", "parallel"), vmem_limit_bytes=VMEM_LIMIT),
        name="inproj",
    )(h, an, w_perm, cos_t, sin_t, g256, vecs, w2_pad, sm_bias)


def _prep_layer_params(l, attn_norm, w_in, dsa_q_norm, dsa_k_norm, fox_q_norm, fox_k_norm,
                       fox_f_bias, gla_gate_w2, gla_gate_b):
    w = w_in[l]
    splits = np.cumsum([256, 64, 64, 256, 4, 64, 256, 256, 256, 4, 256, 256, 512, 512, 16])[:-1]
    (dq, dk, dv, iq, iw, ik, fq, fk, fv, ff, gq, gk, gv, gg, glr) = jnp.split(w, splits, axis=1)
    small = jnp.concatenate(
        [iw, ff, glr, jnp.zeros((D_MODEL, LANES - 4 - 4 - GLA_RANK), w.dtype)], axis=1)
    z64 = jnp.zeros((D_MODEL, HEAD_DIM), w.dtype)
    w_perm = jnp.concatenate([dq, dk, dk, dv, z64, iq, ik, ik, fq, fk, fv, gq, gk, gv, gg, small],
                             axis=1).astype(BF16)
    tile4 = lambda g: jnp.tile(g, N_HEADS)
    vecs = jnp.zeros((8, 256), F32)
    vecs = vecs.at[0].set(tile4(dsa_q_norm[l])).at[1].set(tile4(dsa_k_norm[l]))
    vecs = vecs.at[2].set(tile4(fox_q_norm[l])).at[3].set(tile4(fox_k_norm[l]))
    vecs = vecs.at[4].set(gla_gate_b[l])
    w2_pad = jnp.zeros((LANES, 256), F32).at[SM_GLR:SM_GLR + GLA_RANK].set(gla_gate_w2[l]).astype(BF16)
    sm_bias = jnp.zeros((8, LANES), F32).at[0, SM_FF:SM_FF + N_HEADS].set(fox_f_bias[l])
    return attn_norm[l][None, :], w_perm, vecs, w2_pad, sm_bias


def _rope_tables(t_work):
    inv = jnp.power(ROPE_THETA, -jnp.arange(ROPE_HALF, dtype=F32) * 2.0 / ROPE_DIM)
    ang = jnp.arange(t_work).astype(F32)[:, None] * inv[None, :]
    cos, sin = jnp.cos(ang), jnp.sin(ang)
    rest = HEAD_DIM - ROPE_DIM
    cos64 = jnp.concatenate([cos, cos, jnp.ones((t_work, rest), F32)], axis=1)
    sin64 = jnp.concatenate([-sin, sin, jnp.zeros((t_work, rest), F32)], axis=1)
    return jnp.tile(cos64, (1, N_HEADS)), jnp.tile(sin64, (1, N_HEADS))


def _group_matrix():
    idx = np.arange(256) // HEAD_DIM
    return jnp.asarray((idx[:, None] == idx[None, :]).astype(np.float32), dtype=BF16)


DSA_TQ = 256
DSA_TK = 512
KEY_NEG_INF = -2139095041
KEY_NEG_ZERO = -1
KEY_NEG_MIN_NORMAL = -8388609
NEG_MIN_NORMAL = -1.1754943508222875e-38
SEARCH_FEW = 4.0
SEARCH_MANY = 16.0
UNCHECKED_PROBES = 12
GUIDED_PROBES = 64
MAX_PROBES = GUIDED_PROBES + 40
COUNT_ROWS = 128


def _key_to_f32(key):
    bits = key ^ ((key >> 31) & 0x7FFFFFFF)
    return lax.bitcast_convert_type(bits, F32)


def _f32_to_key(value):
    bits = lax.bitcast_convert_type(value, jnp.int32)
    return bits ^ ((bits >> 31) & 0x7FFFFFFF)


def _head_lane_mask(shape, head):
    lane = lax.broadcasted_iota(jnp.int32, shape, 1)
    return (lane < HEAD_DIM) if head % 2 == 0 else (lane >= HEAD_DIM)


def _masked_heads(slabs):
    return [jnp.where(_head_lane_mask(slabs[h // 2].shape, h), slabs[h // 2],
                      jnp.zeros_like(slabs[h // 2])) for h in range(N_HEADS)]


def _softmax_step(s, m, acc, v_aug):
    m_new = jnp.maximum(m, jnp.max(s, axis=-1, keepdims=True))
    p = jnp.exp2(s - m_new)
    return m_new, acc * jnp.exp2(m - m_new) + _dot(p.astype(BF16), v_aug)


def _normalise_heads(accs):
    outs = [acc / pltpu.roll(acc, HEAD_DIM, axis=1) for acc in accs]
    lane = lax.broadcasted_iota(jnp.int32, outs[0].shape, 1)
    return [jnp.where(lane < HEAD_DIM, outs[2 * p], pltpu.roll(outs[2 * p + 1], HEAD_DIM, axis=1))
            for p in range(2)]


def _dsa_kernel(k_top, n_tok, *refs):
    o_ref = refs[-2]
    real = pl.program_id(1) * DSA_TQ < n_tok
    pl.when(real)(functools.partial(_dsa_tile, k_top, *refs))

    @pl.when(jnp.logical_not(real))
    def _():
        o_ref[...] = jnp.zeros_like(o_ref)


def _dsa_tile(k_top, dq_ref, kk_ref, va_ref, iq_ref, ik_ref, wt_ref, tri_ref, kn_ref, o_ref,
              s_ref):
    i = pl.program_id(1)
    q0 = i * DSA_TQ
    n_tiles = (q0 + DSA_TQ + DSA_TK - 1) // DSA_TK
    last = n_tiles - 1
    key = lax.broadcasted_iota(jnp.int32, (DSA_TK, DSA_TQ), 0)
    qry = q0 + lax.broadcasted_iota(jnp.int32, (DSA_TK, DSA_TQ), 1)
    head = lambda x, h: x[:, h * DSA_TQ:(h + 1) * DSA_TQ]

    iq_all = jnp.concatenate(
        _masked_heads([iq_ref[0, :, 0:LANES], iq_ref[0, :, LANES:2 * LANES]]), axis=0)
    w_h = [wt_ref[0, SM_IW + h:SM_IW + h + 1, :] for h in range(N_HEADS)]

    def score_tile(j, carry, causal):
        k0 = pl.multiple_of(j * DSA_TK, DSA_TK)
        dots = jnp.maximum(_dot_nt(ik_ref[0, pl.ds(k0, DSA_TK), :], iq_all), 0.0)
        s = w_h[0] * head(dots, 0)
        for h in range(1, N_HEADS):
            s = s + w_h[h] * head(dots, h)
        if causal:
            s = jnp.where(k0 + key <= qry, s, NEG)
        s_ref[pl.ds(k0, DSA_TK), :] = s
        row_max, above_zero, above_neg = carry
        for part in range(DSA_TK // COUNT_ROWS):
            chunk = s[part * COUNT_ROWS:(part + 1) * COUNT_ROWS]
            above_zero = above_zero + jnp.where(chunk > 0.0, 1.0, 0.0)
            above_neg = above_neg + jnp.where(chunk > NEG_MIN_NORMAL, 1.0, 0.0)
        return jnp.maximum(row_max, jnp.max(s, axis=0, keepdims=True)), above_zero, above_neg

    zeros = jnp.zeros((COUNT_ROWS, DSA_TQ), F32)
    row_max, above_zero, above_neg = score_tile(last, lax.fori_loop(
        0, last, functools.partial(score_tile, causal=False),
        (jnp.full((1, DSA_TQ), NEG, F32), zeros, zeros)), causal=True)

    kf = jnp.float32(k_top)

    def count_above(t):
        tb = jnp.broadcast_to(t, (COUNT_ROWS, DSA_TQ))

        def body(j, acc, span):
            k0 = pl.multiple_of(j * span, span)
            for part in range(span // COUNT_ROWS):
                s = s_ref[pl.ds(k0 + part * COUNT_ROWS, COUNT_ROWS), :]
                acc = acc + jnp.where(s > tb, 1.0, 0.0)
            return acc

        acc = lax.fori_loop(0, n_tiles // 2, functools.partial(body, span=2 * DSA_TK),
                            jnp.zeros((COUNT_ROWS, DSA_TQ), F32))
        acc = lax.fori_loop(2 * (n_tiles // 2), n_tiles, functools.partial(body, span=DSA_TK), acc)
        return jnp.sum(acc, axis=0, keepdims=True)

    def midpoint(lo, hi):
        return (lo >> 1) + (hi >> 1) + (lo & hi & 1)

    def converged(lo, hi):
        return (midpoint(lo, hi) == lo) | ((lo >= KEY_NEG_MIN_NORMAL) & (hi <= 0))

    col1 = lambda value, dtype: jnp.full((1, DSA_TQ), value, dtype)

    def absorb(state, probe, c):
        lo, hi, clo, chi, wlo, whi, side = state
        live = probe != lo
        up = live & (c >= kf)
        down = live & (c <= kf)
        wlo = jnp.where(down & (side < 0), wlo * 0.5, jnp.where(up, 1.0, wlo))
        whi = jnp.where(up & (side > 0), whi * 0.5, jnp.where(down, 1.0, whi))
        side = jnp.where(up, 1, jnp.where(down, -1, side))
        lo, clo = jnp.where(up, probe, lo), jnp.where(up, c, clo)
        hi, chi = jnp.where(down, probe, hi), jnp.where(down, c, chi)
        return lo, hi, clo, chi, wlo, whi, side

    def probe_once(it, state):
        lo, hi, clo, chi, wlo, whi, side = state
        f_lo, f_hi = _key_to_f32(lo), _key_to_f32(hi)
        target = kf - 0.5
        log_count = lambda c: jnp.log2(jnp.maximum(c, 0.25))
        many = clo - chi > SEARCH_MANY
        g_lo = jnp.where(many, log_count(clo) - np.log2(k_top - 0.5), clo - target) * wlo
        g_hi = jnp.where(many, np.log2(k_top - 0.5) - log_count(chi), target - chi) * whi
        halve = (clo - chi <= SEARCH_FEW) | (col1(it % 8, jnp.int32) == 7)
        guess = _f32_to_key(f_lo + (f_hi - f_lo) * jnp.where(halve, 0.5, g_lo / (g_lo + g_hi)))
        guided = col1(it, jnp.int32) < GUIDED_PROBES
        probe = jnp.where((guess > lo) & (guess < hi) & guided, guess, midpoint(lo, hi))
        probe = jnp.where(converged(lo, hi), lo, probe)
        return absorb(state, probe, count_above(_key_to_f32(probe)))

    def search_cond(carry):
        it, pending = carry[0], carry[1]
        return (pending > 0) & (it < MAX_PROBES)

    def search_body(carry):
        it, state = carry[0], carry[2]
        state = probe_once(it + 1, probe_once(it, state))
        return it + 2, jnp.max(jnp.where(converged(state[0], state[1]), 0, 1)), state

    n_swept = (n_tiles * DSA_TK).astype(F32)
    state = (col1(KEY_NEG_INF, jnp.int32), _f32_to_key(row_max),
             jnp.broadcast_to(n_swept, (1, DSA_TQ)), col1(0.0, F32), col1(1.0, F32), col1(1.0, F32),
             col1(0, jnp.int32))
    for fixed, counts in ((KEY_NEG_ZERO, above_zero), (KEY_NEG_MIN_NORMAL, above_neg)):
        inside = (state[0] < fixed) & (fixed < state[1])
        state = absorb(state, jnp.where(inside, fixed, state[0]),
                       jnp.sum(counts, axis=0, keepdims=True))
    state = lax.fori_loop(0, UNCHECKED_PROBES, probe_once, state)
    state = lax.while_loop(search_cond, search_body,
                           (jnp.int32(UNCHECKED_PROBES), jnp.int32(1), state))[2]
    thr = _key_to_f32(state[1])
    n_ties = kf - state[3]

    to_column = lambda r: jnp.broadcast_to(r, (8, DSA_TQ)).T[:, 0:1]
    thr_c, ties_c = to_column(thr), to_column(n_ties)
    q_all = jnp.concatenate(
        _masked_heads([dq_ref[0, :, 0:LANES], dq_ref[0, :, LANES:2 * LANES]]), axis=0)
    tri = tri_ref[...]
    tri_lo = jnp.where(lax.broadcasted_iota(jnp.int32, (LANES, LANES), 0)
                       >= lax.broadcasted_iota(jnp.int32, (LANES, LANES), 1), 1.0, 0.0).astype(BF16)
    qrow = q0 + lax.broadcasted_iota(jnp.int32, (DSA_TQ, DSA_TK), 0)
    kcol = lax.broadcasted_iota(jnp.int32, (DSA_TQ, DSA_TK), 1)
    n_blocks = DSA_TK // LANES

    reach = _logit_reach(q_all, jnp.max(kn_ref[0], axis=0)[KN_DSA:KN_DSA + 1, 0:1])

    def selected_logits(j, seen, causal):
        k0 = pl.multiple_of(j * DSA_TK, DSA_TK)
        s = s_ref[pl.ds(k0, DSA_TK), :].T
        tie = s == thr_c
        tie_b = jnp.where(tie, 1.0, 0.0).astype(BF16)
        local = [_dot(tie_b[:, b * LANES:(b + 1) * LANES], tri) for b in range(n_blocks)]
        ranks = []
        for b in range(n_blocks):
            ranks.append(local[b] + seen)
            seen = seen + local[b][:, LANES - 1:LANES]
        sel = (s > thr_c) | (tie & (jnp.concatenate(ranks, axis=1) <= ties_c))
        if causal:
            sel = sel & (k0 + kcol <= qrow)
        logits = _dot_nt(q_all, kk_ref[0, pl.ds(k0, DSA_TK), :])
        logits = jnp.where(sel[None], logits.reshape(N_HEADS, DSA_TQ, DSA_TK), NEG)
        return seen, logits.reshape(N_HEADS * DSA_TQ, DSA_TK), va_ref[0, pl.ds(k0, DSA_TK), :]

    def capped(j, carry, causal):
        seen, acc = carry
        k0 = pl.multiple_of(j * DSA_TK, DSA_TK)
        s = s_ref[pl.ds(k0, DSA_TK), :]
        tie = s == thr
        tie_b = jnp.where(tie, 1.0, 0.0).astype(BF16)
        ranks = []
        for b in range(n_blocks):
            local = _dot(tri_lo, tie_b[b * LANES:(b + 1) * LANES, :])
            ranks.append(local + seen)
            seen = seen + local[LANES - 1:LANES, :]
        sel = (s > thr) | (tie & (jnp.concatenate(ranks, axis=0) <= n_ties))
        if causal:
            sel = sel & (k0 + key <= qry)
        keep = jnp.where(sel, 1.0, 0.0).astype(BF16).T
        logits = _dot_nt(q_all, kk_ref[0, pl.ds(k0, DSA_TK), :])
        p = jnp.exp2(logits - reach).astype(BF16).reshape(N_HEADS, DSA_TQ, DSA_TK) * keep[None]
        return seen, acc + _dot(p.reshape(N_HEADS * DSA_TQ, DSA_TK), va_ref[0, pl.ds(k0, DSA_TK), :])

    def online(j, carry, causal):
        seen, logits, va_t = selected_logits(j, carry[0], causal)
        return (seen,) + _softmax_step(logits, carry[1], carry[2], va_t)

    def sweep(step, *stats):
        carry = stats + (jnp.zeros((N_HEADS * DSA_TQ, LANES), F32),)
        carry = lax.fori_loop(0, last, functools.partial(step, causal=False), carry)
        return step(last, carry, causal=True)[-1]

    acc = lax.cond(2.0 * jnp.max(reach) <= SAFE_LOG2_SPAN,
                   functools.partial(sweep, capped, jnp.zeros((1, DSA_TQ), F32)),
                   functools.partial(sweep, online, jnp.zeros((DSA_TQ, 1), F32),
                                     jnp.full((N_HEADS * DSA_TQ, 1), NEG, F32)))
    o_ref[0] = jnp.concatenate(
        _normalise_heads([acc[h * DSA_TQ:(h + 1) * DSA_TQ] for h in range(N_HEADS)]),
        axis=1).astype(o_ref.dtype)


def _dsa_attention(dq, dkk, dva, iq, ikk, small_t, key_norms, k_top, n_tok):
    bsz, t_work, _ = dq.shape
    tri = jnp.asarray(np.triu(np.ones((LANES, LANES), np.float32)), dtype=BF16)
    tile = lambda width: pl.BlockSpec((1, DSA_TQ, width), lambda b, i: (b, i, 0))
    full = pl.BlockSpec((1, t_work, LANES), lambda b, i: (b, 0, 0), pipeline_mode=pl.Buffered(1))
    return pl.pallas_call(
        functools.partial(_dsa_kernel, k_top, n_tok),
        grid=(bsz, t_work // DSA_TQ),
        in_specs=[tile(256), full, full, tile(256), full,
                  pl.BlockSpec((1, 8, DSA_TQ), lambda b, i: (b, 0, i)),
                  pl.BlockSpec((LANES, LANES), lambda b, i: (0, 0)),
                  pl.BlockSpec((1,) + key_norms.shape[1:], lambda b, i: (b, 0, 0, 0))],
        out_specs=tile(256),
        out_shape=jax.ShapeDtypeStruct((bsz, t_work, 256), BF16),
        scratch_shapes=[pltpu.VMEM((t_work, DSA_TQ), F32)],
        compiler_params=pltpu.CompilerParams(
            dimension_semantics=("parallel", "parallel"), vmem_limit_bytes=VMEM_LIMIT),
        name="dsa_attention",
    )(dq, dkk, dva, iq, ikk, small_t, tri, key_norms)


FOX_TQ = 512
FOX_TK = 512
CUM_T = 256


def _fox_prep_kernel(x_ref, tri_ref, place_ref, k_ref, xt_ref, kb_ref, carry_ref):
    @pl.when(pl.program_id(1) == 0)
    def _():
        carry_ref[...] = jnp.zeros_like(carry_ref)

    tri = tri_ref[...]
    x = x_ref[0]
    h1, h2, h3 = _split3(x)
    c = _dot(tri, h1) + _dot(tri, h2) + _dot(tri, h3) + carry_ref[0:1, :]
    carry_ref[...] = jnp.broadcast_to(c[CUM_T - 1:CUM_T, :], carry_ref.shape)
    xt_ref[0] = x.T[0:8, :]
    terms = _split3(c * -LOG2E)
    bias = sum(_dot(terms[t], place_ref[t]) for t in range(BIAS_TERMS))
    lane = lax.broadcasted_iota(jnp.int32, bias.shape, 1) % LANES
    kb_ref[0] = jnp.where((lane >= HEAD_DIM) & (lane < HEAD_DIM + BIAS_TERMS),
                          bias.astype(BF16), k_ref[0])


def _fox_prep(small, fk):
    bsz, t_work, width = fk.shape
    tri = jnp.asarray(np.tril(np.ones((CUM_T, CUM_T), np.float32)), dtype=BF16)
    place = np.zeros((BIAS_TERMS, LANES, width), np.float32)
    for t in range(BIAS_TERMS):
        for h in range(N_HEADS):
            place[t, SM_FF + h, h * LANES + HEAD_DIM + t] = 1.0
    rows = lambda w: pl.BlockSpec((1, CUM_T, w), lambda b, j: (b, j, 0))
    return pl.pallas_call(
        _fox_prep_kernel,
        grid=(bsz, t_work // CUM_T),
        in_specs=[rows(LANES), pl.BlockSpec((CUM_T, CUM_T), lambda b, j: (0, 0)),
                  pl.BlockSpec((BIAS_TERMS, LANES, width), lambda b, j: (0, 0, 0)), rows(width)],
        out_specs=[pl.BlockSpec((1, 8, CUM_T), lambda b, j: (b, 0, j)), rows(width)],
        out_shape=[jax.ShapeDtypeStruct((bsz, 8, t_work), F32),
                   jax.ShapeDtypeStruct(fk.shape, BF16)],
        scratch_shapes=[pltpu.VMEM((8, LANES), F32)],
        compiler_params=pltpu.CompilerParams(dimension_semantics=("parallel", "arbitrary")),
        name="fox_prep",
    )(small, tri, jnp.asarray(place, dtype=BF16), fk)


def _logit_reach(q, kmax_sq):
    qf = q.astype(F32)
    return jnp.sqrt(jnp.sum(qf * qf, axis=-1, keepdims=True) * kmax_sq) * NORM_MARGIN


def _fox_kernel(q_ref, k_ref, v_ref, kn_ref, o_ref):
    i = pl.program_id(1)
    q0 = pl.multiple_of(i * FOX_TQ, FOX_TQ)
    n_full = q0 // FOX_TK
    row = q0 + lax.broadcasted_iota(jnp.int32, (FOX_TQ, FOX_TK), 0)
    col = n_full * FOX_TK + lax.broadcasted_iota(jnp.int32, (FOX_TQ, FOX_TK), 1)
    heads = [slice(h * LANES, (h + 1) * LANES) for h in range(N_HEADS)]

    lane = lax.broadcasted_iota(jnp.int32, (FOX_TQ, LANES), 1)
    bias_lanes = (lane >= HEAD_DIM) & (lane < HEAD_DIM + BIAS_TERMS)
    kmax_sq = jnp.max(kn_ref[0], axis=0)
    caps, span = [], jnp.float32(0.0)
    for h in range(N_HEADS):
        q_h = q_ref[0, :, heads[h]]
        reach = _logit_reach(jnp.where(lane < HEAD_DIM, q_h, jnp.zeros_like(q_h)),
                             kmax_sq[KN_FOX + h:KN_FOX + h + 1, 0:1])
        own = k_ref[0, pl.ds(q0, FOX_TQ), heads[h]].astype(F32)
        caps.append(reach + jnp.sum(jnp.where(bias_lanes, own, 0.0), axis=-1, keepdims=True))
        span = jnp.maximum(span, 2.0 * jnp.max(reach))

    def logits(j, h, diag):
        k0 = pl.multiple_of(j * FOX_TK, FOX_TK)
        s = _dot_nt(q_ref[0, :, heads[h]], k_ref[0, pl.ds(k0, FOX_TK), heads[h]])
        return jnp.where(col <= row, s, NEG) if diag else s

    def values(j, h):
        return v_ref[0, pl.ds(pl.multiple_of(j * FOX_TK, FOX_TK), FOX_TK), heads[h]]

    def capped(j, accs, diag):
        return tuple(accs[h] + _dot(jnp.exp2(logits(j, h, diag) - caps[h]).astype(BF16), values(j, h))
                     for h in range(N_HEADS))

    def online(j, carry, diag):
        return tuple(_softmax_step(logits(j, h, diag), *carry[h], values(j, h))
                     for h in range(N_HEADS))

    def capped_sweep():
        accs = tuple(jnp.zeros((FOX_TQ, LANES), F32) for _ in range(N_HEADS))
        accs = lax.fori_loop(0, n_full, functools.partial(capped, diag=False), accs)
        return capped(n_full, accs, diag=True)

    def online_sweep():
        carry = tuple((jnp.full((FOX_TQ, 1), NEG, F32), jnp.zeros((FOX_TQ, LANES), F32))
                      for _ in range(N_HEADS))
        carry = lax.fori_loop(0, n_full, functools.partial(online, diag=False), carry)
        return tuple(acc for _, acc in online(n_full, carry, diag=True))

    accs = lax.cond(span <= SAFE_LOG2_SPAN, capped_sweep, online_sweep)
    o_ref[0] = jnp.concatenate(_normalise_heads(list(accs)), axis=1).astype(o_ref.dtype)


def _fox_attention(fqa, fkb, fva, key_norms):
    bsz, t_work, width = fqa.shape
    full = pl.BlockSpec((1, t_work, width), lambda b, i: (b, 0, 0), pipeline_mode=pl.Buffered(1))
    return pl.pallas_call(
        _fox_kernel,
        grid=(bsz, t_work // FOX_TQ),
        in_specs=[pl.BlockSpec((1, FOX_TQ, width), lambda b, i: (b, i, 0)), full, full,
                  pl.BlockSpec((1,) + key_norms.shape[1:], lambda b, i: (b, 0, 0, 0))],
        out_specs=pl.BlockSpec((1, FOX_TQ, 2 * LANES), lambda b, i: (b, i, 0)),
        out_shape=jax.ShapeDtypeStruct((bsz, t_work, 2 * LANES), BF16),
        compiler_params=pltpu.CompilerParams(
            dimension_semantics=("parallel", "parallel"), vmem_limit_bytes=VMEM_LIMIT),
        name="fox_attention",
    )(fqa, fkb, fva, key_norms)


GLA_NSUB = GLA_CHUNK // GLA_SUB


def _gla_kernel(q_ref, k_ref, v_ref, g_ref, la_ref, tri_ref, e_ref, gn_ref, o_ref, st_ref):
    @pl.when(pl.program_id(1) == 0)
    def _():
        st_ref[...] = jnp.zeros_like(st_ref)

    tri = tri_ref[...]
    emat = e_ref[...]
    lane = lax.broadcasted_iota(jnp.int32, (GLA_CHUNK, LANES), 1)
    rowblk = lax.broadcasted_iota(jnp.int32, (GLA_CHUNK, LANES), 0) // GLA_SUB
    tblk = lax.broadcasted_iota(jnp.int32, (GLA_CHUNK, GLA_CHUNK), 0) // GLA_SUB
    sblk = lax.broadcasted_iota(jnp.int32, (GLA_CHUNK, GLA_CHUNK), 1) // GLA_SUB
    trow = lax.broadcasted_iota(jnp.int32, (GLA_SUB, 256), 0)

    def chunk(c, carry):
        r0 = pl.multiple_of(c * GLA_CHUNK, GLA_CHUNK)
        rows = pl.ds(r0, GLA_CHUNK)
        h1, h2, h3 = _split3(la_ref[0, rows, :])
        b = _dot(tri, h1) + _dot(tri, h2) + _dot(tri, h3)
        q = q_ref[0, rows, :]
        k = k_ref[0, rows, :]
        v = v_ref[0, rows, :]
        b_last = b[GLA_CHUNK - 1:GLA_CHUNK, :]
        qd = q * jnp.exp2(b)
        kd = (k * jnp.exp2(b_last - b)).astype(BF16)
        starts = [jnp.zeros((1, 256), F32)] + [b[GLA_SUB * i - 1:GLA_SUB * i, :]
                                               for i in range(1, GLA_NSUB)]
        bsel = jnp.concatenate([jnp.broadcast_to(s, (GLA_SUB, 256)) for s in starts], axis=0)
        qn = q * jnp.exp2(b - bsel)

        diag = []
        for i in range(GLA_NSUB):
            rs = slice(GLA_SUB * i, GLA_SUB * (i + 1))
            b_i, q_i, k_i = b[rs], q[rs], k[rs]
            v_i = v[rs].astype(F32)
            ps = []
            for s in range(GLA_SUB):
                d = jnp.exp2(jnp.minimum(b_i - b_i[s:s + 1], 0.0))
                ps.append(jnp.where(trow >= s, q_i * d * k_i[s:s + 1], 0.0).astype(BF16))
            r = _dot(jnp.concatenate(ps, axis=0), emat)
            od = r[0:GLA_SUB] * v_i[0:1]
            for s in range(1, GLA_SUB):
                od = od + r[GLA_SUB * s:GLA_SUB * (s + 1)] * v_i[s:s + 1]
            diag.append(od)
        o_diag = jnp.concatenate(diag, axis=0)

        for slab in range(2):
            ls = slice(slab * LANES, (slab + 1) * LANES)
            qn_s, k_s, b_s = qn[:, ls], k[:, ls], b[:, ls]
            khat = jnp.concatenate(
                [(k_s * jnp.exp2(jnp.minimum(starts[i][:, ls] - b_s, 0.0))).astype(BF16)
                 for i in range(1, GLA_NSUB)], axis=1)
            for half in range(2):
                head = 2 * slab + half
                hs = slice(head * GLA_DV, (head + 1) * GLA_DV)
                in_head = (lane < HEAD_DIM) if half == 0 else (lane >= HEAD_DIM)
                qm = jnp.where(in_head, qn_s, 0.0)
                qhat = jnp.concatenate([jnp.where(rowblk == i, qm, 0.0).astype(BF16)
                                        for i in range(1, GLA_NSUB)], axis=1)
                att = jnp.where(sblk < tblk, _dot_nt(qhat, khat), 0.0)
                v_h = v[:, hs]
                st = st_ref[head]
                o = (_dot_nt(jnp.where(in_head, qd[:, ls], 0.0).astype(BF16), st.astype(BF16))
                     + _dot(att.astype(BF16), v_h) + o_diag[:, hs])
                st_ref[head] = st * jnp.exp2(b_last[:, ls]) + _dot_tn(v_h, kd[:, ls])
                y = o * lax.rsqrt(jnp.mean(o * o, axis=-1, keepdims=True) + EPS) * gn_ref[:, hs]
                o_ref[0, rows, hs] = (y * _silu(g_ref[0, rows, hs])).astype(o_ref.dtype)
        return carry

    lax.fori_loop(0, q_ref.shape[1] // GLA_CHUNK, chunk, 0)


def _gla(gq, gk, gv, gg, la, gain):
    bsz, t_work, _ = gq.shape
    tg = _row_tile(t_work)
    tri = jnp.asarray(np.tril(np.ones((GLA_CHUNK, GLA_CHUNK), np.float32)), dtype=BF16)
    emat = jnp.asarray(
        (np.arange(256)[:, None] // HEAD_DIM == np.arange(512)[None, :] // GLA_DV).astype(np.float32),
        dtype=BF16)
    rows = lambda width: pl.BlockSpec((1, tg, width), lambda b, j: (b, j, 0))
    const = lambda shape: pl.BlockSpec(shape, lambda b, j: (0,) * len(shape))
    return pl.pallas_call(
        _gla_kernel,
        grid=(bsz, t_work // tg),
        in_specs=[rows(256), rows(256), rows(512), rows(512), rows(256),
                  const((GLA_CHUNK, GLA_CHUNK)), const((256, 512)), const((1, 512))],
        out_specs=rows(512),
        out_shape=jax.ShapeDtypeStruct((bsz, t_work, 512), BF16),
        scratch_shapes=[pltpu.VMEM((N_HEADS, GLA_DV, LANES), F32)],
        compiler_params=pltpu.CompilerParams(
            dimension_semantics=("parallel", "arbitrary"), vmem_limit_bytes=VMEM_LIMIT),
        name="gla",
    )(gq, gk, gv, gg, la, tri, emat, gain)


FFN_CHUNK = 256


def _ffn_kernel(h_ref, oa_ref, ob_ref, oc_ref, wo_ref, fn_ref, wgu_ref, wd_ref, out_ref):
    h1 = (h_ref[0] + _dot(oa_ref[0], wo_ref[0:256, :]) + _dot(ob_ref[0], wo_ref[256:512, :])
          + _dot(oc_ref[0], wo_ref[512:1024, :]))
    ms = jnp.mean(h1 * h1, axis=-1, keepdims=True)
    f = (h1 * lax.rsqrt(ms + EPS) * fn_ref[...]).astype(BF16)
    out_ref[0] = h1
    for c in range(0, D_FF, FFN_CHUNK):
        gate = _dot(f, wgu_ref[:, c:c + FFN_CHUNK])
        up = _dot(f, wgu_ref[:, D_FF + c:D_FF + c + FFN_CHUNK])
        out_ref[0] += _dot((_silu(gate) * up).astype(BF16), wd_ref[c:c + FFN_CHUNK, :])


def _outproj_ffn(h, oa, ob, oc, wo, fn, wgu, wd):
    bsz, t_work, _ = h.shape
    tm = _row_tile(t_work)
    rows = lambda width: pl.BlockSpec((1, tm, width), lambda b, j: (b, j, 0))
    const = lambda shape: pl.BlockSpec(shape, lambda b, j: (0,) * len(shape),
                                       pipeline_mode=pl.Buffered(1))
    return pl.pallas_call(
        _ffn_kernel,
        grid=(bsz, t_work // tm),
        in_specs=[rows(D_MODEL), rows(256), rows(256), rows(512), const((D_MODEL, D_MODEL)),
                  const((1, D_MODEL)), const((D_MODEL, 2 * D_FF)), const((D_FF, D_MODEL))],
        out_specs=rows(D_MODEL),
        out_shape=jax.ShapeDtypeStruct(h.shape, F32),
        compiler_params=pltpu.CompilerParams(
            dimension_semantics=("parallel", "parallel"), vmem_limit_bytes=VMEM_LIMIT),
        name="outproj_ffn",
    )(h, oa, ob, oc, wo, fn, wgu, wd)


def kernel(x, meta_tokens, attn_norm, w_in, dsa_q_norm, dsa_k_norm, fox_q_norm, fox_k_norm,
           fox_f_bias, gla_gate_w2, gla_gate_b, gla_out_norm, w_out, ffn_norm, w_gate_up, w_down):
    bsz, seq, _ = x.shape
    n_tok = N_META + seq
    t_work = _work_len(n_tok)
    meta = jnp.broadcast_to(meta_tokens[None].astype(x.dtype), (bsz, N_META, D_MODEL))
    h = jnp.concatenate([meta, x, jnp.zeros((bsz, t_work - n_tok, D_MODEL), x.dtype)], axis=1)
    cos_t, sin_t = _rope_tables(t_work)
    g256 = _group_matrix()
    k_top = min(TOPK_MAX, seq // 4)
    for l in range(w_in.shape[0]):
        an, w_perm, vecs, w2_pad, sm_bias = _prep_layer_params(
            l, attn_norm, w_in, dsa_q_norm, dsa_k_norm, fox_q_norm, fox_k_norm, fox_f_bias,
            gla_gate_w2, gla_gate_b)
        (dq, dkk, dva, iq, ikk, fq, fk, fva, gq, gk, gv, gg, la, small, key_norms) = _inproj(
            h, an, w_perm, cos_t, sin_t, g256, vecs, w2_pad, sm_bias)
        small_t, fkb = _fox_prep(small, fk)
        oa = _dsa_attention(dq, dkk, dva, iq, ikk, small_t, key_norms, k_top, n_tok)
        ob = _fox_attention(fq, fkb, fva, key_norms)
        oc = _gla(gq, gk, gv, gg, la, jnp.tile(gla_out_norm[l], N_HEADS)[None, :])
        h = _outproj_ffn(h, oa, ob, oc, w_out[l].astype(BF16), ffn_norm[l][None, :],
                         w_gate_up[l].astype(BF16), w_down[l].astype(BF16))
    return h[:, N_META:n_tok]
```

```python
import functools

import numpy as np
import jax
import jax.numpy as jnp
from jax import lax
from jax.experimental import pallas as pl
from jax.experimental.pallas import tpu as pltpu

F32 = jnp.float32
BF16 = jnp.bfloat16

D_MODEL = 1024
HEAD_DIM = 64
N_META = 16
ROPE_THETA = 500000.0
ROPE_DIM = HEAD_DIM // 4
ROPE_HALF = ROPE_DIM // 2
NEG = -1e30
EPS = 1e-6

N_HEADS = 4
TOPK_MAX = 256
GLA_DV = 128
GLA_RANK = 16
GLA_TAU = 16.0
GLA_CHUNK = 64
GLA_SUB = 16
D_FF = 2816

LOG2E = 1.4426950408889634
BIAS_TERMS = 3
KN_DSA, KN_FOX = 0, 1
NORM_MARGIN = 1.01
SAFE_LOG2_SPAN = 100.0
LANES = 128
SEQ_ALIGN = 512
VMEM_LIMIT = 56 * 1024 * 1024

_SLABS = (("dq", 256), ("dkk", 128), ("dva", 128), ("iq", 256), ("ikk", 128),
          ("fq", 256), ("fk", 256), ("fv", 256), ("gq", 256), ("gk", 256),
          ("gv", 512), ("gg", 512), ("small", 128))
_SLAB_OFF = {}
_off = 0
for _name, _width in _SLABS:
    _SLAB_OFF[_name] = (_off, _width)
    _off += _width
N_PROJ = _off
SM_IW, SM_FF, SM_GLR = 0, 4, 8


def _work_len(n_tok):
    return -(-n_tok // SEQ_ALIGN) * SEQ_ALIGN


def _row_tile(t_work):
    for cand in (768, 640, 512):
        if t_work % cand == 0:
            return cand
    raise ValueError(f"unsupported working length {t_work}")


def _dot(a, b):
    return jnp.dot(a, b, preferred_element_type=F32)


def _dot_nt(a, b):
    return lax.dot_general(a, b, (((1,), (1,)), ((), ())), preferred_element_type=F32)


def _dot_tn(a, b):
    return lax.dot_general(a, b, (((0,), (0,)), ((), ())), preferred_element_type=F32)


def _split3(x):
    h1 = x.astype(BF16)
    r1 = x - h1.astype(F32)
    h2 = r1.astype(BF16)
    h3 = (r1 - h2.astype(F32)).astype(BF16)
    return h1, h2, h3


def _log_sigmoid(x):
    return jnp.minimum(x, 0.0) - jnp.log1p(jnp.exp(-jnp.abs(x)))


def _silu(x):
    return x / (1.0 + jnp.exp(-x))


def _group_rms(y, gmat, gain):
    yy = y * y
    hi = yy.astype(BF16)
    lo = (yy - hi.astype(F32)).astype(BF16)
    ss = _dot(hi, gmat) + _dot(lo, gmat)
    return y * lax.rsqrt(ss * (1.0 / HEAD_DIM) + EPS) * gain


def _rope(y, cos, sin):
    width = y.shape[-1]
    lane = lax.broadcasted_iota(jnp.int32, y.shape, 1) % HEAD_DIM
    upper = pltpu.roll(y, width - ROPE_HALF, axis=1)
    lower = pltpu.roll(y, ROPE_HALF, axis=1)
    partner = jnp.where(lane < ROPE_HALF, upper, lower)
    return y * cos + partner * sin


def _inproj_kernel(pad, x_ref, an_ref, w_ref, cos_ref, sin_ref, g256_ref, vec_ref, w2_ref, sm_ref,
                   dq_ref, dkk_ref, dva_ref, iq_ref, ikk_ref, fq_ref, fk_ref, fva_ref,
                   gq_ref, gk_ref, gv_ref, gg_ref, la_ref, small_ref, kn_ref):
    x = x_ref[0]
    ms = jnp.mean(x * x, axis=-1, keepdims=True)
    a = x * lax.rsqrt(ms + EPS) * an_ref[...]
    slot = pl.program_id(1) * x.shape[0] + lax.broadcasted_iota(jnp.int32, a.shape, 0)
    a = jnp.where(slot >= pad, a, 0.0).astype(BF16)

    def proj(name):
        off, width = _SLAB_OFF[name]
        return _dot(a, w_ref[:, off:off + width])

    def ones_in_upper_half(y):
        lane = lax.broadcasted_iota(jnp.int32, y.shape, 1) % LANES
        return jnp.where(lane < HEAD_DIM, y, 1.0)

    cos = cos_ref[...]
    sin = sin_ref[...]
    g256 = g256_ref[...]
    g128 = g256[:LANES, :LANES]
    dqn, dkn, fqn, fkn = vec_ref[0:1, :], vec_ref[1:2, :LANES], vec_ref[2:3, :], vec_ref[3:4, :]
    scale = HEAD_DIM ** -0.5

    def max_sq_norm(k):
        kf = k.astype(F32)
        return jnp.max(_dot((kf * kf).astype(BF16), g128), axis=0, keepdims=True)

    dq_ref[0] = (_rope(_group_rms(proj("dq"), g256, dqn), cos, sin) * (scale * LOG2E)).astype(BF16)
    dkk = _rope(_group_rms(proj("dkk"), g128, dkn), cos[:, :LANES], sin[:, :LANES]).astype(BF16)
    dkk_ref[0] = dkk
    key_norms = [max_sq_norm(dkk)]
    dva_ref[0] = ones_in_upper_half(proj("dva")).astype(BF16)
    iq_ref[0] = (_rope(proj("iq"), cos, sin) * scale).astype(BF16)
    ikk_ref[0] = _rope(proj("ikk"), cos[:, :LANES], sin[:, :LANES]).astype(BF16)
    fq, fk, fv = proj("fq"), proj("fk"), proj("fv")
    lane = lax.broadcasted_iota(jnp.int32, (fq.shape[0], LANES), 1)
    in_head = lane < HEAD_DIM
    q_pad = jnp.where(lane < HEAD_DIM + BIAS_TERMS, 1.0, 0.0)
    for pair in range(N_HEADS // 2):
        ps = slice(pair * LANES, (pair + 1) * LANES)
        q_pair = _group_rms(fq[:, ps], g128, fqn[:, :LANES]) * (scale * LOG2E)
        k_pair = _group_rms(fk[:, ps], g128, fkn[:, :LANES])
        v_pair = fv[:, ps]
        for odd in range(2):
            hs = slice((2 * pair + odd) * LANES, (2 * pair + odd + 1) * LANES)
            down = (lambda y: pltpu.roll(y, HEAD_DIM, axis=1)) if odd else (lambda y: y)
            fq_ref[0, :, hs] = jnp.where(in_head, down(q_pair), q_pad).astype(BF16)
            k_h = jnp.where(in_head, down(k_pair), 0.0).astype(BF16)
            fk_ref[0, :, hs] = k_h
            key_norms.append(max_sq_norm(k_h))
            fva_ref[0, :, hs] = jnp.where(in_head, down(v_pair), 1.0).astype(BF16)
    kn_ref[0, 0] = jnp.concatenate(key_norms + [jnp.zeros((8 - len(key_norms), LANES), F32)], axis=0)
    gq_ref[0] = proj("gq") * scale
    gk_ref[0] = proj("gk")
    gv_ref[0] = proj("gv").astype(BF16)
    gg_ref[0] = proj("gg")

    small = proj("small")
    lane = lax.broadcasted_iota(jnp.int32, small.shape, 1)
    small_ref[0] = jnp.where(lane < SM_FF, small * (N_HEADS ** -0.5),
                             _log_sigmoid(small + sm_ref[0:1, :]))
    gate = _dot(small.astype(BF16), w2_ref[...]) + vec_ref[4:5, :]
    la_ref[0] = _log_sigmoid(gate) * (LOG2E / GLA_TAU)


def _inproj(h, an, w_perm, cos_t, sin_t, g256, vecs, w2_pad, sm_bias, pad):
    bsz, t_work, _ = h.shape
    tm = _row_tile(t_work)
    grid = (bsz, t_work // tm)

    def rows(width, dtype):
        return (jax.ShapeDtypeStruct((bsz, t_work, width), dtype),
                pl.BlockSpec((1, tm, width), lambda b, j: (b, j, 0)))

    outs = [rows(256, BF16), rows(128, BF16), rows(128, BF16), rows(256, BF16), rows(128, BF16),
            rows(512, BF16), rows(512, BF16), rows(512, BF16), rows(256, F32), rows(256, F32),
            rows(512, BF16), rows(512, F32), rows(256, F32), rows(128, F32),
            (jax.ShapeDtypeStruct((bsz, t_work // tm, 8, LANES), F32),
             pl.BlockSpec((1, 1, 8, LANES), lambda b, j: (b, j, 0, 0)))]
    const = lambda shape: pl.BlockSpec(shape, lambda b, j: (0,) * len(shape))
    return pl.pallas_call(
        functools.partial(_inproj_kernel, pad),
        grid=grid,
        in_specs=[pl.BlockSpec((1, tm, D_MODEL), lambda b, j: (b, j, 0)),
                  const((1, D_MODEL)), const((D_MODEL, N_PROJ)),
                  pl.BlockSpec((tm, 256), lambda b, j: (j, 0)),
                  pl.BlockSpec((tm, 256), lambda b, j: (j, 0)),
                  const((256, 256)), const((8, 256)), const((LANES, 256)), const((8, LANES))],
        out_specs=[o[1] for o in outs],
        out_shape=[o[0] for o in outs],
        compiler_params=pltpu.CompilerParams(
            dimension_semantics=("parallel", "parallel"), vmem_limit_bytes=VMEM_LIMIT),
        name="inproj",
    )(h, an, w_perm, cos_t, sin_t, g256, vecs, w2_pad, sm_bias)


def _prep_layer_params(l, attn_norm, w_in, dsa_q_norm, dsa_k_norm, fox_q_norm, fox_k_norm,
                       fox_f_bias, gla_gate_w2, gla_gate_b):
    w = w_in[l]
    splits = np.cumsum([256, 64, 64, 256, 4, 64, 256, 256, 256, 4, 256, 256, 512, 512, 16])[:-1]
    (dq, dk, dv, iq, iw, ik, fq, fk, fv, ff, gq, gk, gv, gg, glr) = jnp.split(w, splits, axis=1)
    small = jnp.concatenate(
        [iw, ff, glr, jnp.zeros((D_MODEL, LANES - 4 - 4 - GLA_RANK), w.dtype)], axis=1)
    z64 = jnp.zeros((D_MODEL, HEAD_DIM), w.dtype)
    w_perm = jnp.concatenate([dq, dk, dk, dv, z64, iq, ik, ik, fq, fk, fv, gq, gk, gv, gg, small],
                             axis=1).astype(BF16)
    tile4 = lambda g: jnp.tile(g, N_HEADS)
    vecs = jnp.zeros((8, 256), F32)
    vecs = vecs.at[0].set(tile4(dsa_q_norm[l])).at[1].set(tile4(dsa_k_norm[l]))
    vecs = vecs.at[2].set(tile4(fox_q_norm[l])).at[3].set(tile4(fox_k_norm[l]))
    vecs = vecs.at[4].set(gla_gate_b[l])
    w2_pad = jnp.zeros((LANES, 256), F32).at[SM_GLR:SM_GLR + GLA_RANK].set(gla_gate_w2[l]).astype(BF16)
    sm_bias = jnp.zeros((8, LANES), F32).at[0, SM_FF:SM_FF + N_HEADS].set(fox_f_bias[l])
    return attn_norm[l][None, :], w_perm, vecs, w2_pad, sm_bias


def _rope_tables(t_work, pad):
    inv = jnp.power(ROPE_THETA, -jnp.arange(ROPE_HALF, dtype=F32) * 2.0 / ROPE_DIM)
    ang = (jnp.arange(t_work) - pad).astype(F32)[:, None] * inv[None, :]
    cos, sin = jnp.cos(ang), jnp.sin(ang)
    rest = HEAD_DIM - ROPE_DIM
    cos64 = jnp.concatenate([cos, cos, jnp.ones((t_work, rest), F32)], axis=1)
    sin64 = jnp.concatenate([-sin, sin, jnp.zeros((t_work, rest), F32)], axis=1)
    return jnp.tile(cos64, (1, N_HEADS)), jnp.tile(sin64, (1, N_HEADS))


def _group_matrix():
    idx = np.arange(256) // HEAD_DIM
    return jnp.asarray((idx[:, None] == idx[None, :]).astype(np.float32), dtype=BF16)


DSA_TQ = 256
DSA_TK = 512
KEY_NEG_INF = -2139095041
KEY_NEG_ZERO = -1
KEY_NEG_MIN_NORMAL = -8388609
NEG_MIN_NORMAL = -1.1754943508222875e-38
SEARCH_FEW = 4.0
SEARCH_MANY = 16.0
UNCHECKED_PROBES = 12
GUIDED_PROBES = 64
MAX_PROBES = GUIDED_PROBES + 40
COUNT_ROWS = 128


def _key_to_f32(key):
    bits = key ^ ((key >> 31) & 0x7FFFFFFF)
    return lax.bitcast_convert_type(bits, F32)


def _f32_to_key(value):
    bits = lax.bitcast_convert_type(value, jnp.int32)
    return bits ^ ((bits >> 31) & 0x7FFFFFFF)


def _head_lane_mask(shape, head):
    lane = lax.broadcasted_iota(jnp.int32, shape, 1)
    return (lane < HEAD_DIM) if head % 2 == 0 else (lane >= HEAD_DIM)


def _masked_heads(slabs):
    return [jnp.where(_head_lane_mask(slabs[h // 2].shape, h), slabs[h // 2],
                      jnp.zeros_like(slabs[h // 2])) for h in range(N_HEADS)]


def _softmax_step(s, m, acc, v_aug):
    m_new = jnp.maximum(m, jnp.max(s, axis=-1, keepdims=True))
    p = jnp.exp2(s - m_new)
    return m_new, acc * jnp.exp2(m - m_new) + _dot(p.astype(BF16), v_aug)


def _normalise_heads(accs):
    outs = [acc / pltpu.roll(acc, HEAD_DIM, axis=1) for acc in accs]
    lane = lax.broadcasted_iota(jnp.int32, outs[0].shape, 1)
    return [jnp.where(lane < HEAD_DIM, outs[2 * p], pltpu.roll(outs[2 * p + 1], HEAD_DIM, axis=1))
            for p in range(2)]


def _dsa_kernel(k_top, pad, *refs):
    o_ref = refs[-2]
    real = (pl.program_id(1) + 1) * DSA_TQ > pad
    pl.when(real)(functools.partial(_dsa_tile, k_top, pad, *refs))

    @pl.when(jnp.logical_not(real))
    def _():
        o_ref[...] = jnp.zeros_like(o_ref)


def _edge_tiles(step, carry, last):
    carry = step(0, carry, masked=True)
    carry = lax.fori_loop(1, last, functools.partial(step, masked=False), carry)
    return lax.cond(last > 0, lambda c: step(last, c, masked=True), lambda c: c, carry)


def _dsa_tile(k_top, pad, dq_ref, kk_ref, va_ref, iq_ref, ik_ref, wt_ref, tri_ref, kn_ref,
              o_ref, s_ref):
    i = pl.program_id(1)
    q0 = i * DSA_TQ
    n_tiles = (q0 + DSA_TQ + DSA_TK - 1) // DSA_TK
    last = n_tiles - 1
    key = lax.broadcasted_iota(jnp.int32, (DSA_TK, DSA_TQ), 0)
    qry = q0 + lax.broadcasted_iota(jnp.int32, (DSA_TK, DSA_TQ), 1)
    head = lambda x, h: x[:, h * DSA_TQ:(h + 1) * DSA_TQ]

    iq_all = jnp.concatenate(
        _masked_heads([iq_ref[0, :, 0:LANES], iq_ref[0, :, LANES:2 * LANES]]), axis=0)
    w_h = [wt_ref[0, SM_IW + h:SM_IW + h + 1, :] for h in range(N_HEADS)]

    def score_tile(j, carry, masked):
        k0 = pl.multiple_of(j * DSA_TK, DSA_TK)
        dots = jnp.maximum(_dot_nt(ik_ref[0, pl.ds(k0, DSA_TK), :], iq_all), 0.0)
        s = w_h[0] * head(dots, 0)
        for h in range(1, N_HEADS):
            s = s + w_h[h] * head(dots, h)
        if masked:
            s = jnp.where((k0 + key >= pad) & (k0 + key <= qry), s, NEG)
        s_ref[pl.ds(k0, DSA_TK), :] = s
        row_max, above_zero, above_neg = carry
        for part in range(DSA_TK // COUNT_ROWS):
            chunk = s[part * COUNT_ROWS:(part + 1) * COUNT_ROWS]
            above_zero = above_zero + jnp.where(chunk > 0.0, 1.0, 0.0)
            above_neg = above_neg + jnp.where(chunk > NEG_MIN_NORMAL, 1.0, 0.0)
        return jnp.maximum(row_max, jnp.max(s, axis=0, keepdims=True)), above_zero, above_neg

    zeros = jnp.zeros((COUNT_ROWS, DSA_TQ), F32)
    row_max, above_zero, above_neg = _edge_tiles(
        score_tile, (jnp.full((1, DSA_TQ), NEG, F32), zeros, zeros), last)

    kf = jnp.float32(k_top)

    def count_above(t):
        tb = jnp.broadcast_to(t, (COUNT_ROWS, DSA_TQ))

        def body(j, acc, span):
            k0 = pl.multiple_of(j * span, span)
            for part in range(span // COUNT_ROWS):
                s = s_ref[pl.ds(k0 + part * COUNT_ROWS, COUNT_ROWS), :]
                acc = acc + jnp.where(s > tb, 1.0, 0.0)
            return acc

        acc = lax.fori_loop(0, n_tiles // 2, functools.partial(body, span=2 * DSA_TK),
                            jnp.zeros((COUNT_ROWS, DSA_TQ), F32))
        acc = lax.fori_loop(2 * (n_tiles // 2), n_tiles, functools.partial(body, span=DSA_TK), acc)
        return jnp.sum(acc, axis=0, keepdims=True)

    def midpoint(lo, hi):
        return (lo >> 1) + (hi >> 1) + (lo & hi & 1)

    def converged(lo, hi):
        return (midpoint(lo, hi) == lo) | ((lo >= KEY_NEG_MIN_NORMAL) & (hi <= 0))

    col1 = lambda value, dtype: jnp.full((1, DSA_TQ), value, dtype)

    def absorb(state, probe, c):
        lo, hi, clo, chi, wlo, whi, side = state
        live = probe != lo
        up = live & (c >= kf)
        down = live & (c <= kf)
        wlo = jnp.where(down & (side < 0), wlo * 0.5, jnp.where(up, 1.0, wlo))
        whi = jnp.where(up & (side > 0), whi * 0.5, jnp.where(down, 1.0, whi))
        side = jnp.where(up, 1, jnp.where(down, -1, side))
        lo, clo = jnp.where(up, probe, lo), jnp.where(up, c, clo)
        hi, chi = jnp.where(down, probe, hi), jnp.where(down, c, chi)
        return lo, hi, clo, chi, wlo, whi, side

    def probe_once(it, state):
        lo, hi, clo, chi, wlo, whi, side = state
        f_lo, f_hi = _key_to_f32(lo), _key_to_f32(hi)
        target = kf - 0.5
        log_count = lambda c: jnp.log2(jnp.maximum(c, 0.25))
        many = clo - chi > SEARCH_MANY
        g_lo = jnp.where(many, log_count(clo) - np.log2(k_top - 0.5), clo - target) * wlo
        g_hi = jnp.where(many, np.log2(k_top - 0.5) - log_count(chi), target - chi) * whi
        halve = (clo - chi <= SEARCH_FEW) | (col1(it % 8, jnp.int32) == 7)
        guess = _f32_to_key(f_lo + (f_hi - f_lo) * jnp.where(halve, 0.5, g_lo / (g_lo + g_hi)))
        guided = col1(it, jnp.int32) < GUIDED_PROBES
        probe = jnp.where((guess > lo) & (guess < hi) & guided, guess, midpoint(lo, hi))
        probe = jnp.where(converged(lo, hi), lo, probe)
        return absorb(state, probe, count_above(_key_to_f32(probe)))

    def search_cond(carry):
        it, pending = carry[0], carry[1]
        return (pending > 0) & (it < MAX_PROBES)

    def search_body(carry):
        it, state = carry[0], carry[2]
        state = probe_once(it + 1, probe_once(it, state))
        return it + 2, jnp.max(jnp.where(converged(state[0], state[1]), 0, 1)), state

    n_swept = (n_tiles * DSA_TK).astype(F32)
    state = (col1(KEY_NEG_INF, jnp.int32), _f32_to_key(row_max),
             jnp.broadcast_to(n_swept, (1, DSA_TQ)), col1(0.0, F32), col1(1.0, F32), col1(1.0, F32),
             col1(0, jnp.int32))
    for fixed, counts in ((KEY_NEG_ZERO, above_zero), (KEY_NEG_MIN_NORMAL, above_neg)):
        inside = (state[0] < fixed) & (fixed < state[1])
        state = absorb(state, jnp.where(inside, fixed, state[0]),
                       jnp.sum(counts, axis=0, keepdims=True))
    state = lax.fori_loop(0, UNCHECKED_PROBES, probe_once, state)
    state = lax.while_loop(search_cond, search_body,
                           (jnp.int32(UNCHECKED_PROBES), jnp.int32(1), state))[2]
    thr = _key_to_f32(state[1])
    n_ties = kf - state[3]

    to_column = lambda r: jnp.broadcast_to(r, (8, DSA_TQ)).T[:, 0:1]
    thr_c, ties_c = to_column(thr), to_column(n_ties)
    q_all = jnp.concatenate(
        _masked_heads([dq_ref[0, :, 0:LANES], dq_ref[0, :, LANES:2 * LANES]]), axis=0)
    tri = tri_ref[...]
    tri_lo = jnp.where(lax.broadcasted_iota(jnp.int32, (LANES, LANES), 0)
                       >= lax.broadcasted_iota(jnp.int32, (LANES, LANES), 1), 1.0, 0.0).astype(BF16)
    qrow = q0 + lax.broadcasted_iota(jnp.int32, (DSA_TQ, DSA_TK), 0)
    kcol = lax.broadcasted_iota(jnp.int32, (DSA_TQ, DSA_TK), 1)
    n_blocks = DSA_TK // LANES

    reach = _logit_reach(q_all, jnp.max(kn_ref[0], axis=0)[KN_DSA:KN_DSA + 1, 0:1])

    def selected_logits(j, seen, masked):
        k0 = pl.multiple_of(j * DSA_TK, DSA_TK)
        s = s_ref[pl.ds(k0, DSA_TK), :].T
        tie = s == thr_c
        tie_b = jnp.where(tie, 1.0, 0.0).astype(BF16)
        local = [_dot(tie_b[:, b * LANES:(b + 1) * LANES], tri) for b in range(n_blocks)]
        ranks = []
        for b in range(n_blocks):
            ranks.append(local[b] + seen)
            seen = seen + local[b][:, LANES - 1:LANES]
        sel = (s > thr_c) | (tie & (jnp.concatenate(ranks, axis=1) <= ties_c))
        if masked:
            sel = sel & (k0 + kcol >= pad) & (k0 + kcol <= qrow)
        logits = _dot_nt(q_all, kk_ref[0, pl.ds(k0, DSA_TK), :])
        logits = jnp.where(sel[None], logits.reshape(N_HEADS, DSA_TQ, DSA_TK), NEG)
        return seen, logits.reshape(N_HEADS * DSA_TQ, DSA_TK), va_ref[0, pl.ds(k0, DSA_TK), :]

    def capped(j, carry, masked):
        seen, acc = carry
        k0 = pl.multiple_of(j * DSA_TK, DSA_TK)
        s = s_ref[pl.ds(k0, DSA_TK), :]
        tie = s == thr
        tie_b = jnp.where(tie, 1.0, 0.0).astype(BF16)
        ranks = []
        for b in range(n_blocks):
            local = _dot(tri_lo, tie_b[b * LANES:(b + 1) * LANES, :])
            ranks.append(local + seen)
            seen = seen + local[LANES - 1:LANES, :]
        sel = (s > thr) | (tie & (jnp.concatenate(ranks, axis=0) <= n_ties))
        if masked:
            sel = sel & (k0 + key >= pad) & (k0 + key <= qry)
        keep = jnp.where(sel, 1.0, 0.0).astype(BF16).T
        logits = _dot_nt(q_all, kk_ref[0, pl.ds(k0, DSA_TK), :])
        p = jnp.exp2(logits - reach).astype(BF16).reshape(N_HEADS, DSA_TQ, DSA_TK) * keep[None]
        return seen, acc + _dot(p.reshape(N_HEADS * DSA_TQ, DSA_TK), va_ref[0, pl.ds(k0, DSA_TK), :])

    def online(j, carry, masked):
        seen, logits, va_t = selected_logits(j, carry[0], masked)
        return (seen,) + _softmax_step(logits, carry[1], carry[2], va_t)

    def sweep(step, *stats):
        carry = stats + (jnp.zeros((N_HEADS * DSA_TQ, LANES), F32),)
        return _edge_tiles(step, carry, last)[-1]

    acc = lax.cond(2.0 * jnp.max(reach) <= SAFE_LOG2_SPAN,
                   functools.partial(sweep, capped, jnp.zeros((1, DSA_TQ), F32)),
                   functools.partial(sweep, online, jnp.zeros((DSA_TQ, 1), F32),
                                     jnp.full((N_HEADS * DSA_TQ, 1), NEG, F32)))
    o_ref[0] = jnp.concatenate(
        _normalise_heads([acc[h * DSA_TQ:(h + 1) * DSA_TQ] for h in range(N_HEADS)]),
        axis=1).astype(o_ref.dtype)


def _dsa_attention(dq, dkk, dva, iq, ikk, small_t, key_norms, k_top, pad):
    bsz, t_work, _ = dq.shape
    tri = jnp.asarray(np.triu(np.ones((LANES, LANES), np.float32)), dtype=BF16)
    tile = lambda width: pl.BlockSpec((1, DSA_TQ, width), lambda b, i: (b, i, 0))
    full = pl.BlockSpec((1, t_work, LANES), lambda b, i: (b, 0, 0), pipeline_mode=pl.Buffered(1))
    return pl.pallas_call(
        functools.partial(_dsa_kernel, k_top, pad),
        grid=(bsz, t_work // DSA_TQ),
        in_specs=[tile(256), full, full, tile(256), full,
                  pl.BlockSpec((1, 8, DSA_TQ), lambda b, i: (b, 0, i)),
                  pl.BlockSpec((LANES, LANES), lambda b, i: (0, 0)),
                  pl.BlockSpec((1,) + key_norms.shape[1:], lambda b, i: (b, 0, 0, 0))],
        out_specs=tile(256),
        out_shape=jax.ShapeDtypeStruct((bsz, t_work, 256), BF16),
        scratch_shapes=[pltpu.VMEM((t_work, DSA_TQ), F32)],
        compiler_params=pltpu.CompilerParams(
            dimension_semantics=("parallel", "parallel"), vmem_limit_bytes=VMEM_LIMIT),
        name="dsa_attention",
    )(dq, dkk, dva, iq, ikk, small_t, tri, key_norms)


FOX_TQ = 512
FOX_TK = 512
CUM_T = 256


def _fox_prep_kernel(x_ref, tri_ref, place_ref, k_ref, xt_ref, kb_ref, carry_ref):
    @pl.when(pl.program_id(1) == 0)
    def _():
        carry_ref[...] = jnp.zeros_like(carry_ref)

    tri = tri_ref[...]
    x = x_ref[0]
    h1, h2, h3 = _split3(x)
    c = _dot(tri, h1) + _dot(tri, h2) + _dot(tri, h3) + carry_ref[0:1, :]
    carry_ref[...] = jnp.broadcast_to(c[CUM_T - 1:CUM_T, :], carry_ref.shape)
    xt_ref[0] = x.T[0:8, :]
    terms = _split3(c * -LOG2E)
    bias = sum(_dot(terms[t], place_ref[t]) for t in range(BIAS_TERMS))
    lane = lax.broadcasted_iota(jnp.int32, bias.shape, 1) % LANES
    kb_ref[0] = jnp.where((lane >= HEAD_DIM) & (lane < HEAD_DIM + BIAS_TERMS),
                          bias.astype(BF16), k_ref[0])


def _fox_prep(small, fk):
    bsz, t_work, width = fk.shape
    tri = jnp.asarray(np.tril(np.ones((CUM_T, CUM_T), np.float32)), dtype=BF16)
    place = np.zeros((BIAS_TERMS, LANES, width), np.float32)
    for t in range(BIAS_TERMS):
        for h in range(N_HEADS):
            place[t, SM_FF + h, h * LANES + HEAD_DIM + t] = 1.0
    rows = lambda w: pl.BlockSpec((1, CUM_T, w), lambda b, j: (b, j, 0))
    return pl.pallas_call(
        _fox_prep_kernel,
        grid=(bsz, t_work // CUM_T),
        in_specs=[rows(LANES), pl.BlockSpec((CUM_T, CUM_T), lambda b, j: (0, 0)),
                  pl.BlockSpec((BIAS_TERMS, LANES, width), lambda b, j: (0, 0, 0)), rows(width)],
        out_specs=[pl.BlockSpec((1, 8, CUM_T), lambda b, j: (b, 0, j)), rows(width)],
        out_shape=[jax.ShapeDtypeStruct((bsz, 8, t_work), F32),
                   jax.ShapeDtypeStruct(fk.shape, BF16)],
        scratch_shapes=[pltpu.VMEM((8, LANES), F32)],
        compiler_params=pltpu.CompilerParams(dimension_semantics=("parallel", "arbitrary")),
        name="fox_prep",
    )(small, tri, jnp.asarray(place, dtype=BF16), fk)


def _logit_reach(q, kmax_sq):
    qf = q.astype(F32)
    return jnp.sqrt(jnp.sum(qf * qf, axis=-1, keepdims=True) * kmax_sq) * NORM_MARGIN


def _fox_kernel(pad, q_ref, k_ref, v_ref, kn_ref, o_ref):
    i = pl.program_id(1)
    q0 = pl.multiple_of(i * FOX_TQ, FOX_TQ)
    last = q0 // FOX_TK
    row = q0 + lax.broadcasted_iota(jnp.int32, (FOX_TQ, FOX_TK), 0)
    col = lax.broadcasted_iota(jnp.int32, (FOX_TQ, FOX_TK), 1)
    heads = [slice(h * LANES, (h + 1) * LANES) for h in range(N_HEADS)]

    lane = lax.broadcasted_iota(jnp.int32, (FOX_TQ, LANES), 1)
    bias_lanes = (lane >= HEAD_DIM) & (lane < HEAD_DIM + BIAS_TERMS)
    kmax_sq = jnp.max(kn_ref[0], axis=0)
    caps, span = [], jnp.float32(0.0)
    for h in range(N_HEADS):
        q_h = q_ref[0, :, heads[h]]
        reach = _logit_reach(jnp.where(lane < HEAD_DIM, q_h, jnp.zeros_like(q_h)),
                             kmax_sq[KN_FOX + h:KN_FOX + h + 1, 0:1])
        own = k_ref[0, pl.ds(q0, FOX_TQ), heads[h]].astype(F32)
        caps.append(reach + jnp.sum(jnp.where(bias_lanes, own, 0.0), axis=-1, keepdims=True))
        span = jnp.maximum(span, 2.0 * jnp.max(reach))

    def logits(j, h, masked):
        k0 = pl.multiple_of(j * FOX_TK, FOX_TK)
        s = _dot_nt(q_ref[0, :, heads[h]], k_ref[0, pl.ds(k0, FOX_TK), heads[h]])
        return jnp.where((k0 + col >= pad) & (k0 + col <= row), s, NEG) if masked else s

    def values(j, h):
        return v_ref[0, pl.ds(pl.multiple_of(j * FOX_TK, FOX_TK), FOX_TK), heads[h]]

    def capped(j, accs, masked):
        return tuple(accs[h] + _dot(jnp.exp2(logits(j, h, masked) - caps[h]).astype(BF16), values(j, h))
                     for h in range(N_HEADS))

    def online(j, carry, masked):
        return tuple(_softmax_step(logits(j, h, masked), *carry[h], values(j, h))
                     for h in range(N_HEADS))

    def capped_sweep():
        accs = tuple(jnp.zeros((FOX_TQ, LANES), F32) for _ in range(N_HEADS))
        return _edge_tiles(capped, accs, last)

    def online_sweep():
        carry = tuple((jnp.full((FOX_TQ, 1), NEG, F32), jnp.zeros((FOX_TQ, LANES), F32))
                      for _ in range(N_HEADS))
        return tuple(acc for _, acc in _edge_tiles(online, carry, last))

    accs = lax.cond(span <= SAFE_LOG2_SPAN, capped_sweep, online_sweep)
    o_ref[0] = jnp.concatenate(_normalise_heads(list(accs)), axis=1).astype(o_ref.dtype)


def _fox_attention(fqa, fkb, fva, key_norms, pad):
    bsz, t_work, width = fqa.shape
    full = pl.BlockSpec((1, t_work, width), lambda b, i: (b, 0, 0), pipeline_mode=pl.Buffered(1))
    return pl.pallas_call(
        functools.partial(_fox_kernel, pad),
        grid=(bsz, t_work // FOX_TQ),
        in_specs=[pl.BlockSpec((1, FOX_TQ, width), lambda b, i: (b, i, 0)), full, full,
                  pl.BlockSpec((1,) + key_norms.shape[1:], lambda b, i: (b, 0, 0, 0))],
        out_specs=pl.BlockSpec((1, FOX_TQ, 2 * LANES), lambda b, i: (b, i, 0)),
        out_shape=jax.ShapeDtypeStruct((bsz, t_work, 2 * LANES), BF16),
        compiler_params=pltpu.CompilerParams(
            dimension_semantics=("parallel", "parallel"), vmem_limit_bytes=VMEM_LIMIT),
        name="fox_attention",
    )(fqa, fkb, fva, key_norms)


GLA_NSUB = GLA_CHUNK // GLA_SUB


def _gla_kernel(q_ref, k_ref, v_ref, g_ref, la_ref, tri_ref, e_ref, gn_ref, o_ref, st_ref):
    @pl.when(pl.program_id(1) == 0)
    def _():
        st_ref[...] = jnp.zeros_like(st_ref)

    tri = tri_ref[...]
    emat = e_ref[...]
    lane = lax.broadcasted_iota(jnp.int32, (GLA_CHUNK, LANES), 1)
    rowblk = lax.broadcasted_iota(jnp.int32, (GLA_CHUNK, LANES), 0) // GLA_SUB
    tblk = lax.broadcasted_iota(jnp.int32, (GLA_CHUNK, GLA_CHUNK), 0) // GLA_SUB
    sblk = lax.broadcasted_iota(jnp.int32, (GLA_CHUNK, GLA_CHUNK), 1) // GLA_SUB
    trow = lax.broadcasted_iota(jnp.int32, (GLA_SUB, 256), 0)

    def chunk(c, carry):
        r0 = pl.multiple_of(c * GLA_CHUNK, GLA_CHUNK)
        rows = pl.ds(r0, GLA_CHUNK)
        h1, h2, h3 = _split3(la_ref[0, rows, :])
        b = _dot(tri, h1) + _dot(tri, h2) + _dot(tri, h3)
        q = q_ref[0, rows, :]
        k = k_ref[0, rows, :]
        v = v_ref[0, rows, :]
        b_last = b[GLA_CHUNK - 1:GLA_CHUNK, :]
        qd = q * jnp.exp2(b)
        kd = (k * jnp.exp2(b_last - b)).astype(BF16)
        starts = [jnp.zeros((1, 256), F32)] + [b[GLA_SUB * i - 1:GLA_SUB * i, :]
                                               for i in range(1, GLA_NSUB)]
        bsel = jnp.concatenate([jnp.broadcast_to(s, (GLA_SUB, 256)) for s in starts], axis=0)
        qn = q * jnp.exp2(b - bsel)

        diag = []
        for i in range(GLA_NSUB):
            rs = slice(GLA_SUB * i, GLA_SUB * (i + 1))
            b_i, q_i, k_i = b[rs], q[rs], k[rs]
            v_i = v[rs].astype(F32)
            ps = []
            for s in range(GLA_SUB):
                d = jnp.exp2(jnp.minimum(b_i - b_i[s:s + 1], 0.0))
                ps.append(jnp.where(trow >= s, q_i * d * k_i[s:s + 1], 0.0).astype(BF16))
            r = _dot(jnp.concatenate(ps, axis=0), emat)
            od = r[0:GLA_SUB] * v_i[0:1]
            for s in range(1, GLA_SUB):
                od = od + r[GLA_SUB * s:GLA_SUB * (s + 1)] * v_i[s:s + 1]
            diag.append(od)
        o_diag = jnp.concatenate(diag, axis=0)

        for slab in range(2):
            ls = slice(slab * LANES, (slab + 1) * LANES)
            qn_s, k_s, b_s = qn[:, ls], k[:, ls], b[:, ls]
            khat = jnp.concatenate(
                [(k_s * jnp.exp2(jnp.minimum(starts[i][:, ls] - b_s, 0.0))).astype(BF16)
                 for i in range(1, GLA_NSUB)], axis=1)
            for half in range(2):
                head = 2 * slab + half
                hs = slice(head * GLA_DV, (head + 1) * GLA_DV)
                in_head = (lane < HEAD_DIM) if half == 0 else (lane >= HEAD_DIM)
                qm = jnp.where(in_head, qn_s, 0.0)
                qhat = jnp.concatenate([jnp.where(rowblk == i, qm, 0.0).astype(BF16)
                                        for i in range(1, GLA_NSUB)], axis=1)
                att = jnp.where(sblk < tblk, _dot_nt(qhat, khat), 0.0)
                v_h = v[:, hs]
                st = st_ref[head]
                o = (_dot_nt(jnp.where(in_head, qd[:, ls], 0.0).astype(BF16), st.astype(BF16))
                     + _dot(att.astype(BF16), v_h) + o_diag[:, hs])
                st_ref[head] = st * jnp.exp2(b_last[:, ls]) + _dot_tn(v_h, kd[:, ls])
                y = o * lax.rsqrt(jnp.mean(o * o, axis=-1, keepdims=True) + EPS) * gn_ref[:, hs]
                o_ref[0, rows, hs] = (y * _silu(g_ref[0, rows, hs])).astype(o_ref.dtype)
        return carry

    lax.fori_loop(0, q_ref.shape[1] // GLA_CHUNK, chunk, 0)


def _gla(gq, gk, gv, gg, la, gain):
    bsz, t_work, _ = gq.shape
    tg = _row_tile(t_work)
    tri = jnp.asarray(np.tril(np.ones((GLA_CHUNK, GLA_CHUNK), np.float32)), dtype=BF16)
    emat = jnp.asarray(
        (np.arange(256)[:, None] // HEAD_DIM == np.arange(512)[None, :] // GLA_DV).astype(np.float32),
        dtype=BF16)
    rows = lambda width: pl.BlockSpec((1, tg, width), lambda b, j: (b, j, 0))
    const = lambda shape: pl.BlockSpec(shape, lambda b, j: (0,) * len(shape))
    return pl.pallas_call(
        _gla_kernel,
        grid=(bsz, t_work // tg),
        in_specs=[rows(256), rows(256), rows(512), rows(512), rows(256),
                  const((GLA_CHUNK, GLA_CHUNK)), const((256, 512)), const((1, 512))],
        out_specs=rows(512),
        out_shape=jax.ShapeDtypeStruct((bsz, t_work, 512), BF16),
        scratch_shapes=[pltpu.VMEM((N_HEADS, GLA_DV, LANES), F32)],
        compiler_params=pltpu.CompilerParams(
            dimension_semantics=("parallel", "arbitrary"), vmem_limit_bytes=VMEM_LIMIT),
        name="gla",
    )(gq, gk, gv, gg, la, tri, emat, gain)


FFN_CHUNK = 256


def _ffn_kernel(h_ref, oa_ref, ob_ref, oc_ref, wo_ref, fn_ref, wgu_ref, wd_ref, out_ref):
    h1 = (h_ref[0] + _dot(oa_ref[0], wo_ref[0:256, :]) + _dot(ob_ref[0], wo_ref[256:512, :])
          + _dot(oc_ref[0], wo_ref[512:1024, :]))
    ms = jnp.mean(h1 * h1, axis=-1, keepdims=True)
    f = (h1 * lax.rsqrt(ms + EPS) * fn_ref[...]).astype(BF16)
    out_ref[0] = h1
    for c in range(0, D_FF, FFN_CHUNK):
        gate = _dot(f, wgu_ref[:, c:c + FFN_CHUNK])
        up = _dot(f, wgu_ref[:, D_FF + c:D_FF + c + FFN_CHUNK])
        out_ref[0] += _dot((_silu(gate) * up).astype(BF16), wd_ref[c:c + FFN_CHUNK, :])


def _outproj_ffn(h, oa, ob, oc, wo, fn, wgu, wd):
    bsz, t_work, _ = h.shape
    tm = _row_tile(t_work)
    rows = lambda width: pl.BlockSpec((1, tm, width), lambda b, j: (b, j, 0))
    const = lambda shape: pl.BlockSpec(shape, lambda b, j: (0,) * len(shape),
                                       pipeline_mode=pl.Buffered(1))
    return pl.pallas_call(
        _ffn_kernel,
        grid=(bsz, t_work // tm),
        in_specs=[rows(D_MODEL), rows(256), rows(256), rows(512), const((D_MODEL, D_MODEL)),
                  const((1, D_MODEL)), const((D_MODEL, 2 * D_FF)), const((D_FF, D_MODEL))],
        out_specs=rows(D_MODEL),
        out_shape=jax.ShapeDtypeStruct(h.shape, F32),
        compiler_params=pltpu.CompilerParams(
            dimension_semantics=("parallel", "parallel"), vmem_limit_bytes=VMEM_LIMIT),
        name="outproj_ffn",
    )(h, oa, ob, oc, wo, fn, wgu, wd)


def kernel(x, meta_tokens, attn_norm, w_in, dsa_q_norm, dsa_k_norm, fox_q_norm, fox_k_norm,
           fox_f_bias, gla_gate_w2, gla_gate_b, gla_out_norm, w_out, ffn_norm, w_gate_up, w_down):
    bsz, seq, _ = x.shape
    n_tok = N_META + seq
    t_work = _work_len(n_tok)
    pad = t_work - n_tok
    meta = jnp.broadcast_to(meta_tokens[None].astype(x.dtype), (bsz, N_META, D_MODEL))
    h = jnp.concatenate([jnp.zeros((bsz, pad, D_MODEL), x.dtype), meta, x], axis=1)
    cos_t, sin_t = _rope_tables(t_work, pad)
    g256 = _group_matrix()
    k_top = min(TOPK_MAX, seq // 4)
    for l in range(w_in.shape[0]):
        an, w_perm, vecs, w2_pad, sm_bias = _prep_layer_params(
            l, attn_norm, w_in, dsa_q_norm, dsa_k_norm, fox_q_norm, fox_k_norm, fox_f_bias,
            gla_gate_w2, gla_gate_b)
        (dq, dkk, dva, iq, ikk, fq, fk, fva, gq, gk, gv, gg, la, small, key_norms) = _inproj(
            h, an, w_perm, cos_t, sin_t, g256, vecs, w2_pad, sm_bias, pad)
        small_t, fkb = _fox_prep(small, fk)
        oa = _dsa_attention(dq, dkk, dva, iq, ikk, small_t, key_norms, k_top, pad)
        ob = _fox_attention(fq, fkb, fva, key_norms, pad)
        oc = _gla(gq, gk, gv, gg, la, jnp.tile(gla_out_norm[l], N_HEADS)[None, :])
        h = _outproj_ffn(h, oa, ob, oc, w_out[l].astype(BF16), ffn_norm[l][None, :],
                         w_gate_up[l].astype(BF16), w_down[l].astype(BF16))
    return h[:, pad + N_META:]
```

```python
import functools

import numpy as np
import jax
import jax.numpy as jnp
from jax import lax
from jax.experimental import pallas as pl
from jax.experimental.pallas import tpu as pltpu

F32 = jnp.float32
BF16 = jnp.bfloat16

D_MODEL = 1024
HEAD_DIM = 64
N_META = 16
ROPE_THETA = 500000.0
ROPE_DIM = HEAD_DIM // 4
ROPE_HALF = ROPE_DIM // 2
NEG = -1e30
EPS = 1e-6

N_HEADS = 4
TOPK_MAX = 256
GLA_DV = 128
GLA_RANK = 16
GLA_TAU = 16.0
GLA_CHUNK = 64
GLA_SUB = 16
D_FF = 2816

LOG2E = 1.4426950408889634
BIAS_TERMS = 3
KN_DSA, KN_FOX = 0, 1
NORM_MARGIN = 1.01
SAFE_LOG2_SPAN = 100.0
LANES = 128
SEQ_ALIGN = 512
VMEM_LIMIT = 56 * 1024 * 1024

_SLABS = (("dq", 256), ("dkk", 128), ("dva", 128), ("iq", 256), ("ikk", 128),
          ("fq", 256), ("fk", 256), ("fv", 256), ("gq", 256), ("gk", 256),
          ("gv", 512), ("gg", 512), ("small", 128))
_SLAB_OFF = {}
_off = 0
for _name, _width in _SLABS:
    _SLAB_OFF[_name] = (_off, _width)
    _off += _width
N_PROJ = _off
SM_IW, SM_FF, SM_GLR = 0, 4, 8


def _work_len(n_tok):
    return -(-n_tok // SEQ_ALIGN) * SEQ_ALIGN


def _row_tile(t_work):
    for cand in (768, 640, 512):
        if t_work % cand == 0:
            return cand
    raise ValueError(f"unsupported working length {t_work}")


def _dot(a, b):
    return jnp.dot(a, b, preferred_element_type=F32)


def _dot_nt(a, b):
    return lax.dot_general(a, b, (((1,), (1,)), ((), ())), preferred_element_type=F32)


def _dot_tn(a, b):
    return lax.dot_general(a, b, (((0,), (0,)), ((), ())), preferred_element_type=F32)


def _split3(x):
    h1 = x.astype(BF16)
    r1 = x - h1.astype(F32)
    h2 = r1.astype(BF16)
    h3 = (r1 - h2.astype(F32)).astype(BF16)
    return h1, h2, h3


def _log_sigmoid(x):
    return jnp.minimum(x, 0.0) - jnp.log1p(jnp.exp(-jnp.abs(x)))


def _silu(x):
    return x / (1.0 + jnp.exp(-x))


def _group_rms(y, gmat, gain):
    yy = y * y
    hi = yy.astype(BF16)
    lo = (yy - hi.astype(F32)).astype(BF16)
    ss = _dot(hi, gmat) + _dot(lo, gmat)
    return y * lax.rsqrt(ss * (1.0 / HEAD_DIM) + EPS) * gain


def _rope(y, cos, sin):
    width = y.shape[-1]
    lane = lax.broadcasted_iota(jnp.int32, y.shape, 1) % HEAD_DIM
    upper = pltpu.roll(y, width - ROPE_HALF, axis=1)
    lower = pltpu.roll(y, ROPE_HALF, axis=1)
    partner = jnp.where(lane < ROPE_HALF, upper, lower)
    return y * cos + partner * sin


def _inproj_kernel(x_ref, an_ref, w_ref, cos_ref, sin_ref, g256_ref, vec_ref, w2_ref, sm_ref,
                   dq_ref, dkk_ref, dva_ref, iq_ref, ikk_ref, fq_ref, fk_ref, fva_ref,
                   gq_ref, gk_ref, gv_ref, gg_ref, la_ref, small_ref, kn_ref):
    x = x_ref[0]
    ms = jnp.mean(x * x, axis=-1, keepdims=True)
    a = (x * lax.rsqrt(ms + EPS) * an_ref[...]).astype(BF16)

    def proj(name):
        off, width = _SLAB_OFF[name]
        return _dot(a, w_ref[:, off:off + width])

    def ones_in_upper_half(y):
        lane = lax.broadcasted_iota(jnp.int32, y.shape, 1) % LANES
        return jnp.where(lane < HEAD_DIM, y, 1.0)

    cos = cos_ref[...]
    sin = sin_ref[...]
    g256 = g256_ref[...]
    g128 = g256[:LANES, :LANES]
    dqn, dkn, fqn, fkn = vec_ref[0:1, :], vec_ref[1:2, :LANES], vec_ref[2:3, :], vec_ref[3:4, :]
    scale = HEAD_DIM ** -0.5

    def max_sq_norm(k):
        kf = k.astype(F32)
        return jnp.max(_dot((kf * kf).astype(BF16), g128), axis=0, keepdims=True)

    dq_ref[0] = (_rope(_group_rms(proj("dq"), g256, dqn), cos, sin) * (scale * LOG2E)).astype(BF16)
    dkk = _rope(_group_rms(proj("dkk"), g128, dkn), cos[:, :LANES], sin[:, :LANES]).astype(BF16)
    dkk_ref[0] = dkk
    key_norms = [max_sq_norm(dkk)]
    dva_ref[0] = ones_in_upper_half(proj("dva")).astype(BF16)
    iq_ref[0] = (_rope(proj("iq"), cos, sin) * scale).astype(BF16)
    ikk_ref[0] = _rope(proj("ikk"), cos[:, :LANES], sin[:, :LANES]).astype(BF16)
    fq, fk, fv = proj("fq"), proj("fk"), proj("fv")
    lane = lax.broadcasted_iota(jnp.int32, (fq.shape[0], LANES), 1)
    in_head = lane < HEAD_DIM
    q_pad = jnp.where(lane < HEAD_DIM + BIAS_TERMS, 1.0, 0.0)
    for pair in range(N_HEADS // 2):
        ps = slice(pair * LANES, (pair + 1) * LANES)
        q_pair = _group_rms(fq[:, ps], g128, fqn[:, :LANES]) * (scale * LOG2E)
        k_pair = _group_rms(fk[:, ps], g128, fkn[:, :LANES])
        v_pair = fv[:, ps]
        for odd in range(2):
            hs = slice((2 * pair + odd) * LANES, (2 * pair + odd + 1) * LANES)
            down = (lambda y: pltpu.roll(y, HEAD_DIM, axis=1)) if odd else (lambda y: y)
            fq_ref[0, :, hs] = jnp.where(in_head, down(q_pair), q_pad).astype(BF16)
            k_h = jnp.where(in_head, down(k_pair), 0.0).astype(BF16)
            fk_ref[0, :, hs] = k_h
            key_norms.append(max_sq_norm(k_h))
            fva_ref[0, :, hs] = jnp.where(in_head, down(v_pair), 1.0).astype(BF16)
    kn_ref[0, 0] = jnp.concatenate(key_norms + [jnp.zeros((8 - len(key_norms), LANES), F32)], axis=0)
    gq_ref[0] = proj("gq") * scale
    gk_ref[0] = proj("gk")
    gv_ref[0] = proj("gv").astype(BF16)
    gg_ref[0] = proj("gg")

    small = proj("small")
    lane = lax.broadcasted_iota(jnp.int32, small.shape, 1)
    small_ref[0] = jnp.where(lane < SM_FF, small * (N_HEADS ** -0.5),
                             _log_sigmoid(small + sm_ref[0:1, :]))
    gate = _dot(small.astype(BF16), w2_ref[...]) + vec_ref[4:5, :]
    la_ref[0] = _log_sigmoid(gate) * (LOG2E / GLA_TAU)


def _inproj(h, an, w_perm, cos_t, sin_t, g256, vecs, w2_pad, sm_bias):
    bsz, t_work, _ = h.shape
    tm = _row_tile(t_work)
    grid = (bsz, t_work // tm)

    def rows(width, dtype):
        return (jax.ShapeDtypeStruct((bsz, t_work, width), dtype),
                pl.BlockSpec((1, tm, width), lambda b, j: (b, j, 0)))

    outs = [rows(256, BF16), rows(128, BF16), rows(128, BF16), rows(256, BF16), rows(128, BF16),
            rows(512, BF16), rows(512, BF16), rows(512, BF16), rows(256, F32), rows(256, F32),
            rows(512, BF16), rows(512, F32), rows(256, F32), rows(128, F32),
            (jax.ShapeDtypeStruct((bsz, t_work // tm, 8, LANES), F32),
             pl.BlockSpec((1, 1, 8, LANES), lambda b, j: (b, j, 0, 0)))]
    const = lambda shape: pl.BlockSpec(shape, lambda b, j: (0,) * len(shape))
    return pl.pallas_call(
        _inproj_kernel,
        grid=grid,
        in_specs=[pl.BlockSpec((1, tm, D_MODEL), lambda b, j: (b, j, 0)),
                  const((1, D_MODEL)), const((D_MODEL, N_PROJ)),
                  pl.BlockSpec((tm, 256), lambda b, j: (j, 0)),
                  pl.BlockSpec((tm, 256), lambda b, j: (j, 0)),
                  const((256, 256)), const((8, 256)), const((LANES, 256)), const((8, LANES))],
        out_specs=[o[1] for o in outs],
        out_shape=[o[0] for o in outs],
        compiler_params=pltpu.CompilerParams(
            dimension_semantics=("parallel", "parallel"), vmem_limit_bytes=VMEM_LIMIT),
        name="inproj",
    )(h, an, w_perm, cos_t, sin_t, g256, vecs, w2_pad, sm_bias)


def _prep_layer_params(l, attn_norm, w_in, dsa_q_norm, dsa_k_norm, fox_q_norm, fox_k_norm,
                       fox_f_bias, gla_gate_w2, gla_gate_b):
    w = w_in[l]
    splits = np.cumsum([256, 64, 64, 256, 4, 64, 256, 256, 256, 4, 256, 256, 512, 512, 16])[:-1]
    (dq, dk, dv, iq, iw, ik, fq, fk, fv, ff, gq, gk, gv, gg, glr) = jnp.split(w, splits, axis=1)
    small = jnp.concatenate(
        [iw, ff, glr, jnp.zeros((D_MODEL, LANES - 4 - 4 - GLA_RANK), w.dtype)], axis=1)
    z64 = jnp.zeros((D_MODEL, HEAD_DIM), w.dtype)
    w_perm = jnp.concatenate([dq, dk, dk, dv, z64, iq, ik, ik, fq, fk, fv, gq, gk, gv, gg, small],
                             axis=1).astype(BF16)
    tile4 = lambda g: jnp.tile(g, N_HEADS)
    vecs = jnp.zeros((8, 256), F32)
    vecs = vecs.at[0].set(tile4(dsa_q_norm[l])).at[1].set(tile4(dsa_k_norm[l]))
    vecs = vecs.at[2].set(tile4(fox_q_norm[l])).at[3].set(tile4(fox_k_norm[l]))
    vecs = vecs.at[4].set(gla_gate_b[l])
    w2_pad = jnp.zeros((LANES, 256), F32).at[SM_GLR:SM_GLR + GLA_RANK].set(gla_gate_w2[l]).astype(BF16)
    sm_bias = jnp.zeros((8, LANES), F32).at[0, SM_FF:SM_FF + N_HEADS].set(fox_f_bias[l])
    return attn_norm[l][None, :], w_perm, vecs, w2_pad, sm_bias


def _rope_tables(t_work):
    inv = jnp.power(ROPE_THETA, -jnp.arange(ROPE_HALF, dtype=F32) * 2.0 / ROPE_DIM)
    ang = jnp.arange(t_work).astype(F32)[:, None] * inv[None, :]
    cos, sin = jnp.cos(ang), jnp.sin(ang)
    rest = HEAD_DIM - ROPE_DIM
    cos64 = jnp.concatenate([cos, cos, jnp.ones((t_work, rest), F32)], axis=1)
    sin64 = jnp.concatenate([-sin, sin, jnp.zeros((t_work, rest), F32)], axis=1)
    return jnp.tile(cos64, (1, N_HEADS)), jnp.tile(sin64, (1, N_HEADS))


def _group_matrix():
    idx = np.arange(256) // HEAD_DIM
    return jnp.asarray((idx[:, None] == idx[None, :]).astype(np.float32), dtype=BF16)


DSA_TQ = 256
DSA_TK = 512
KEY_NEG_INF = -2139095041
KEY_NEG_ZERO = -1
KEY_NEG_MIN_NORMAL = -8388609
NEG_MIN_NORMAL = -1.1754943508222875e-38
SEARCH_FEW = 4.0
SEARCH_MANY = 16.0
UNCHECKED_PROBES = 12
GUIDED_PROBES = 64
MAX_PROBES = GUIDED_PROBES + 40
COUNT_ROWS = 128


def _key_to_f32(key):
    bits = key ^ ((key >> 31) & 0x7FFFFFFF)
    return lax.bitcast_convert_type(bits, F32)


def _f32_to_key(value):
    bits = lax.bitcast_convert_type(value, jnp.int32)
    return bits ^ ((bits >> 31) & 0x7FFFFFFF)


def _head_lane_mask(shape, head):
    lane = lax.broadcasted_iota(jnp.int32, shape, 1)
    return (lane < HEAD_DIM) if head % 2 == 0 else (lane >= HEAD_DIM)


def _masked_heads(slabs):
    return [jnp.where(_head_lane_mask(slabs[h // 2].shape, h), slabs[h // 2],
                      jnp.zeros_like(slabs[h // 2])) for h in range(N_HEADS)]


def _softmax_step(s, m, acc, v_aug):
    m_new = jnp.maximum(m, jnp.max(s, axis=-1, keepdims=True))
    p = jnp.exp2(s - m_new)
    return m_new, acc * jnp.exp2(m - m_new) + _dot(p.astype(BF16), v_aug)


def _normalise_heads(accs):
    outs = [acc / pltpu.roll(acc, HEAD_DIM, axis=1) for acc in accs]
    lane = lax.broadcasted_iota(jnp.int32, outs[0].shape, 1)
    return [jnp.where(lane < HEAD_DIM, outs[2 * p], pltpu.roll(outs[2 * p + 1], HEAD_DIM, axis=1))
            for p in range(2)]


def _dsa_kernel(k_top, n_tok, *refs):
    o_ref = refs[-2]
    real = pl.program_id(1) * DSA_TQ < n_tok
    pl.when(real)(functools.partial(_dsa_tile, k_top, *refs))

    @pl.when(jnp.logical_not(real))
    def _():
        o_ref[...] = jnp.zeros_like(o_ref)


def _dsa_tile(k_top, dq_ref, kk_ref, va_ref, iq_ref, ik_ref, wt_ref, tri_ref, kn_ref, o_ref,
              s_ref):
    i = pl.program_id(1)
    q0 = i * DSA_TQ
    n_tiles = (q0 + DSA_TQ + DSA_TK - 1) // DSA_TK
    last = n_tiles - 1
    key = lax.broadcasted_iota(jnp.int32, (DSA_TK, DSA_TQ), 0)
    qry = q0 + lax.broadcasted_iota(jnp.int32, (DSA_TK, DSA_TQ), 1)
    head = lambda x, h: x[:, h * DSA_TQ:(h + 1) * DSA_TQ]

    iq_all = jnp.concatenate(
        _masked_heads([iq_ref[0, :, 0:LANES], iq_ref[0, :, LANES:2 * LANES]]), axis=0)
    w_h = [wt_ref[0, SM_IW + h:SM_IW + h + 1, :] for h in range(N_HEADS)]

    def score_tile(j, carry, causal):
        k0 = pl.multiple_of(j * DSA_TK, DSA_TK)
        dots = jnp.maximum(_dot_nt(ik_ref[0, pl.ds(k0, DSA_TK), :], iq_all), 0.0)
        s = w_h[0] * head(dots, 0)
        for h in range(1, N_HEADS):
            s = s + w_h[h] * head(dots, h)
        if causal:
            s = jnp.where(k0 + key <= qry, s, NEG)
        s_ref[pl.ds(k0, DSA_TK), :] = s
        row_max, above_zero, above_neg = carry
        for part in range(DSA_TK // COUNT_ROWS):
            chunk = s[part * COUNT_ROWS:(part + 1) * COUNT_ROWS]
            above_zero = above_zero + jnp.where(chunk > 0.0, 1.0, 0.0)
            above_neg = above_neg + jnp.where(chunk > NEG_MIN_NORMAL, 1.0, 0.0)
        return jnp.maximum(row_max, jnp.max(s, axis=0, keepdims=True)), above_zero, above_neg

    zeros = jnp.zeros((COUNT_ROWS, DSA_TQ), F32)
    row_max, above_zero, above_neg = score_tile(last, lax.fori_loop(
        0, last, functools.partial(score_tile, causal=False),
        (jnp.full((1, DSA_TQ), NEG, F32), zeros, zeros)), causal=True)

    kf = jnp.float32(k_top)

    def count_above(t):
        tb = jnp.broadcast_to(t, (COUNT_ROWS, DSA_TQ))

        def body(j, acc, span):
            k0 = pl.multiple_of(j * span, span)
            for part in range(span // COUNT_ROWS):
                s = s_ref[pl.ds(k0 + part * COUNT_ROWS, COUNT_ROWS), :]
                acc = acc + jnp.where(s > tb, 1.0, 0.0)
            return acc

        acc = lax.fori_loop(0, n_tiles // 2, functools.partial(body, span=2 * DSA_TK),
                            jnp.zeros((COUNT_ROWS, DSA_TQ), F32))
        acc = lax.fori_loop(2 * (n_tiles // 2), n_tiles, functools.partial(body, span=DSA_TK), acc)
        return jnp.sum(acc, axis=0, keepdims=True)

    def midpoint(lo, hi):
        return (lo >> 1) + (hi >> 1) + (lo & hi & 1)

    def converged(lo, hi):
        return (midpoint(lo, hi) == lo) | ((lo >= KEY_NEG_MIN_NORMAL) & (hi <= 0))

    col1 = lambda value, dtype: jnp.full((1, DSA_TQ), value, dtype)

    def absorb(state, probe, c):
        lo, hi, clo, chi, wlo, whi, side = state
        live = probe != lo
        up = live & (c >= kf)
        down = live & (c <= kf)
        wlo = jnp.where(down & (side < 0), wlo * 0.5, jnp.where(up, 1.0, wlo))
        whi = jnp.where(up & (side > 0), whi * 0.5, jnp.where(down, 1.0, whi))
        side = jnp.where(up, 1, jnp.where(down, -1, side))
        lo, clo = jnp.where(up, probe, lo), jnp.where(up, c, clo)
        hi, chi = jnp.where(down, probe, hi), jnp.where(down, c, chi)
        return lo, hi, clo, chi, wlo, whi, side

    def probe_once(it, state):
        lo, hi, clo, chi, wlo, whi, side = state
        f_lo, f_hi = _key_to_f32(lo), _key_to_f32(hi)
        target = kf - 0.5
        log_count = lambda c: jnp.log2(jnp.maximum(c, 0.25))
        many = clo - chi > SEARCH_MANY
        g_lo = jnp.where(many, log_count(clo) - np.log2(k_top - 0.5), clo - target) * wlo
        g_hi = jnp.where(many, np.log2(k_top - 0.5) - log_count(chi), target - chi) * whi
        halve = (clo - chi <= SEARCH_FEW) | (col1(it % 8, jnp.int32) == 7)
        guess = _f32_to_key(f_lo + (f_hi - f_lo) * jnp.where(halve, 0.5, g_lo / (g_lo + g_hi)))
        guided = col1(it, jnp.int32) < GUIDED_PROBES
        probe = jnp.where((guess > lo) & (guess < hi) & guided, guess, midpoint(lo, hi))
        probe = jnp.where(converged(lo, hi), lo, probe)
        return absorb(state, probe, count_above(_key_to_f32(probe)))

    def search_cond(carry):
        it, pending = carry[0], carry[1]
        return (pending > 0) & (it < MAX_PROBES)

    def search_body(carry):
        it, state = carry[0], carry[2]
        state = probe_once(it + 1, probe_once(it, state))
        return it + 2, jnp.max(jnp.where(converged(state[0], state[1]), 0, 1)), state

    n_swept = (n_tiles * DSA_TK).astype(F32)
    state = (col1(KEY_NEG_INF, jnp.int32), _f32_to_key(row_max),
             jnp.broadcast_to(n_swept, (1, DSA_TQ)), col1(0.0, F32), col1(1.0, F32), col1(1.0, F32),
             col1(0, jnp.int32))
    for fixed, counts in ((KEY_NEG_ZERO, above_zero), (KEY_NEG_MIN_NORMAL, above_neg)):
        inside = (state[0] < fixed) & (fixed < state[1])
        state = absorb(state, jnp.where(inside, fixed, state[0]),
                       jnp.sum(counts, axis=0, keepdims=True))
    state = lax.fori_loop(0, UNCHECKED_PROBES, probe_once, state)
    state = lax.while_loop(search_cond, search_body,
                           (jnp.int32(UNCHECKED_PROBES), jnp.int32(1), state))[2]
    thr = _key_to_f32(state[1])
    n_ties = kf - state[3]

    to_column = lambda r: jnp.broadcast_to(r, (8, DSA_TQ)).T[:, 0:1]
    thr_c, ties_c = to_column(thr), to_column(n_ties)
    q_all = jnp.concatenate(
        _masked_heads([dq_ref[0, :, 0:LANES], dq_ref[0, :, LANES:2 * LANES]]), axis=0)
    tri = tri_ref[...]
    tri_lo = jnp.where(lax.broadcasted_iota(jnp.int32, (LANES, LANES), 0)
                       >= lax.broadcasted_iota(jnp.int32, (LANES, LANES), 1), 1.0, 0.0).astype(BF16)
    qrow = q0 + lax.broadcasted_iota(jnp.int32, (DSA_TQ, DSA_TK), 0)
    kcol = lax.broadcasted_iota(jnp.int32, (DSA_TQ, DSA_TK), 1)
    n_blocks = DSA_TK // LANES

    reach = _logit_reach(q_all, jnp.max(kn_ref[0], axis=0)[KN_DSA:KN_DSA + 1, 0:1])

    def selected_logits(j, seen, causal):
        k0 = pl.multiple_of(j * DSA_TK, DSA_TK)
        s = s_ref[pl.ds(k0, DSA_TK), :].T
        tie = s == thr_c
        tie_b = jnp.where(tie, 1.0, 0.0).astype(BF16)
        local = [_dot(tie_b[:, b * LANES:(b + 1) * LANES], tri) for b in range(n_blocks)]
        ranks = []
        for b in range(n_blocks):
            ranks.append(local[b] + seen)
            seen = seen + local[b][:, LANES - 1:LANES]
        sel = (s > thr_c) | (tie & (jnp.concatenate(ranks, axis=1) <= ties_c))
        if causal:
            sel = sel & (k0 + kcol <= qrow)
        logits = _dot_nt(q_all, kk_ref[0, pl.ds(k0, DSA_TK), :])
        logits = jnp.where(sel[None], logits.reshape(N_HEADS, DSA_TQ, DSA_TK), NEG)
        return seen, logits.reshape(N_HEADS * DSA_TQ, DSA_TK), va_ref[0, pl.ds(k0, DSA_TK), :]

    def capped(j, carry, causal):
        seen, acc = carry
        k0 = pl.multiple_of(j * DSA_TK, DSA_TK)
        s = s_ref[pl.ds(k0, DSA_TK), :]
        tie = s == thr
        tie_b = jnp.where(tie, 1.0, 0.0).astype(BF16)
        ranks = []
        for b in range(n_blocks):
            local = _dot(tri_lo, tie_b[b * LANES:(b + 1) * LANES, :])
            ranks.append(local + seen)
            seen = seen + local[LANES - 1:LANES, :]
        sel = (s > thr) | (tie & (jnp.concatenate(ranks, axis=0) <= n_ties))
        if causal:
            sel = sel & (k0 + key <= qry)
        keep = jnp.where(sel, 1.0, 0.0).astype(BF16).T
        logits = _dot_nt(q_all, kk_ref[0, pl.ds(k0, DSA_TK), :])
        p = jnp.exp2(logits - reach).astype(BF16).reshape(N_HEADS, DSA_TQ, DSA_TK) * keep[None]
        return seen, acc + _dot(p.reshape(N_HEADS * DSA_TQ, DSA_TK), va_ref[0, pl.ds(k0, DSA_TK), :])

    def online(j, carry, causal):
        seen, logits, va_t = selected_logits(j, carry[0], causal)
        return (seen,) + _softmax_step(logits, carry[1], carry[2], va_t)

    def sweep(step, *stats):
        carry = stats + (jnp.zeros((N_HEADS * DSA_TQ, LANES), F32),)
        carry = lax.fori_loop(0, last, functools.partial(step, causal=False), carry)
        return step(last, carry, causal=True)[-1]

    acc = lax.cond(2.0 * jnp.max(reach) <= SAFE_LOG2_SPAN,
                   functools.partial(sweep, capped, jnp.zeros((1, DSA_TQ), F32)),
                   functools.partial(sweep, online, jnp.zeros((DSA_TQ, 1), F32),
                                     jnp.full((N_HEADS * DSA_TQ, 1), NEG, F32)))
    o_ref[0] = jnp.concatenate(
        _normalise_heads([acc[h * DSA_TQ:(h + 1) * DSA_TQ] for h in range(N_HEADS)]),
        axis=1).astype(o_ref.dtype)


def _dsa_attention(dq, dkk, dva, iq, ikk, small_t, key_norms, k_top, n_tok):
    bsz, t_work, _ = dq.shape
    tri = jnp.asarray(np.triu(np.ones((LANES, LANES), np.float32)), dtype=BF16)
    tile = lambda width: pl.BlockSpec((1, DSA_TQ, width), lambda b, i: (b, i, 0))
    full = pl.BlockSpec((1, t_work, LANES), lambda b, i: (b, 0, 0), pipeline_mode=pl.Buffered(1))
    return pl.pallas_call(
        functools.partial(_dsa_kernel, k_top, n_tok),
        grid=(bsz, t_work // DSA_TQ),
        in_specs=[tile(256), full, full, tile(256), full,
                  pl.BlockSpec((1, 8, DSA_TQ), lambda b, i: (b, 0, i)),
                  pl.BlockSpec((LANES, LANES), lambda b, i: (0, 0)),
                  pl.BlockSpec((1,) + key_norms.shape[1:], lambda b, i: (b, 0, 0, 0))],
        out_specs=tile(256),
        out_shape=jax.ShapeDtypeStruct((bsz, t_work, 256), BF16),
        scratch_shapes=[pltpu.VMEM((t_work, DSA_TQ), F32)],
        compiler_params=pltpu.CompilerParams(
            dimension_semantics=("parallel", "parallel"), vmem_limit_bytes=VMEM_LIMIT),
        name="dsa_attention",
    )(dq, dkk, dva, iq, ikk, small_t, tri, key_norms)


FOX_TQ = 512
FOX_TK = 512
CUM_T = 256


def _fox_prep_kernel(x_ref, tri_ref, place_ref, k_ref, xt_ref, kb_ref, carry_ref):
    @pl.when(pl.program_id(1) == 0)
    def _():
        carry_ref[...] = jnp.zeros_like(carry_ref)

    tri = tri_ref[...]
    x = x_ref[0]
    h1, h2, h3 = _split3(x)
    c = _dot(tri, h1) + _dot(tri, h2) + _dot(tri, h3) + carry_ref[0:1, :]
    carry_ref[...] = jnp.broadcast_to(c[CUM_T - 1:CUM_T, :], carry_ref.shape)
    xt_ref[0] = x.T[0:8, :]
    terms = _split3(c * -LOG2E)
    bias = sum(_dot(terms[t], place_ref[t]) for t in range(BIAS_TERMS))
    lane = lax.broadcasted_iota(jnp.int32, bias.shape, 1) % LANES
    kb_ref[0] = jnp.where((lane >= HEAD_DIM) & (lane < HEAD_DIM + BIAS_TERMS),
                          bias.astype(BF16), k_ref[0])


def _fox_prep(small, fk):
    bsz, t_work, width = fk.shape
    tri = jnp.asarray(np.tril(np.ones((CUM_T, CUM_T), np.float32)), dtype=BF16)
    place = np.zeros((BIAS_TERMS, LANES, width), np.float32)
    for t in range(BIAS_TERMS):
        for h in range(N_HEADS):
            place[t, SM_FF + h, h * LANES + HEAD_DIM + t] = 1.0
    rows = lambda w: pl.BlockSpec((1, CUM_T, w), lambda b, j: (b, j, 0))
    return pl.pallas_call(
        _fox_prep_kernel,
        grid=(bsz, t_work // CUM_T),
        in_specs=[rows(LANES), pl.BlockSpec((CUM_T, CUM_T), lambda b, j: (0, 0)),
                  pl.BlockSpec((BIAS_TERMS, LANES, width), lambda b, j: (0, 0, 0)), rows(width)],
        out_specs=[pl.BlockSpec((1, 8, CUM_T), lambda b, j: (b, 0, j)), rows(width)],
        out_shape=[jax.ShapeDtypeStruct((bsz, 8, t_work), F32),
                   jax.ShapeDtypeStruct(fk.shape, BF16)],
        scratch_shapes=[pltpu.VMEM((8, LANES), F32)],
        compiler_params=pltpu.CompilerParams(dimension_semantics=("parallel", "arbitrary")),
        name="fox_prep",
    )(small, tri, jnp.asarray(place, dtype=BF16), fk)


def _logit_reach(q, kmax_sq):
    qf = q.astype(F32)
    return jnp.sqrt(jnp.sum(qf * qf, axis=-1, keepdims=True) * kmax_sq) * NORM_MARGIN


def _fox_kernel(q_ref, k_ref, v_ref, kn_ref, o_ref):
    i = pl.program_id(1)
    q0 = pl.multiple_of(i * FOX_TQ, FOX_TQ)
    n_full = q0 // FOX_TK
    row = q0 + lax.broadcasted_iota(jnp.int32, (FOX_TQ, FOX_TK), 0)
    col = n_full * FOX_TK + lax.broadcasted_iota(jnp.int32, (FOX_TQ, FOX_TK), 1)
    heads = [slice(h * LANES, (h + 1) * LANES) for h in range(N_HEADS)]

    lane = lax.broadcasted_iota(jnp.int32, (FOX_TQ, LANES), 1)
    bias_lanes = (lane >= HEAD_DIM) & (lane < HEAD_DIM + BIAS_TERMS)
    kmax_sq = jnp.max(kn_ref[0], axis=0)
    caps, span = [], jnp.float32(0.0)
    for h in range(N_HEADS):
        q_h = q_ref[0, :, heads[h]]
        reach = _logit_reach(jnp.where(lane < HEAD_DIM, q_h, jnp.zeros_like(q_h)),
                             kmax_sq[KN_FOX + h:KN_FOX + h + 1, 0:1])
        own = k_ref[0, pl.ds(q0, FOX_TQ), heads[h]].astype(F32)
        caps.append(reach + jnp.sum(jnp.where(bias_lanes, own, 0.0), axis=-1, keepdims=True))
        span = jnp.maximum(span, 2.0 * jnp.max(reach))

    def logits(j, h, diag):
        k0 = pl.multiple_of(j * FOX_TK, FOX_TK)
        s = _dot_nt(q_ref[0, :, heads[h]], k_ref[0, pl.ds(k0, FOX_TK), heads[h]])
        return jnp.where(col <= row, s, NEG) if diag else s

    def values(j, h):
        return v_ref[0, pl.ds(pl.multiple_of(j * FOX_TK, FOX_TK), FOX_TK), heads[h]]

    def capped(j, accs, diag):
        return tuple(accs[h] + _dot(jnp.exp2(logits(j, h, diag) - caps[h]).astype(BF16), values(j, h))
                     for h in range(N_HEADS))

    def online(j, carry, diag):
        return tuple(_softmax_step(logits(j, h, diag), *carry[h], values(j, h))
                     for h in range(N_HEADS))

    def capped_sweep():
        accs = tuple(jnp.zeros((FOX_TQ, LANES), F32) for _ in range(N_HEADS))
        accs = lax.fori_loop(0, n_full, functools.partial(capped, diag=False), accs)
        return capped(n_full, accs, diag=True)

    def online_sweep():
        carry = tuple((jnp.full((FOX_TQ, 1), NEG, F32), jnp.zeros((FOX_TQ, LANES), F32))
                      for _ in range(N_HEADS))
        carry = lax.fori_loop(0, n_full, functools.partial(online, diag=False), carry)
        return tuple(acc for _, acc in online(n_full, carry, diag=True))

    accs = lax.cond(span <= SAFE_LOG2_SPAN, capped_sweep, online_sweep)
    o_ref[0] = jnp.concatenate(_normalise_heads(list(accs)), axis=1).astype(o_ref.dtype)


def _fox_attention(fqa, fkb, fva, key_norms):
    bsz, t_work, width = fqa.shape
    full = pl.BlockSpec((1, t_work, width), lambda b, i: (b, 0, 0), pipeline_mode=pl.Buffered(1))
    return pl.pallas_call(
        _fox_kernel,
        grid=(bsz, t_work // FOX_TQ),
        in_specs=[pl.BlockSpec((1, FOX_TQ, width), lambda b, i: (b, i, 0)), full, full,
                  pl.BlockSpec((1,) + key_norms.shape[1:], lambda b, i: (b, 0, 0, 0))],
        out_specs=pl.BlockSpec((1, FOX_TQ, 2 * LANES), lambda b, i: (b, i, 0)),
        out_shape=jax.ShapeDtypeStruct((bsz, t_work, 2 * LANES), BF16),
        compiler_params=pltpu.CompilerParams(
            dimension_semantics=("parallel", "parallel"), vmem_limit_bytes=VMEM_LIMIT),
        name="fox_attention",
    )(fqa, fkb, fva, key_norms)


GLA_NSUB = GLA_CHUNK // GLA_SUB


def _gla_kernel(q_ref, k_ref, v_ref, g_ref, la_ref, tri_ref, e_ref, gn_ref, o_ref, st_ref):
    @pl.when(pl.program_id(1) == 0)
    def _():
        st_ref[...] = jnp.zeros_like(st_ref)

    tri = tri_ref[...]
    emat = e_ref[...]
    lane = lax.broadcasted_iota(jnp.int32, (GLA_CHUNK, LANES), 1)
    rowblk = lax.broadcasted_iota(jnp.int32, (GLA_CHUNK, LANES), 0) // GLA_SUB
    tblk = lax.broadcasted_iota(jnp.int32, (GLA_CHUNK, GLA_CHUNK), 0) // GLA_SUB
    sblk = lax.broadcasted_iota(jnp.int32, (GLA_CHUNK, GLA_CHUNK), 1) // GLA_SUB
    trow = lax.broadcasted_iota(jnp.int32, (GLA_SUB, 256), 0)

    def chunk(c, carry):
        r0 = pl.multiple_of(c * GLA_CHUNK, GLA_CHUNK)
        rows = pl.ds(r0, GLA_CHUNK)
        h1, h2, h3 = _split3(la_ref[0, rows, :])
        b = _dot(tri, h1) + _dot(tri, h2) + _dot(tri, h3)
        q = q_ref[0, rows, :]
        k = k_ref[0, rows, :]
        v = v_ref[0, rows, :]
        b_last = b[GLA_CHUNK - 1:GLA_CHUNK, :]
        qd = q * jnp.exp2(b)
        kd = (k * jnp.exp2(b_last - b)).astype(BF16)
        starts = [jnp.zeros((1, 256), F32)] + [b[GLA_SUB * i - 1:GLA_SUB * i, :]
                                               for i in range(1, GLA_NSUB)]
        bsel = jnp.concatenate([jnp.broadcast_to(s, (GLA_SUB, 256)) for s in starts], axis=0)
        qn = q * jnp.exp2(b - bsel)

        diag = []
        for i in range(GLA_NSUB):
            rs = slice(GLA_SUB * i, GLA_SUB * (i + 1))
            b_i, q_i, k_i = b[rs], q[rs], k[rs]
            v_i = v[rs].astype(F32)
            ps = []
            for s in range(GLA_SUB):
                d = jnp.exp2(jnp.minimum(b_i - b_i[s:s + 1], 0.0))
                ps.append(jnp.where(trow >= s, q_i * d * k_i[s:s + 1], 0.0).astype(BF16))
            r = _dot(jnp.concatenate(ps, axis=0), emat)
            od = r[0:GLA_SUB] * v_i[0:1]
            for s in range(1, GLA_SUB):
                od = od + r[GLA_SUB * s:GLA_SUB * (s + 1)] * v_i[s:s + 1]
            diag.append(od)
        o_diag = jnp.concatenate(diag, axis=0)

        for slab in range(2):
            ls = slice(slab * LANES, (slab + 1) * LANES)
            qn_s, k_s, b_s = qn[:, ls], k[:, ls], b[:, ls]
            khat = jnp.concatenate(
                [(k_s * jnp.exp2(jnp.minimum(starts[i][:, ls] - b_s, 0.0))).astype(BF16)
                 for i in range(1, GLA_NSUB)], axis=1)
            for half in range(2):
                head = 2 * slab + half
                hs = slice(head * GLA_DV, (head + 1) * GLA_DV)
                in_head = (lane < HEAD_DIM) if half == 0 else (lane >= HEAD_DIM)
                qm = jnp.where(in_head, qn_s, 0.0)
                qhat = jnp.concatenate([jnp.where(rowblk == i, qm, 0.0).astype(BF16)
                                        for i in range(1, GLA_NSUB)], axis=1)
                att = jnp.where(sblk < tblk, _dot_nt(qhat, khat), 0.0)
                v_h = v[:, hs]
                st = st_ref[head]
                o = (_dot_nt(jnp.where(in_head, qd[:, ls], 0.0).astype(BF16), st.astype(BF16))
                     + _dot(att.astype(BF16), v_h) + o_diag[:, hs])
                st_ref[head] = st * jnp.exp2(b_last[:, ls]) + _dot_tn(v_h, kd[:, ls])
                y = o * lax.rsqrt(jnp.mean(o * o, axis=-1, keepdims=True) + EPS) * gn_ref[:, hs]
                o_ref[0, rows, hs] = (y * _silu(g_ref[0, rows, hs])).astype(o_ref.dtype)
        return carry

    lax.fori_loop(0, q_ref.shape[1] // GLA_CHUNK, chunk, 0, unroll=4)


def _gla(gq, gk, gv, gg, la, gain):
    bsz, t_work, _ = gq.shape
    tg = _row_tile(t_work)
    tri = jnp.asarray(np.tril(np.ones((GLA_CHUNK, GLA_CHUNK), np.float32)), dtype=BF16)
    emat = jnp.asarray(
        (np.arange(256)[:, None] // HEAD_DIM == np.arange(512)[None, :] // GLA_DV).astype(np.float32),
        dtype=BF16)
    rows = lambda width: pl.BlockSpec((1, tg, width), lambda b, j: (b, j, 0))
    const = lambda shape: pl.BlockSpec(shape, lambda b, j: (0,) * len(shape))
    return pl.pallas_call(
        _gla_kernel,
        grid=(bsz, t_work // tg),
        in_specs=[rows(256), rows(256), rows(512), rows(512), rows(256),
                  const((GLA_CHUNK, GLA_CHUNK)), const((256, 512)), const((1, 512))],
        out_specs=rows(512),
        out_shape=jax.ShapeDtypeStruct((bsz, t_work, 512), BF16),
        scratch_shapes=[pltpu.VMEM((N_HEADS, GLA_DV, LANES), F32)],
        compiler_params=pltpu.CompilerParams(
            dimension_semantics=("parallel", "arbitrary"), vmem_limit_bytes=VMEM_LIMIT),
        name="gla",
    )(gq, gk, gv, gg, la, tri, emat, gain)


FFN_CHUNK = 256


def _ffn_kernel(h_ref, oa_ref, ob_ref, oc_ref, wo_ref, fn_ref, wgu_ref, wd_ref, out_ref):
    h1 = (h_ref[0] + _dot(oa_ref[0], wo_ref[0:256, :]) + _dot(ob_ref[0], wo_ref[256:512, :])
          + _dot(oc_ref[0], wo_ref[512:1024, :]))
    ms = jnp.mean(h1 * h1, axis=-1, keepdims=True)
    f = (h1 * lax.rsqrt(ms + EPS) * fn_ref[...]).astype(BF16)
    out_ref[0] = h1
    for c in range(0, D_FF, FFN_CHUNK):
        gate = _dot(f, wgu_ref[:, c:c + FFN_CHUNK])
        up = _dot(f, wgu_ref[:, D_FF + c:D_FF + c + FFN_CHUNK])
        out_ref[0] += _dot((_silu(gate) * up).astype(BF16), wd_ref[c:c + FFN_CHUNK, :])


def _outproj_ffn(h, oa, ob, oc, wo, fn, wgu, wd):
    bsz, t_work, _ = h.shape
    tm = _row_tile(t_work)
    rows = lambda width: pl.BlockSpec((1, tm, width), lambda b, j: (b, j, 0))
    const = lambda shape: pl.BlockSpec(shape, lambda b, j: (0,) * len(shape),
                                       pipeline_mode=pl.Buffered(1))
    return pl.pallas_call(
        _ffn_kernel,
        grid=(bsz, t_work // tm),
        in_specs=[rows(D_MODEL), rows(256), rows(256), rows(512), const((D_MODEL, D_MODEL)),
                  const((1, D_MODEL)), const((D_MODEL, 2 * D_FF)), const((D_FF, D_MODEL))],
        out_specs=rows(D_MODEL),
        out_shape=jax.ShapeDtypeStruct(h.shape, F32),
        compiler_params=pltpu.CompilerParams(
            dimension_semantics=("parallel", "parallel"), vmem_limit_bytes=VMEM_LIMIT),
        name="outproj_ffn",
    )(h, oa, ob, oc, wo, fn, wgu, wd)


def kernel(x, meta_tokens, attn_norm, w_in, dsa_q_norm, dsa_k_norm, fox_q_norm, fox_k_norm,
           fox_f_bias, gla_gate_w2, gla_gate_b, gla_out_norm, w_out, ffn_norm, w_gate_up, w_down):
    bsz, seq, _ = x.shape
    n_tok = N_META + seq
    t_work = _work_len(n_tok)
    meta = jnp.broadcast_to(meta_tokens[None].astype(x.dtype), (bsz, N_META, D_MODEL))
    h = jnp.concatenate([meta, x, jnp.zeros((bsz, t_work - n_tok, D_MODEL), x.dtype)], axis=1)
    cos_t, sin_t = _rope_tables(t_work)
    g256 = _group_matrix()
    k_top = min(TOPK_MAX, seq // 4)
    for l in range(w_in.shape[0]):
        an, w_perm, vecs, w2_pad, sm_bias = _prep_layer_params(
            l, attn_norm, w_in, dsa_q_norm, dsa_k_norm, fox_q_norm, fox_k_norm, fox_f_bias,
            gla_gate_w2, gla_gate_b)
        (dq, dkk, dva, iq, ikk, fq, fk, fva, gq, gk, gv, gg, la, small, key_norms) = _inproj(
            h, an, w_perm, cos_t, sin_t, g256, vecs, w2_pad, sm_bias)
        small_t, fkb = _fox_prep(small, fk)
        oa = _dsa_attention(dq, dkk, dva, iq, ikk, small_t, key_norms, k_top, n_tok)
        ob = _fox_attention(fq, fkb, fva, key_norms)
        oc = _gla(gq, gk, gv, gg, la, jnp.tile(gla_out_norm[l], N_HEADS)[None, :])
        h = _outproj_ffn(h, oa, ob, oc, w_out[l].astype(BF16), ffn_norm[l][None, :],
                         w_gate_up[l].astype(BF16), w_down[l].astype(BF16))
    return h[:, N_META:n_tok]
```

```python
import functools

import numpy as np
import jax
import jax.numpy as jnp
from jax import lax
from jax.experimental import pallas as pl
from jax.experimental.pallas import tpu as pltpu

F32 = jnp.float32
BF16 = jnp.bfloat16

D_MODEL = 1024
HEAD_DIM = 64
N_META = 16
ROPE_THETA = 500000.0
ROPE_DIM = HEAD_DIM // 4
ROPE_HALF = ROPE_DIM // 2
NEG = -1e30
EPS = 1e-6

N_HEADS = 4
TOPK_MAX = 256
GLA_DV = 128
GLA_RANK = 16
GLA_TAU = 16.0
GLA_CHUNK = 64
GLA_SUB = 16
D_FF = 2816

LOG2E = 1.4426950408889634
BIAS_TERMS = 3
KN_DSA, KN_FOX = 0, 1
NORM_MARGIN = 1.01
SAFE_LOG2_SPAN = 100.0
LANES = 128
SEQ_ALIGN = 512
VMEM_LIMIT = 56 * 1024 * 1024

_SLABS = (("dq", 256), ("dkk", 128), ("dva", 128), ("iq", 256), ("ikk", 128),
          ("fq", 256), ("fk", 256), ("fv", 256), ("gq", 256), ("gk", 256),
          ("gv", 512), ("gg", 512), ("small", 128))
_SLAB_OFF = {}
_off = 0
for _name, _width in _SLABS:
    _SLAB_OFF[_name] = (_off, _width)
    _off += _width
N_PROJ = _off
SM_IW, SM_FF, SM_GLR = 0, 4, 8


def _work_len(n_tok):
    return -(-n_tok // SEQ_ALIGN) * SEQ_ALIGN


def _row_tile(t_work):
    for cand in (768, 640, 512):
        if t_work % cand == 0:
            return cand
    raise ValueError(f"unsupported working length {t_work}")


def _dot(a, b):
    return jnp.dot(a, b, preferred_element_type=F32)


def _dot_nt(a, b):
    return lax.dot_general(a, b, (((1,), (1,)), ((), ())), preferred_element_type=F32)


def _dot_tn(a, b):
    return lax.dot_general(a, b, (((0,), (0,)), ((), ())), preferred_element_type=F32)


def _split3(x):
    h1 = x.astype(BF16)
    r1 = x - h1.astype(F32)
    h2 = r1.astype(BF16)
    h3 = (r1 - h2.astype(F32)).astype(BF16)
    return h1, h2, h3


def _log_sigmoid(x):
    return jnp.minimum(x, 0.0) - jnp.log1p(jnp.exp(-jnp.abs(x)))


def _silu(x):
    return x / (1.0 + jnp.exp(-x))


def _group_rms(y, gmat, gain):
    yy = y * y
    hi = yy.astype(BF16)
    lo = (yy - hi.astype(F32)).astype(BF16)
    ss = _dot(hi, gmat) + _dot(lo, gmat)
    return y * lax.rsqrt(ss * (1.0 / HEAD_DIM) + EPS) * gain


def _rope(y, cos, sin):
    width = y.shape[-1]
    lane = lax.broadcasted_iota(jnp.int32, y.shape, 1) % HEAD_DIM
    upper = pltpu.roll(y, width - ROPE_HALF, axis=1)
    lower = pltpu.roll(y, ROPE_HALF, axis=1)
    partner = jnp.where(lane < ROPE_HALF, upper, lower)
    return y * cos + partner * sin


def _inproj_kernel(x_ref, an_ref, w_ref, cos_ref, sin_ref, g256_ref, vec_ref, w2_ref, sm_ref,
                   dq_ref, dkk_ref, dva_ref, iq_ref, ikk_ref, fq_ref, fk_ref, fva_ref,
                   gq_ref, gk_ref, gv_ref, gg_ref, la_ref, small_ref, kn_ref):
    x = x_ref[0]
    ms = jnp.mean(x * x, axis=-1, keepdims=True)
    a = (x * lax.rsqrt(ms + EPS) * an_ref[...]).astype(BF16)

    def proj(name):
        off, width = _SLAB_OFF[name]
        return _dot(a, w_ref[:, off:off + width])

    def ones_in_upper_half(y):
        lane = lax.broadcasted_iota(jnp.int32, y.shape, 1) % LANES
        return jnp.where(lane < HEAD_DIM, y, 1.0)

    cos = cos_ref[...]
    sin = sin_ref[...]
    g256 = g256_ref[...]
    g128 = g256[:LANES, :LANES]
    dqn, dkn, fqn, fkn = vec_ref[0:1, :], vec_ref[1:2, :LANES], vec_ref[2:3, :], vec_ref[3:4, :]
    scale = HEAD_DIM ** -0.5

    def max_sq_norm(k):
        kf = k.astype(F32)
        return jnp.max(_dot((kf * kf).astype(BF16), g128), axis=0, keepdims=True)

    dq_ref[0] = (_rope(_group_rms(proj("dq"), g256, dqn), cos, sin) * (scale * LOG2E)).astype(BF16)
    dkk = _rope(_group_rms(proj("dkk"), g128, dkn), cos[:, :LANES], sin[:, :LANES]).astype(BF16)
    dkk_ref[0] = dkk
    key_norms = [max_sq_norm(dkk)]
    dva_ref[0] = ones_in_upper_half(proj("dva")).astype(BF16)
    iq_ref[0] = (_rope(proj("iq"), cos, sin) * scale).astype(BF16)
    ikk_ref[0] = _rope(proj("ikk"), cos[:, :LANES], sin[:, :LANES]).astype(BF16)
    fq, fk, fv = proj("fq"), proj("fk"), proj("fv")
    lane = lax.broadcasted_iota(jnp.int32, (fq.shape[0], LANES), 1)
    in_head = lane < HEAD_DIM
    q_pad = jnp.where(lane < HEAD_DIM + BIAS_TERMS, 1.0, 0.0)
    for pair in range(N_HEADS // 2):
        ps = slice(pair * LANES, (pair + 1) * LANES)
        q_pair = _group_rms(fq[:, ps], g128, fqn[:, :LANES]) * (scale * LOG2E)
        k_pair = _group_rms(fk[:, ps], g128, fkn[:, :LANES])
        v_pair = fv[:, ps]
        for odd in range(2):
            hs = slice((2 * pair + odd) * LANES, (2 * pair + odd + 1) * LANES)
            down = (lambda y: pltpu.roll(y, HEAD_DIM, axis=1)) if odd else (lambda y: y)
            fq_ref[0, :, hs] = jnp.where(in_head, down(q_pair), q_pad).astype(BF16)
            k_h = jnp.where(in_head, down(k_pair), 0.0).astype(BF16)
            fk_ref[0, :, hs] = k_h
            key_norms.append(max_sq_norm(k_h))
            fva_ref[0, :, hs] = jnp.where(in_head, down(v_pair), 1.0).astype(BF16)
    kn_ref[0, 0] = jnp.concatenate(key_norms + [jnp.zeros((8 - len(key_norms), LANES), F32)], axis=0)
    gq_ref[0] = proj("gq") * scale
    gk_ref[0] = proj("gk")
    gv_ref[0] = proj("gv").astype(BF16)
    gg_ref[0] = proj("gg")

    small = proj("small")
    lane = lax.broadcasted_iota(jnp.int32, small.shape, 1)
    small_ref[0] = jnp.where(lane < SM_FF, small * (N_HEADS ** -0.5),
                             _log_sigmoid(small + sm_ref[0:1, :]))
    gate = _dot(small.astype(BF16), w2_ref[...]) + vec_ref[4:5, :]
    la_ref[0] = _log_sigmoid(gate) * (LOG2E / GLA_TAU)


def _inproj(h, an, w_perm, cos_t, sin_t, g256, vecs, w2_pad, sm_bias):
    bsz, t_work, _ = h.shape
    tm = _row_tile(t_work)
    grid = (bsz, t_work // tm)

    def rows(width, dtype):
        return (jax.ShapeDtypeStruct((bsz, t_work, width), dtype),
                pl.BlockSpec((1, tm, width), lambda b, j: (b, j, 0)))

    outs = [rows(256, BF16), rows(128, BF16), rows(128, BF16), rows(256, BF16), rows(128, BF16),
            rows(512, BF16), rows(512, BF16), rows(512, BF16), rows(256, F32), rows(256, F32),
            rows(512, BF16), rows(512, F32), rows(256, F32), rows(128, F32),
            (jax.ShapeDtypeStruct((bsz, t_work // tm, 8, LANES), F32),
             pl.BlockSpec((1, 1, 8, LANES), lambda b, j: (b, j, 0, 0)))]
    const = lambda shape: pl.BlockSpec(shape, lambda b, j: (0,) * len(shape))
    return pl.pallas_call(
        _inproj_kernel,
        grid=grid,
        in_specs=[pl.BlockSpec((1, tm, D_MODEL), lambda b, j: (b, j, 0)),
                  const((1, D_MODEL)), const((D_MODEL, N_PROJ)),
                  pl.BlockSpec((tm, 256), lambda b, j: (j, 0)),
                  pl.BlockSpec((tm, 256), lambda b, j: (j, 0)),
                  const((256, 256)), const((8, 256)), const((LANES, 256)), const((8, LANES))],
        out_specs=[o[1] for o in outs],
        out_shape=[o[0] for o in outs],
        compiler_params=pltpu.CompilerParams(
            dimension_semantics=("parallel", "parallel"), vmem_limit_bytes=VMEM_LIMIT),
        name="inproj",
    )(h, an, w_perm, cos_t, sin_t, g256, vecs, w2_pad, sm_bias)


def _prep_layer_params(l, attn_norm, w_in, dsa_q_norm, dsa_k_norm, fox_q_norm, fox_k_norm,
                       fox_f_bias, gla_gate_w2, gla_gate_b):
    w = w_in[l]
    splits = np.cumsum([256, 64, 64, 256, 4, 64, 256, 256, 256, 4, 256, 256, 512, 512, 16])[:-1]
    (dq, dk, dv, iq, iw, ik, fq, fk, fv, ff, gq, gk, gv, gg, glr) = jnp.split(w, splits, axis=1)
    small = jnp.concatenate(
        [iw, ff, glr, jnp.zeros((D_MODEL, LANES - 4 - 4 - GLA_RANK), w.dtype)], axis=1)
    z64 = jnp.zeros((D_MODEL, HEAD_DIM), w.dtype)
    w_perm = jnp.concatenate([dq, dk, dk, dv, z64, iq, ik, ik, fq, fk, fv, gq, gk, gv, gg, small],
                             axis=1).astype(BF16)
    tile4 = lambda g: jnp.tile(g, N_HEADS)
    vecs = jnp.zeros((8, 256), F32)
    vecs = vecs.at[0].set(tile4(dsa_q_norm[l])).at[1].set(tile4(dsa_k_norm[l]))
    vecs = vecs.at[2].set(tile4(fox_q_norm[l])).at[3].set(tile4(fox_k_norm[l]))
    vecs = vecs.at[4].set(gla_gate_b[l])
    w2_pad = jnp.zeros((LANES, 256), F32).at[SM_GLR:SM_GLR + GLA_RANK].set(gla_gate_w2[l]).astype(BF16)
    sm_bias = jnp.zeros((8, LANES), F32).at[0, SM_FF:SM_FF + N_HEADS].set(fox_f_bias[l])
    return attn_norm[l][None, :], w_perm, vecs, w2_pad, sm_bias


def _rope_tables(t_work):
    inv = jnp.power(ROPE_THETA, -jnp.arange(ROPE_HALF, dtype=F32) * 2.0 / ROPE_DIM)
    ang = jnp.arange(t_work).astype(F32)[:, None] * inv[None, :]
    cos, sin = jnp.cos(ang), jnp.sin(ang)
    rest = HEAD_DIM - ROPE_DIM
    cos64 = jnp.concatenate([cos, cos, jnp.ones((t_work, rest), F32)], axis=1)
    sin64 = jnp.concatenate([-sin, sin, jnp.zeros((t_work, rest), F32)], axis=1)
    return jnp.tile(cos64, (1, N_HEADS)), jnp.tile(sin64, (1, N_HEADS))


def _group_matrix():
    idx = np.arange(256) // HEAD_DIM
    return jnp.asarray((idx[:, None] == idx[None, :]).astype(np.float32), dtype=BF16)


DSA_TQ = 256
DSA_TK = 512
KEY_NEG_INF = -2139095041
KEY_NEG_ZERO = -1
KEY_NEG_MIN_NORMAL = -8388609
NEG_MIN_NORMAL = -1.1754943508222875e-38
SEARCH_FEW = 4.0
SEARCH_MANY = 16.0
UNCHECKED_PROBES = 12
GUIDED_PROBES = 64
MAX_PROBES = GUIDED_PROBES + 40
COUNT_ROWS = 128


def _key_to_f32(key):
    bits = key ^ ((key >> 31) & 0x7FFFFFFF)
    return lax.bitcast_convert_type(bits, F32)


def _f32_to_key(value):
    bits = lax.bitcast_convert_type(value, jnp.int32)
    return bits ^ ((bits >> 31) & 0x7FFFFFFF)


def _head_lane_mask(shape, head):
    lane = lax.broadcasted_iota(jnp.int32, shape, 1)
    return (lane < HEAD_DIM) if head % 2 == 0 else (lane >= HEAD_DIM)


def _masked_heads(slabs):
    return [jnp.where(_head_lane_mask(slabs[h // 2].shape, h), slabs[h // 2],
                      jnp.zeros_like(slabs[h // 2])) for h in range(N_HEADS)]


def _softmax_step(s, m, acc, v_aug):
    m_new = jnp.maximum(m, jnp.max(s, axis=-1, keepdims=True))
    p = jnp.exp2(s - m_new)
    return m_new, acc * jnp.exp2(m - m_new) + _dot(p.astype(BF16), v_aug)


def _normalise_heads(accs):
    outs = [acc / pltpu.roll(acc, HEAD_DIM, axis=1) for acc in accs]
    lane = lax.broadcasted_iota(jnp.int32, outs[0].shape, 1)
    return [jnp.where(lane < HEAD_DIM, outs[2 * p], pltpu.roll(outs[2 * p + 1], HEAD_DIM, axis=1))
            for p in range(2)]


def _dsa_kernel(k_top, n_tok, *refs):
    o_ref = refs[-2]
    real = pl.program_id(1) * DSA_TQ < n_tok
    pl.when(real)(functools.partial(_dsa_tile, k_top, *refs))

    @pl.when(jnp.logical_not(real))
    def _():
        o_ref[...] = jnp.zeros_like(o_ref)


def _dsa_tile(k_top, dq_ref, kk_ref, va_ref, iq_ref, ik_ref, wt_ref, tri_ref, kn_ref, o_ref,
              s_ref):
    i = pl.program_id(1)
    q0 = i * DSA_TQ
    n_tiles = (q0 + DSA_TQ + DSA_TK - 1) // DSA_TK
    last = n_tiles - 1
    key = lax.broadcasted_iota(jnp.int32, (DSA_TK, DSA_TQ), 0)
    qry = q0 + lax.broadcasted_iota(jnp.int32, (DSA_TK, DSA_TQ), 1)
    head = lambda x, h: x[:, h * DSA_TQ:(h + 1) * DSA_TQ]

    iq_all = jnp.concatenate(
        _masked_heads([iq_ref[0, :, 0:LANES], iq_ref[0, :, LANES:2 * LANES]]), axis=0)
    w_h = [wt_ref[0, SM_IW + h:SM_IW + h + 1, :] for h in range(N_HEADS)]

    def score_tile(j, carry, causal):
        k0 = pl.multiple_of(j * DSA_TK, DSA_TK)
        dots = jnp.maximum(_dot_nt(ik_ref[0, pl.ds(k0, DSA_TK), :], iq_all), 0.0)
        s = w_h[0] * head(dots, 0)
        for h in range(1, N_HEADS):
            s = s + w_h[h] * head(dots, h)
        if causal:
            s = jnp.where(k0 + key <= qry, s, NEG)
        s_ref[pl.ds(k0, DSA_TK), :] = s
        row_max, above_zero, above_neg = carry
        for part in range(DSA_TK // COUNT_ROWS):
            chunk = s[part * COUNT_ROWS:(part + 1) * COUNT_ROWS]
            above_zero = above_zero + jnp.where(chunk > 0.0, 1.0, 0.0)
            above_neg = above_neg + jnp.where(chunk > NEG_MIN_NORMAL, 1.0, 0.0)
        return jnp.maximum(row_max, jnp.max(s, axis=0, keepdims=True)), above_zero, above_neg

    zeros = jnp.zeros((COUNT_ROWS, DSA_TQ), F32)
    row_max, above_zero, above_neg = score_tile(last, lax.fori_loop(
        0, last, functools.partial(score_tile, causal=False),
        (jnp.full((1, DSA_TQ), NEG, F32), zeros, zeros)), causal=True)

    kf = jnp.float32(k_top)

    def count_above(t):
        tb = jnp.broadcast_to(t, (COUNT_ROWS, DSA_TQ))

        def body(j, acc, span):
            k0 = pl.multiple_of(j * span, span)
            for part in range(span // COUNT_ROWS):
                s = s_ref[pl.ds(k0 + part * COUNT_ROWS, COUNT_ROWS), :]
                acc = acc + jnp.where(s > tb, 1.0, 0.0)
            return acc

        acc = lax.fori_loop(0, n_tiles // 2, functools.partial(body, span=2 * DSA_TK),
                            jnp.zeros((COUNT_ROWS, DSA_TQ), F32))
        acc = lax.fori_loop(2 * (n_tiles // 2), n_tiles, functools.partial(body, span=DSA_TK), acc)
        return jnp.sum(acc, axis=0, keepdims=True)

    def midpoint(lo, hi):
        return (lo >> 1) + (hi >> 1) + (lo & hi & 1)

    def converged(lo, hi):
        return (midpoint(lo, hi) == lo) | ((lo >= KEY_NEG_MIN_NORMAL) & (hi <= 0))

    col1 = lambda value, dtype: jnp.full((1, DSA_TQ), value, dtype)

    def absorb(state, probe, c):
        lo, hi, clo, chi, wlo, whi, side = state
        live = probe != lo
        up = live & (c >= kf)
        down = live & (c <= kf)
        wlo = jnp.where(down & (side < 0), wlo * 0.5, jnp.where(up, 1.0, wlo))
        whi = jnp.where(up & (side > 0), whi * 0.5, jnp.where(down, 1.0, whi))
        side = jnp.where(up, 1, jnp.where(down, -1, side))
        lo, clo = jnp.where(up, probe, lo), jnp.where(up, c, clo)
        hi, chi = jnp.where(down, probe, hi), jnp.where(down, c, chi)
        return lo, hi, clo, chi, wlo, whi, side

    def probe_once(it, state):
        lo, hi, clo, chi, wlo, whi, side = state
        f_lo, f_hi = _key_to_f32(lo), _key_to_f32(hi)
        target = kf - 0.5
        log_count = lambda c: jnp.log2(jnp.maximum(c, 0.25))
        many = clo - chi > SEARCH_MANY
        g_lo = jnp.where(many, log_count(clo) - np.log2(k_top - 0.5), clo - target) * wlo
        g_hi = jnp.where(many, np.log2(k_top - 0.5) - log_count(chi), target - chi) * whi
        halve = (clo - chi <= SEARCH_FEW) | (col1(it % 8, jnp.int32) == 7)
        guess = _f32_to_key(f_lo + (f_hi - f_lo) * jnp.where(halve, 0.5, g_lo / (g_lo + g_hi)))
        guided = col1(it, jnp.int32) < GUIDED_PROBES
        probe = jnp.where((guess > lo) & (guess < hi) & guided, guess, midpoint(lo, hi))
        probe = jnp.where(converged(lo, hi), lo, probe)
        return absorb(state, probe, count_above(_key_to_f32(probe)))

    def search_cond(carry):
        it, pending = carry[0], carry[1]
        return (pending > 0) & (it < MAX_PROBES)

    def search_body(carry):
        it, state = carry[0], carry[2]
        state = probe_once(it + 1, probe_once(it, state))
        return it + 2, jnp.max(jnp.where(converged(state[0], state[1]), 0, 1)), state

    n_swept = (n_tiles * DSA_TK).astype(F32)
    state = (col1(KEY_NEG_INF, jnp.int32), _f32_to_key(row_max),
             jnp.broadcast_to(n_swept, (1, DSA_TQ)), col1(0.0, F32), col1(1.0, F32), col1(1.0, F32),
             col1(0, jnp.int32))
    for fixed, counts in ((KEY_NEG_ZERO, above_zero), (KEY_NEG_MIN_NORMAL, above_neg)):
        inside = (state[0] < fixed) & (fixed < state[1])
        state = absorb(state, jnp.where(inside, fixed, state[0]),
                       jnp.sum(counts, axis=0, keepdims=True))
    state = lax.fori_loop(0, UNCHECKED_PROBES, probe_once, state)
    state = lax.while_loop(search_cond, search_body,
                           (jnp.int32(UNCHECKED_PROBES), jnp.int32(1), state))[2]
    thr = _key_to_f32(state[1])
    n_ties = kf - state[3]

    to_column = lambda r: jnp.broadcast_to(r, (8, DSA_TQ)).T[:, 0:1]
    thr_c, ties_c = to_column(thr), to_column(n_ties)
    q_all = jnp.concatenate(
        _masked_heads([dq_ref[0, :, 0:LANES], dq_ref[0, :, LANES:2 * LANES]]), axis=0)
    tri = tri_ref[...]
    tri_lo = jnp.where(lax.broadcasted_iota(jnp.int32, (LANES, LANES), 0)
                       >= lax.broadcasted_iota(jnp.int32, (LANES, LANES), 1), 1.0, 0.0).astype(BF16)
    qrow = q0 + lax.broadcasted_iota(jnp.int32, (DSA_TQ, DSA_TK), 0)
    kcol = lax.broadcasted_iota(jnp.int32, (DSA_TQ, DSA_TK), 1)
    n_blocks = DSA_TK // LANES

    reach = _logit_reach(q_all, jnp.max(kn_ref[0], axis=0)[KN_DSA:KN_DSA + 1, 0:1])

    def selected_logits(j, seen, causal):
        k0 = pl.multiple_of(j * DSA_TK, DSA_TK)
        s = s_ref[pl.ds(k0, DSA_TK), :].T
        tie = s == thr_c
        tie_b = jnp.where(tie, 1.0, 0.0).astype(BF16)
        local = [_dot(tie_b[:, b * LANES:(b + 1) * LANES], tri) for b in range(n_blocks)]
        ranks = []
        for b in range(n_blocks):
            ranks.append(local[b] + seen)
            seen = seen + local[b][:, LANES - 1:LANES]
        sel = (s > thr_c) | (tie & (jnp.concatenate(ranks, axis=1) <= ties_c))
        if causal:
            sel = sel & (k0 + kcol <= qrow)
        logits = _dot_nt(q_all, kk_ref[0, pl.ds(k0, DSA_TK), :])
        logits = jnp.where(sel[None], logits.reshape(N_HEADS, DSA_TQ, DSA_TK), NEG)
        return seen, logits.reshape(N_HEADS * DSA_TQ, DSA_TK), va_ref[0, pl.ds(k0, DSA_TK), :]

    def capped(j, carry, causal):
        seen, acc = carry
        k0 = pl.multiple_of(j * DSA_TK, DSA_TK)
        s = s_ref[pl.ds(k0, DSA_TK), :]
        tie = s == thr
        tie_b = jnp.where(tie, 1.0, 0.0).astype(BF16)
        ranks = []
        for b in range(n_blocks):
            local = _dot(tri_lo, tie_b[b * LANES:(b + 1) * LANES, :])
            ranks.append(local + seen)
            seen = seen + local[LANES - 1:LANES, :]
        sel = (s > thr) | (tie & (jnp.concatenate(ranks, axis=0) <= n_ties))
        if causal:
            sel = sel & (k0 + key <= qry)
        keep = jnp.where(sel, 1.0, 0.0).astype(BF16).T
        logits = _dot_nt(q_all, kk_ref[0, pl.ds(k0, DSA_TK), :])
        p = jnp.exp2(logits - reach).astype(BF16).reshape(N_HEADS, DSA_TQ, DSA_TK) * keep[None]
        return seen, acc + _dot(p.reshape(N_HEADS * DSA_TQ, DSA_TK), va_ref[0, pl.ds(k0, DSA_TK), :])

    def online(j, carry, causal):
        seen, logits, va_t = selected_logits(j, carry[0], causal)
        return (seen,) + _softmax_step(logits, carry[1], carry[2], va_t)

    def sweep(step, *stats):
        carry = stats + (jnp.zeros((N_HEADS * DSA_TQ, LANES), F32),)
        carry = lax.fori_loop(0, last, functools.partial(step, causal=False), carry)
        return step(last, carry, causal=True)[-1]

    acc = lax.cond(2.0 * jnp.max(reach) <= SAFE_LOG2_SPAN,
                   functools.partial(sweep, capped, jnp.zeros((1, DSA_TQ), F32)),
                   functools.partial(sweep, online, jnp.zeros((DSA_TQ, 1), F32),
                                     jnp.full((N_HEADS * DSA_TQ, 1), NEG, F32)))
    o_ref[0] = jnp.concatenate(
        _normalise_heads([acc[h * DSA_TQ:(h + 1) * DSA_TQ] for h in range(N_HEADS)]),
        axis=1).astype(o_ref.dtype)


def _dsa_attention(dq, dkk, dva, iq, ikk, small_t, key_norms, k_top, n_tok):
    bsz, t_work, _ = dq.shape
    tri = jnp.asarray(np.triu(np.ones((LANES, LANES), np.float32)), dtype=BF16)
    tile = lambda width: pl.BlockSpec((1, DSA_TQ, width), lambda b, i: (b, i, 0))
    full = pl.BlockSpec((1, t_work, LANES), lambda b, i: (b, 0, 0), pipeline_mode=pl.Buffered(1))
    return pl.pallas_call(
        functools.partial(_dsa_kernel, k_top, n_tok),
        grid=(bsz, t_work // DSA_TQ),
        in_specs=[tile(256), full, full, tile(256), full,
                  pl.BlockSpec((1, 8, DSA_TQ), lambda b, i: (b, 0, i)),
                  pl.BlockSpec((LANES, LANES), lambda b, i: (0, 0)),
                  pl.BlockSpec((1,) + key_norms.shape[1:], lambda b, i: (b, 0, 0, 0))],
        out_specs=tile(256),
        out_shape=jax.ShapeDtypeStruct((bsz, t_work, 256), BF16),
        scratch_shapes=[pltpu.VMEM((t_work, DSA_TQ), F32)],
        compiler_params=pltpu.CompilerParams(
            dimension_semantics=("parallel", "parallel"), vmem_limit_bytes=VMEM_LIMIT),
        name="dsa_attention",
    )(dq, dkk, dva, iq, ikk, small_t, tri, key_norms)


FOX_TQ = 512
FOX_TK = 512
CUM_T = 256


def _fox_prep_kernel(x_ref, tri_ref, place_ref, k_ref, xt_ref, kb_ref, carry_ref):
    @pl.when(pl.program_id(1) == 0)
    def _():
        carry_ref[...] = jnp.zeros_like(carry_ref)

    tri = tri_ref[...]
    x = x_ref[0]
    h1, h2, h3 = _split3(x)
    c = _dot(tri, h1) + _dot(tri, h2) + _dot(tri, h3) + carry_ref[0:1, :]
    carry_ref[...] = jnp.broadcast_to(c[CUM_T - 1:CUM_T, :], carry_ref.shape)
    xt_ref[0] = x.T[0:8, :]
    terms = _split3(c * -LOG2E)
    bias = sum(_dot(terms[t], place_ref[t]) for t in range(BIAS_TERMS))
    lane = lax.broadcasted_iota(jnp.int32, bias.shape, 1) % LANES
    kb_ref[0] = jnp.where((lane >= HEAD_DIM) & (lane < HEAD_DIM + BIAS_TERMS),
                          bias.astype(BF16), k_ref[0])


def _fox_prep(small, fk):
    bsz, t_work, width = fk.shape
    tri = jnp.asarray(np.tril(np.ones((CUM_T, CUM_T), np.float32)), dtype=BF16)
    place = np.zeros((BIAS_TERMS, LANES, width), np.float32)
    for t in range(BIAS_TERMS):
        for h in range(N_HEADS):
            place[t, SM_FF + h, h * LANES + HEAD_DIM + t] = 1.0
    rows = lambda w: pl.BlockSpec((1, CUM_T, w), lambda b, j: (b, j, 0))
    return pl.pallas_call(
        _fox_prep_kernel,
        grid=(bsz, t_work // CUM_T),
        in_specs=[rows(LANES), pl.BlockSpec((CUM_T, CUM_T), lambda b, j: (0, 0)),
                  pl.BlockSpec((BIAS_TERMS, LANES, width), lambda b, j: (0, 0, 0)), rows(width)],
        out_specs=[pl.BlockSpec((1, 8, CUM_T), lambda b, j: (b, 0, j)), rows(width)],
        out_shape=[jax.ShapeDtypeStruct((bsz, 8, t_work), F32),
                   jax.ShapeDtypeStruct(fk.shape, BF16)],
        scratch_shapes=[pltpu.VMEM((8, LANES), F32)],
        compiler_params=pltpu.CompilerParams(dimension_semantics=("parallel", "arbitrary")),
        name="fox_prep",
    )(small, tri, jnp.asarray(place, dtype=BF16), fk)


def _logit_reach(q, kmax_sq):
    qf = q.astype(F32)
    return jnp.sqrt(jnp.sum(qf * qf, axis=-1, keepdims=True) * kmax_sq) * NORM_MARGIN


def _fox_kernel(q_ref, k_ref, v_ref, kn_ref, o_ref):
    i = pl.program_id(1)
    q0 = pl.multiple_of(i * FOX_TQ, FOX_TQ)
    n_full = q0 // FOX_TK
    row = q0 + lax.broadcasted_iota(jnp.int32, (FOX_TQ, FOX_TK), 0)
    col = n_full * FOX_TK + lax.broadcasted_iota(jnp.int32, (FOX_TQ, FOX_TK), 1)
    heads = [slice(h * LANES, (h + 1) * LANES) for h in range(N_HEADS)]

    lane = lax.broadcasted_iota(jnp.int32, (FOX_TQ, LANES), 1)
    bias_lanes = (lane >= HEAD_DIM) & (lane < HEAD_DIM + BIAS_TERMS)
    kmax_sq = jnp.max(kn_ref[0], axis=0)
    caps, span = [], jnp.float32(0.0)
    for h in range(N_HEADS):
        q_h = q_ref[0, :, heads[h]]
        reach = _logit_reach(jnp.where(lane < HEAD_DIM, q_h, jnp.zeros_like(q_h)),
                             kmax_sq[KN_FOX + h:KN_FOX + h + 1, 0:1])
        own = k_ref[0, pl.ds(q0, FOX_TQ), heads[h]].astype(F32)
        caps.append(reach + jnp.sum(jnp.where(bias_lanes, own, 0.0), axis=-1, keepdims=True))
        span = jnp.maximum(span, 2.0 * jnp.max(reach))

    def logits(j, h, diag):
        k0 = pl.multiple_of(j * FOX_TK, FOX_TK)
        s = _dot_nt(q_ref[0, :, heads[h]], k_ref[0, pl.ds(k0, FOX_TK), heads[h]])
        return jnp.where(col <= row, s, NEG) if diag else s

    def values(j, h):
        return v_ref[0, pl.ds(pl.multiple_of(j * FOX_TK, FOX_TK), FOX_TK), heads[h]]

    def capped(j, accs, diag):
        return tuple(accs[h] + _dot(jnp.exp2(logits(j, h, diag) - caps[h]).astype(BF16), values(j, h))
                     for h in range(N_HEADS))

    def online(j, carry, diag):
        return tuple(_softmax_step(logits(j, h, diag), *carry[h], values(j, h))
                     for h in range(N_HEADS))

    def capped_sweep():
        accs = tuple(jnp.zeros((FOX_TQ, LANES), F32) for _ in range(N_HEADS))
        accs = lax.fori_loop(0, n_full, functools.partial(capped, diag=False), accs)
        return capped(n_full, accs, diag=True)

    def online_sweep():
        carry = tuple((jnp.full((FOX_TQ, 1), NEG, F32), jnp.zeros((FOX_TQ, LANES), F32))
                      for _ in range(N_HEADS))
        carry = lax.fori_loop(0, n_full, functools.partial(online, diag=False), carry)
        return tuple(acc for _, acc in online(n_full, carry, diag=True))

    accs = lax.cond(span <= SAFE_LOG2_SPAN, capped_sweep, online_sweep)
    o_ref[0] = jnp.concatenate(_normalise_heads(list(accs)), axis=1).astype(o_ref.dtype)


def _fox_attention(fqa, fkb, fva, key_norms):
    bsz, t_work, width = fqa.shape
    full = pl.BlockSpec((1, t_work, width), lambda b, i: (b, 0, 0), pipeline_mode=pl.Buffered(1))
    return pl.pallas_call(
        _fox_kernel,
        grid=(bsz, t_work // FOX_TQ),
        in_specs=[pl.BlockSpec((1, FOX_TQ, width), lambda b, i: (b, i, 0)), full, full,
                  pl.BlockSpec((1,) + key_norms.shape[1:], lambda b, i: (b, 0, 0, 0))],
        out_specs=pl.BlockSpec((1, FOX_TQ, 2 * LANES), lambda b, i: (b, i, 0)),
        out_shape=jax.ShapeDtypeStruct((bsz, t_work, 2 * LANES), BF16),
        compiler_params=pltpu.CompilerParams(
            dimension_semantics=("parallel", "parallel"), vmem_limit_bytes=VMEM_LIMIT),
        name="fox_attention",
    )(fqa, fkb, fva, key_norms)


GLA_NSUB = GLA_CHUNK // GLA_SUB


def _gla_kernel(q_ref, k_ref, v_ref, g_ref, la_ref, tri_ref, e_ref, gn_ref, o_ref, st_ref):
    @pl.when(pl.program_id(1) == 0)
    def _():
        st_ref[...] = jnp.zeros_like(st_ref)

    tri = tri_ref[...]
    emat = e_ref[...]
    lane = lax.broadcasted_iota(jnp.int32, (GLA_CHUNK, LANES), 1)
    rowblk = lax.broadcasted_iota(jnp.int32, (GLA_CHUNK, LANES), 0) // GLA_SUB
    tblk = lax.broadcasted_iota(jnp.int32, (GLA_CHUNK, GLA_CHUNK), 0) // GLA_SUB
    sblk = lax.broadcasted_iota(jnp.int32, (GLA_CHUNK, GLA_CHUNK), 1) // GLA_SUB
    trow = lax.broadcasted_iota(jnp.int32, (GLA_SUB, 256), 0)

    def chunk(c, carry):
        r0 = pl.multiple_of(c * GLA_CHUNK, GLA_CHUNK)
        rows = pl.ds(r0, GLA_CHUNK)
        h1, h2, h3 = _split3(la_ref[0, rows, :])
        b = _dot(tri, h1) + _dot(tri, h2) + _dot(tri, h3)
        q = q_ref[0, rows, :]
        k = k_ref[0, rows, :]
        v = v_ref[0, rows, :]
        b_last = b[GLA_CHUNK - 1:GLA_CHUNK, :]
        qd = q * jnp.exp2(b)
        kd = (k * jnp.exp2(b_last - b)).astype(BF16)
        starts = [jnp.zeros((1, 256), F32)] + [b[GLA_SUB * i - 1:GLA_SUB * i, :]
                                               for i in range(1, GLA_NSUB)]
        bsel = jnp.concatenate([jnp.broadcast_to(s, (GLA_SUB, 256)) for s in starts], axis=0)
        qn = q * jnp.exp2(b - bsel)

        diag = []
        for i in range(GLA_NSUB):
            rs = slice(GLA_SUB * i, GLA_SUB * (i + 1))
            b_i, q_i, k_i = b[rs], q[rs], k[rs]
            v_i = v[rs].astype(F32)
            ps = []
            for s in range(GLA_SUB):
                d = jnp.exp2(jnp.minimum(b_i - b_i[s:s + 1], 0.0))
                ps.append(jnp.where(trow >= s, q_i * d * k_i[s:s + 1], 0.0).astype(BF16))
            r = _dot(jnp.concatenate(ps, axis=0), emat)
            od = r[0:GLA_SUB] * v_i[0:1]
            for s in range(1, GLA_SUB):
                od = od + r[GLA_SUB * s:GLA_SUB * (s + 1)] * v_i[s:s + 1]
            diag.append(od)
        o_diag = jnp.concatenate(diag, axis=0)

        for slab in range(2):
            ls = slice(slab * LANES, (slab + 1) * LANES)
            qn_s, k_s, b_s = qn[:, ls], k[:, ls], b[:, ls]
            khat = jnp.concatenate(
                [(k_s * jnp.exp2(jnp.minimum(starts[i][:, ls] - b_s, 0.0))).astype(BF16)
                 for i in range(1, GLA_NSUB)], axis=1)
            for half in range(2):
                head = 2 * slab + half
                hs = slice(head * GLA_DV, (head + 1) * GLA_DV)
                in_head = (lane < HEAD_DIM) if half == 0 else (lane >= HEAD_DIM)
                qm = jnp.where(in_head, qn_s, 0.0)
                qhat = jnp.concatenate([jnp.where(rowblk == i, qm, 0.0).astype(BF16)
                                        for i in range(1, GLA_NSUB)], axis=1)
                att = jnp.where(sblk < tblk, _dot_nt(qhat, khat), 0.0)
                v_h = v[:, hs]
                st = st_ref[head]
                o = (_dot_nt(jnp.where(in_head, qd[:, ls], 0.0).astype(BF16), st.astype(BF16))
                     + _dot(att.astype(BF16), v_h) + o_diag[:, hs])
                st_ref[head] = st * jnp.exp2(b_last[:, ls]) + _dot_tn(v_h, kd[:, ls])
                y = o * lax.rsqrt(jnp.mean(o * o, axis=-1, keepdims=True) + EPS) * gn_ref[:, hs]
                o_ref[0, rows, hs] = (y * _silu(g_ref[0, rows, hs])).astype(o_ref.dtype)
        return carry

    lax.fori_loop(0, q_ref.shape[1] // GLA_CHUNK, chunk, 0, unroll=6)


def _gla(gq, gk, gv, gg, la, gain):
    bsz, t_work, _ = gq.shape
    tg = _row_tile(t_work)
    tri = jnp.asarray(np.tril(np.ones((GLA_CHUNK, GLA_CHUNK), np.float32)), dtype=BF16)
    emat = jnp.asarray(
        (np.arange(256)[:, None] // HEAD_DIM == np.arange(512)[None, :] // GLA_DV).astype(np.float32),
        dtype=BF16)
    rows = lambda width: pl.BlockSpec((1, tg, width), lambda b, j: (b, j, 0))
    const = lambda shape: pl.BlockSpec(shape, lambda b, j: (0,) * len(shape))
    return pl.pallas_call(
        _gla_kernel,
        grid=(bsz, t_work // tg),
        in_specs=[rows(256), rows(256), rows(512), rows(512), rows(256),
                  const((GLA_CHUNK, GLA_CHUNK)), const((256, 512)), const((1, 512))],
        out_specs=rows(512),
        out_shape=jax.ShapeDtypeStruct((bsz, t_work, 512), BF16),
        scratch_shapes=[pltpu.VMEM((N_HEADS, GLA_DV, LANES), F32)],
        compiler_params=pltpu.CompilerParams(
            dimension_semantics=("parallel", "arbitrary"), vmem_limit_bytes=VMEM_LIMIT),
        name="gla",
    )(gq, gk, gv, gg, la, tri, emat, gain)


FFN_CHUNK = 256


def _ffn_kernel(h_ref, oa_ref, ob_ref, oc_ref, wo_ref, fn_ref, wgu_ref, wd_ref, out_ref):
    h1 = (h_ref[0] + _dot(oa_ref[0], wo_ref[0:256, :]) + _dot(ob_ref[0], wo_ref[256:512, :])
          + _dot(oc_ref[0], wo_ref[512:1024, :]))
    ms = jnp.mean(h1 * h1, axis=-1, keepdims=True)
    f = (h1 * lax.rsqrt(ms + EPS) * fn_ref[...]).astype(BF16)
    out_ref[0] = h1
    for c in range(0, D_FF, FFN_CHUNK):
        gate = _dot(f, wgu_ref[:, c:c + FFN_CHUNK])
        up = _dot(f, wgu_ref[:, D_FF + c:D_FF + c + FFN_CHUNK])
        out_ref[0] += _dot((_silu(gate) * up).astype(BF16), wd_ref[c:c + FFN_CHUNK, :])


def _outproj_ffn(h, oa, ob, oc, wo, fn, wgu, wd):
    bsz, t_work, _ = h.shape
    tm = _row_tile(t_work)
    rows = lambda width: pl.BlockSpec((1, tm, width), lambda b, j: (b, j, 0))
    const = lambda shape: pl.BlockSpec(shape, lambda b, j: (0,) * len(shape),
                                       pipeline_mode=pl.Buffered(1))
    return pl.pallas_call(
        _ffn_kernel,
        grid=(bsz, t_work // tm),
        in_specs=[rows(D_MODEL), rows(256), rows(256), rows(512), const((D_MODEL, D_MODEL)),
                  const((1, D_MODEL)), const((D_MODEL, 2 * D_FF)), const((D_FF, D_MODEL))],
        out_specs=rows(D_MODEL),
        out_shape=jax.ShapeDtypeStruct(h.shape, F32),
        compiler_params=pltpu.CompilerParams(
            dimension_semantics=("parallel", "parallel"), vmem_limit_bytes=VMEM_LIMIT),
        name="outproj_ffn",
    )(h, oa, ob, oc, wo, fn, wgu, wd)


def kernel(x, meta_tokens, attn_norm, w_in, dsa_q_norm, dsa_k_norm, fox_q_norm, fox_k_norm,
           fox_f_bias, gla_gate_w2, gla_gate_b, gla_out_norm, w_out, ffn_norm, w_gate_up, w_down):
    bsz, seq, _ = x.shape
    n_tok = N_META + seq
    t_work = _work_len(n_tok)
    meta = jnp.broadcast_to(meta_tokens[None].astype(x.dtype), (bsz, N_META, D_MODEL))
    h = jnp.concatenate([meta, x, jnp.zeros((bsz, t_work - n_tok, D_MODEL), x.dtype)], axis=1)
    cos_t, sin_t = _rope_tables(t_work)
    g256 = _group_matrix()
    k_top = min(TOPK_MAX, seq // 4)
    for l in range(w_in.shape[0]):
        an, w_perm, vecs, w2_pad, sm_bias = _prep_layer_params(
            l, attn_norm, w_in, dsa_q_norm, dsa_k_norm, fox_q_norm, fox_k_norm, fox_f_bias,
            gla_gate_w2, gla_gate_b)
        (dq, dkk, dva, iq, ikk, fq, fk, fva, gq, gk, gv, gg, la, small, key_norms) = _inproj(
            h, an, w_perm, cos_t, sin_t, g256, vecs, w2_pad, sm_bias)
        small_t, fkb = _fox_prep(small, fk)
        oa = _dsa_attention(dq, dkk, dva, iq, ikk, small_t, key_norms, k_top, n_tok)
        ob = _fox_attention(fq, fkb, fva, key_norms)
        oc = _gla(gq, gk, gv, gg, la, jnp.tile(gla_out_norm[l], N_HEADS)[None, :])
        h = _outproj_ffn(h, oa, ob, oc, w_out[l].astype(BF16), ffn_norm[l][None, :],
                         w_gate_up[l].astype(BF16), w_down[l].astype(BF16))
    return h[:, N_META:n_tok]
```

```python
import functools

import numpy as np
import jax
import jax.numpy as jnp
from jax import lax
from jax.experimental import pallas as pl
from jax.experimental.pallas import tpu as pltpu

F32 = jnp.float32
BF16 = jnp.bfloat16

D_MODEL = 1024
HEAD_DIM = 64
N_META = 16
ROPE_THETA = 500000.0
ROPE_DIM = HEAD_DIM // 4
ROPE_HALF = ROPE_DIM // 2
NEG = -1e30
EPS = 1e-6

N_HEADS = 4
TOPK_MAX = 256
GLA_DV = 128
GLA_RANK = 16
GLA_TAU = 16.0
GLA_CHUNK = 64
GLA_SUB = 16
D_FF = 2816

LOG2E = 1.4426950408889634
BIAS_TERMS = 3
KN_DSA, KN_FOX = 0, 1
NORM_MARGIN = 1.01
SAFE_LOG2_SPAN = 100.0
LANES = 128
SEQ_ALIGN = 512
VMEM_LIMIT = 56 * 1024 * 1024

_SLABS = (("dq", 256), ("dkk", 128), ("dva", 128), ("iq", 256), ("ikk", 128),
          ("fq", 256), ("fk", 256), ("fv", 256), ("gq", 256), ("gk", 256),
          ("gv", 512), ("gg", 512), ("small", 128))
_SLAB_OFF = {}
_off = 0
for _name, _width in _SLABS:
    _SLAB_OFF[_name] = (_off, _width)
    _off += _width
N_PROJ = _off
SM_IW, SM_FF, SM_GLR = 0, 4, 8


def _work_len(n_tok):
    return -(-n_tok // SEQ_ALIGN) * SEQ_ALIGN


def _row_tile(t_work):
    for cand in (768, 640, 512):
        if t_work % cand == 0:
            return cand
    raise ValueError(f"unsupported working length {t_work}")


def _dot(a, b):
    return jnp.dot(a, b, preferred_element_type=F32)


def _dot_nt(a, b):
    return lax.dot_general(a, b, (((1,), (1,)), ((), ())), preferred_element_type=F32)


def _dot_tn(a, b):
    return lax.dot_general(a, b, (((0,), (0,)), ((), ())), preferred_element_type=F32)


def _split3(x):
    h1 = x.astype(BF16)
    r1 = x - h1.astype(F32)
    h2 = r1.astype(BF16)
    h3 = (r1 - h2.astype(F32)).astype(BF16)
    return h1, h2, h3


def _log_sigmoid(x):
    return jnp.minimum(x, 0.0) - jnp.log1p(jnp.exp(-jnp.abs(x)))


def _silu(x):
    return x / (1.0 + jnp.exp(-x))


def _group_rms(y, gmat, gain):
    yy = y * y
    hi = yy.astype(BF16)
    lo = (yy - hi.astype(F32)).astype(BF16)
    ss = _dot(hi, gmat) + _dot(lo, gmat)
    return y * lax.rsqrt(ss * (1.0 / HEAD_DIM) + EPS) * gain


def _rope(y, cos, sin):
    width = y.shape[-1]
    lane = lax.broadcasted_iota(jnp.int32, y.shape, 1) % HEAD_DIM
    upper = pltpu.roll(y, width - ROPE_HALF, axis=1)
    lower = pltpu.roll(y, ROPE_HALF, axis=1)
    partner = jnp.where(lane < ROPE_HALF, upper, lower)
    return y * cos + partner * sin


def _inproj_kernel(x_ref, an_ref, w_ref, cos_ref, sin_ref, g256_ref, vec_ref, w2_ref, sm_ref,
                   dq_ref, dkk_ref, dva_ref, iq_ref, ikk_ref, fq_ref, fk_ref, fva_ref,
                   gq_ref, gk_ref, gv_ref, gg_ref, la_ref, small_ref, kn_ref):
    x = x_ref[0]
    ms = jnp.mean(x * x, axis=-1, keepdims=True)
    a = (x * lax.rsqrt(ms + EPS) * an_ref[...]).astype(BF16)

    def proj(name):
        off, width = _SLAB_OFF[name]
        return _dot(a, w_ref[:, off:off + width])

    def ones_in_upper_half(y):
        lane = lax.broadcasted_iota(jnp.int32, y.shape, 1) % LANES
        return jnp.where(lane < HEAD_DIM, y, 1.0)

    cos = cos_ref[...]
    sin = sin_ref[...]
    g256 = g256_ref[...]
    g128 = g256[:LANES, :LANES]
    dqn, dkn, fqn, fkn = vec_ref[0:1, :], vec_ref[1:2, :LANES], vec_ref[2:3, :], vec_ref[3:4, :]
    scale = HEAD_DIM ** -0.5

    def max_sq_norm(k):
        kf = k.astype(F32)
        return jnp.max(_dot((kf * kf).astype(BF16), g128), axis=0, keepdims=True)

    dq_ref[0] = (_rope(_group_rms(proj("dq"), g256, dqn), cos, sin) * (scale * LOG2E)).astype(BF16)
    dkk = _rope(_group_rms(proj("dkk"), g128, dkn), cos[:, :LANES], sin[:, :LANES]).astype(BF16)
    dkk_ref[0] = dkk
    key_norms = [max_sq_norm(dkk)]
    dva_ref[0] = ones_in_upper_half(proj("dva")).astype(BF16)
    iq_ref[0] = (_rope(proj("iq"), cos, sin) * scale).astype(BF16)
    ikk_ref[0] = _rope(proj("ikk"), cos[:, :LANES], sin[:, :LANES]).astype(BF16)
    fq, fk, fv = proj("fq"), proj("fk"), proj("fv")
    lane = lax.broadcasted_iota(jnp.int32, (fq.shape[0], LANES), 1)
    in_head = lane < HEAD_DIM
    q_pad = jnp.where(lane < HEAD_DIM + BIAS_TERMS, 1.0, 0.0)
    for pair in range(N_HEADS // 2):
        ps = slice(pair * LANES, (pair + 1) * LANES)
        q_pair = _group_rms(fq[:, ps], g128, fqn[:, :LANES]) * (scale * LOG2E)
        k_pair = _group_rms(fk[:, ps], g128, fkn[:, :LANES])
        v_pair = fv[:, ps]
        for odd in range(2):
            hs = slice((2 * pair + odd) * LANES, (2 * pair + odd + 1) * LANES)
            down = (lambda y: pltpu.roll(y, HEAD_DIM, axis=1)) if odd else (lambda y: y)
            fq_ref[0, :, hs] = jnp.where(in_head, down(q_pair), q_pad).astype(BF16)
            k_h = jnp.where(in_head, down(k_pair), 0.0).astype(BF16)
            fk_ref[0, :, hs] = k_h
            key_norms.append(max_sq_norm(k_h))
            fva_ref[0, :, hs] = jnp.where(in_head, down(v_pair), 1.0).astype(BF16)
    kn_ref[0, 0] = jnp.concatenate(key_norms + [jnp.zeros((8 - len(key_norms), LANES), F32)], axis=0)
    gq_ref[0] = proj("gq") * scale
    gk_ref[0] = proj("gk")
    gv_ref[0] = proj("gv").astype(BF16)
    gg_ref[0] = proj("gg")

    small = proj("small")
    lane = lax.broadcasted_iota(jnp.int32, small.shape, 1)
    small_ref[0] = jnp.where(lane < SM_FF, small * (N_HEADS ** -0.5),
                             _log_sigmoid(small + sm_ref[0:1, :]))
    gate = _dot(small.astype(BF16), w2_ref[...]) + vec_ref[4:5, :]
    la_ref[0] = _log_sigmoid(gate) * (LOG2E / GLA_TAU)


def _inproj(h, an, w_perm, cos_t, sin_t, g256, vecs, w2_pad, sm_bias):
    bsz, t_work, _ = h.shape
    tm = _row_tile(t_work)
    grid = (bsz, t_work // tm)

    def rows(width, dtype):
        return (jax.ShapeDtypeStruct((bsz, t_work, width), dtype),
                pl.BlockSpec((1, tm, width), lambda b, j: (b, j, 0)))

    outs = [rows(256, BF16), rows(128, BF16), rows(128, BF16), rows(256, BF16), rows(128, BF16),
            rows(512, BF16), rows(512, BF16), rows(512, BF16), rows(256, F32), rows(256, F32),
            rows(512, BF16), rows(512, F32), rows(256, F32), rows(128, F32),
            (jax.ShapeDtypeStruct((bsz, t_work // tm, 8, LANES), F32),
             pl.BlockSpec((1, 1, 8, LANES), lambda b, j: (b, j, 0, 0)))]
    const = lambda shape: pl.BlockSpec(shape, lambda b, j: (0,) * len(shape))
    return pl.pallas_call(
        _inproj_kernel,
        grid=grid,
        in_specs=[pl.BlockSpec((1, tm, D_MODEL), lambda b, j: (b, j, 0)),
                  const((1, D_MODEL)), const((D_MODEL, N_PROJ)),
                  pl.BlockSpec((tm, 256), lambda b, j: (j, 0)),
                  pl.BlockSpec((tm, 256), lambda b, j: (j, 0)),
                  const((256, 256)), const((8, 256)), const((LANES, 256)), const((8, LANES))],
        out_specs=[o[1] for o in outs],
        out_shape=[o[0] for o in outs],
        compiler_params=pltpu.CompilerParams(
            dimension_semantics=("parallel", "parallel"), vmem_limit_bytes=VMEM_LIMIT),
        name="inproj",
    )(h, an, w_perm, cos_t, sin_t, g256, vecs, w2_pad, sm_bias)


def _prep_layer_params(l, attn_norm, w_in, dsa_q_norm, dsa_k_norm, fox_q_norm, fox_k_norm,
                       fox_f_bias, gla_gate_w2, gla_gate_b):
    w = w_in[l]
    splits = np.cumsum([256, 64, 64, 256, 4, 64, 256, 256, 256, 4, 256, 256, 512, 512, 16])[:-1]
    (dq, dk, dv, iq, iw, ik, fq, fk, fv, ff, gq, gk, gv, gg, glr) = jnp.split(w, splits, axis=1)
    small = jnp.concatenate(
        [iw, ff, glr, jnp.zeros((D_MODEL, LANES - 4 - 4 - GLA_RANK), w.dtype)], axis=1)
    z64 = jnp.zeros((D_MODEL, HEAD_DIM), w.dtype)
    w_perm = jnp.concatenate([dq, dk, dk, dv, z64, iq, ik, ik, fq, fk, fv, gq, gk, gv, gg, small],
                             axis=1).astype(BF16)
    tile4 = lambda g: jnp.tile(g, N_HEADS)
    vecs = jnp.zeros((8, 256), F32)
    vecs = vecs.at[0].set(tile4(dsa_q_norm[l])).at[1].set(tile4(dsa_k_norm[l]))
    vecs = vecs.at[2].set(tile4(fox_q_norm[l])).at[3].set(tile4(fox_k_norm[l]))
    vecs = vecs.at[4].set(gla_gate_b[l])
    w2_pad = jnp.zeros((LANES, 256), F32).at[SM_GLR:SM_GLR + GLA_RANK].set(gla_gate_w2[l]).astype(BF16)
    sm_bias = jnp.zeros((8, LANES), F32).at[0, SM_FF:SM_FF + N_HEADS].set(fox_f_bias[l])
    return attn_norm[l][None, :], w_perm, vecs, w2_pad, sm_bias


def _rope_tables(t_work):
    inv = jnp.power(ROPE_THETA, -jnp.arange(ROPE_HALF, dtype=F32) * 2.0 / ROPE_DIM)
    ang = jnp.arange(t_work).astype(F32)[:, None] * inv[None, :]
    cos, sin = jnp.cos(ang), jnp.sin(ang)
    rest = HEAD_DIM - ROPE_DIM
    cos64 = jnp.concatenate([cos, cos, jnp.ones((t_work, rest), F32)], axis=1)
    sin64 = jnp.concatenate([-sin, sin, jnp.zeros((t_work, rest), F32)], axis=1)
    return jnp.tile(cos64, (1, N_HEADS)), jnp.tile(sin64, (1, N_HEADS))


def _group_matrix():
    idx = np.arange(256) // HEAD_DIM
    return jnp.asarray((idx[:, None] == idx[None, :]).astype(np.float32), dtype=BF16)


DSA_TQ = 256
DSA_TK = 512
KEY_NEG_INF = -2139095041
KEY_NEG_ZERO = -1
KEY_NEG_MIN_NORMAL = -8388609
NEG_MIN_NORMAL = -1.1754943508222875e-38
SEARCH_FEW = 4.0
SEARCH_MANY = 16.0
UNCHECKED_PROBES = 12
GUIDED_PROBES = 64
MAX_PROBES = GUIDED_PROBES + 40
COUNT_ROWS = 128


def _key_to_f32(key):
    bits = key ^ ((key >> 31) & 0x7FFFFFFF)
    return lax.bitcast_convert_type(bits, F32)


def _f32_to_key(value):
    bits = lax.bitcast_convert_type(value, jnp.int32)
    return bits ^ ((bits >> 31) & 0x7FFFFFFF)


def _head_lane_mask(shape, head):
    lane = lax.broadcasted_iota(jnp.int32, shape, 1)
    return (lane < HEAD_DIM) if head % 2 == 0 else (lane >= HEAD_DIM)


def _masked_heads(slabs):
    return [jnp.where(_head_lane_mask(slabs[h // 2].shape, h), slabs[h // 2],
                      jnp.zeros_like(slabs[h // 2])) for h in range(N_HEADS)]


def _softmax_step(s, m, acc, v_aug):
    m_new = jnp.maximum(m, jnp.max(s, axis=-1, keepdims=True))
    p = jnp.exp2(s - m_new)
    return m_new, acc * jnp.exp2(m - m_new) + _dot(p.astype(BF16), v_aug)


def _normalise_heads(accs):
    outs = [acc / pltpu.roll(acc, HEAD_DIM, axis=1) for acc in accs]
    lane = lax.broadcasted_iota(jnp.int32, outs[0].shape, 1)
    return [jnp.where(lane < HEAD_DIM, outs[2 * p], pltpu.roll(outs[2 * p + 1], HEAD_DIM, axis=1))
            for p in range(2)]


def _dsa_kernel(k_top, n_tok, *refs):
    o_ref = refs[-2]
    real = pl.program_id(1) * DSA_TQ < n_tok
    pl.when(real)(functools.partial(_dsa_tile, k_top, *refs))

    @pl.when(jnp.logical_not(real))
    def _():
        o_ref[...] = jnp.zeros_like(o_ref)


def _dsa_tile(k_top, dq_ref, kk_ref, va_ref, iq_ref, ik_ref, wt_ref, tri_ref, kn_ref, o_ref,
              s_ref):
    i = pl.program_id(1)
    q0 = i * DSA_TQ
    n_tiles = (q0 + DSA_TQ + DSA_TK - 1) // DSA_TK
    last = n_tiles - 1
    key = lax.broadcasted_iota(jnp.int32, (DSA_TK, DSA_TQ), 0)
    qry = q0 + lax.broadcasted_iota(jnp.int32, (DSA_TK, DSA_TQ), 1)
    head = lambda x, h: x[:, h * DSA_TQ:(h + 1) * DSA_TQ]

    iq_all = jnp.concatenate(
        _masked_heads([iq_ref[0, :, 0:LANES], iq_ref[0, :, LANES:2 * LANES]]), axis=0)
    w_h = [wt_ref[0, SM_IW + h:SM_IW + h + 1, :] for h in range(N_HEADS)]

    def score_tile(j, carry, causal):
        k0 = pl.multiple_of(j * DSA_TK, DSA_TK)
        dots = jnp.maximum(_dot_nt(ik_ref[0, pl.ds(k0, DSA_TK), :], iq_all), 0.0)
        s = w_h[0] * head(dots, 0)
        for h in range(1, N_HEADS):
            s = s + w_h[h] * head(dots, h)
        if causal:
            s = jnp.where(k0 + key <= qry, s, NEG)
        s_ref[pl.ds(k0, DSA_TK), :] = s
        row_max, above_zero, above_neg = carry
        for part in range(DSA_TK // COUNT_ROWS):
            chunk = s[part * COUNT_ROWS:(part + 1) * COUNT_ROWS]
            above_zero = above_zero + jnp.where(chunk > 0.0, 1.0, 0.0)
            above_neg = above_neg + jnp.where(chunk > NEG_MIN_NORMAL, 1.0, 0.0)
        return jnp.maximum(row_max, jnp.max(s, axis=0, keepdims=True)), above_zero, above_neg

    zeros = jnp.zeros((COUNT_ROWS, DSA_TQ), F32)
    row_max, above_zero, above_neg = score_tile(last, lax.fori_loop(
        0, last, functools.partial(score_tile, causal=False),
        (jnp.full((1, DSA_TQ), NEG, F32), zeros, zeros)), causal=True)

    kf = jnp.float32(k_top)

    def count_above(t):
        tb = jnp.broadcast_to(t, (COUNT_ROWS, DSA_TQ))

        def body(j, acc, span):
            k0 = pl.multiple_of(j * span, span)
            for part in range(span // COUNT_ROWS):
                s = s_ref[pl.ds(k0 + part * COUNT_ROWS, COUNT_ROWS), :]
                acc = acc + jnp.where(s > tb, 1.0, 0.0)
            return acc

        acc = lax.fori_loop(0, n_tiles // 2, functools.partial(body, span=2 * DSA_TK),
                            jnp.zeros((COUNT_ROWS, DSA_TQ), F32))
        acc = lax.fori_loop(2 * (n_tiles // 2), n_tiles, functools.partial(body, span=DSA_TK), acc)
        return jnp.sum(acc, axis=0, keepdims=True)

    def midpoint(lo, hi):
        return (lo >> 1) + (hi >> 1) + (lo & hi & 1)

    def converged(lo, hi):
        return (midpoint(lo, hi) == lo) | ((lo >= KEY_NEG_MIN_NORMAL) & (hi <= 0))

    col1 = lambda value, dtype: jnp.full((1, DSA_TQ), value, dtype)

    def absorb(state, probe, c):
        lo, hi, clo, chi, wlo, whi, side = state
        live = probe != lo
        up = live & (c >= kf)
        down = live & (c <= kf)
        wlo = jnp.where(down & (side < 0), wlo * 0.5, jnp.where(up, 1.0, wlo))
        whi = jnp.where(up & (side > 0), whi * 0.5, jnp.where(down, 1.0, whi))
        side = jnp.where(up, 1, jnp.where(down, -1, side))
        lo, clo = jnp.where(up, probe, lo), jnp.where(up, c, clo)
        hi, chi = jnp.where(down, probe, hi), jnp.where(down, c, chi)
        return lo, hi, clo, chi, wlo, whi, side

    def probe_once(it, state):
        lo, hi, clo, chi, wlo, whi, side = state
        f_lo, f_hi = _key_to_f32(lo), _key_to_f32(hi)
        target = kf - 0.5
        log_count = lambda c: jnp.log2(jnp.maximum(c, 0.25))
        many = clo - chi > SEARCH_MANY
        g_lo = jnp.where(many, log_count(clo) - np.log2(k_top - 0.5), clo - target) * wlo
        g_hi = jnp.where(many, np.log2(k_top - 0.5) - log_count(chi), target - chi) * whi
        halve = (clo - chi <= SEARCH_FEW) | (col1(it % 8, jnp.int32) == 7)
        guess = _f32_to_key(f_lo + (f_hi - f_lo) * jnp.where(halve, 0.5, g_lo / (g_lo + g_hi)))
        guided = col1(it, jnp.int32) < GUIDED_PROBES
        probe = jnp.where((guess > lo) & (guess < hi) & guided, guess, midpoint(lo, hi))
        probe = jnp.where(converged(lo, hi), lo, probe)
        return absorb(state, probe, count_above(_key_to_f32(probe)))

    def search_cond(carry):
        it, pending = carry[0], carry[1]
        return (pending > 0) & (it < MAX_PROBES)

    def search_body(carry):
        it, state = carry[0], carry[2]
        state = probe_once(it + 1, probe_once(it, state))
        return it + 2, jnp.max(jnp.where(converged(state[0], state[1]), 0, 1)), state

    n_swept = (n_tiles * DSA_TK).astype(F32)
    state = (col1(KEY_NEG_INF, jnp.int32), _f32_to_key(row_max),
             jnp.broadcast_to(n_swept, (1, DSA_TQ)), col1(0.0, F32), col1(1.0, F32), col1(1.0, F32),
             col1(0, jnp.int32))
    for fixed, counts in ((KEY_NEG_ZERO, above_zero), (KEY_NEG_MIN_NORMAL, above_neg)):
        inside = (state[0] < fixed) & (fixed < state[1])
        state = absorb(state, jnp.where(inside, fixed, state[0]),
                       jnp.sum(counts, axis=0, keepdims=True))
    state = lax.fori_loop(0, UNCHECKED_PROBES, probe_once, state)
    state = lax.while_loop(search_cond, search_body,
                           (jnp.int32(UNCHECKED_PROBES), jnp.int32(1), state))[2]
    thr = _key_to_f32(state[1])
    n_ties = kf - state[3]

    to_column = lambda r: jnp.broadcast_to(r, (8, DSA_TQ)).T[:, 0:1]
    thr_c, ties_c = to_column(thr), to_column(n_ties)
    q_all = jnp.concatenate(
        _masked_heads([dq_ref[0, :, 0:LANES], dq_ref[0, :, LANES:2 * LANES]]), axis=0)
    tri = tri_ref[...]
    tri_lo = jnp.where(lax.broadcasted_iota(jnp.int32, (LANES, LANES), 0)
                       >= lax.broadcasted_iota(jnp.int32, (LANES, LANES), 1), 1.0, 0.0).astype(BF16)
    qrow = q0 + lax.broadcasted_iota(jnp.int32, (DSA_TQ, DSA_TK), 0)
    kcol = lax.broadcasted_iota(jnp.int32, (DSA_TQ, DSA_TK), 1)
    n_blocks = DSA_TK // LANES

    reach = _logit_reach(q_all, jnp.max(kn_ref[0], axis=0)[KN_DSA:KN_DSA + 1, 0:1])

    def selected_logits(j, seen, causal):
        k0 = pl.multiple_of(j * DSA_TK, DSA_TK)
        s = s_ref[pl.ds(k0, DSA_TK), :].T
        tie = s == thr_c
        tie_b = jnp.where(tie, 1.0, 0.0).astype(BF16)
        local = [_dot(tie_b[:, b * LANES:(b + 1) * LANES], tri) for b in range(n_blocks)]
        ranks = []
        for b in range(n_blocks):
            ranks.append(local[b] + seen)
            seen = seen + local[b][:, LANES - 1:LANES]
        sel = (s > thr_c) | (tie & (jnp.concatenate(ranks, axis=1) <= ties_c))
        if causal:
            sel = sel & (k0 + kcol <= qrow)
        logits = _dot_nt(q_all, kk_ref[0, pl.ds(k0, DSA_TK), :])
        logits = jnp.where(sel[None], logits.reshape(N_HEADS, DSA_TQ, DSA_TK), NEG)
        return seen, logits.reshape(N_HEADS * DSA_TQ, DSA_TK), va_ref[0, pl.ds(k0, DSA_TK), :]

    def capped(j, carry, causal):
        seen, acc = carry
        k0 = pl.multiple_of(j * DSA_TK, DSA_TK)
        s = s_ref[pl.ds(k0, DSA_TK), :]
        tie = s == thr
        tie_b = jnp.where(tie, 1.0, 0.0).astype(BF16)
        ranks = []
        for b in range(n_blocks):
            local = _dot(tri_lo, tie_b[b * LANES:(b + 1) * LANES, :])
            ranks.append(local + seen)
            seen = seen + local[LANES - 1:LANES, :]
        sel = (s > thr) | (tie & (jnp.concatenate(ranks, axis=0) <= n_ties))
        if causal:
            sel = sel & (k0 + key <= qry)
        keep = jnp.where(sel, 1.0, 0.0).astype(BF16).T
        logits = _dot_nt(q_all, kk_ref[0, pl.ds(k0, DSA_TK), :])
        p = jnp.exp2(logits - reach).astype(BF16).reshape(N_HEADS, DSA_TQ, DSA_TK) * keep[None]
        return seen, acc + _dot(p.reshape(N_HEADS * DSA_TQ, DSA_TK), va_ref[0, pl.ds(k0, DSA_TK), :])

    def online(j, carry, causal):
        seen, logits, va_t = selected_logits(j, carry[0], causal)
        return (seen,) + _softmax_step(logits, carry[1], carry[2], va_t)

    def sweep(step, *stats):
        carry = stats + (jnp.zeros((N_HEADS * DSA_TQ, LANES), F32),)

        def pair(j, c):
            return step(2 * j + 1, step(2 * j, c, causal=False), causal=False)

        carry = lax.fori_loop(0, last // 2, pair, carry)
        carry = lax.fori_loop(2 * (last // 2), last, functools.partial(step, causal=False), carry)
        return step(last, carry, causal=True)[-1]

    acc = lax.cond(2.0 * jnp.max(reach) <= SAFE_LOG2_SPAN,
                   functools.partial(sweep, capped, jnp.zeros((1, DSA_TQ), F32)),
                   functools.partial(sweep, online, jnp.zeros((DSA_TQ, 1), F32),
                                     jnp.full((N_HEADS * DSA_TQ, 1), NEG, F32)))
    o_ref[0] = jnp.concatenate(
        _normalise_heads([acc[h * DSA_TQ:(h + 1) * DSA_TQ] for h in range(N_HEADS)]),
        axis=1).astype(o_ref.dtype)


def _dsa_attention(dq, dkk, dva, iq, ikk, small_t, key_norms, k_top, n_tok):
    bsz, t_work, _ = dq.shape
    tri = jnp.asarray(np.triu(np.ones((LANES, LANES), np.float32)), dtype=BF16)
    tile = lambda width: pl.BlockSpec((1, DSA_TQ, width), lambda b, i: (b, i, 0))
    full = pl.BlockSpec((1, t_work, LANES), lambda b, i: (b, 0, 0), pipeline_mode=pl.Buffered(1))
    return pl.pallas_call(
        functools.partial(_dsa_kernel, k_top, n_tok),
        grid=(bsz, t_work // DSA_TQ),
        in_specs=[tile(256), full, full, tile(256), full,
                  pl.BlockSpec((1, 8, DSA_TQ), lambda b, i: (b, 0, i)),
                  pl.BlockSpec((LANES, LANES), lambda b, i: (0, 0)),
                  pl.BlockSpec((1,) + key_norms.shape[1:], lambda b, i: (b, 0, 0, 0))],
        out_specs=tile(256),
        out_shape=jax.ShapeDtypeStruct((bsz, t_work, 256), BF16),
        scratch_shapes=[pltpu.VMEM((t_work, DSA_TQ), F32)],
        compiler_params=pltpu.CompilerParams(
            dimension_semantics=("parallel", "parallel"), vmem_limit_bytes=VMEM_LIMIT),
        name="dsa_attention",
    )(dq, dkk, dva, iq, ikk, small_t, tri, key_norms)


FOX_TQ = 512
FOX_TK = 512
CUM_T = 256


def _fox_prep_kernel(x_ref, tri_ref, place_ref, k_ref, xt_ref, kb_ref, carry_ref):
    @pl.when(pl.program_id(1) == 0)
    def _():
        carry_ref[...] = jnp.zeros_like(carry_ref)

    tri = tri_ref[...]
    x = x_ref[0]
    h1, h2, h3 = _split3(x)
    c = _dot(tri, h1) + _dot(tri, h2) + _dot(tri, h3) + carry_ref[0:1, :]
    carry_ref[...] = jnp.broadcast_to(c[CUM_T - 1:CUM_T, :], carry_ref.shape)
    xt_ref[0] = x.T[0:8, :]
    terms = _split3(c * -LOG2E)
    bias = sum(_dot(terms[t], place_ref[t]) for t in range(BIAS_TERMS))
    lane = lax.broadcasted_iota(jnp.int32, bias.shape, 1) % LANES
    kb_ref[0] = jnp.where((lane >= HEAD_DIM) & (lane < HEAD_DIM + BIAS_TERMS),
                          bias.astype(BF16), k_ref[0])


def _fox_prep(small, fk):
    bsz, t_work, width = fk.shape
    tri = jnp.asarray(np.tril(np.ones((CUM_T, CUM_T), np.float32)), dtype=BF16)
    place = np.zeros((BIAS_TERMS, LANES, width), np.float32)
    for t in range(BIAS_TERMS):
        for h in range(N_HEADS):
            place[t, SM_FF + h, h * LANES + HEAD_DIM + t] = 1.0
    rows = lambda w: pl.BlockSpec((1, CUM_T, w), lambda b, j: (b, j, 0))
    return pl.pallas_call(
        _fox_prep_kernel,
        grid=(bsz, t_work // CUM_T),
        in_specs=[rows(LANES), pl.BlockSpec((CUM_T, CUM_T), lambda b, j: (0, 0)),
                  pl.BlockSpec((BIAS_TERMS, LANES, width), lambda b, j: (0, 0, 0)), rows(width)],
        out_specs=[pl.BlockSpec((1, 8, CUM_T), lambda b, j: (b, 0, j)), rows(width)],
        out_shape=[jax.ShapeDtypeStruct((bsz, 8, t_work), F32),
                   jax.ShapeDtypeStruct(fk.shape, BF16)],
        scratch_shapes=[pltpu.VMEM((8, LANES), F32)],
        compiler_params=pltpu.CompilerParams(dimension_semantics=("parallel", "arbitrary")),
        name="fox_prep",
    )(small, tri, jnp.asarray(place, dtype=BF16), fk)


def _logit_reach(q, kmax_sq):
    qf = q.astype(F32)
    return jnp.sqrt(jnp.sum(qf * qf, axis=-1, keepdims=True) * kmax_sq) * NORM_MARGIN


def _fox_kernel(q_ref, k_ref, v_ref, kn_ref, o_ref):
    i = pl.program_id(1)
    q0 = pl.multiple_of(i * FOX_TQ, FOX_TQ)
    n_full = q0 // FOX_TK
    row = q0 + lax.broadcasted_iota(jnp.int32, (FOX_TQ, FOX_TK), 0)
    col = n_full * FOX_TK + lax.broadcasted_iota(jnp.int32, (FOX_TQ, FOX_TK), 1)
    heads = [slice(h * LANES, (h + 1) * LANES) for h in range(N_HEADS)]

    lane = lax.broadcasted_iota(jnp.int32, (FOX_TQ, LANES), 1)
    bias_lanes = (lane >= HEAD_DIM) & (lane < HEAD_DIM + BIAS_TERMS)
    kmax_sq = jnp.max(kn_ref[0], axis=0)
    caps, span = [], jnp.float32(0.0)
    for h in range(N_HEADS):
        q_h = q_ref[0, :, heads[h]]
        reach = _logit_reach(jnp.where(lane < HEAD_DIM, q_h, jnp.zeros_like(q_h)),
                             kmax_sq[KN_FOX + h:KN_FOX + h + 1, 0:1])
        own = k_ref[0, pl.ds(q0, FOX_TQ), heads[h]].astype(F32)
        caps.append(reach + jnp.sum(jnp.where(bias_lanes, own, 0.0), axis=-1, keepdims=True))
        span = jnp.maximum(span, 2.0 * jnp.max(reach))

    def logits(j, h, diag):
        k0 = pl.multiple_of(j * FOX_TK, FOX_TK)
        s = _dot_nt(q_ref[0, :, heads[h]], k_ref[0, pl.ds(k0, FOX_TK), heads[h]])
        return jnp.where(col <= row, s, NEG) if diag else s

    def values(j, h):
        return v_ref[0, pl.ds(pl.multiple_of(j * FOX_TK, FOX_TK), FOX_TK), heads[h]]

    def capped(j, accs, diag):
        return tuple(accs[h] + _dot(jnp.exp2(logits(j, h, diag) - caps[h]).astype(BF16), values(j, h))
                     for h in range(N_HEADS))

    def online(j, carry, diag):
        return tuple(_softmax_step(logits(j, h, diag), *carry[h], values(j, h))
                     for h in range(N_HEADS))

    def capped_sweep():
        accs = tuple(jnp.zeros((FOX_TQ, LANES), F32) for _ in range(N_HEADS))
        accs = lax.fori_loop(0, n_full, functools.partial(capped, diag=False), accs)
        return capped(n_full, accs, diag=True)

    def online_sweep():
        carry = tuple((jnp.full((FOX_TQ, 1), NEG, F32), jnp.zeros((FOX_TQ, LANES), F32))
                      for _ in range(N_HEADS))
        carry = lax.fori_loop(0, n_full, functools.partial(online, diag=False), carry)
        return tuple(acc for _, acc in online(n_full, carry, diag=True))

    accs = lax.cond(span <= SAFE_LOG2_SPAN, capped_sweep, online_sweep)
    o_ref[0] = jnp.concatenate(_normalise_heads(list(accs)), axis=1).astype(o_ref.dtype)


def _fox_attention(fqa, fkb, fva, key_norms):
    bsz, t_work, width = fqa.shape
    full = pl.BlockSpec((1, t_work, width), lambda b, i: (b, 0, 0), pipeline_mode=pl.Buffered(1))
    return pl.pallas_call(
        _fox_kernel,
        grid=(bsz, t_work // FOX_TQ),
        in_specs=[pl.BlockSpec((1, FOX_TQ, width), lambda b, i: (b, i, 0)), full, full,
                  pl.BlockSpec((1,) + key_norms.shape[1:], lambda b, i: (b, 0, 0, 0))],
        out_specs=pl.BlockSpec((1, FOX_TQ, 2 * LANES), lambda b, i: (b, i, 0)),
        out_shape=jax.ShapeDtypeStruct((bsz, t_work, 2 * LANES), BF16),
        compiler_params=pltpu.CompilerParams(
            dimension_semantics=("parallel", "parallel"), vmem_limit_bytes=VMEM_LIMIT),
        name="fox_attention",
    )(fqa, fkb, fva, key_norms)


GLA_NSUB = GLA_CHUNK // GLA_SUB


def _gla_kernel(q_ref, k_ref, v_ref, g_ref, la_ref, tri_ref, e_ref, gn_ref, o_ref, st_ref):
    @pl.when(pl.program_id(1) == 0)
    def _():
        st_ref[...] = jnp.zeros_like(st_ref)

    tri = tri_ref[...]
    emat = e_ref[...]
    lane = lax.broadcasted_iota(jnp.int32, (GLA_CHUNK, LANES), 1)
    rowblk = lax.broadcasted_iota(jnp.int32, (GLA_CHUNK, LANES), 0) // GLA_SUB
    tblk = lax.broadcasted_iota(jnp.int32, (GLA_CHUNK, GLA_CHUNK), 0) // GLA_SUB
    sblk = lax.broadcasted_iota(jnp.int32, (GLA_CHUNK, GLA_CHUNK), 1) // GLA_SUB
    trow = lax.broadcasted_iota(jnp.int32, (GLA_SUB, 256), 0)

    def chunk(c, carry):
        r0 = pl.multiple_of(c * GLA_CHUNK, GLA_CHUNK)
        rows = pl.ds(r0, GLA_CHUNK)
        h1, h2, h3 = _split3(la_ref[0, rows, :])
        b = _dot(tri, h1) + _dot(tri, h2) + _dot(tri, h3)
        q = q_ref[0, rows, :]
        k = k_ref[0, rows, :]
        v = v_ref[0, rows, :]
        b_last = b[GLA_CHUNK - 1:GLA_CHUNK, :]
        qd = q * jnp.exp2(b)
        kd = (k * jnp.exp2(b_last - b)).astype(BF16)
        starts = [jnp.zeros((1, 256), F32)] + [b[GLA_SUB * i - 1:GLA_SUB * i, :]
                                               for i in range(1, GLA_NSUB)]
        bsel = jnp.concatenate([jnp.broadcast_to(s, (GLA_SUB, 256)) for s in starts], axis=0)
        qn = q * jnp.exp2(b - bsel)

        diag = []
        for i in range(GLA_NSUB):
            rs = slice(GLA_SUB * i, GLA_SUB * (i + 1))
            b_i, q_i, k_i = b[rs], q[rs], k[rs]
            v_i = v[rs].astype(F32)
            ps = []
            for s in range(GLA_SUB):
                d = jnp.exp2(jnp.minimum(b_i - b_i[s:s + 1], 0.0))
                ps.append(jnp.where(trow >= s, q_i * d * k_i[s:s + 1], 0.0).astype(BF16))
            r = _dot(jnp.concatenate(ps, axis=0), emat)
            od = r[0:GLA_SUB] * v_i[0:1]
            for s in range(1, GLA_SUB):
                od = od + r[GLA_SUB * s:GLA_SUB * (s + 1)] * v_i[s:s + 1]
            diag.append(od)
        o_diag = jnp.concatenate(diag, axis=0)

        for slab in range(2):
            ls = slice(slab * LANES, (slab + 1) * LANES)
            qn_s, k_s, b_s = qn[:, ls], k[:, ls], b[:, ls]
            khat = jnp.concatenate(
                [(k_s * jnp.exp2(jnp.minimum(starts[i][:, ls] - b_s, 0.0))).astype(BF16)
                 for i in range(1, GLA_NSUB)], axis=1)
            for half in range(2):
                head = 2 * slab + half
                hs = slice(head * GLA_DV, (head + 1) * GLA_DV)
                in_head = (lane < HEAD_DIM) if half == 0 else (lane >= HEAD_DIM)
                qm = jnp.where(in_head, qn_s, 0.0)
                qhat = jnp.concatenate([jnp.where(rowblk == i, qm, 0.0).astype(BF16)
                                        for i in range(1, GLA_NSUB)], axis=1)
                att = jnp.where(sblk < tblk, _dot_nt(qhat, khat), 0.0)
                v_h = v[:, hs]
                st = st_ref[head]
                o = (_dot_nt(jnp.where(in_head, qd[:, ls], 0.0).astype(BF16), st.astype(BF16))
                     + _dot(att.astype(BF16), v_h) + o_diag[:, hs])
                st_ref[head] = st * jnp.exp2(b_last[:, ls]) + _dot_tn(v_h, kd[:, ls])
                y = o * lax.rsqrt(jnp.mean(o * o, axis=-1, keepdims=True) + EPS) * gn_ref[:, hs]
                o_ref[0, rows, hs] = (y * _silu(g_ref[0, rows, hs])).astype(o_ref.dtype)
        return carry

    lax.fori_loop(0, q_ref.shape[1] // GLA_CHUNK, chunk, 0, unroll=6)


def _gla(gq, gk, gv, gg, la, gain):
    bsz, t_work, _ = gq.shape
    tg = _row_tile(t_work)
    tri = jnp.asarray(np.tril(np.ones((GLA_CHUNK, GLA_CHUNK), np.float32)), dtype=BF16)
    emat = jnp.asarray(
        (np.arange(256)[:, None] // HEAD_DIM == np.arange(512)[None, :] // GLA_DV).astype(np.float32),
        dtype=BF16)
    rows = lambda width: pl.BlockSpec((1, tg, width), lambda b, j: (b, j, 0))
    const = lambda shape: pl.BlockSpec(shape, lambda b, j: (0,) * len(shape))
    return pl.pallas_call(
        _gla_kernel,
        grid=(bsz, t_work // tg),
        in_specs=[rows(256), rows(256), rows(512), rows(512), rows(256),
                  const((GLA_CHUNK, GLA_CHUNK)), const((256, 512)), const((1, 512))],
        out_specs=rows(512),
        out_shape=jax.ShapeDtypeStruct((bsz, t_work, 512), BF16),
        scratch_shapes=[pltpu.VMEM((N_HEADS, GLA_DV, LANES), F32)],
        compiler_params=pltpu.CompilerParams(
            dimension_semantics=("parallel", "arbitrary"), vmem_limit_bytes=VMEM_LIMIT),
        name="gla",
    )(gq, gk, gv, gg, la, tri, emat, gain)


FFN_CHUNK = 256


def _ffn_kernel(h_ref, oa_ref, ob_ref, oc_ref, wo_ref, fn_ref, wgu_ref, wd_ref, out_ref):
    h1 = (h_ref[0] + _dot(oa_ref[0], wo_ref[0:256, :]) + _dot(ob_ref[0], wo_ref[256:512, :])
          + _dot(oc_ref[0], wo_ref[512:1024, :]))
    ms = jnp.mean(h1 * h1, axis=-1, keepdims=True)
    f = (h1 * lax.rsqrt(ms + EPS) * fn_ref[...]).astype(BF16)
    out_ref[0] = h1
    for c in range(0, D_FF, FFN_CHUNK):
        gate = _dot(f, wgu_ref[:, c:c + FFN_CHUNK])
        up = _dot(f, wgu_ref[:, D_FF + c:D_FF + c + FFN_CHUNK])
        out_ref[0] += _dot((_silu(gate) * up).astype(BF16), wd_ref[c:c + FFN_CHUNK, :])


def _outproj_ffn(h, oa, ob, oc, wo, fn, wgu, wd):
    bsz, t_work, _ = h.shape
    tm = _row_tile(t_work)
    rows = lambda width: pl.BlockSpec((1, tm, width), lambda b, j: (b, j, 0))
    const = lambda shape: pl.BlockSpec(shape, lambda b, j: (0,) * len(shape),
                                       pipeline_mode=pl.Buffered(1))
    return pl.pallas_call(
        _ffn_kernel,
        grid=(bsz, t_work // tm),
        in_specs=[rows(D_MODEL), rows(256), rows(256), rows(512), const((D_MODEL, D_MODEL)),
                  const((1, D_MODEL)), const((D_MODEL, 2 * D_FF)), const((D_FF, D_MODEL))],
        out_specs=rows(D_MODEL),
        out_shape=jax.ShapeDtypeStruct(h.shape, F32),
        compiler_params=pltpu.CompilerParams(
            dimension_semantics=("parallel", "parallel"), vmem_limit_bytes=VMEM_LIMIT),
        name="outproj_ffn",
    )(h, oa, ob, oc, wo, fn, wgu, wd)


def kernel(x, meta_tokens, attn_norm, w_in, dsa_q_norm, dsa_k_norm, fox_q_norm, fox_k_norm,
           fox_f_bias, gla_gate_w2, gla_gate_b, gla_out_norm, w_out, ffn_norm, w_gate_up, w_down):
    bsz, seq, _ = x.shape
    n_tok = N_META + seq
    t_work = _work_len(n_tok)
    meta = jnp.broadcast_to(meta_tokens[None].astype(x.dtype), (bsz, N_META, D_MODEL))
    h = jnp.concatenate([meta, x, jnp.zeros((bsz, t_work - n_tok, D_MODEL), x.dtype)], axis=1)
    cos_t, sin_t = _rope_tables(t_work)
    g256 = _group_matrix()
    k_top = min(TOPK_MAX, seq // 4)
    for l in range(w_in.shape[0]):
        an, w_perm, vecs, w2_pad, sm_bias = _prep_layer_params(
            l, attn_norm, w_in, dsa_q_norm, dsa_k_norm, fox_q_norm, fox_k_norm, fox_f_bias,
            gla_gate_w2, gla_gate_b)
        (dq, dkk, dva, iq, ikk, fq, fk, fva, gq, gk, gv, gg, la, small, key_norms) = _inproj(
            h, an, w_perm, cos_t, sin_t, g256, vecs, w2_pad, sm_bias)
        small_t, fkb = _fox_prep(small, fk)
        oa = _dsa_attention(dq, dkk, dva, iq, ikk, small_t, key_norms, k_top, n_tok)
        ob = _fox_attention(fq, fkb, fva, key_norms)
        oc = _gla(gq, gk, gv, gg, la, jnp.tile(gla_out_norm[l], N_HEADS)[None, :])
        h = _outproj_ffn(h, oa, ob, oc, w_out[l].astype(BF16), ffn_norm[l][None, :],
                         w_gate_up[l].astype(BF16), w_down[l].astype(BF16))
    return h[:, N_META:n_tok]
```

```python
import functools

import numpy as np
import jax
import jax.numpy as jnp
from jax import lax
from jax.experimental import pallas as pl
from jax.experimental.pallas import tpu as pltpu

F32 = jnp.float32
BF16 = jnp.bfloat16

D_MODEL = 1024
HEAD_DIM = 64
N_META = 16
ROPE_THETA = 500000.0
ROPE_DIM = HEAD_DIM // 4
ROPE_HALF = ROPE_DIM // 2
NEG = -1e30
EPS = 1e-6

N_HEADS = 4
TOPK_MAX = 256
GLA_DV = 128
GLA_RANK = 16
GLA_TAU = 16.0
GLA_CHUNK = 64
GLA_SUB = 16
D_FF = 2816

LOG2E = 1.4426950408889634
BIAS_TERMS = 3
KN_DSA, KN_FOX = 0, 1
NORM_MARGIN = 1.01
SAFE_LOG2_SPAN = 100.0
LANES = 128
SEQ_ALIGN = 512
VMEM_LIMIT = 56 * 1024 * 1024

_SLABS = (("dq", 256), ("dkk", 128), ("dva", 128), ("iq", 256), ("ikk", 128),
          ("fq", 256), ("fk", 256), ("fv", 256), ("gq", 256), ("gk", 256),
          ("gv", 512), ("gg", 512), ("small", 128))
_SLAB_OFF = {}
_off = 0
for _name, _width in _SLABS:
    _SLAB_OFF[_name] = (_off, _width)
    _off += _width
N_PROJ = _off
SM_IW, SM_FF, SM_GLR = 0, 4, 8


def _work_len(n_tok):
    return -(-n_tok // SEQ_ALIGN) * SEQ_ALIGN


def _row_tile(t_work):
    for cand in (768, 640, 512):
        if t_work % cand == 0:
            return cand
    raise ValueError(f"unsupported working length {t_work}")


def _dot(a, b):
    return jnp.dot(a, b, preferred_element_type=F32)


def _dot_nt(a, b):
    return lax.dot_general(a, b, (((1,), (1,)), ((), ())), preferred_element_type=F32)


def _dot_tn(a, b):
    return lax.dot_general(a, b, (((0,), (0,)), ((), ())), preferred_element_type=F32)


def _split3(x):
    h1 = x.astype(BF16)
    r1 = x - h1.astype(F32)
    h2 = r1.astype(BF16)
    h3 = (r1 - h2.astype(F32)).astype(BF16)
    return h1, h2, h3


def _log_sigmoid(x):
    return jnp.minimum(x, 0.0) - jnp.log1p(jnp.exp(-jnp.abs(x)))


def _silu(x):
    return x / (1.0 + jnp.exp(-x))


def _group_rms(y, gmat, gain):
    yy = y * y
    hi = yy.astype(BF16)
    lo = (yy - hi.astype(F32)).astype(BF16)
    ss = _dot(hi, gmat) + _dot(lo, gmat)
    return y * lax.rsqrt(ss * (1.0 / HEAD_DIM) + EPS) * gain


def _rope(y, cos, sin):
    width = y.shape[-1]
    lane = lax.broadcasted_iota(jnp.int32, y.shape, 1) % HEAD_DIM
    upper = pltpu.roll(y, width - ROPE_HALF, axis=1)
    lower = pltpu.roll(y, ROPE_HALF, axis=1)
    partner = jnp.where(lane < ROPE_HALF, upper, lower)
    return y * cos + partner * sin


def _inproj_kernel(x_ref, an_ref, w_ref, cos_ref, sin_ref, g256_ref, vec_ref, w2_ref, sm_ref,
                   dq_ref, dkk_ref, dva_ref, iq_ref, ikk_ref, fq_ref, fk_ref, fva_ref,
                   gq_ref, gk_ref, gv_ref, gg_ref, la_ref, small_ref, kn_ref):
    x = x_ref[0]
    ms = jnp.mean(x * x, axis=-1, keepdims=True)
    a = (x * lax.rsqrt(ms + EPS) * an_ref[...]).astype(BF16)

    def proj(name):
        off, width = _SLAB_OFF[name]
        return _dot(a, w_ref[:, off:off + width])

    def ones_in_upper_half(y):
        lane = lax.broadcasted_iota(jnp.int32, y.shape, 1) % LANES
        return jnp.where(lane < HEAD_DIM, y, 1.0)

    cos = cos_ref[...]
    sin = sin_ref[...]
    g256 = g256_ref[...]
    g128 = g256[:LANES, :LANES]
    dqn, dkn, fqn, fkn = vec_ref[0:1, :], vec_ref[1:2, :LANES], vec_ref[2:3, :], vec_ref[3:4, :]
    scale = HEAD_DIM ** -0.5

    def max_sq_norm(k):
        kf = k.astype(F32)
        return jnp.max(_dot((kf * kf).astype(BF16), g128), axis=0, keepdims=True)

    dq_ref[0] = (_rope(_group_rms(proj("dq"), g256, dqn), cos, sin) * (scale * LOG2E)).astype(BF16)
    dkk = _rope(_group_rms(proj("dkk"), g128, dkn), cos[:, :LANES], sin[:, :LANES]).astype(BF16)
    dkk_ref[0] = dkk
    key_norms = [max_sq_norm(dkk)]
    dva_ref[0] = ones_in_upper_half(proj("dva")).astype(BF16)
    iq_ref[0] = (_rope(proj("iq"), cos, sin) * scale).astype(BF16)
    ikk_ref[0] = _rope(proj("ikk"), cos[:, :LANES], sin[:, :LANES]).astype(BF16)
    fq, fk, fv = proj("fq"), proj("fk"), proj("fv")
    lane = lax.broadcasted_iota(jnp.int32, (fq.shape[0], LANES), 1)
    in_head = lane < HEAD_DIM
    q_pad = jnp.where(lane < HEAD_DIM + BIAS_TERMS, 1.0, 0.0)
    for pair in range(N_HEADS // 2):
        ps = slice(pair * LANES, (pair + 1) * LANES)
        q_pair = _group_rms(fq[:, ps], g128, fqn[:, :LANES]) * (scale * LOG2E)
        k_pair = _group_rms(fk[:, ps], g128, fkn[:, :LANES])
        v_pair = fv[:, ps]
        for odd in range(2):
            hs = slice((2 * pair + odd) * LANES, (2 * pair + odd + 1) * LANES)
            down = (lambda y: pltpu.roll(y, HEAD_DIM, axis=1)) if odd else (lambda y: y)
            fq_ref[0, :, hs] = jnp.where(in_head, down(q_pair), q_pad).astype(BF16)
            k_h = jnp.where(in_head, down(k_pair), 0.0).astype(BF16)
            fk_ref[0, :, hs] = k_h
            key_norms.append(max_sq_norm(k_h))
            fva_ref[0, :, hs] = jnp.where(in_head, down(v_pair), 1.0).astype(BF16)
    kn_ref[0, 0] = jnp.concatenate(key_norms + [jnp.zeros((8 - len(key_norms), LANES), F32)], axis=0)
    gq_ref[0] = proj("gq") * scale
    gk_ref[0] = proj("gk")
    gv_ref[0] = proj("gv").astype(BF16)
    gg_ref[0] = proj("gg")

    small = proj("small")
    lane = lax.broadcasted_iota(jnp.int32, small.shape, 1)
    small_ref[0] = jnp.where(lane < SM_FF, small * (N_HEADS ** -0.5),
                             _log_sigmoid(small + sm_ref[0:1, :]))
    gate = _dot(small.astype(BF16), w2_ref[...]) + vec_ref[4:5, :]
    la_ref[0] = _log_sigmoid(gate) * (LOG2E / GLA_TAU)


def _inproj(h, an, w_perm, cos_t, sin_t, g256, vecs, w2_pad, sm_bias):
    bsz, t_work, _ = h.shape
    tm = _row_tile(t_work)
    grid = (bsz, t_work // tm)

    def rows(width, dtype):
        return (jax.ShapeDtypeStruct((bsz, t_work, width), dtype),
                pl.BlockSpec((1, tm, width), lambda b, j: (b, j, 0)))

    outs = [rows(256, BF16), rows(128, BF16), rows(128, BF16), rows(256, BF16), rows(128, BF16),
            rows(512, BF16), rows(512, BF16), rows(512, BF16), rows(256, F32), rows(256, F32),
            rows(512, BF16), rows(512, F32), rows(256, F32), rows(128, F32),
            (jax.ShapeDtypeStruct((bsz, t_work // tm, 8, LANES), F32),
             pl.BlockSpec((1, 1, 8, LANES), lambda b, j: (b, j, 0, 0)))]
    const = lambda shape: pl.BlockSpec(shape, lambda b, j: (0,) * len(shape))
    return pl.pallas_call(
        _inproj_kernel,
        grid=grid,
        in_specs=[pl.BlockSpec((1, tm, D_MODEL), lambda b, j: (b, j, 0)),
                  const((1, D_MODEL)), const((D_MODEL, N_PROJ)),
                  pl.BlockSpec((tm, 256), lambda b, j: (j, 0)),
                  pl.BlockSpec((tm, 256), lambda b, j: (j, 0)),
                  const((256, 256)), const((8, 256)), const((LANES, 256)), const((8, LANES))],
        out_specs=[o[1] for o in outs],
        out_shape=[o[0] for o in outs],
        compiler_params=pltpu.CompilerParams(
            dimension_semantics=("parallel", "parallel"), vmem_limit_bytes=VMEM_LIMIT),
        name="inproj",
    )(h, an, w_perm, cos_t, sin_t, g256, vecs, w2_pad, sm_bias)


def _prep_layer_params(l, attn_norm, w_in, dsa_q_norm, dsa_k_norm, fox_q_norm, fox_k_norm,
                       fox_f_bias, gla_gate_w2, gla_gate_b):
    w = w_in[l]
    splits = np.cumsum([256, 64, 64, 256, 4, 64, 256, 256, 256, 4, 256, 256, 512, 512, 16])[:-1]
    (dq, dk, dv, iq, iw, ik, fq, fk, fv, ff, gq, gk, gv, gg, glr) = jnp.split(w, splits, axis=1)
    small = jnp.concatenate(
        [iw, ff, glr, jnp.zeros((D_MODEL, LANES - 4 - 4 - GLA_RANK), w.dtype)], axis=1)
    z64 = jnp.zeros((D_MODEL, HEAD_DIM), w.dtype)
    w_perm = jnp.concatenate([dq, dk, dk, dv, z64, iq, ik, ik, fq, fk, fv, gq, gk, gv, gg, small],
                             axis=1).astype(BF16)
    tile4 = lambda g: jnp.tile(g, N_HEADS)
    vecs = jnp.zeros((8, 256), F32)
    vecs = vecs.at[0].set(tile4(dsa_q_norm[l])).at[1].set(tile4(dsa_k_norm[l]))
    vecs = vecs.at[2].set(tile4(fox_q_norm[l])).at[3].set(tile4(fox_k_norm[l]))
    vecs = vecs.at[4].set(gla_gate_b[l])
    w2_pad = jnp.zeros((LANES, 256), F32).at[SM_GLR:SM_GLR + GLA_RANK].set(gla_gate_w2[l]).astype(BF16)
    sm_bias = jnp.zeros((8, LANES), F32).at[0, SM_FF:SM_FF + N_HEADS].set(fox_f_bias[l])
    return attn_norm[l][None, :], w_perm, vecs, w2_pad, sm_bias


def _rope_tables(t_work):
    inv = jnp.power(ROPE_THETA, -jnp.arange(ROPE_HALF, dtype=F32) * 2.0 / ROPE_DIM)
    ang = jnp.arange(t_work).astype(F32)[:, None] * inv[None, :]
    cos, sin = jnp.cos(ang), jnp.sin(ang)
    rest = HEAD_DIM - ROPE_DIM
    cos64 = jnp.concatenate([cos, cos, jnp.ones((t_work, rest), F32)], axis=1)
    sin64 = jnp.concatenate([-sin, sin, jnp.zeros((t_work, rest), F32)], axis=1)
    return jnp.tile(cos64, (1, N_HEADS)), jnp.tile(sin64, (1, N_HEADS))


def _group_matrix():
    idx = np.arange(256) // HEAD_DIM
    return jnp.asarray((idx[:, None] == idx[None, :]).astype(np.float32), dtype=BF16)


DSA_TQ = 256
DSA_TK = 512
KEY_NEG_INF = -2139095041
KEY_NEG_ZERO = -1
KEY_NEG_MIN_NORMAL = -8388609
NEG_MIN_NORMAL = -1.1754943508222875e-38
SEARCH_FEW = 4.0
SEARCH_MANY = 16.0
UNCHECKED_PROBES = 12
GUIDED_PROBES = 64
MAX_PROBES = GUIDED_PROBES + 40
COUNT_ROWS = 128


def _key_to_f32(key):
    bits = key ^ ((key >> 31) & 0x7FFFFFFF)
    return lax.bitcast_convert_type(bits, F32)


def _f32_to_key(value):
    bits = lax.bitcast_convert_type(value, jnp.int32)
    return bits ^ ((bits >> 31) & 0x7FFFFFFF)


def _head_lane_mask(shape, head):
    lane = lax.broadcasted_iota(jnp.int32, shape, 1)
    return (lane < HEAD_DIM) if head % 2 == 0 else (lane >= HEAD_DIM)


def _masked_heads(slabs):
    return [jnp.where(_head_lane_mask(slabs[h // 2].shape, h), slabs[h // 2],
                      jnp.zeros_like(slabs[h // 2])) for h in range(N_HEADS)]


def _softmax_step(s, m, acc, v_aug):
    m_new = jnp.maximum(m, jnp.max(s, axis=-1, keepdims=True))
    p = jnp.exp2(s - m_new)
    return m_new, acc * jnp.exp2(m - m_new) + _dot(p.astype(BF16), v_aug)


def _normalise_heads(accs):
    outs = [acc / pltpu.roll(acc, HEAD_DIM, axis=1) for acc in accs]
    lane = lax.broadcasted_iota(jnp.int32, outs[0].shape, 1)
    return [jnp.where(lane < HEAD_DIM, outs[2 * p], pltpu.roll(outs[2 * p + 1], HEAD_DIM, axis=1))
            for p in range(2)]


def _dsa_kernel(k_top, n_tok, *refs):
    o_ref = refs[-2]
    real = pl.program_id(1) * DSA_TQ < n_tok
    pl.when(real)(functools.partial(_dsa_tile, k_top, *refs))

    @pl.when(jnp.logical_not(real))
    def _():
        o_ref[...] = jnp.zeros_like(o_ref)


def _dsa_tile(k_top, dq_ref, kk_ref, va_ref, iq_ref, ik_ref, wt_ref, tri_ref, kn_ref, o_ref,
              s_ref):
    i = pl.program_id(1)
    q0 = i * DSA_TQ
    n_tiles = (q0 + DSA_TQ + DSA_TK - 1) // DSA_TK
    last = n_tiles - 1
    key = lax.broadcasted_iota(jnp.int32, (DSA_TK, DSA_TQ), 0)
    qry = q0 + lax.broadcasted_iota(jnp.int32, (DSA_TK, DSA_TQ), 1)
    head = lambda x, h: x[:, h * DSA_TQ:(h + 1) * DSA_TQ]

    iq_all = jnp.concatenate(
        _masked_heads([iq_ref[0, :, 0:LANES], iq_ref[0, :, LANES:2 * LANES]]), axis=0)
    w_h = [wt_ref[0, SM_IW + h:SM_IW + h + 1, :] for h in range(N_HEADS)]

    def score_tile(j, carry, causal):
        k0 = pl.multiple_of(j * DSA_TK, DSA_TK)
        dots = jnp.maximum(_dot_nt(ik_ref[0, pl.ds(k0, DSA_TK), :], iq_all), 0.0)
        s = w_h[0] * head(dots, 0)
        for h in range(1, N_HEADS):
            s = s + w_h[h] * head(dots, h)
        if causal:
            s = jnp.where(k0 + key <= qry, s, NEG)
        s_ref[pl.ds(k0, DSA_TK), :] = s
        row_max, above_zero, above_neg = carry
        for part in range(DSA_TK // COUNT_ROWS):
            chunk = s[part * COUNT_ROWS:(part + 1) * COUNT_ROWS]
            above_zero = above_zero + jnp.where(chunk > 0.0, 1.0, 0.0)
            above_neg = above_neg + jnp.where(chunk > NEG_MIN_NORMAL, 1.0, 0.0)
        return jnp.maximum(row_max, jnp.max(s, axis=0, keepdims=True)), above_zero, above_neg

    zeros = jnp.zeros((COUNT_ROWS, DSA_TQ), F32)
    def score_pair(j, carry):
        return score_tile(2 * j + 1, score_tile(2 * j, carry, causal=False), causal=False)

    stats = lax.fori_loop(0, last // 2, score_pair, (jnp.full((1, DSA_TQ), NEG, F32), zeros, zeros))
    stats = lax.fori_loop(2 * (last // 2), last, functools.partial(score_tile, causal=False), stats)
    row_max, above_zero, above_neg = score_tile(last, stats, causal=True)

    kf = jnp.float32(k_top)

    def count_above(t):
        tb = jnp.broadcast_to(t, (COUNT_ROWS, DSA_TQ))

        def body(j, acc, span):
            k0 = pl.multiple_of(j * span, span)
            for part in range(span // COUNT_ROWS):
                s = s_ref[pl.ds(k0 + part * COUNT_ROWS, COUNT_ROWS), :]
                acc = acc + jnp.where(s > tb, 1.0, 0.0)
            return acc

        acc = lax.fori_loop(0, n_tiles // 2, functools.partial(body, span=2 * DSA_TK),
                            jnp.zeros((COUNT_ROWS, DSA_TQ), F32))
        acc = lax.fori_loop(2 * (n_tiles // 2), n_tiles, functools.partial(body, span=DSA_TK), acc)
        return jnp.sum(acc, axis=0, keepdims=True)

    def midpoint(lo, hi):
        return (lo >> 1) + (hi >> 1) + (lo & hi & 1)

    def converged(lo, hi):
        return (midpoint(lo, hi) == lo) | ((lo >= KEY_NEG_MIN_NORMAL) & (hi <= 0))

    col1 = lambda value, dtype: jnp.full((1, DSA_TQ), value, dtype)

    def absorb(state, probe, c):
        lo, hi, clo, chi, wlo, whi, side = state
        live = probe != lo
        up = live & (c >= kf)
        down = live & (c <= kf)
        wlo = jnp.where(down & (side < 0), wlo * 0.5, jnp.where(up, 1.0, wlo))
        whi = jnp.where(up & (side > 0), whi * 0.5, jnp.where(down, 1.0, whi))
        side = jnp.where(up, 1, jnp.where(down, -1, side))
        lo, clo = jnp.where(up, probe, lo), jnp.where(up, c, clo)
        hi, chi = jnp.where(down, probe, hi), jnp.where(down, c, chi)
        return lo, hi, clo, chi, wlo, whi, side

    def probe_once(it, state):
        lo, hi, clo, chi, wlo, whi, side = state
        f_lo, f_hi = _key_to_f32(lo), _key_to_f32(hi)
        target = kf - 0.5
        log_count = lambda c: jnp.log2(jnp.maximum(c, 0.25))
        many = clo - chi > SEARCH_MANY
        g_lo = jnp.where(many, log_count(clo) - np.log2(k_top - 0.5), clo - target) * wlo
        g_hi = jnp.where(many, np.log2(k_top - 0.5) - log_count(chi), target - chi) * whi
        halve = (clo - chi <= SEARCH_FEW) | (col1(it % 8, jnp.int32) == 7)
        guess = _f32_to_key(f_lo + (f_hi - f_lo) * jnp.where(halve, 0.5, g_lo / (g_lo + g_hi)))
        guided = col1(it, jnp.int32) < GUIDED_PROBES
        probe = jnp.where((guess > lo) & (guess < hi) & guided, guess, midpoint(lo, hi))
        probe = jnp.where(converged(lo, hi), lo, probe)
        return absorb(state, probe, count_above(_key_to_f32(probe)))

    def search_cond(carry):
        it, pending = carry[0], carry[1]
        return (pending > 0) & (it < MAX_PROBES)

    def search_body(carry):
        it, state = carry[0], carry[2]
        state = probe_once(it + 1, probe_once(it, state))
        return it + 2, jnp.max(jnp.where(converged(state[0], state[1]), 0, 1)), state

    n_swept = (n_tiles * DSA_TK).astype(F32)
    state = (col1(KEY_NEG_INF, jnp.int32), _f32_to_key(row_max),
             jnp.broadcast_to(n_swept, (1, DSA_TQ)), col1(0.0, F32), col1(1.0, F32), col1(1.0, F32),
             col1(0, jnp.int32))
    for fixed, counts in ((KEY_NEG_ZERO, above_zero), (KEY_NEG_MIN_NORMAL, above_neg)):
        inside = (state[0] < fixed) & (fixed < state[1])
        state = absorb(state, jnp.where(inside, fixed, state[0]),
                       jnp.sum(counts, axis=0, keepdims=True))
    state = lax.fori_loop(0, UNCHECKED_PROBES, probe_once, state)
    state = lax.while_loop(search_cond, search_body,
                           (jnp.int32(UNCHECKED_PROBES), jnp.int32(1), state))[2]
    thr = _key_to_f32(state[1])
    n_ties = kf - state[3]

    to_column = lambda r: jnp.broadcast_to(r, (8, DSA_TQ)).T[:, 0:1]
    thr_c, ties_c = to_column(thr), to_column(n_ties)
    q_all = jnp.concatenate(
        _masked_heads([dq_ref[0, :, 0:LANES], dq_ref[0, :, LANES:2 * LANES]]), axis=0)
    tri = tri_ref[...]
    tri_lo = jnp.where(lax.broadcasted_iota(jnp.int32, (LANES, LANES), 0)
                       >= lax.broadcasted_iota(jnp.int32, (LANES, LANES), 1), 1.0, 0.0).astype(BF16)
    qrow = q0 + lax.broadcasted_iota(jnp.int32, (DSA_TQ, DSA_TK), 0)
    kcol = lax.broadcasted_iota(jnp.int32, (DSA_TQ, DSA_TK), 1)
    n_blocks = DSA_TK // LANES

    reach = _logit_reach(q_all, jnp.max(kn_ref[0], axis=0)[KN_DSA:KN_DSA + 1, 0:1])

    def selected_logits(j, seen, causal):
        k0 = pl.multiple_of(j * DSA_TK, DSA_TK)
        s = s_ref[pl.ds(k0, DSA_TK), :].T
        tie = s == thr_c
        tie_b = jnp.where(tie, 1.0, 0.0).astype(BF16)
        local = [_dot(tie_b[:, b * LANES:(b + 1) * LANES], tri) for b in range(n_blocks)]
        ranks = []
        for b in range(n_blocks):
            ranks.append(local[b] + seen)
            seen = seen + local[b][:, LANES - 1:LANES]
        sel = (s > thr_c) | (tie & (jnp.concatenate(ranks, axis=1) <= ties_c))
        if causal:
            sel = sel & (k0 + kcol <= qrow)
        logits = _dot_nt(q_all, kk_ref[0, pl.ds(k0, DSA_TK), :])
        logits = jnp.where(sel[None], logits.reshape(N_HEADS, DSA_TQ, DSA_TK), NEG)
        return seen, logits.reshape(N_HEADS * DSA_TQ, DSA_TK), va_ref[0, pl.ds(k0, DSA_TK), :]

    def capped(j, carry, causal):
        seen, acc = carry
        k0 = pl.multiple_of(j * DSA_TK, DSA_TK)
        s = s_ref[pl.ds(k0, DSA_TK), :]
        tie = s == thr
        tie_b = jnp.where(tie, 1.0, 0.0).astype(BF16)
        ranks = []
        for b in range(n_blocks):
            local = _dot(tri_lo, tie_b[b * LANES:(b + 1) * LANES, :])
            ranks.append(local + seen)
            seen = seen + local[LANES - 1:LANES, :]
        sel = (s > thr) | (tie & (jnp.concatenate(ranks, axis=0) <= n_ties))
        if causal:
            sel = sel & (k0 + key <= qry)
        keep = jnp.where(sel, 1.0, 0.0).astype(BF16).T
        logits = _dot_nt(q_all, kk_ref[0, pl.ds(k0, DSA_TK), :])
        p = jnp.exp2(logits - reach).astype(BF16).reshape(N_HEADS, DSA_TQ, DSA_TK) * keep[None]
        return seen, acc + _dot(p.reshape(N_HEADS * DSA_TQ, DSA_TK), va_ref[0, pl.ds(k0, DSA_TK), :])

    def online(j, carry, causal):
        seen, logits, va_t = selected_logits(j, carry[0], causal)
        return (seen,) + _softmax_step(logits, carry[1], carry[2], va_t)

    def sweep(step, *stats):
        carry = stats + (jnp.zeros((N_HEADS * DSA_TQ, LANES), F32),)

        def pair(j, c):
            return step(2 * j + 1, step(2 * j, c, causal=False), causal=False)

        carry = lax.fori_loop(0, last // 2, pair, carry)
        carry = lax.fori_loop(2 * (last // 2), last, functools.partial(step, causal=False), carry)
        return step(last, carry, causal=True)[-1]

    acc = lax.cond(2.0 * jnp.max(reach) <= SAFE_LOG2_SPAN,
                   functools.partial(sweep, capped, jnp.zeros((1, DSA_TQ), F32)),
                   functools.partial(sweep, online, jnp.zeros((DSA_TQ, 1), F32),
                                     jnp.full((N_HEADS * DSA_TQ, 1), NEG, F32)))
    o_ref[0] = jnp.concatenate(
        _normalise_heads([acc[h * DSA_TQ:(h + 1) * DSA_TQ] for h in range(N_HEADS)]),
        axis=1).astype(o_ref.dtype)


def _dsa_attention(dq, dkk, dva, iq, ikk, small_t, key_norms, k_top, n_tok):
    bsz, t_work, _ = dq.shape
    tri = jnp.asarray(np.triu(np.ones((LANES, LANES), np.float32)), dtype=BF16)
    tile = lambda width: pl.BlockSpec((1, DSA_TQ, width), lambda b, i: (b, i, 0))
    full = pl.BlockSpec((1, t_work, LANES), lambda b, i: (b, 0, 0), pipeline_mode=pl.Buffered(1))
    return pl.pallas_call(
        functools.partial(_dsa_kernel, k_top, n_tok),
        grid=(bsz, t_work // DSA_TQ),
        in_specs=[tile(256), full, full, tile(256), full,
                  pl.BlockSpec((1, 8, DSA_TQ), lambda b, i: (b, 0, i)),
                  pl.BlockSpec((LANES, LANES), lambda b, i: (0, 0)),
                  pl.BlockSpec((1,) + key_norms.shape[1:], lambda b, i: (b, 0, 0, 0))],
        out_specs=tile(256),
        out_shape=jax.ShapeDtypeStruct((bsz, t_work, 256), BF16),
        scratch_shapes=[pltpu.VMEM((t_work, DSA_TQ), F32)],
        compiler_params=pltpu.CompilerParams(
            dimension_semantics=("parallel", "parallel"), vmem_limit_bytes=VMEM_LIMIT),
        name="dsa_attention",
    )(dq, dkk, dva, iq, ikk, small_t, tri, key_norms)


FOX_TQ = 512
FOX_TK = 512
CUM_T = 256


def _fox_prep_kernel(x_ref, tri_ref, place_ref, k_ref, xt_ref, kb_ref, carry_ref):
    @pl.when(pl.program_id(1) == 0)
    def _():
        carry_ref[...] = jnp.zeros_like(carry_ref)

    tri = tri_ref[...]
    x = x_ref[0]
    h1, h2, h3 = _split3(x)
    c = _dot(tri, h1) + _dot(tri, h2) + _dot(tri, h3) + carry_ref[0:1, :]
    carry_ref[...] = jnp.broadcast_to(c[CUM_T - 1:CUM_T, :], carry_ref.shape)
    xt_ref[0] = x.T[0:8, :]
    terms = _split3(c * -LOG2E)
    bias = sum(_dot(terms[t], place_ref[t]) for t in range(BIAS_TERMS))
    lane = lax.broadcasted_iota(jnp.int32, bias.shape, 1) % LANES
    kb_ref[0] = jnp.where((lane >= HEAD_DIM) & (lane < HEAD_DIM + BIAS_TERMS),
                          bias.astype(BF16), k_ref[0])


def _fox_prep(small, fk):
    bsz, t_work, width = fk.shape
    tri = jnp.asarray(np.tril(np.ones((CUM_T, CUM_T), np.float32)), dtype=BF16)
    place = np.zeros((BIAS_TERMS, LANES, width), np.float32)
    for t in range(BIAS_TERMS):
        for h in range(N_HEADS):
            place[t, SM_FF + h, h * LANES + HEAD_DIM + t] = 1.0
    rows = lambda w: pl.BlockSpec((1, CUM_T, w), lambda b, j: (b, j, 0))
    return pl.pallas_call(
        _fox_prep_kernel,
        grid=(bsz, t_work // CUM_T),
        in_specs=[rows(LANES), pl.BlockSpec((CUM_T, CUM_T), lambda b, j: (0, 0)),
                  pl.BlockSpec((BIAS_TERMS, LANES, width), lambda b, j: (0, 0, 0)), rows(width)],
        out_specs=[pl.BlockSpec((1, 8, CUM_T), lambda b, j: (b, 0, j)), rows(width)],
        out_shape=[jax.ShapeDtypeStruct((bsz, 8, t_work), F32),
                   jax.ShapeDtypeStruct(fk.shape, BF16)],
        scratch_shapes=[pltpu.VMEM((8, LANES), F32)],
        compiler_params=pltpu.CompilerParams(dimension_semantics=("parallel", "arbitrary")),
        name="fox_prep",
    )(small, tri, jnp.asarray(place, dtype=BF16), fk)


def _logit_reach(q, kmax_sq):
    qf = q.astype(F32)
    return jnp.sqrt(jnp.sum(qf * qf, axis=-1, keepdims=True) * kmax_sq) * NORM_MARGIN


def _fox_kernel(q_ref, k_ref, v_ref, kn_ref, o_ref):
    i = pl.program_id(1)
    q0 = pl.multiple_of(i * FOX_TQ, FOX_TQ)
    n_full = q0 // FOX_TK
    row = q0 + lax.broadcasted_iota(jnp.int32, (FOX_TQ, FOX_TK), 0)
    col = n_full * FOX_TK + lax.broadcasted_iota(jnp.int32, (FOX_TQ, FOX_TK), 1)
    heads = [slice(h * LANES, (h + 1) * LANES) for h in range(N_HEADS)]

    lane = lax.broadcasted_iota(jnp.int32, (FOX_TQ, LANES), 1)
    bias_lanes = (lane >= HEAD_DIM) & (lane < HEAD_DIM + BIAS_TERMS)
    kmax_sq = jnp.max(kn_ref[0], axis=0)
    caps, span = [], jnp.float32(0.0)
    for h in range(N_HEADS):
        q_h = q_ref[0, :, heads[h]]
        reach = _logit_reach(jnp.where(lane < HEAD_DIM, q_h, jnp.zeros_like(q_h)),
                             kmax_sq[KN_FOX + h:KN_FOX + h + 1, 0:1])
        own = k_ref[0, pl.ds(q0, FOX_TQ), heads[h]].astype(F32)
        caps.append(reach + jnp.sum(jnp.where(bias_lanes, own, 0.0), axis=-1, keepdims=True))
        span = jnp.maximum(span, 2.0 * jnp.max(reach))

    def logits(j, h, diag):
        k0 = pl.multiple_of(j * FOX_TK, FOX_TK)
        s = _dot_nt(q_ref[0, :, heads[h]], k_ref[0, pl.ds(k0, FOX_TK), heads[h]])
        return jnp.where(col <= row, s, NEG) if diag else s

    def values(j, h):
        return v_ref[0, pl.ds(pl.multiple_of(j * FOX_TK, FOX_TK), FOX_TK), heads[h]]

    def capped(j, accs, diag):
        return tuple(accs[h] + _dot(jnp.exp2(logits(j, h, diag) - caps[h]).astype(BF16), values(j, h))
                     for h in range(N_HEADS))

    def online(j, carry, diag):
        return tuple(_softmax_step(logits(j, h, diag), *carry[h], values(j, h))
                     for h in range(N_HEADS))

    def capped_sweep():
        accs = tuple(jnp.zeros((FOX_TQ, LANES), F32) for _ in range(N_HEADS))
        accs = lax.fori_loop(0, n_full, functools.partial(capped, diag=False), accs)
        return capped(n_full, accs, diag=True)

    def online_sweep():
        carry = tuple((jnp.full((FOX_TQ, 1), NEG, F32), jnp.zeros((FOX_TQ, LANES), F32))
                      for _ in range(N_HEADS))
        carry = lax.fori_loop(0, n_full, functools.partial(online, diag=False), carry)
        return tuple(acc for _, acc in online(n_full, carry, diag=True))

    accs = lax.cond(span <= SAFE_LOG2_SPAN, capped_sweep, online_sweep)
    o_ref[0] = jnp.concatenate(_normalise_heads(list(accs)), axis=1).astype(o_ref.dtype)


def _fox_attention(fqa, fkb, fva, key_norms):
    bsz, t_work, width = fqa.shape
    full = pl.BlockSpec((1, t_work, width), lambda b, i: (b, 0, 0), pipeline_mode=pl.Buffered(1))
    return pl.pallas_call(
        _fox_kernel,
        grid=(bsz, t_work // FOX_TQ),
        in_specs=[pl.BlockSpec((1, FOX_TQ, width), lambda b, i: (b, i, 0)), full, full,
                  pl.BlockSpec((1,) + key_norms.shape[1:], lambda b, i: (b, 0, 0, 0))],
        out_specs=pl.BlockSpec((1, FOX_TQ, 2 * LANES), lambda b, i: (b, i, 0)),
        out_shape=jax.ShapeDtypeStruct((bsz, t_work, 2 * LANES), BF16),
        compiler_params=pltpu.CompilerParams(
            dimension_semantics=("parallel", "parallel"), vmem_limit_bytes=VMEM_LIMIT),
        name="fox_attention",
    )(fqa, fkb, fva, key_norms)


GLA_NSUB = GLA_CHUNK // GLA_SUB


def _gla_kernel(q_ref, k_ref, v_ref, g_ref, la_ref, tri_ref, e_ref, gn_ref, o_ref, st_ref):
    @pl.when(pl.program_id(1) == 0)
    def _():
        st_ref[...] = jnp.zeros_like(st_ref)

    tri = tri_ref[...]
    emat = e_ref[...]
    lane = lax.broadcasted_iota(jnp.int32, (GLA_CHUNK, LANES), 1)
    rowblk = lax.broadcasted_iota(jnp.int32, (GLA_CHUNK, LANES), 0) // GLA_SUB
    tblk = lax.broadcasted_iota(jnp.int32, (GLA_CHUNK, GLA_CHUNK), 0) // GLA_SUB
    sblk = lax.broadcasted_iota(jnp.int32, (GLA_CHUNK, GLA_CHUNK), 1) // GLA_SUB
    trow = lax.broadcasted_iota(jnp.int32, (GLA_SUB, 256), 0)

    def chunk(c, carry):
        r0 = pl.multiple_of(c * GLA_CHUNK, GLA_CHUNK)
        rows = pl.ds(r0, GLA_CHUNK)
        h1, h2, h3 = _split3(la_ref[0, rows, :])
        b = _dot(tri, h1) + _dot(tri, h2) + _dot(tri, h3)
        q = q_ref[0, rows, :]
        k = k_ref[0, rows, :]
        v = v_ref[0, rows, :]
        b_last = b[GLA_CHUNK - 1:GLA_CHUNK, :]
        qd = q * jnp.exp2(b)
        kd = (k * jnp.exp2(b_last - b)).astype(BF16)
        starts = [jnp.zeros((1, 256), F32)] + [b[GLA_SUB * i - 1:GLA_SUB * i, :]
                                               for i in range(1, GLA_NSUB)]
        bsel = jnp.concatenate([jnp.broadcast_to(s, (GLA_SUB, 256)) for s in starts], axis=0)
        qn = q * jnp.exp2(b - bsel)

        diag = []
        for i in range(GLA_NSUB):
            rs = slice(GLA_SUB * i, GLA_SUB * (i + 1))
            b_i, q_i, k_i = b[rs], q[rs], k[rs]
            v_i = v[rs].astype(F32)
            ps = []
            for s in range(GLA_SUB):
                d = jnp.exp2(jnp.minimum(b_i - b_i[s:s + 1], 0.0))
                ps.append(jnp.where(trow >= s, q_i * d * k_i[s:s + 1], 0.0).astype(BF16))
            r = _dot(jnp.concatenate(ps, axis=0), emat)
            od = r[0:GLA_SUB] * v_i[0:1]
            for s in range(1, GLA_SUB):
                od = od + r[GLA_SUB * s:GLA_SUB * (s + 1)] * v_i[s:s + 1]
            diag.append(od)
        o_diag = jnp.concatenate(diag, axis=0)

        for slab in range(2):
            ls = slice(slab * LANES, (slab + 1) * LANES)
            qn_s, k_s, b_s = qn[:, ls], k[:, ls], b[:, ls]
            khat = jnp.concatenate(
                [(k_s * jnp.exp2(jnp.minimum(starts[i][:, ls] - b_s, 0.0))).astype(BF16)
                 for i in range(1, GLA_NSUB)], axis=1)
            for half in range(2):
                head = 2 * slab + half
                hs = slice(head * GLA_DV, (head + 1) * GLA_DV)
                in_head = (lane < HEAD_DIM) if half == 0 else (lane >= HEAD_DIM)
                qm = jnp.where(in_head, qn_s, 0.0)
                qhat = jnp.concatenate([jnp.where(rowblk == i, qm, 0.0).astype(BF16)
                                        for i in range(1, GLA_NSUB)], axis=1)
                att = jnp.where(sblk < tblk, _dot_nt(qhat, khat), 0.0)
                v_h = v[:, hs]
                st = st_ref[head]
                o = (_dot_nt(jnp.where(in_head, qd[:, ls], 0.0).astype(BF16), st.astype(BF16))
                     + _dot(att.astype(BF16), v_h) + o_diag[:, hs])
                st_ref[head] = st * jnp.exp2(b_last[:, ls]) + _dot_tn(v_h, kd[:, ls])
                y = o * lax.rsqrt(jnp.mean(o * o, axis=-1, keepdims=True) + EPS) * gn_ref[:, hs]
                o_ref[0, rows, hs] = (y * _silu(g_ref[0, rows, hs])).astype(o_ref.dtype)
        return carry

    lax.fori_loop(0, q_ref.shape[1] // GLA_CHUNK, chunk, 0, unroll=6)


def _gla(gq, gk, gv, gg, la, gain):
    bsz, t_work, _ = gq.shape
    tg = _row_tile(t_work)
    tri = jnp.asarray(np.tril(np.ones((GLA_CHUNK, GLA_CHUNK), np.float32)), dtype=BF16)
    emat = jnp.asarray(
        (np.arange(256)[:, None] // HEAD_DIM == np.arange(512)[None, :] // GLA_DV).astype(np.float32),
        dtype=BF16)
    rows = lambda width: pl.BlockSpec((1, tg, width), lambda b, j: (b, j, 0))
    const = lambda shape: pl.BlockSpec(shape, lambda b, j: (0,) * len(shape))
    return pl.pallas_call(
        _gla_kernel,
        grid=(bsz, t_work // tg),
        in_specs=[rows(256), rows(256), rows(512), rows(512), rows(256),
                  const((GLA_CHUNK, GLA_CHUNK)), const((256, 512)), const((1, 512))],
        out_specs=rows(512),
        out_shape=jax.ShapeDtypeStruct((bsz, t_work, 512), BF16),
        scratch_shapes=[pltpu.VMEM((N_HEADS, GLA_DV, LANES), F32)],
        compiler_params=pltpu.CompilerParams(
            dimension_semantics=("parallel", "arbitrary"), vmem_limit_bytes=VMEM_LIMIT),
        name="gla",
    )(gq, gk, gv, gg, la, tri, emat, gain)


FFN_CHUNK = 256


def _ffn_kernel(h_ref, oa_ref, ob_ref, oc_ref, wo_ref, fn_ref, wgu_ref, wd_ref, out_ref):
    h1 = (h_ref[0] + _dot(oa_ref[0], wo_ref[0:256, :]) + _dot(ob_ref[0], wo_ref[256:512, :])
          + _dot(oc_ref[0], wo_ref[512:1024, :]))
    ms = jnp.mean(h1 * h1, axis=-1, keepdims=True)
    f = (h1 * lax.rsqrt(ms + EPS) * fn_ref[...]).astype(BF16)
    out_ref[0] = h1
    for c in range(0, D_FF, FFN_CHUNK):
        gate = _dot(f, wgu_ref[:, c:c + FFN_CHUNK])
        up = _dot(f, wgu_ref[:, D_FF + c:D_FF + c + FFN_CHUNK])
        out_ref[0] += _dot((_silu(gate) * up).astype(BF16), wd_ref[c:c + FFN_CHUNK, :])


def _outproj_ffn(h, oa, ob, oc, wo, fn, wgu, wd):
    bsz, t_work, _ = h.shape
    tm = _row_tile(t_work)
    rows = lambda width: pl.BlockSpec((1, tm, width), lambda b, j: (b, j, 0))
    const = lambda shape: pl.BlockSpec(shape, lambda b, j: (0,) * len(shape),
                                       pipeline_mode=pl.Buffered(1))
    return pl.pallas_call(
        _ffn_kernel,
        grid=(bsz, t_work // tm),
        in_specs=[rows(D_MODEL), rows(256), rows(256), rows(512), const((D_MODEL, D_MODEL)),
                  const((1, D_MODEL)), const((D_MODEL, 2 * D_FF)), const((D_FF, D_MODEL))],
        out_specs=rows(D_MODEL),
        out_shape=jax.ShapeDtypeStruct(h.shape, F32),
        compiler_params=pltpu.CompilerParams(
            dimension_semantics=("parallel", "parallel"), vmem_limit_bytes=VMEM_LIMIT),
        name="outproj_ffn",
    )(h, oa, ob, oc, wo, fn, wgu, wd)


def kernel(x, meta_tokens, attn_norm, w_in, dsa_q_norm, dsa_k_norm, fox_q_norm, fox_k_norm,
           fox_f_bias, gla_gate_w2, gla_gate_b, gla_out_norm, w_out, ffn_norm, w_gate_up, w_down):
    bsz, seq, _ = x.shape
    n_tok = N_META + seq
    t_work = _work_len(n_tok)
    meta = jnp.broadcast_to(meta_tokens[None].astype(x.dtype), (bsz, N_META, D_MODEL))
    h = jnp.concatenate([meta, x, jnp.zeros((bsz, t_work - n_tok, D_MODEL), x.dtype)], axis=1)
    cos_t, sin_t = _rope_tables(t_work)
    g256 = _group_matrix()
    k_top = min(TOPK_MAX, seq // 4)
    for l in range(w_in.shape[0]):
        an, w_perm, vecs, w2_pad, sm_bias = _prep_layer_params(
            l, attn_norm, w_in, dsa_q_norm, dsa_k_norm, fox_q_norm, fox_k_norm, fox_f_bias,
            gla_gate_w2, gla_gate_b)
        (dq, dkk, dva, iq, ikk, fq, fk, fva, gq, gk, gv, gg, la, small, key_norms) = _inproj(
            h, an, w_perm, cos_t, sin_t, g256, vecs, w2_pad, sm_bias)
        small_t, fkb = _fox_prep(small, fk)
        oa = _dsa_attention(dq, dkk, dva, iq, ikk, small_t, key_norms, k_top, n_tok)
        ob = _fox_attention(fq, fkb, fva, key_norms)
        oc = _gla(gq, gk, gv, gg, la, jnp.tile(gla_out_norm[l], N_HEADS)[None, :])
        h = _outproj_ffn(h, oa, ob, oc, w_out[l].astype(BF16), ffn_norm[l][None, :],
                         w_gate_up[l].astype(BF16), w_down[l].astype(BF16))
    return h[:, N_META:n_tok]
```

```python
import functools

import numpy as np
import jax
import jax.numpy as jnp
from jax import lax
from jax.experimental import pallas as pl
from jax.experimental.pallas import tpu as pltpu

F32 = jnp.float32
BF16 = jnp.bfloat16

D_MODEL = 1024
HEAD_DIM = 64
N_META = 16
ROPE_THETA = 500000.0
ROPE_DIM = HEAD_DIM // 4
ROPE_HALF = ROPE_DIM // 2
NEG = -1e30
EPS = 1e-6

N_HEADS = 4
TOPK_MAX = 256
GLA_DV = 128
GLA_RANK = 16
GLA_TAU = 16.0
GLA_CHUNK = 64
GLA_SUB = 16
D_FF = 2816

LOG2E = 1.4426950408889634
BIAS_TERMS = 3
KN_DSA, KN_FOX = 0, 1
NORM_MARGIN = 1.01
SAFE_LOG2_SPAN = 100.0
LANES = 128
SEQ_ALIGN = 512
VMEM_LIMIT = 56 * 1024 * 1024

_SLABS = (("dq", 256), ("dkk", 128), ("dva", 128), ("iq", 256), ("ikk", 128),
          ("fq", 256), ("fk", 256), ("fv", 256), ("gq", 256), ("gk", 256),
          ("gv", 512), ("gg", 512), ("small", 128))
_SLAB_OFF = {}
_off = 0
for _name, _width in _SLABS:
    _SLAB_OFF[_name] = (_off, _width)
    _off += _width
N_PROJ = _off
SM_IW, SM_FF, SM_GLR = 0, 4, 8


def _work_len(n_tok):
    return -(-n_tok // SEQ_ALIGN) * SEQ_ALIGN


def _row_tile(t_work):
    for cand in (768, 640, 512):
        if t_work % cand == 0:
            return cand
    raise ValueError(f"unsupported working length {t_work}")


def _dot(a, b):
    return jnp.dot(a, b, preferred_element_type=F32)


def _dot_nt(a, b):
    return lax.dot_general(a, b, (((1,), (1,)), ((), ())), preferred_element_type=F32)


def _dot_tn(a, b):
    return lax.dot_general(a, b, (((0,), (0,)), ((), ())), preferred_element_type=F32)


def _split3(x):
    h1 = x.astype(BF16)
    r1 = x - h1.astype(F32)
    h2 = r1.astype(BF16)
    h3 = (r1 - h2.astype(F32)).astype(BF16)
    return h1, h2, h3


def _log_sigmoid(x):
    return jnp.minimum(x, 0.0) - jnp.log1p(jnp.exp(-jnp.abs(x)))


def _silu(x):
    return x / (1.0 + jnp.exp(-x))


def _group_rms(y, gmat, gain):
    yy = y * y
    hi = yy.astype(BF16)
    lo = (yy - hi.astype(F32)).astype(BF16)
    ss = _dot(hi, gmat) + _dot(lo, gmat)
    return y * lax.rsqrt(ss * (1.0 / HEAD_DIM) + EPS) * gain


def _rope(y, cos, sin):
    width = y.shape[-1]
    lane = lax.broadcasted_iota(jnp.int32, y.shape, 1) % HEAD_DIM
    upper = pltpu.roll(y, width - ROPE_HALF, axis=1)
    lower = pltpu.roll(y, ROPE_HALF, axis=1)
    partner = jnp.where(lane < ROPE_HALF, upper, lower)
    return y * cos + partner * sin


def _inproj_kernel(x_ref, an_ref, w_ref, cos_ref, sin_ref, g256_ref, vec_ref, w2_ref, sm_ref,
                   dq_ref, dkk_ref, dva_ref, iq_ref, ikk_ref, fq_ref, fk_ref, fva_ref,
                   gq_ref, gk_ref, gv_ref, gg_ref, la_ref, small_ref, kn_ref):
    x = x_ref[0]
    ms = jnp.mean(x * x, axis=-1, keepdims=True)
    a = (x * lax.rsqrt(ms + EPS) * an_ref[...]).astype(BF16)

    def proj(name):
        off, width = _SLAB_OFF[name]
        return _dot(a, w_ref[:, off:off + width])

    def ones_in_upper_half(y):
        lane = lax.broadcasted_iota(jnp.int32, y.shape, 1) % LANES
        return jnp.where(lane < HEAD_DIM, y, 1.0)

    cos = cos_ref[...]
    sin = sin_ref[...]
    g256 = g256_ref[...]
    g128 = g256[:LANES, :LANES]
    dqn, dkn, fqn, fkn = vec_ref[0:1, :], vec_ref[1:2, :LANES], vec_ref[2:3, :], vec_ref[3:4, :]
    scale = HEAD_DIM ** -0.5

    def max_sq_norm(k):
        kf = k.astype(F32)
        return jnp.max(_dot((kf * kf).astype(BF16), g128), axis=0, keepdims=True)

    dq_ref[0] = (_rope(_group_rms(proj("dq"), g256, dqn), cos, sin) * (scale * LOG2E)).astype(BF16)
    dkk = _rope(_group_rms(proj("dkk"), g128, dkn), cos[:, :LANES], sin[:, :LANES]).astype(BF16)
    dkk_ref[0] = dkk
    key_norms = [max_sq_norm(dkk)]
    dva_ref[0] = ones_in_upper_half(proj("dva")).astype(BF16)
    iq_ref[0] = (_rope(proj("iq"), cos, sin) * scale).astype(BF16)
    ikk_ref[0] = _rope(proj("ikk"), cos[:, :LANES], sin[:, :LANES]).astype(BF16)
    fq, fk, fv = proj("fq"), proj("fk"), proj("fv")
    lane = lax.broadcasted_iota(jnp.int32, (fq.shape[0], LANES), 1)
    in_head = lane < HEAD_DIM
    q_pad = jnp.where(lane < HEAD_DIM + BIAS_TERMS, 1.0, 0.0)
    for pair in range(N_HEADS // 2):
        ps = slice(pair * LANES, (pair + 1) * LANES)
        q_pair = _group_rms(fq[:, ps], g128, fqn[:, :LANES]) * (scale * LOG2E)
        k_pair = _group_rms(fk[:, ps], g128, fkn[:, :LANES])
        v_pair = fv[:, ps]
        for odd in range(2):
            hs = slice((2 * pair + odd) * LANES, (2 * pair + odd + 1) * LANES)
            down = (lambda y: pltpu.roll(y, HEAD_DIM, axis=1)) if odd else (lambda y: y)
            fq_ref[0, :, hs] = jnp.where(in_head, down(q_pair), q_pad).astype(BF16)
            k_h = jnp.where(in_head, down(k_pair), 0.0).astype(BF16)
            fk_ref[0, :, hs] = k_h
            key_norms.append(max_sq_norm(k_h))
            fva_ref[0, :, hs] = jnp.where(in_head, down(v_pair), 1.0).astype(BF16)
    kn_ref[0, 0] = jnp.concatenate(key_norms + [jnp.zeros((8 - len(key_norms), LANES), F32)], axis=0)
    gq_ref[0] = proj("gq") * scale
    gk_ref[0] = proj("gk")
    gv_ref[0] = proj("gv").astype(BF16)
    gg_ref[0] = proj("gg")

    small = proj("small")
    lane = lax.broadcasted_iota(jnp.int32, small.shape, 1)
    small_ref[0] = jnp.where(lane < SM_FF, small * (N_HEADS ** -0.5),
                             _log_sigmoid(small + sm_ref[0:1, :]))
    gate = _dot(small.astype(BF16), w2_ref[...]) + vec_ref[4:5, :]
    la_ref[0] = _log_sigmoid(gate) * (LOG2E / GLA_TAU)


def _inproj(h, an, w_perm, cos_t, sin_t, g256, vecs, w2_pad, sm_bias):
    bsz, t_work, _ = h.shape
    tm = _row_tile(t_work)
    grid = (bsz, t_work // tm)

    def rows(width, dtype):
        return (jax.ShapeDtypeStruct((bsz, t_work, width), dtype),
                pl.BlockSpec((1, tm, width), lambda b, j: (b, j, 0)))

    outs = [rows(256, BF16), rows(128, BF16), rows(128, BF16), rows(256, BF16), rows(128, BF16),
            rows(512, BF16), rows(512, BF16), rows(512, BF16), rows(256, F32), rows(256, F32),
            rows(512, BF16), rows(512, F32), rows(256, F32), rows(128, F32),
            (jax.ShapeDtypeStruct((bsz, t_work // tm, 8, LANES), F32),
             pl.BlockSpec((1, 1, 8, LANES), lambda b, j: (b, j, 0, 0)))]
    const = lambda shape: pl.BlockSpec(shape, lambda b, j: (0,) * len(shape))
    return pl.pallas_call(
        _inproj_kernel,
        grid=grid,
        in_specs=[pl.BlockSpec((1, tm, D_MODEL), lambda b, j: (b, j, 0)),
                  const((1, D_MODEL)), const((D_MODEL, N_PROJ)),
                  pl.BlockSpec((tm, 256), lambda b, j: (j, 0)),
                  pl.BlockSpec((tm, 256), lambda b, j: (j, 0)),
                  const((256, 256)), const((8, 256)), const((LANES, 256)), const((8, LANES))],
        out_specs=[o[1] for o in outs],
        out_shape=[o[0] for o in outs],
        compiler_params=pltpu.CompilerParams(
            dimension_semantics=("parallel", "parallel"), vmem_limit_bytes=VMEM_LIMIT),
        name="inproj",
    )(h, an, w_perm, cos_t, sin_t, g256, vecs, w2_pad, sm_bias)


def _prep_layer_params(l, attn_norm, w_in, dsa_q_norm, dsa_k_norm, fox_q_norm, fox_k_norm,
                       fox_f_bias, gla_gate_w2, gla_gate_b):
    w = w_in[l]
    splits = np.cumsum([256, 64, 64, 256, 4, 64, 256, 256, 256, 4, 256, 256, 512, 512, 16])[:-1]
    (dq, dk, dv, iq, iw, ik, fq, fk, fv, ff, gq, gk, gv, gg, glr) = jnp.split(w, splits, axis=1)
    small = jnp.concatenate(
        [iw, ff, glr, jnp.zeros((D_MODEL, LANES - 4 - 4 - GLA_RANK), w.dtype)], axis=1)
    z64 = jnp.zeros((D_MODEL, HEAD_DIM), w.dtype)
    w_perm = jnp.concatenate([dq, dk, dk, dv, z64, iq, ik, ik, fq, fk, fv, gq, gk, gv, gg, small],
                             axis=1).astype(BF16)
    tile4 = lambda g: jnp.tile(g, N_HEADS)
    vecs = jnp.zeros((8, 256), F32)
    vecs = vecs.at[0].set(tile4(dsa_q_norm[l])).at[1].set(tile4(dsa_k_norm[l]))
    vecs = vecs.at[2].set(tile4(fox_q_norm[l])).at[3].set(tile4(fox_k_norm[l]))
    vecs = vecs.at[4].set(gla_gate_b[l])
    w2_pad = jnp.zeros((LANES, 256), F32).at[SM_GLR:SM_GLR + GLA_RANK].set(gla_gate_w2[l]).astype(BF16)
    sm_bias = jnp.zeros((8, LANES), F32).at[0, SM_FF:SM_FF + N_HEADS].set(fox_f_bias[l])
    return attn_norm[l][None, :], w_perm, vecs, w2_pad, sm_bias


def _rope_tables(t_work):
    inv = jnp.power(ROPE_THETA, -jnp.arange(ROPE_HALF, dtype=F32) * 2.0 / ROPE_DIM)
    ang = jnp.arange(t_work).astype(F32)[:, None] * inv[None, :]
    cos, sin = jnp.cos(ang), jnp.sin(ang)
    rest = HEAD_DIM - ROPE_DIM
    cos64 = jnp.concatenate([cos, cos, jnp.ones((t_work, rest), F32)], axis=1)
    sin64 = jnp.concatenate([-sin, sin, jnp.zeros((t_work, rest), F32)], axis=1)
    return jnp.tile(cos64, (1, N_HEADS)), jnp.tile(sin64, (1, N_HEADS))


def _group_matrix():
    idx = np.arange(256) // HEAD_DIM
    return jnp.asarray((idx[:, None] == idx[None, :]).astype(np.float32), dtype=BF16)


DSA_TQ = 256
DSA_TK = 512
KEY_NEG_INF = -2139095041
KEY_NEG_ZERO = -1
KEY_NEG_MIN_NORMAL = -8388609
NEG_MIN_NORMAL = -1.1754943508222875e-38
SEARCH_FEW = 4.0
SEARCH_MANY = 16.0
UNCHECKED_PROBES = 12
GUIDED_PROBES = 64
MAX_PROBES = GUIDED_PROBES + 40
COUNT_ROWS = 128


def _key_to_f32(key):
    bits = key ^ ((key >> 31) & 0x7FFFFFFF)
    return lax.bitcast_convert_type(bits, F32)


def _f32_to_key(value):
    bits = lax.bitcast_convert_type(value, jnp.int32)
    return bits ^ ((bits >> 31) & 0x7FFFFFFF)


def _head_lane_mask(shape, head):
    lane = lax.broadcasted_iota(jnp.int32, shape, 1)
    return (lane < HEAD_DIM) if head % 2 == 0 else (lane >= HEAD_DIM)


def _masked_heads(slabs):
    return [jnp.where(_head_lane_mask(slabs[h // 2].shape, h), slabs[h // 2],
                      jnp.zeros_like(slabs[h // 2])) for h in range(N_HEADS)]


def _softmax_step(s, m, acc, v_aug):
    m_new = jnp.maximum(m, jnp.max(s, axis=-1, keepdims=True))
    p = jnp.exp2(s - m_new)
    return m_new, acc * jnp.exp2(m - m_new) + _dot(p.astype(BF16), v_aug)


def _normalise_heads(accs):
    outs = [acc / pltpu.roll(acc, HEAD_DIM, axis=1) for acc in accs]
    lane = lax.broadcasted_iota(jnp.int32, outs[0].shape, 1)
    return [jnp.where(lane < HEAD_DIM, outs[2 * p], pltpu.roll(outs[2 * p + 1], HEAD_DIM, axis=1))
            for p in range(2)]


def _dsa_kernel(k_top, n_tok, *refs):
    o_ref = refs[-2]
    real = pl.program_id(1) * DSA_TQ < n_tok
    pl.when(real)(functools.partial(_dsa_tile, k_top, *refs))

    @pl.when(jnp.logical_not(real))
    def _():
        o_ref[...] = jnp.zeros_like(o_ref)


def _dsa_tile(k_top, dq_ref, kk_ref, va_ref, iq_ref, ik_ref, wt_ref, tri_ref, kn_ref, o_ref,
              s_ref):
    i = pl.program_id(1)
    q0 = i * DSA_TQ
    n_tiles = (q0 + DSA_TQ + DSA_TK - 1) // DSA_TK
    last = n_tiles - 1
    key = lax.broadcasted_iota(jnp.int32, (DSA_TK, DSA_TQ), 0)
    qry = q0 + lax.broadcasted_iota(jnp.int32, (DSA_TK, DSA_TQ), 1)
    head = lambda x, h: x[:, h * DSA_TQ:(h + 1) * DSA_TQ]

    iq_all = jnp.concatenate(
        _masked_heads([iq_ref[0, :, 0:LANES], iq_ref[0, :, LANES:2 * LANES]]), axis=0)
    w_h = [wt_ref[0, SM_IW + h:SM_IW + h + 1, :] for h in range(N_HEADS)]

    def score_tile(j, carry, causal):
        k0 = pl.multiple_of(j * DSA_TK, DSA_TK)
        dots = jnp.maximum(_dot_nt(ik_ref[0, pl.ds(k0, DSA_TK), :], iq_all), 0.0)
        s = w_h[0] * head(dots, 0)
        for h in range(1, N_HEADS):
            s = s + w_h[h] * head(dots, h)
        if causal:
            s = jnp.where(k0 + key <= qry, s, NEG)
        s_ref[pl.ds(k0, DSA_TK), :] = s
        row_max, above_zero, above_neg = carry
        for part in range(DSA_TK // COUNT_ROWS):
            chunk = s[part * COUNT_ROWS:(part + 1) * COUNT_ROWS]
            above_zero = above_zero + jnp.where(chunk > 0.0, 1.0, 0.0)
            above_neg = above_neg + jnp.where(chunk > NEG_MIN_NORMAL, 1.0, 0.0)
        return jnp.maximum(row_max, jnp.max(s, axis=0, keepdims=True)), above_zero, above_neg

    zeros = jnp.zeros((COUNT_ROWS, DSA_TQ), F32)
    def score_pair(j, carry):
        return score_tile(2 * j + 1, score_tile(2 * j, carry, causal=False), causal=False)

    stats = lax.fori_loop(0, last // 2, score_pair, (jnp.full((1, DSA_TQ), NEG, F32), zeros, zeros))
    stats = lax.fori_loop(2 * (last // 2), last, functools.partial(score_tile, causal=False), stats)
    row_max, above_zero, above_neg = score_tile(last, stats, causal=True)

    kf = jnp.float32(k_top)

    def count_above(t):
        tb = jnp.broadcast_to(t, (COUNT_ROWS, DSA_TQ))

        def body(j, acc, span):
            k0 = pl.multiple_of(j * span, span)
            for part in range(span // COUNT_ROWS):
                s = s_ref[pl.ds(k0 + part * COUNT_ROWS, COUNT_ROWS), :]
                acc = acc + jnp.where(s > tb, 1.0, 0.0)
            return acc

        acc = lax.fori_loop(0, n_tiles // 2, functools.partial(body, span=2 * DSA_TK),
                            jnp.zeros((COUNT_ROWS, DSA_TQ), F32))
        acc = lax.fori_loop(2 * (n_tiles // 2), n_tiles, functools.partial(body, span=DSA_TK), acc)
        return jnp.sum(acc, axis=0, keepdims=True)

    def midpoint(lo, hi):
        return (lo >> 1) + (hi >> 1) + (lo & hi & 1)

    def converged(lo, hi):
        return (midpoint(lo, hi) == lo) | ((lo >= KEY_NEG_MIN_NORMAL) & (hi <= 0))

    col1 = lambda value, dtype: jnp.full((1, DSA_TQ), value, dtype)

    def absorb(state, probe, c):
        lo, hi, clo, chi, wlo, whi, side = state
        live = probe != lo
        up = live & (c >= kf)
        down = live & (c <= kf)
        wlo = jnp.where(down & (side < 0), wlo * 0.5, jnp.where(up, 1.0, wlo))
        whi = jnp.where(up & (side > 0), whi * 0.5, jnp.where(down, 1.0, whi))
        side = jnp.where(up, 1, jnp.where(down, -1, side))
        lo, clo = jnp.where(up, probe, lo), jnp.where(up, c, clo)
        hi, chi = jnp.where(down, probe, hi), jnp.where(down, c, chi)
        return lo, hi, clo, chi, wlo, whi, side

    def probe_once(it, state):
        lo, hi, clo, chi, wlo, whi, side = state
        f_lo, f_hi = _key_to_f32(lo), _key_to_f32(hi)
        target = kf - 0.5
        log_count = lambda c: jnp.log2(jnp.maximum(c, 0.25))
        many = clo - chi > SEARCH_MANY
        g_lo = jnp.where(many, log_count(clo) - np.log2(k_top - 0.5), clo - target) * wlo
        g_hi = jnp.where(many, np.log2(k_top - 0.5) - log_count(chi), target - chi) * whi
        halve = (clo - chi <= SEARCH_FEW) | (col1(it % 8, jnp.int32) == 7)
        guess = _f32_to_key(f_lo + (f_hi - f_lo) * jnp.where(halve, 0.5, g_lo / (g_lo + g_hi)))
        guided = col1(it, jnp.int32) < GUIDED_PROBES
        probe = jnp.where((guess > lo) & (guess < hi) & guided, guess, midpoint(lo, hi))
        probe = jnp.where(converged(lo, hi), lo, probe)
        return absorb(state, probe, count_above(_key_to_f32(probe)))

    def search_cond(carry):
        it, pending = carry[0], carry[1]
        return (pending > 0) & (it < MAX_PROBES)

    def search_body(carry):
        it, state = carry[0], carry[2]
        state = probe_once(it + 1, probe_once(it, state))
        return it + 2, jnp.max(jnp.where(converged(state[0], state[1]), 0, 1)), state

    n_swept = (n_tiles * DSA_TK).astype(F32)
    state = (col1(KEY_NEG_INF, jnp.int32), _f32_to_key(row_max),
             jnp.broadcast_to(n_swept, (1, DSA_TQ)), col1(0.0, F32), col1(1.0, F32), col1(1.0, F32),
             col1(0, jnp.int32))
    for fixed, counts in ((KEY_NEG_ZERO, above_zero), (KEY_NEG_MIN_NORMAL, above_neg)):
        inside = (state[0] < fixed) & (fixed < state[1])
        state = absorb(state, jnp.where(inside, fixed, state[0]),
                       jnp.sum(counts, axis=0, keepdims=True))
    state = lax.fori_loop(0, UNCHECKED_PROBES, probe_once, state)
    state = lax.while_loop(search_cond, search_body,
                           (jnp.int32(UNCHECKED_PROBES), jnp.int32(1), state))[2]
    thr = _key_to_f32(state[1])
    n_ties = kf - state[3]

    to_column = lambda r: jnp.broadcast_to(r, (8, DSA_TQ)).T[:, 0:1]
    thr_c, ties_c = to_column(thr), to_column(n_ties)
    q_all = jnp.concatenate(
        _masked_heads([dq_ref[0, :, 0:LANES], dq_ref[0, :, LANES:2 * LANES]]), axis=0)
    tri = tri_ref[...]
    tri_lo = jnp.where(lax.broadcasted_iota(jnp.int32, (LANES, LANES), 0)
                       >= lax.broadcasted_iota(jnp.int32, (LANES, LANES), 1), 1.0, 0.0).astype(BF16)
    qrow = q0 + lax.broadcasted_iota(jnp.int32, (DSA_TQ, DSA_TK), 0)
    kcol = lax.broadcasted_iota(jnp.int32, (DSA_TQ, DSA_TK), 1)
    n_blocks = DSA_TK // LANES

    reach = _logit_reach(q_all, jnp.max(kn_ref[0], axis=0)[KN_DSA:KN_DSA + 1, 0:1])

    def selected_logits(j, seen, causal):
        k0 = pl.multiple_of(j * DSA_TK, DSA_TK)
        s = s_ref[pl.ds(k0, DSA_TK), :].T
        tie = s == thr_c
        tie_b = jnp.where(tie, 1.0, 0.0).astype(BF16)
        local = [_dot(tie_b[:, b * LANES:(b + 1) * LANES], tri) for b in range(n_blocks)]
        ranks = []
        for b in range(n_blocks):
            ranks.append(local[b] + seen)
            seen = seen + local[b][:, LANES - 1:LANES]
        sel = (s > thr_c) | (tie & (jnp.concatenate(ranks, axis=1) <= ties_c))
        if causal:
            sel = sel & (k0 + kcol <= qrow)
        logits = _dot_nt(q_all, kk_ref[0, pl.ds(k0, DSA_TK), :])
        logits = jnp.where(sel[None], logits.reshape(N_HEADS, DSA_TQ, DSA_TK), NEG)
        return seen, logits.reshape(N_HEADS * DSA_TQ, DSA_TK), va_ref[0, pl.ds(k0, DSA_TK), :]

    def capped(j, carry, causal):
        seen, acc = carry
        k0 = pl.multiple_of(j * DSA_TK, DSA_TK)
        s = s_ref[pl.ds(k0, DSA_TK), :]
        tie = s == thr
        tie_b = jnp.where(tie, 1.0, 0.0).astype(BF16)
        ranks = []
        for b in range(n_blocks):
            local = _dot(tri_lo, tie_b[b * LANES:(b + 1) * LANES, :])
            ranks.append(local + seen)
            seen = seen + local[LANES - 1:LANES, :]
        sel = (s > thr) | (tie & (jnp.concatenate(ranks, axis=0) <= n_ties))
        if causal:
            sel = sel & (k0 + key <= qry)
        keep = jnp.where(sel, 1.0, 0.0).astype(BF16).T
        logits = _dot_nt(q_all, kk_ref[0, pl.ds(k0, DSA_TK), :])
        p = jnp.exp2(logits - reach).astype(BF16).reshape(N_HEADS, DSA_TQ, DSA_TK) * keep[None]
        return seen, acc + _dot(p.reshape(N_HEADS * DSA_TQ, DSA_TK), va_ref[0, pl.ds(k0, DSA_TK), :])

    def online(j, carry, causal):
        seen, logits, va_t = selected_logits(j, carry[0], causal)
        return (seen,) + _softmax_step(logits, carry[1], carry[2], va_t)

    def sweep(step, *stats):
        carry = stats + (jnp.zeros((N_HEADS * DSA_TQ, LANES), F32),)

        def pair(j, c):
            return step(2 * j + 1, step(2 * j, c, causal=False), causal=False)

        carry = lax.fori_loop(0, last // 2, pair, carry)
        carry = lax.fori_loop(2 * (last // 2), last, functools.partial(step, causal=False), carry)
        return step(last, carry, causal=True)[-1]

    acc = lax.cond(2.0 * jnp.max(reach) <= SAFE_LOG2_SPAN,
                   functools.partial(sweep, capped, jnp.zeros((1, DSA_TQ), F32)),
                   functools.partial(sweep, online, jnp.zeros((DSA_TQ, 1), F32),
                                     jnp.full((N_HEADS * DSA_TQ, 1), NEG, F32)))
    o_ref[0] = jnp.concatenate(
        _normalise_heads([acc[h * DSA_TQ:(h + 1) * DSA_TQ] for h in range(N_HEADS)]),
        axis=1).astype(o_ref.dtype)


def _dsa_attention(dq, dkk, dva, iq, ikk, small_t, key_norms, k_top, n_tok):
    bsz, t_work, _ = dq.shape
    tri = jnp.asarray(np.triu(np.ones((LANES, LANES), np.float32)), dtype=BF16)
    tile = lambda width: pl.BlockSpec((1, DSA_TQ, width), lambda b, i: (b, i, 0))
    full = pl.BlockSpec((1, t_work, LANES), lambda b, i: (b, 0, 0), pipeline_mode=pl.Buffered(1))
    return pl.pallas_call(
        functools.partial(_dsa_kernel, k_top, n_tok),
        grid=(bsz, t_work // DSA_TQ),
        in_specs=[tile(256), full, full, tile(256), full,
                  pl.BlockSpec((1, 8, DSA_TQ), lambda b, i: (b, 0, i)),
                  pl.BlockSpec((LANES, LANES), lambda b, i: (0, 0)),
                  pl.BlockSpec((1,) + key_norms.shape[1:], lambda b, i: (b, 0, 0, 0))],
        out_specs=tile(256),
        out_shape=jax.ShapeDtypeStruct((bsz, t_work, 256), BF16),
        scratch_shapes=[pltpu.VMEM((t_work, DSA_TQ), F32)],
        compiler_params=pltpu.CompilerParams(
            dimension_semantics=("parallel", "parallel"), vmem_limit_bytes=VMEM_LIMIT),
        name="dsa_attention",
    )(dq, dkk, dva, iq, ikk, small_t, tri, key_norms)


FOX_TQ = 512
FOX_TK = 512
CUM_T = 256


def _fox_prep_kernel(x_ref, tri_ref, place_ref, k_ref, xt_ref, kb_ref, carry_ref):
    @pl.when(pl.program_id(1) == 0)
    def _():
        carry_ref[...] = jnp.zeros_like(carry_ref)

    tri = tri_ref[...]
    x = x_ref[0]
    h1, h2, h3 = _split3(x)
    c = _dot(tri, h1) + _dot(tri, h2) + _dot(tri, h3) + carry_ref[0:1, :]
    carry_ref[...] = jnp.broadcast_to(c[CUM_T - 1:CUM_T, :], carry_ref.shape)
    xt_ref[0] = x.T[0:8, :]
    terms = _split3(c * -LOG2E)
    bias = sum(_dot(terms[t], place_ref[t]) for t in range(BIAS_TERMS))
    lane = lax.broadcasted_iota(jnp.int32, bias.shape, 1) % LANES
    kb_ref[0] = jnp.where((lane >= HEAD_DIM) & (lane < HEAD_DIM + BIAS_TERMS),
                          bias.astype(BF16), k_ref[0])


def _fox_prep(small, fk):
    bsz, t_work, width = fk.shape
    tri = jnp.asarray(np.tril(np.ones((CUM_T, CUM_T), np.float32)), dtype=BF16)
    place = np.zeros((BIAS_TERMS, LANES, width), np.float32)
    for t in range(BIAS_TERMS):
        for h in range(N_HEADS):
            place[t, SM_FF + h, h * LANES + HEAD_DIM + t] = 1.0
    rows = lambda w: pl.BlockSpec((1, CUM_T, w), lambda b, j: (b, j, 0))
    return pl.pallas_call(
        _fox_prep_kernel,
        grid=(bsz, t_work // CUM_T),
        in_specs=[rows(LANES), pl.BlockSpec((CUM_T, CUM_T), lambda b, j: (0, 0)),
                  pl.BlockSpec((BIAS_TERMS, LANES, width), lambda b, j: (0, 0, 0)), rows(width)],
        out_specs=[pl.BlockSpec((1, 8, CUM_T), lambda b, j: (b, 0, j)), rows(width)],
        out_shape=[jax.ShapeDtypeStruct((bsz, 8, t_work), F32),
                   jax.ShapeDtypeStruct(fk.shape, BF16)],
        scratch_shapes=[pltpu.VMEM((8, LANES), F32)],
        compiler_params=pltpu.CompilerParams(dimension_semantics=("parallel", "arbitrary")),
        name="fox_prep",
    )(small, tri, jnp.asarray(place, dtype=BF16), fk)


def _logit_reach(q, kmax_sq):
    qf = q.astype(F32)
    return jnp.sqrt(jnp.sum(qf * qf, axis=-1, keepdims=True) * kmax_sq) * NORM_MARGIN


def _fox_kernel(q_ref, k_ref, v_ref, kn_ref, o_ref):
    i = pl.program_id(1)
    q0 = pl.multiple_of(i * FOX_TQ, FOX_TQ)
    n_full = q0 // FOX_TK
    row = q0 + lax.broadcasted_iota(jnp.int32, (FOX_TQ, FOX_TK), 0)
    col = n_full * FOX_TK + lax.broadcasted_iota(jnp.int32, (FOX_TQ, FOX_TK), 1)
    heads = [slice(h * LANES, (h + 1) * LANES) for h in range(N_HEADS)]

    lane = lax.broadcasted_iota(jnp.int32, (FOX_TQ, LANES), 1)
    bias_lanes = (lane >= HEAD_DIM) & (lane < HEAD_DIM + BIAS_TERMS)
    kmax_sq = jnp.max(kn_ref[0], axis=0)
    caps, span = [], jnp.float32(0.0)
    for h in range(N_HEADS):
        q_h = q_ref[0, :, heads[h]]
        reach = _logit_reach(jnp.where(lane < HEAD_DIM, q_h, jnp.zeros_like(q_h)),
                             kmax_sq[KN_FOX + h:KN_FOX + h + 1, 0:1])
        own = k_ref[0, pl.ds(q0, FOX_TQ), heads[h]].astype(F32)
        caps.append(reach + jnp.sum(jnp.where(bias_lanes, own, 0.0), axis=-1, keepdims=True))
        span = jnp.maximum(span, 2.0 * jnp.max(reach))

    def logits(j, h, diag):
        k0 = pl.multiple_of(j * FOX_TK, FOX_TK)
        s = _dot_nt(q_ref[0, :, heads[h]], k_ref[0, pl.ds(k0, FOX_TK), heads[h]])
        return jnp.where(col <= row, s, NEG) if diag else s

    def values(j, h):
        return v_ref[0, pl.ds(pl.multiple_of(j * FOX_TK, FOX_TK), FOX_TK), heads[h]]

    def capped(j, accs, diag):
        return tuple(accs[h] + _dot(jnp.exp2(logits(j, h, diag) - caps[h]).astype(BF16), values(j, h))
                     for h in range(N_HEADS))

    def online(j, carry, diag):
        return tuple(_softmax_step(logits(j, h, diag), *carry[h], values(j, h))
                     for h in range(N_HEADS))

    def capped_sweep():
        accs = tuple(jnp.zeros((FOX_TQ, LANES), F32) for _ in range(N_HEADS))

        def pair(j, a):
            return capped(2 * j + 1, capped(2 * j, a, diag=False), diag=False)

        accs = lax.fori_loop(0, n_full // 2, pair, accs)
        accs = lax.fori_loop(2 * (n_full // 2), n_full, functools.partial(capped, diag=False), accs)
        return capped(n_full, accs, diag=True)

    def online_sweep():
        carry = tuple((jnp.full((FOX_TQ, 1), NEG, F32), jnp.zeros((FOX_TQ, LANES), F32))
                      for _ in range(N_HEADS))
        carry = lax.fori_loop(0, n_full, functools.partial(online, diag=False), carry)
        return tuple(acc for _, acc in online(n_full, carry, diag=True))

    accs = lax.cond(span <= SAFE_LOG2_SPAN, capped_sweep, online_sweep)
    o_ref[0] = jnp.concatenate(_normalise_heads(list(accs)), axis=1).astype(o_ref.dtype)


def _fox_attention(fqa, fkb, fva, key_norms):
    bsz, t_work, width = fqa.shape
    full = pl.BlockSpec((1, t_work, width), lambda b, i: (b, 0, 0), pipeline_mode=pl.Buffered(1))
    return pl.pallas_call(
        _fox_kernel,
        grid=(bsz, t_work // FOX_TQ),
        in_specs=[pl.BlockSpec((1, FOX_TQ, width), lambda b, i: (b, i, 0)), full, full,
                  pl.BlockSpec((1,) + key_norms.shape[1:], lambda b, i: (b, 0, 0, 0))],
        out_specs=pl.BlockSpec((1, FOX_TQ, 2 * LANES), lambda b, i: (b, i, 0)),
        out_shape=jax.ShapeDtypeStruct((bsz, t_work, 2 * LANES), BF16),
        compiler_params=pltpu.CompilerParams(
            dimension_semantics=("parallel", "parallel"), vmem_limit_bytes=VMEM_LIMIT),
        name="fox_attention",
    )(fqa, fkb, fva, key_norms)


GLA_NSUB = GLA_CHUNK // GLA_SUB


def _gla_kernel(q_ref, k_ref, v_ref, g_ref, la_ref, tri_ref, e_ref, gn_ref, o_ref, st_ref):
    @pl.when(pl.program_id(1) == 0)
    def _():
        st_ref[...] = jnp.zeros_like(st_ref)

    tri = tri_ref[...]
    emat = e_ref[...]
    lane = lax.broadcasted_iota(jnp.int32, (GLA_CHUNK, LANES), 1)
    rowblk = lax.broadcasted_iota(jnp.int32, (GLA_CHUNK, LANES), 0) // GLA_SUB
    tblk = lax.broadcasted_iota(jnp.int32, (GLA_CHUNK, GLA_CHUNK), 0) // GLA_SUB
    sblk = lax.broadcasted_iota(jnp.int32, (GLA_CHUNK, GLA_CHUNK), 1) // GLA_SUB
    trow = lax.broadcasted_iota(jnp.int32, (GLA_SUB, 256), 0)

    def chunk(c, carry):
        r0 = pl.multiple_of(c * GLA_CHUNK, GLA_CHUNK)
        rows = pl.ds(r0, GLA_CHUNK)
        h1, h2, h3 = _split3(la_ref[0, rows, :])
        b = _dot(tri, h1) + _dot(tri, h2) + _dot(tri, h3)
        q = q_ref[0, rows, :]
        k = k_ref[0, rows, :]
        v = v_ref[0, rows, :]
        b_last = b[GLA_CHUNK - 1:GLA_CHUNK, :]
        qd = q * jnp.exp2(b)
        kd = (k * jnp.exp2(b_last - b)).astype(BF16)
        starts = [jnp.zeros((1, 256), F32)] + [b[GLA_SUB * i - 1:GLA_SUB * i, :]
                                               for i in range(1, GLA_NSUB)]
        bsel = jnp.concatenate([jnp.broadcast_to(s, (GLA_SUB, 256)) for s in starts], axis=0)
        qn = q * jnp.exp2(b - bsel)

        diag = []
        for i in range(GLA_NSUB):
            rs = slice(GLA_SUB * i, GLA_SUB * (i + 1))
            b_i, q_i, k_i = b[rs], q[rs], k[rs]
            v_i = v[rs].astype(F32)
            ps = []
            for s in range(GLA_SUB):
                d = jnp.exp2(jnp.minimum(b_i - b_i[s:s + 1], 0.0))
                ps.append(jnp.where(trow >= s, q_i * d * k_i[s:s + 1], 0.0).astype(BF16))
            r = _dot(jnp.concatenate(ps, axis=0), emat)
            od = r[0:GLA_SUB] * v_i[0:1]
            for s in range(1, GLA_SUB):
                od = od + r[GLA_SUB * s:GLA_SUB * (s + 1)] * v_i[s:s + 1]
            diag.append(od)
        o_diag = jnp.concatenate(diag, axis=0)

        for slab in range(2):
            ls = slice(slab * LANES, (slab + 1) * LANES)
            qn_s, k_s, b_s = qn[:, ls], k[:, ls], b[:, ls]
            khat = jnp.concatenate(
                [(k_s * jnp.exp2(jnp.minimum(starts[i][:, ls] - b_s, 0.0))).astype(BF16)
                 for i in range(1, GLA_NSUB)], axis=1)
            for half in range(2):
                head = 2 * slab + half
                hs = slice(head * GLA_DV, (head + 1) * GLA_DV)
                in_head = (lane < HEAD_DIM) if half == 0 else (lane >= HEAD_DIM)
                qm = jnp.where(in_head, qn_s, 0.0)
                qhat = jnp.concatenate([jnp.where(rowblk == i, qm, 0.0).astype(BF16)
                                        for i in range(1, GLA_NSUB)], axis=1)
                att = jnp.where(sblk < tblk, _dot_nt(qhat, khat), 0.0)
                v_h = v[:, hs]
                st = st_ref[head]
                o = (_dot_nt(jnp.where(in_head, qd[:, ls], 0.0).astype(BF16), st.astype(BF16))
                     + _dot(att.astype(BF16), v_h) + o_diag[:, hs])
                st_ref[head] = st * jnp.exp2(b_last[:, ls]) + _dot_tn(v_h, kd[:, ls])
                y = o * lax.rsqrt(jnp.mean(o * o, axis=-1, keepdims=True) + EPS) * gn_ref[:, hs]
                o_ref[0, rows, hs] = (y * _silu(g_ref[0, rows, hs])).astype(o_ref.dtype)
        return carry

    lax.fori_loop(0, q_ref.shape[1] // GLA_CHUNK, chunk, 0, unroll=6)


def _gla(gq, gk, gv, gg, la, gain):
    bsz, t_work, _ = gq.shape
    tg = _row_tile(t_work)
    tri = jnp.asarray(np.tril(np.ones((GLA_CHUNK, GLA_CHUNK), np.float32)), dtype=BF16)
    emat = jnp.asarray(
        (np.arange(256)[:, None] // HEAD_DIM == np.arange(512)[None, :] // GLA_DV).astype(np.float32),
        dtype=BF16)
    rows = lambda width: pl.BlockSpec((1, tg, width), lambda b, j: (b, j, 0))
    const = lambda shape: pl.BlockSpec(shape, lambda b, j: (0,) * len(shape))
    return pl.pallas_call(
        _gla_kernel,
        grid=(bsz, t_work // tg),
        in_specs=[rows(256), rows(256), rows(512), rows(512), rows(256),
                  const((GLA_CHUNK, GLA_CHUNK)), const((256, 512)), const((1, 512))],
        out_specs=rows(512),
        out_shape=jax.ShapeDtypeStruct((bsz, t_work, 512), BF16),
        scratch_shapes=[pltpu.VMEM((N_HEADS, GLA_DV, LANES), F32)],
        compiler_params=pltpu.CompilerParams(
            dimension_semantics=("parallel", "arbitrary"), vmem_limit_bytes=VMEM_LIMIT),
        name="gla",
    )(gq, gk, gv, gg, la, tri, emat, gain)


FFN_CHUNK = 256


def _ffn_kernel(h_ref, oa_ref, ob_ref, oc_ref, wo_ref, fn_ref, wgu_ref, wd_ref, out_ref):
    h1 = (h_ref[0] + _dot(oa_ref[0], wo_ref[0:256, :]) + _dot(ob_ref[0], wo_ref[256:512, :])
          + _dot(oc_ref[0], wo_ref[512:1024, :]))
    ms = jnp.mean(h1 * h1, axis=-1, keepdims=True)
    f = (h1 * lax.rsqrt(ms + EPS) * fn_ref[...]).astype(BF16)
    out_ref[0] = h1
    for c in range(0, D_FF, FFN_CHUNK):
        gate = _dot(f, wgu_ref[:, c:c + FFN_CHUNK])
        up = _dot(f, wgu_ref[:, D_FF + c:D_FF + c + FFN_CHUNK])
        out_ref[0] += _dot((_silu(gate) * up).astype(BF16), wd_ref[c:c + FFN_CHUNK, :])


def _outproj_ffn(h, oa, ob, oc, wo, fn, wgu, wd):
    bsz, t_work, _ = h.shape
    tm = _row_tile(t_work)
    rows = lambda width: pl.BlockSpec((1, tm, width), lambda b, j: (b, j, 0))
    const = lambda shape: pl.BlockSpec(shape, lambda b, j: (0,) * len(shape),
                                       pipeline_mode=pl.Buffered(1))
    return pl.pallas_call(
        _ffn_kernel,
        grid=(bsz, t_work // tm),
        in_specs=[rows(D_MODEL), rows(256), rows(256), rows(512), const((D_MODEL, D_MODEL)),
                  const((1, D_MODEL)), const((D_MODEL, 2 * D_FF)), const((D_FF, D_MODEL))],
        out_specs=rows(D_MODEL),
        out_shape=jax.ShapeDtypeStruct(h.shape, F32),
        compiler_params=pltpu.CompilerParams(
            dimension_semantics=("parallel", "parallel"), vmem_limit_bytes=VMEM_LIMIT),
        name="outproj_ffn",
    )(h, oa, ob, oc, wo, fn, wgu, wd)


def kernel(x, meta_tokens, attn_norm, w_in, dsa_q_norm, dsa_k_norm, fox_q_norm, fox_k_norm,
           fox_f_bias, gla_gate_w2, gla_gate_b, gla_out_norm, w_out, ffn_norm, w_gate_up, w_down):
    bsz, seq, _ = x.shape
    n_tok = N_META + seq
    t_work = _work_len(n_tok)
    meta = jnp.broadcast_to(meta_tokens[None].astype(x.dtype), (bsz, N_META, D_MODEL))
    h = jnp.concatenate([meta, x, jnp.zeros((bsz, t_work - n_tok, D_MODEL), x.dtype)], axis=1)
    cos_t, sin_t = _rope_tables(t_work)
    g256 = _group_matrix()
    k_top = min(TOPK_MAX, seq // 4)
    for l in range(w_in.shape[0]):
        an, w_perm, vecs, w2_pad, sm_bias = _prep_layer_params(
            l, attn_norm, w_in, dsa_q_norm, dsa_k_norm, fox_q_norm, fox_k_norm, fox_f_bias,
            gla_gate_w2, gla_gate_b)
        (dq, dkk, dva, iq, ikk, fq, fk, fva, gq, gk, gv, gg, la, small, key_norms) = _inproj(
            h, an, w_perm, cos_t, sin_t, g256, vecs, w2_pad, sm_bias)
        small_t, fkb = _fox_prep(small, fk)
        oa = _dsa_attention(dq, dkk, dva, iq, ikk, small_t, key_norms, k_top, n_tok)
        ob = _fox_attention(fq, fkb, fva, key_norms)
        oc = _gla(gq, gk, gv, gg, la, jnp.tile(gla_out_norm[l], N_HEADS)[None, :])
        h = _outproj_ffn(h, oa, ob, oc, w_out[l].astype(BF16), ffn_norm[l][None, :],
                         w_gate_up[l].astype(BF16), w_down[l].astype(BF16))
    return h[:, N_META:n_tok]
```
